```python
import jax, jax.numpy as jnp
from jax import lax
import numpy as np

D_MODEL = 1024
BATCH = 8
SEQ = 2048
DEPTH = 1
DEC_BATCH = 128
DEC_SEQ = 8
PAST_LEN = 16384
PAGE_SIZE = 128

D_MIX = D_MODEL
POOL_WIDTH = D_MIX // 4
POOL_WINDOWS = (2, 4, 8, 16)
POOL_GROUP = POOL_WIDTH // len(POOL_WINDOWS)
POOL_HIST = max(POOL_WINDOWS) - 1
HGRN_WIDTH = D_MIX // 2
HGRN_EXPAND = 128
HGRN_HEADS = HGRN_WIDTH // HGRN_EXPAND
HGRN_DK = HGRN_EXPAND
HGRN_DV = HGRN_WIDTH // HGRN_HEADS
HGRN_FDIM = HGRN_HEADS * HGRN_DK
HGRN_CHUNK = 64
XATTN_WIDTH = D_MIX - POOL_WIDTH - HGRN_WIDTH
XATTN_HEADS = 4
XATTN_DH = XATTN_WIDTH // XATTN_HEADS
N_MEM = 256
D_FF = 2816
CONV_W = 3
EPS = 1e-6
SPLITS = (POOL_WIDTH,
          POOL_WIDTH + HGRN_FDIM,
          POOL_WIDTH + 2 * HGRN_FDIM,
          POOL_WIDTH + 2 * HGRN_FDIM + HGRN_WIDTH,
          POOL_WIDTH + 2 * HGRN_FDIM + 2 * HGRN_WIDTH)
D_IN = POOL_WIDTH + 2 * HGRN_FDIM + 2 * HGRN_WIDTH + XATTN_WIDTH

kernel_name = "hymba_pool_hgrn2_memxattn_convffn_step"


def rmsnorm(x, g):
    xf = x.astype(jnp.float32)
    y = xf * lax.rsqrt(jnp.mean(xf * xf, axis=-1, keepdims=True) + EPS)
    return (y * g.astype(jnp.float32)).astype(x.dtype)


def pool_mixer(u, hist, pos0, pool_w, pool_scale):
    B, L, _ = u.shape
    xp = jnp.concatenate([hist.astype(jnp.float32), u.astype(jnp.float32)], axis=1)
    cs = jnp.concatenate([jnp.zeros((B, 1, POOL_WIDTH), jnp.float32), jnp.cumsum(xp, axis=1)], axis=1)
    pos = (pos0 + jnp.arange(L)).astype(jnp.float32)
    uf = u.astype(jnp.float32)
    outs = []
    for gi, w in enumerate(POOL_WINDOWS):
        sl = slice(gi * POOL_GROUP, (gi + 1) * POOL_GROUP)
        total = cs[:, POOL_HIST + 1:POOL_HIST + 1 + L, sl] - cs[:, POOL_HIST + 1 - w:POOL_HIST + 1 - w + L, sl]
        cnt = jnp.minimum(jnp.float32(w), pos + 1.0)
        mean = total / cnt[None, :, None]
        outs.append(jnp.einsum("blc,cd->bld", mean - uf[..., sl], pool_w[gi].astype(jnp.float32)))
    out = jnp.concatenate(outs, axis=-1) * pool_scale.astype(jnp.float32)
    new_hist = xp[:, -POOL_HIST:].astype(hist.dtype)
    return out.astype(u.dtype), new_hist


def hgrn2_scan(q, k, v, log_f, S0):
    B, L, H, DK = q.shape
    DV = v.shape[-1]
    C = min(HGRN_CHUNK, L)
    n = -(-L // C)
    pad = n * C - L

    def blocks(a):
        a = jnp.pad(a, ((0, 0), (0, pad), (0, 0), (0, 0)))
        return a.reshape(B, n, C, H, a.shape[-1]).transpose(1, 0, 3, 2, 4)

    mask = jnp.tril(jnp.ones((C, C), bool))[:, :, None]

    def step(S, blk):
        qc, kc, vc, gc = blk
        A = jnp.cumsum(gc, axis=2)
        decay = jnp.exp(jnp.where(mask, A[:, :, :, None, :] - A[:, :, None, :, :], -jnp.inf))
        scores = jnp.einsum("bhtd,bhsd,bhtsd->bhts", qc, kc, decay)
        o = jnp.einsum("bhts,bhsv->bhtv", scores, vc) + jnp.einsum("bhtd,bhdv->bhtv", qc * jnp.exp(A), S)
        A_end = A[:, :, -1:, :]
        S = jnp.exp(A_end[:, :, 0, :])[..., None] * S + jnp.einsum("bhsd,bhsv->bhdv", kc * jnp.exp(A_end - A), vc)
        return S, o

    S, o = lax.scan(step, S0, (blocks(q), blocks(k), blocks(v), blocks(log_f)))
    o = o.transpose(1, 0, 3, 2, 4).reshape(B, n * C, H, DV)[:, :L]
    return o, S


def hgrn2_mixer(q, fp, i, g, S0, lb, onorm_g):
    B, L, _ = q.shape
    lbf = lb.astype(jnp.float32)
    fpf = fp.astype(jnp.float32)
    log_f = jnp.log(lbf + (1.0 - lbf) * jax.nn.sigmoid(fpf))
    k = (1.0 - lbf) * jax.nn.sigmoid(-fpf)
    qf = jax.nn.silu(q.astype(jnp.float32))
    heads = lambda a, d: a.reshape(B, L, HGRN_HEADS, d)
    o, S = hgrn2_scan(heads(qf, HGRN_DK), heads(k, HGRN_DK), heads(i.astype(jnp.float32), HGRN_DV),
                      heads(log_f, HGRN_DK), S0.astype(jnp.float32))
    o = o * lax.rsqrt(jnp.mean(o * o, axis=-1, keepdims=True) + EPS)
    o = o.reshape(B, L, HGRN_WIDTH) * onorm_g.astype(jnp.float32) * jax.nn.silu(g.astype(jnp.float32))
    return o.astype(q.dtype), S.astype(S0.dtype)


def cross_attn(qx, mem_k, mem_v):
    B, L, _ = qx.shape
    q = qx.reshape(B, L, XATTN_HEADS, XATTN_DH).astype(jnp.float32)
    s = jnp.einsum("blhd,bmhd->bhlm", q, mem_k.astype(jnp.float32)) * (XATTN_DH ** -0.5)
    p = jax.nn.softmax(s, axis=-1)
    o = jnp.einsum("bhlm,bmhd->blhd", p, mem_v.astype(jnp.float32))
    return o.reshape(B, L, XATTN_WIDTH).astype(qx.dtype)


def conv_ffn(x, hist, ln2_g, w_up, conv_w, conv_b, w_down):
    B, L, _ = x.shape
    h = rmsnorm(x, ln2_g)
    ab = h @ w_up
    a, b = ab[..., :D_FF], ab[..., D_FF:]
    ap = jnp.concatenate([hist.astype(a.dtype), a], axis=1)
    conv = conv_b
    for j in range(CONV_W):
        conv = conv + conv_w[j] * ap[:, j:j + L]
    out = (jax.nn.gelu(conv) * b) @ w_down
    return out, ap[:, -(CONV_W - 1):].astype(hist.dtype)


def trunk_layer(x, mem_k, mem_v, pool_hist, pos0, S0, conv_hist, lb,
                ln1_g, w_in, pool_w, pool_scale, onorm_g, w_out, ln2_g, w_up, conv_w, conv_b, w_down):
    h = rmsnorm(x, ln1_g)
    proj = h @ w_in
    u, q, fp, i, g, qx = jnp.split(proj, SPLITS, axis=-1)
    o_pool, new_pool = pool_mixer(u, pool_hist, pos0, pool_w, pool_scale)
    o_hgrn, new_S = hgrn2_mixer(q, fp, i, g, S0, lb, onorm_g)
    o_x = cross_attn(qx, mem_k, mem_v)
    x = x + jnp.concatenate([o_pool, o_hgrn, o_x], axis=-1) @ w_out
    f_out, new_conv = conv_ffn(x, conv_hist, ln2_g, w_up, conv_w, conv_b, w_down)
    x = x + f_out
    return x, new_pool, new_S, new_conv


def setup_inputs(seed: int = 0) -> dict:
    key = jax.random.key(seed)
    ks = jax.random.split(key, 32)
    nrm = lambda k, shape, s: jax.random.normal(k, shape, jnp.float32) * s
    return {
        "x_prompt": nrm(ks[0], (BATCH, SEQ, D_MODEL), 1.0),
        "x_sample": nrm(ks[1], (DEC_BATCH, DEC_SEQ, D_MODEL), 1.0),
        "mem_prompt": nrm(ks[2], (BATCH, N_MEM, D_MODEL), 1.0),
        "state_pool": nrm(ks[3], (DEPTH, DEC_BATCH, POOL_HIST, POOL_WIDTH), 1.0),
        "state_hgrn": nrm(ks[4], (DEPTH, DEC_BATCH, HGRN_HEADS, HGRN_DK, HGRN_DV), 0.5),
        "state_conv": nrm(ks[5], (DEPTH, DEC_BATCH, CONV_W - 1, D_FF), 1.0),
        "cache_mem_k": nrm(ks[6], (DEPTH, DEC_BATCH, N_MEM, XATTN_HEADS, XATTN_DH), 1.0),
        "cache_mem_v": nrm(ks[7], (DEPTH, DEC_BATCH, N_MEM, XATTN_HEADS, XATTN_DH), 1.0),
        "ln1_g": 1.0 + nrm(ks[8], (DEPTH, D_MODEL), 0.02),
        "w_in": nrm(ks[9], (DEPTH, D_MODEL, D_IN), D_MODEL ** -0.5),
        "pool_w": nrm(ks[10], (DEPTH, len(POOL_WINDOWS), POOL_GROUP, POOL_GROUP), POOL_GROUP ** -0.5),
        "pool_scale": 1.0 + nrm(ks[11], (DEPTH, POOL_WIDTH), 0.02),
        "hgrn_lb_logits": nrm(ks[12], (DEPTH + 1, HGRN_FDIM), 0.1),
        "hgrn_onorm_g": 1.0 + nrm(ks[13], (DEPTH, HGRN_WIDTH), 0.02),
        "mem_norm_g": 1.0 + nrm(ks[14], (DEPTH, D_MODEL), 0.02),
        "w_mem_kv": nrm(ks[15], (DEPTH, D_MODEL, 2 * XATTN_WIDTH), D_MODEL ** -0.5),
        "w_out": nrm(ks[16], (DEPTH, D_MIX, D_MODEL), D_MIX ** -0.5),
        "ln2_g": 1.0 + nrm(ks[17], (DEPTH, D_MODEL), 0.02),
        "w_up": nrm(ks[18], (DEPTH, D_MODEL, 2 * D_FF), D_MODEL ** -0.5),
        "conv_w": nrm(ks[19], (DEPTH, CONV_W, D_FF), CONV_W ** -0.5),
        "conv_b": nrm(ks[20], (DEPTH, D_FF), 0.02),
        "w_down": nrm(ks[21], (DEPTH, D_FF, D_MODEL), D_FF ** -0.5),
        "lnf_g": 1.0 + nrm(ks[22], (D_MODEL,), 0.02),
    }


def reference(x_prompt, x_sample, mem_prompt, state_pool, state_hgrn, state_conv, cache_mem_k, cache_mem_v,
              ln1_g, w_in, pool_w, pool_scale, hgrn_lb_logits, hgrn_onorm_g, mem_norm_g, w_mem_kv, w_out,
              ln2_g, w_up, conv_w, conv_b, w_down, lnf_g):
    lb_all = jnp.cumsum(jax.nn.softmax(hgrn_lb_logits.astype(jnp.float32), axis=0), axis=0)
    yp, ys = x_prompt, x_sample
    pp_l, sp_l, cp_l, mk_l, mv_l, ps_l, ss_l, cs_l = [], [], [], [], [], [], [], []
    dt = x_prompt.dtype
    for l in range(DEPTH):
        weights = (ln1_g[l], w_in[l], pool_w[l], pool_scale[l], hgrn_onorm_g[l], w_out[l],
                   ln2_g[l], w_up[l], conv_w[l], conv_b[l], w_down[l])
        kv = rmsnorm(mem_prompt, mem_norm_g[l]) @ w_mem_kv[l]
        mk = kv[..., :XATTN_WIDTH].reshape(BATCH, N_MEM, XATTN_HEADS, XATTN_DH)
        mv = kv[..., XATTN_WIDTH:].reshape(BATCH, N_MEM, XATTN_HEADS, XATTN_DH)
        yp, pp, sp, cp = trunk_layer(
            yp, mk, mv, jnp.zeros((BATCH, POOL_HIST, POOL_WIDTH), dt), 0,
            jnp.zeros((BATCH, HGRN_HEADS, HGRN_DK, HGRN_DV), dt),
            jnp.zeros((BATCH, CONV_W - 1, D_FF), dt), lb_all[l], *weights)
        ys, ps, ss, cs = trunk_layer(
            ys, cache_mem_k[l], cache_mem_v[l], state_pool[l], PAST_LEN, state_hgrn[l], state_conv[l],
            lb_all[l], *weights)
        pp_l.append(pp); sp_l.append(sp); cp_l.append(cp); mk_l.append(mk); mv_l.append(mv)
        ps_l.append(ps); ss_l.append(ss); cs_l.append(cs)
    y_prompt = rmsnorm(yp, lnf_g)
    y_sample = rmsnorm(ys, lnf_g)
    return (y_prompt, y_sample,
            jnp.stack(pp_l), jnp.stack(sp_l), jnp.stack(cp_l), jnp.stack(mk_l), jnp.stack(mv_l),
            jnp.stack(ps_l), jnp.stack(ss_l), jnp.stack(cs_l))
```

```python
import functools

import jax
import jax.numpy as jnp
from jax import lax
from jax.experimental import pallas as pl
from jax.experimental.pallas import tpu as pltpu

F32 = jnp.float32
BF16 = jnp.bfloat16

D_MODEL = 1024
POOL_WIDTH = 256
POOL_WINDOWS = (2, 4, 8, 16)
POOL_GROUP = 64
POOL_HIST = 15
HGRN_WIDTH = 512
HGRN_HEADS = 4
HGRN_DK = 128
HGRN_DV = 128
XATTN_WIDTH = 256
XATTN_HEADS = 4
XATTN_DH = 64
N_MEM = 256
D_FF = 2816
CONV_W = 3
EPS = 1e-6
PAST_LEN = 16384
D_IN = POOL_WIDTH + 4 * HGRN_WIDTH + XATTN_WIDTH
OFF_U, OFF_Q, OFF_F, OFF_I, OFF_G, OFF_X = 0, 256, 768, 1280, 1792, 2304

CHUNK = 64
POOL_PAD = 16
CONV_PAD = 8
VMEM_LIMIT_BYTES = 56 * 1024 * 1024

_NT = (((1,), (1,)), ((), ()))
_TN = (((0,), (0,)), ((), ()))


def _dot(a, b):
    return jnp.dot(a, b, preferred_element_type=F32)


def _dot_nt(a, b):
    return lax.dot_general(a, b, _NT, preferred_element_type=F32)


def _dot_tn(a, b):
    return lax.dot_general(a, b, _TN, preferred_element_type=F32)


def _rmsnorm(x, g):
    return x * lax.rsqrt(jnp.mean(x * x, axis=-1, keepdims=True) + EPS) * g


def _const_spec(shape):
    nd = len(shape)
    return pl.BlockSpec(shape, lambda *_: (0,) * nd, pipeline_mode=pl.Buffered(1))


def _memkv_body(mem_ref, g_ref, w_ref, k_ref, v_ref):
    h = _rmsnorm(mem_ref[0], g_ref[...]).astype(BF16)
    kv = _dot(h, w_ref[...])
    k_ref[0] = kv[:, :XATTN_WIDTH]
    v_ref[0] = kv[:, XATTN_WIDTH:]


def _memkv_call(mem, g, w):
    b = mem.shape[0]
    out = jax.ShapeDtypeStruct((b, N_MEM, XATTN_WIDTH), F32)
    return pl.pallas_call(
        _memkv_body,
        grid=(b,),
        in_specs=[pl.BlockSpec((1, N_MEM, D_MODEL), lambda i: (i, 0, 0)),
                  _const_spec((1, D_MODEL)),
                  _const_spec((D_MODEL, 2 * XATTN_WIDTH))],
        out_specs=[pl.BlockSpec((1, N_MEM, XATTN_WIDTH), lambda i: (i, 0, 0))] * 2,
        out_shape=[out, out],
        compiler_params=pltpu.CompilerParams(dimension_semantics=("arbitrary",)),
        name="memkv",
    )(mem, g, w)


def _forget_lower_bound(logits):
    z = logits - jnp.max(logits, axis=0, keepdims=True)
    e = jnp.exp(z)
    return e[0:1, :] / jnp.sum(e, axis=0, keepdims=True)


def _hgrn_gates(proj, lb):
    fp = proj[:, OFF_F:OFF_F + HGRN_WIDTH]
    q = proj[:, OFF_Q:OFF_Q + HGRN_WIDTH]
    log_f = jnp.log(lb + (1.0 - lb) * jax.nn.sigmoid(fp))
    k = (1.0 - lb) * jax.nn.sigmoid(-fp)
    qf = q * jax.nn.sigmoid(q)
    return qf, k, log_f


def _chunk_cumsum(x, rows):
    ridx = lax.broadcasted_iota(jnp.int32, x.shape, 0)
    sh = 1
    while sh < rows:
        x = x + jnp.where(ridx >= sh, pltpu.roll(x, sh, 0), 0.0)
        sh *= 2
    return x


def _level_operands(a, qf, k, log_f, m, rows):
    n = a.shape[1]
    ridx = lax.broadcasted_iota(jnp.int32, (rows, n), 0)
    upper = ((ridx // m) % 2) == 1
    if m == 1:
        e = jnp.exp(log_f)
    else:
        if (2 * m) % 8 == 0:
            nb = rows // (2 * m)
            a3 = a.reshape(nb, 2 * m, n)
            ref = jnp.broadcast_to(a3[:, m - 1:m, :], (nb, 2 * m, n)).reshape(rows, n)
        else:
            a3 = a.reshape(rows // 8, 8, n)
            sub = lax.broadcasted_iota(jnp.int32, a3.shape, 1)
            ref = jnp.where(sub < 4,
                            jnp.broadcast_to(a3[:, 1:2, :], a3.shape),
                            jnp.broadcast_to(a3[:, 5:6, :], a3.shape)).reshape(rows, n)
        d = jnp.where(upper, a - ref, ref - a)
        e = jnp.exp(jnp.minimum(d, 0.0))
    qs = jnp.where(upper, qf * e, 0.0)
    ks = jnp.where(upper, 0.0, k if m == 1 else k * e)
    return qs.astype(BF16), ks.astype(BF16)


def _intra_scores(a, qf, k, log_f, head, rows, seq):
    sl = slice(head * HGRN_DK, (head + 1) * HGRN_DK)
    r = lax.broadcasted_iota(jnp.int32, (rows, rows), 0)
    c = lax.broadcasted_iota(jnp.int32, (rows, rows), 1)
    total = None
    m = seq // 2
    while m >= 1:
        qs, ks = _level_operands(a, qf, k, log_f, m, rows)
        p = _dot_nt(qs[:, sl], ks[:, sl])
        if 2 * m < rows:
            p = jnp.where((r // (2 * m)) == (c // (2 * m)), p, 0.0)
        total = p if total is None else total + p
        m //= 2
    return total


def _cross_attention(qx, mk, mv):
    rows = qx.shape[0]
    head_of_lane = lax.broadcasted_iota(jnp.int32, qx.shape, 1) // XATTN_DH
    qs = jnp.concatenate([jnp.where(head_of_lane == h, qx, 0.0) for h in range(XATTN_HEADS)], axis=0)
    s = _dot_nt(qs.astype(BF16), mk)
    s = s - jnp.max(s, axis=-1, keepdims=True)
    e = jnp.exp(s)
    p = e / jnp.sum(e, axis=-1, keepdims=True)
    o = _dot(p.astype(BF16), mv)
    out = jnp.zeros(qx.shape, F32)
    for h in range(XATTN_HEADS):
        out = jnp.where(head_of_lane == h, o[h * rows:(h + 1) * rows, :], out)
    return out


def _pool_means(pbuf, rows, pos0):
    def ld(j, half):
        return pbuf[pl.ds(POOL_PAD - j, rows), pl.ds(128 * half, 128)]

    lane = lax.broadcasted_iota(jnp.int32, (rows, 128), 1)
    first = lane < POOL_GROUP
    posf = (pos0 + 1).astype(F32)
    u_lo, u_hi = ld(0, 0), ld(0, 1)
    t2 = u_lo + ld(1, 0)
    t4 = t2 + ld(2, 0) + ld(3, 0)
    t8 = u_hi
    for j in range(1, 8):
        t8 = t8 + ld(j, 1)
    t16 = t8
    for j in range(8, 16):
        t16 = t16 + ld(j, 1)
    cnt_lo = jnp.where(first, jnp.minimum(2.0, posf), jnp.minimum(4.0, posf))
    cnt_hi = jnp.where(first, jnp.minimum(8.0, posf), jnp.minimum(16.0, posf))
    lo = jnp.where(first, t2, t4) / cnt_lo - u_lo
    hi = jnp.where(first, t8, t16) / cnt_hi - u_hi
    return jnp.concatenate([lo, hi], axis=-1)


def _prompt_mixer_body(x_ref, mk_ref, mv_ref, ln1_ref, win_ref, poolw_ref, pscale_ref, lbl_ref, onorm_ref,
                       wout_ref, x2_ref, npool_ref, ns_ref,
                       pbuf, st, qf_s, k_s, lf_s, v_s, g_s, mix, *, tb):
    j = pl.program_id(1)
    nj = pl.num_programs(1)

    @pl.when(j == 0)
    def _():
        pbuf[pl.ds(0, POOL_PAD), :] = jnp.zeros((POOL_PAD, POOL_WIDTH), F32)
        st[...] = jnp.zeros(st.shape, F32)

    x = x_ref[0]
    h = _rmsnorm(x, ln1_ref[...]).astype(BF16)
    proj = _dot(h, win_ref[...])

    pbuf[pl.ds(POOL_PAD, tb), :] = proj[:, OFF_U:OFF_U + POOL_WIDTH]
    pos = j * tb + lax.broadcasted_iota(jnp.int32, (tb, 1), 0)
    dm = _pool_means(pbuf, tb, pos)
    o_pool = _dot(dm.astype(BF16), poolw_ref[...]) * pscale_ref[...]
    mix[:, pl.ds(0, POOL_WIDTH)] = o_pool.astype(BF16)

    @pl.when(j == nj - 1)
    def _():
        npool_ref[0] = pbuf[pl.ds(tb + 1, POOL_HIST), :]

    pbuf[pl.ds(0, POOL_PAD), :] = pbuf[pl.ds(tb, POOL_PAD), :]

    qx = proj[:, OFF_X:OFF_X + XATTN_WIDTH] * (XATTN_DH ** -0.5)
    o_x = _cross_attention(qx, mk_ref[0].astype(BF16), mv_ref[0].astype(BF16))
    mix[:, pl.ds(POOL_WIDTH + HGRN_WIDTH, XATTN_WIDTH)] = o_x.astype(BF16)

    lb = _forget_lower_bound(lbl_ref[...])
    qf, k, log_f = _hgrn_gates(proj, lb)
    qf_s[...] = qf
    k_s[...] = k
    lf_s[...] = log_f
    v_s[...] = proj[:, OFF_I:OFF_I + HGRN_WIDTH]
    gg = proj[:, OFF_G:OFF_G + HGRN_WIDTH]
    g_s[...] = gg * jax.nn.sigmoid(gg) * onorm_ref[...]

    def chunk(c, carry):
        rows = pl.ds(pl.multiple_of(c * CHUNK, CHUNK), CHUNK)
        qf_c, k_c, lf_c, v_c = qf_s[rows, :], k_s[rows, :], lf_s[rows, :], v_s[rows, :]
        a = _chunk_cumsum(lf_c, CHUNK)
        a_end = a[CHUNK - 1:CHUNK, :]
        q_in = (qf_c * jnp.exp(a)).astype(BF16)
        k_out = (k_c * jnp.exp(a_end - a)).astype(BF16)
        decay = jnp.exp(a_end)
        v_b = v_c.astype(BF16)
        qk = qf_c * k_c
        for hd in range(HGRN_HEADS):
            sl = slice(hd * HGRN_DK, (hd + 1) * HGRN_DK)
            p = _intra_scores(a, qf_c, k_c, lf_c, hd, CHUNK, CHUNK)
            s_t = st[hd]
            o = _dot(p.astype(BF16), v_b[:, sl])
            o = o + _dot_nt(q_in[:, sl], s_t.astype(BF16))
            o = o + jnp.sum(qk[:, sl], axis=-1, keepdims=True) * v_c[:, sl]
            st[hd] = s_t * decay[:, sl] + _dot_tn(v_b[:, sl], k_out[:, sl])
            o = o * lax.rsqrt(jnp.mean(o * o, axis=-1, keepdims=True) + EPS)
            mix[rows, pl.ds(POOL_WIDTH + hd * HGRN_DV, HGRN_DV)] = (o * g_s[rows, sl]).astype(BF16)
        return carry

    lax.fori_loop(0, tb // CHUNK, chunk, 0)

    @pl.when(j == nj - 1)
    def _():
        for hd in range(HGRN_HEADS):
            ns_ref[0, hd] = st[hd].T

    x2_ref[0] = x + _dot(mix[...], wout_ref[...])


def _prompt_mixer_call(x, mk, mv, ln1, w_in, pool_wbd, pool_scale, lb_logits, onorm, w_out, tb):
    b, l, d = x.shape
    grid = (b, l // tb)
    blk = pl.BlockSpec((1, tb, d), lambda i, j: (i, j, 0))
    mem = pl.BlockSpec((1, N_MEM, XATTN_WIDTH), lambda i, j: (i, 0, 0))
    return pl.pallas_call(
        functools.partial(_prompt_mixer_body, tb=tb),
        grid=grid,
        in_specs=[blk, mem, mem,
                  _const_spec((1, d)), _const_spec((d, D_IN)), _const_spec((POOL_WIDTH, POOL_WIDTH)),
                  _const_spec((1, POOL_WIDTH)), _const_spec(lb_logits.shape), _const_spec((1, HGRN_WIDTH)),
                  _const_spec((d, d))],
        out_specs=[blk,
                   pl.BlockSpec((1, POOL_HIST, POOL_WIDTH), lambda i, j: (i, 0, 0)),
                   pl.BlockSpec((1, HGRN_HEADS, HGRN_DK, HGRN_DV), lambda i, j: (i, 0, 0, 0))],
        out_shape=[jax.ShapeDtypeStruct((b, l, d), F32),
                   jax.ShapeDtypeStruct((b, POOL_HIST, POOL_WIDTH), F32),
                   jax.ShapeDtypeStruct((b, HGRN_HEADS, HGRN_DK, HGRN_DV), F32)],
        scratch_shapes=[pltpu.VMEM((POOL_PAD + tb, POOL_WIDTH), F32),
                        pltpu.VMEM((HGRN_HEADS, HGRN_DV, HGRN_DK), F32),
                        pltpu.VMEM((tb, HGRN_WIDTH), F32), pltpu.VMEM((tb, HGRN_WIDTH), F32),
                        pltpu.VMEM((tb, HGRN_WIDTH), F32), pltpu.VMEM((tb, HGRN_WIDTH), F32),
                        pltpu.VMEM((tb, HGRN_WIDTH), F32),
                        pltpu.VMEM((tb, d), BF16)],
        compiler_params=pltpu.CompilerParams(dimension_semantics=("arbitrary", "arbitrary"),
                                             vmem_limit_bytes=VMEM_LIMIT_BYTES),
        name="prompt_mixer",
    )(x, mk, mv, ln1, w_in, pool_wbd, pool_scale, lb_logits, onorm, w_out)


def _gelu_tanh(x):
    return 0.5 * x * (1.0 + jnp.tanh(0.7978845608028654 * (x + 0.044715 * (x * x * x))))


def _ffn_tail(x, act, wdown_ref, lnf_ref):
    y = x + _dot(act.astype(BF16), wdown_ref[...])
    return _rmsnorm(y, lnf_ref[...])


def _prompt_ffn_body(x_ref, ln2_ref, wup_ref, cw_ref, cb_ref, wdown_ref, lnf_ref, y_ref, nconv_ref, abuf, *, tb):
    j = pl.program_id(1)
    nj = pl.num_programs(1)

    @pl.when(j == 0)
    def _():
        abuf[pl.ds(0, CONV_PAD), :] = jnp.zeros((CONV_PAD, D_FF), F32)

    x = x_ref[0]
    h = _rmsnorm(x, ln2_ref[...]).astype(BF16)
    ab = _dot(h, wup_ref[...])
    abuf[pl.ds(CONV_PAD, tb), :] = ab[:, :D_FF]
    conv = cb_ref[...]
    for t in range(CONV_W):
        conv = conv + cw_ref[t:t + 1, :] * abuf[pl.ds(CONV_PAD - (CONV_W - 1) + t, tb), :]
    act = _gelu_tanh(conv) * ab[:, D_FF:]

    @pl.when(j == nj - 1)
    def _():
        nconv_ref[0] = abuf[pl.ds(CONV_PAD + tb - (CONV_W - 1), CONV_W - 1), :]

    abuf[pl.ds(0, CONV_PAD), :] = abuf[pl.ds(tb, CONV_PAD), :]
    y_ref[0] = _ffn_tail(x, act, wdown_ref, lnf_ref)


def _prompt_ffn_call(x, ln2, w_up, conv_w, conv_b, w_down, lnf, tb):
    b, l, d = x.shape
    blk = pl.BlockSpec((1, tb, d), lambda i, j: (i, j, 0))
    return pl.pallas_call(
        functools.partial(_prompt_ffn_body, tb=tb),
        grid=(b, l // tb),
        in_specs=[blk, _const_spec((1, d)), _const_spec((d, 2 * D_FF)), _const_spec((CONV_W, D_FF)),
                  _const_spec((1, D_FF)), _const_spec((D_FF, d)), _const_spec((1, d))],
        out_specs=[blk, pl.BlockSpec((1, CONV_W - 1, D_FF), lambda i, j: (i, 0, 0))],
        out_shape=[jax.ShapeDtypeStruct((b, l, d), F32),
                   jax.ShapeDtypeStruct((b, CONV_W - 1, D_FF), F32)],
        scratch_shapes=[pltpu.VMEM((CONV_PAD + tb, D_FF), F32)],
        compiler_params=pltpu.CompilerParams(dimension_semantics=("arbitrary", "arbitrary"),
                                             vmem_limit_bytes=VMEM_LIMIT_BYTES),
        name="prompt_ffn",
    )(x, ln2, w_up, conv_w, conv_b, w_down, lnf)


def _sample_mixer_body(x_ref, hist_ref, s0_ref, mk_ref, mv_ref, ln1_ref, win_ref, poolw_ref, pscale_ref, lbl_ref,
                       onorm_ref, wout_ref, x2_ref, npool_ref, ns_ref, pbuf, *, gs, sl_len):
    rows = gs * sl_len
    x = x_ref[...].reshape(rows, D_MODEL)
    h = _rmsnorm(x, ln1_ref[...]).astype(BF16)
    proj = _dot(h, win_ref[...])

    u = proj[:, OFF_U:OFF_U + POOL_WIDTH]
    pos = jnp.full((sl_len, 1), PAST_LEN, jnp.int32) + lax.broadcasted_iota(jnp.int32, (sl_len, 1), 0)
    dms = []
    for s in range(gs):
        pbuf[pl.ds(1, POOL_HIST), :] = hist_ref[s]
        pbuf[pl.ds(POOL_PAD, sl_len), :] = u[s * sl_len:(s + 1) * sl_len, :]
        dms.append(_pool_means(pbuf, sl_len, pos))
        npool_ref[s] = pbuf[pl.ds(sl_len + 1, POOL_HIST), :]
    dm = jnp.concatenate(dms, axis=0)
    o_pool = _dot(dm.astype(BF16), poolw_ref[...]) * pscale_ref[...]

    qx = proj[:, OFF_X:OFF_X + XATTN_WIDTH] * (XATTN_DH ** -0.5)
    oxs = []
    for s in range(gs):
        oxs.append(_cross_attention(qx[s * sl_len:(s + 1) * sl_len, :],
                                    mk_ref[s].astype(BF16), mv_ref[s].astype(BF16)))
    o_x = jnp.concatenate(oxs, axis=0)

    lb = _forget_lower_bound(lbl_ref[...])
    qf, k, log_f = _hgrn_gates(proj, lb)
    v = proj[:, OFF_I:OFF_I + HGRN_WIDTH]
    gg = proj[:, OFF_G:OFF_G + HGRN_WIDTH]
    gate = gg * jax.nn.sigmoid(gg) * onorm_ref[...]
    ridx = lax.broadcasted_iota(jnp.int32, (rows, HGRN_WIDTH), 0)
    a = log_f
    sh = 1
    while sh < sl_len:
        a = a + jnp.where((ridx % sl_len) >= sh, pltpu.roll(a, sh, 0), 0.0)
        sh *= 2
    a3 = a.reshape(gs, sl_len, HGRN_WIDTH)
    a_end = jnp.broadcast_to(a3[:, sl_len - 1:sl_len, :], a3.shape).reshape(rows, HGRN_WIDTH)
    q_in = (qf * jnp.exp(a)).astype(BF16)
    k_out = (k * jnp.exp(a_end - a)).astype(BF16)
    decay = jnp.exp(a_end)
    v_b = v.astype(BF16)
    qk = qf * k
    o_heads = []
    for hd in range(HGRN_HEADS):
        sl = slice(hd * HGRN_DK, (hd + 1) * HGRN_DK)
        p = _intra_scores(a, qf, k, log_f, hd, rows, sl_len)
        o = _dot(p.astype(BF16), v_b[:, sl])
        o = o + jnp.sum(qk[:, sl], axis=-1, keepdims=True) * v[:, sl]
        inter = []
        for s in range(gs):
            r0 = s * sl_len
            s0 = s0_ref[s, hd]
            inter.append(_dot(q_in[r0:r0 + sl_len, sl], s0.astype(BF16)))
            dec = decay[r0:r0 + sl_len, sl]
            only_last = lax.broadcasted_iota(jnp.int32, dec.shape, 0) == sl_len - 1
            dec = jnp.where(only_last, dec, 0.0)
            d_hi = dec.astype(BF16)
            r1 = dec - d_hi.astype(F32)
            d_mid = r1.astype(BF16)
            d_lo = (r1 - d_mid.astype(F32)).astype(BF16)
            ones = jnp.ones((sl_len, HGRN_DV), BF16)
            dmat = _dot_tn(d_hi, ones) + _dot_tn(d_mid, ones) + _dot_tn(d_lo, ones)
            ns_ref[s, hd] = dmat * s0 + _dot_tn(k_out[r0:r0 + sl_len, sl], v_b[r0:r0 + sl_len, sl])
        o = o + jnp.concatenate(inter, axis=0)
        o = o * lax.rsqrt(jnp.mean(o * o, axis=-1, keepdims=True) + EPS)
        o_heads.append(o * gate[:, sl])
    mixed = jnp.concatenate([o_pool] + o_heads + [o_x], axis=-1).astype(BF16)
    x2_ref[...] = (x + _dot(mixed, wout_ref[...])).reshape(gs, sl_len, D_MODEL)


def _sample_mixer_call(x, hist, s0, mk, mv, ln1, w_in, pool_wbd, pool_scale, lb_logits, onorm, w_out, gs):
    b, l, d = x.shape
    grid = (b // gs,)
    blk = pl.BlockSpec((gs, l, d), lambda i: (i, 0, 0))
    histb = pl.BlockSpec((gs, POOL_HIST, POOL_WIDTH), lambda i: (i, 0, 0))
    sb = pl.BlockSpec((gs, HGRN_HEADS, HGRN_DK, HGRN_DV), lambda i: (i, 0, 0, 0))
    mem = pl.BlockSpec((gs, N_MEM, XATTN_WIDTH), lambda i: (i, 0, 0))
    return pl.pallas_call(
        functools.partial(_sample_mixer_body, gs=gs, sl_len=l),
        grid=grid,
        in_specs=[blk, histb, sb, mem, mem,
                  _const_spec((1, d)), _const_spec((d, D_IN)), _const_spec((POOL_WIDTH, POOL_WIDTH)),
                  _const_spec((1, POOL_WIDTH)), _const_spec(lb_logits.shape), _const_spec((1, HGRN_WIDTH)),
                  _const_spec((d, d))],
        out_specs=[blk, histb, sb],
        out_shape=[jax.ShapeDtypeStruct((b, l, d), F32),
                   jax.ShapeDtypeStruct((b, POOL_HIST, POOL_WIDTH), F32),
                   jax.ShapeDtypeStruct((b, HGRN_HEADS, HGRN_DK, HGRN_DV), F32)],
        scratch_shapes=[pltpu.VMEM((POOL_PAD + l, POOL_WIDTH), F32)],
        compiler_params=pltpu.CompilerParams(dimension_semantics=("arbitrary",),
                                             vmem_limit_bytes=VMEM_LIMIT_BYTES),
        name="sample_mixer",
    )(x, hist, s0, mk, mv, ln1, w_in, pool_wbd, pool_scale, lb_logits, onorm, w_out)


def _sample_ffn_body(x_ref, chist_ref, ln2_ref, wup_ref, cw_ref, cb_ref, wdown_ref, lnf_ref, y_ref, nconv_ref,
                     *, gs, sl_len):
    rows = gs * sl_len
    x = x_ref[...].reshape(rows, D_MODEL)
    h = _rmsnorm(x, ln2_ref[...]).astype(BF16)
    ab = _dot(h, wup_ref[...])
    a = ab[:, :D_FF]
    ridx = lax.broadcasted_iota(jnp.int32, (rows, D_FF), 0) % sl_len
    hist = chist_ref[...]
    h1 = jnp.broadcast_to(hist[:, 1:2, :], (gs, sl_len, D_FF)).reshape(rows, D_FF)
    h0 = jnp.broadcast_to(hist[:, 0:1, :], (gs, sl_len, D_FF)).reshape(rows, D_FF)
    a_m1 = jnp.where(ridx >= 1, pltpu.roll(a, 1, 0), h1)
    a_m2 = jnp.where(ridx >= 2, pltpu.roll(a, 2, 0), jnp.where(ridx == 1, h1, h0))
    conv = cb_ref[...] + cw_ref[0:1, :] * a_m2 + cw_ref[1:2, :] * a_m1 + cw_ref[2:3, :] * a
    act = _gelu_tanh(conv) * ab[:, D_FF:]
    nconv_ref[...] = a.reshape(gs, sl_len, D_FF)[:, sl_len - (CONV_W - 1):, :]
    y_ref[...] = _ffn_tail(x, act, wdown_ref, lnf_ref).reshape(gs, sl_len, D_MODEL)


def _sample_ffn_call(x, chist, ln2, w_up, conv_w, conv_b, w_down, lnf, gs):
    b, l, d = x.shape
    blk = pl.BlockSpec((gs, l, d), lambda i: (i, 0, 0))
    cblk = pl.BlockSpec((gs, CONV_W - 1, D_FF), lambda i: (i, 0, 0))
    return pl.pallas_call(
        functools.partial(_sample_ffn_body, gs=gs, sl_len=l),
        grid=(b // gs,),
        in_specs=[blk, cblk, _const_spec((1, d)), _const_spec((d, 2 * D_FF)), _const_spec((CONV_W, D_FF)),
                  _const_spec((1, D_FF)), _const_spec((D_FF, d)), _const_spec((1, d))],
        out_specs=[blk, cblk],
        out_shape=[jax.ShapeDtypeStruct((b, l, d), F32),
                   jax.ShapeDtypeStruct((b, CONV_W - 1, D_FF), F32)],
        compiler_params=pltpu.CompilerParams(dimension_semantics=("arbitrary",),
                                             vmem_limit_bytes=VMEM_LIMIT_BYTES),
        name="sample_ffn",
    )(x, chist, ln2, w_up, conv_w, conv_b, w_down, lnf)


def _block_diag(pool_w):
    n = pool_w.shape[0]
    out = jnp.zeros((n * POOL_GROUP, n * POOL_GROUP), pool_w.dtype)
    for g in range(n):
        out = lax.dynamic_update_slice(out, pool_w[g], (g * POOL_GROUP, g * POOL_GROUP))
    return out


def _layer(x_prompt, x_sample, mem_prompt, state_pool, state_hgrn, state_conv, cache_mem_k, cache_mem_v,
           ln1_g, w_in, pool_w, pool_scale, hgrn_lb_logits, hgrn_onorm_g, mem_norm_g, w_mem_kv, w_out,
           ln2_g, w_up, conv_w, conv_b, w_down, lnf_g, *, prompt_tb, sample_gs):
    row = lambda a: a.reshape(1, -1)
    w_in_b, w_out_b, w_up_b, w_down_b = (w.astype(BF16) for w in (w_in, w_out, w_up, w_down))
    pool_wbd = _block_diag(pool_w).astype(BF16)
    mixer_w = (row(ln1_g), w_in_b, pool_wbd, row(pool_scale), hgrn_lb_logits, row(hgrn_onorm_g), w_out_b)
    ffn_w = (row(ln2_g), w_up_b, conv_w, row(conv_b), w_down_b, row(lnf_g))

    mk, mv = _memkv_call(mem_prompt, row(mem_norm_g), w_mem_kv.astype(BF16))
    xp, new_pool_p, new_s_p = _prompt_mixer_call(x_prompt, mk, mv, *mixer_w, tb=prompt_tb)
    y_prompt, new_conv_p = _prompt_ffn_call(xp, *ffn_w, tb=prompt_tb)

    nb = x_sample.shape[0]
    xs, new_pool_s, new_s_s = _sample_mixer_call(
        x_sample, state_pool, state_hgrn,
        cache_mem_k.reshape(nb, N_MEM, XATTN_WIDTH), cache_mem_v.reshape(nb, N_MEM, XATTN_WIDTH),
        *mixer_w, gs=sample_gs)
    y_sample, new_conv_s = _sample_ffn_call(xs, state_conv, *ffn_w, gs=sample_gs)
    bp = x_prompt.shape[0]
    heads = (bp, N_MEM, XATTN_HEADS, XATTN_DH)
    return (y_prompt, y_sample, new_pool_p, new_s_p, new_conv_p, mk.reshape(heads), mv.reshape(heads),
            new_pool_s, new_s_s, new_conv_s)


def kernel(x_prompt, x_sample, mem_prompt, state_pool, state_hgrn, state_conv, cache_mem_k, cache_mem_v,
           ln1_g, w_in, pool_w, pool_scale, hgrn_lb_logits, hgrn_onorm_g, mem_norm_g, w_mem_kv, w_out,
           ln2_g, w_up, conv_w, conv_b, w_down, lnf_g):
    assert w_in.shape[0] == 1, "one layer"
    outs = _layer(x_prompt, x_sample, mem_prompt, state_pool[0], state_hgrn[0], state_conv[0],
                  cache_mem_k[0], cache_mem_v[0], ln1_g[0], w_in[0], pool_w[0], pool_scale[0], hgrn_lb_logits,
                  hgrn_onorm_g[0], mem_norm_g[0], w_mem_kv[0], w_out[0], ln2_g[0], w_up[0], conv_w[0], conv_b[0],
                  w_down[0], lnf_g, prompt_tb=256, sample_gs=16)
    y_prompt, y_sample = outs[0], outs[1]
    return (y_prompt, y_sample) + tuple(o[None] for o in outs[2:])
```

```python
import functools

import jax
import jax.numpy as jnp
from jax import lax
from jax.experimental import pallas as pl
from jax.experimental.pallas import tpu as pltpu

F32 = jnp.float32
BF16 = jnp.bfloat16

D_MODEL = 1024
POOL_WIDTH = 256
POOL_GROUP = 64
POOL_HIST = 15
HGRN_WIDTH = 512
HGRN_HEADS = 4
HGRN_DK = 128
HGRN_DV = 128
XATTN_WIDTH = 256
XATTN_HEADS = 4
XATTN_DH = 64
N_MEM = 256
D_FF = 2816
CONV_W = 3
EPS = 1e-6
PAST_LEN = 16384
D_IN = POOL_WIDTH + 4 * HGRN_WIDTH + XATTN_WIDTH
OFF_U, OFF_Q, OFF_F, OFF_I, OFF_G, OFF_X = 0, 256, 768, 1280, 1792, 2304

CHUNK = 64
POOL_PAD = 16
CONV_PAD = 8
VMEM_LIMIT_BYTES = 56 * 1024 * 1024

_NT = (((1,), (1,)), ((), ()))
_TN = (((0,), (0,)), ((), ()))


def _dot(a, b):
    return jnp.dot(a, b, preferred_element_type=F32)


def _dot_nt(a, b):
    return lax.dot_general(a, b, _NT, preferred_element_type=F32)


def _dot_tn(a, b):
    return lax.dot_general(a, b, _TN, preferred_element_type=F32)


def _rmsnorm(x, g):
    return x * lax.rsqrt(jnp.mean(x * x, axis=-1, keepdims=True) + EPS) * g


def _const_spec(shape):
    nd = len(shape)
    return pl.BlockSpec(shape, lambda *_: (0,) * nd, pipeline_mode=pl.Buffered(1))


def _memkv_body(mem_ref, g_ref, w_ref, k_ref, v_ref):
    h = _rmsnorm(mem_ref[0], g_ref[...]).astype(BF16)
    kv = _dot(h, w_ref[...])
    k_ref[0] = kv[:, :XATTN_WIDTH]
    v_ref[0] = kv[:, XATTN_WIDTH:]


def _memkv_call(mem, g, w):
    b = mem.shape[0]
    out = jax.ShapeDtypeStruct((b, N_MEM, XATTN_WIDTH), F32)
    return pl.pallas_call(
        _memkv_body,
        grid=(b,),
        in_specs=[pl.BlockSpec((1, N_MEM, D_MODEL), lambda i: (i, 0, 0)),
                  _const_spec((1, D_MODEL)),
                  _const_spec((D_MODEL, 2 * XATTN_WIDTH))],
        out_specs=[pl.BlockSpec((1, N_MEM, XATTN_WIDTH), lambda i: (i, 0, 0))] * 2,
        out_shape=[out, out],
        compiler_params=pltpu.CompilerParams(dimension_semantics=("arbitrary",)),
        name="memkv",
    )(mem, g, w)


def _forget_lower_bound(logits):
    z = logits - jnp.max(logits, axis=0, keepdims=True)
    e = jnp.exp(z)
    return e[0:1, :] / jnp.sum(e, axis=0, keepdims=True)


def _hgrn_gates(proj, lb):
    fp = proj[:, OFF_F:OFF_F + HGRN_WIDTH]
    q = proj[:, OFF_Q:OFF_Q + HGRN_WIDTH]
    log_f = jnp.log(lb + (1.0 - lb) * jax.nn.sigmoid(fp))
    k = (1.0 - lb) * jax.nn.sigmoid(-fp)
    qf = q * jax.nn.sigmoid(q)
    return qf, k, log_f


def _segment_cumsum(x, seq):
    ridx = lax.broadcasted_iota(jnp.int32, x.shape, 0) & (seq - 1)
    sh = 1
    while sh < seq:
        x = x + jnp.where(ridx >= sh, pltpu.roll(x, sh, 0), 0.0)
        sh *= 2
    return x


def _level_factor(a, qf, k, log_f, m, rows):
    n = a.shape[1]
    ridx = lax.broadcasted_iota(jnp.int32, (rows, n), 0)
    upper = (ridx & m) != 0
    if m == 1:
        d = jnp.where(upper, log_f, 0.0)
    else:
        if (2 * m) % 8 == 0:
            nb = rows // (2 * m)
            a3 = a.reshape(nb, 2 * m, n)
            ref = jnp.broadcast_to(a3[:, m - 1:m, :], (nb, 2 * m, n)).reshape(rows, n)
        else:
            a3 = a.reshape(rows // 8, 8, n)
            sub = lax.broadcasted_iota(jnp.int32, a3.shape, 1)
            ref = jnp.where(sub < 4,
                            jnp.broadcast_to(a3[:, 1:2, :], a3.shape),
                            jnp.broadcast_to(a3[:, 5:6, :], a3.shape)).reshape(rows, n)
        d = -jnp.abs(a - ref)
    return (jnp.where(upper, qf, k) * jnp.exp(d)).astype(BF16)


def _level_factors(a, qf, k, log_f, rows, seq):
    out, m = [], seq // 2
    while m >= 1:
        out.append((m, _level_factor(a, qf, k, log_f, m, rows)))
        m //= 2
    return out


def _intra_scores(factors, head, rows, seq):
    sl = slice(head * HGRN_DK, (head + 1) * HGRN_DK)
    t = lax.broadcasted_iota(jnp.int32, (rows, rows), 0)
    s = lax.broadcasted_iota(jnp.int32, (rows, rows), 1)
    x = t ^ s
    total = jnp.zeros((rows, rows), F32)
    for m, y in reversed(factors):
        total = jnp.where(x >= m, _dot_nt(y[:, sl], y[:, sl]), total)
    return jnp.where((t > s) & (x < seq), total, 0.0)


def _head_norm_gate(o, gate):
    return o * lax.rsqrt(jnp.mean(o * o, axis=-1, keepdims=True) + EPS) * gate


def _cross_attention(qx, mk, mv):
    rows = qx.shape[0]
    head_of_lane = lax.broadcasted_iota(jnp.int32, qx.shape, 1) // XATTN_DH
    qs = jnp.concatenate([jnp.where(head_of_lane == h, qx, 0.0) for h in range(XATTN_HEADS)], axis=0)
    s = _dot_nt(qs.astype(BF16), mk)
    s = s - jnp.max(s, axis=-1, keepdims=True)
    e = jnp.exp(s)
    p = e / jnp.sum(e, axis=-1, keepdims=True)
    o = _dot(p.astype(BF16), mv)
    out = jnp.zeros(qx.shape, F32)
    for h in range(XATTN_HEADS):
        out = jnp.where(head_of_lane == h, o[h * rows:(h + 1) * rows, :], out)
    return out


def _pool_means(pbuf, rows, pos0):
    def ld(j, half):
        return pbuf[pl.ds(POOL_PAD - j, rows), pl.ds(128 * half, 128)]

    lane = lax.broadcasted_iota(jnp.int32, (rows, 128), 1)
    first = lane < POOL_GROUP
    posf = (pos0 + 1).astype(F32)
    u_lo, u_hi = ld(0, 0), ld(0, 1)
    t2 = u_lo + ld(1, 0)
    t4 = t2 + ld(2, 0) + ld(3, 0)
    t8 = u_hi
    for j in range(1, 8):
        t8 = t8 + ld(j, 1)
    t16 = t8
    for j in range(8, 16):
        t16 = t16 + ld(j, 1)
    cnt_lo = jnp.where(first, jnp.minimum(2.0, posf), jnp.minimum(4.0, posf))
    cnt_hi = jnp.where(first, jnp.minimum(8.0, posf), jnp.minimum(16.0, posf))
    lo = jnp.where(first, t2, t4) / cnt_lo - u_lo
    hi = jnp.where(first, t8, t16) / cnt_hi - u_hi
    return jnp.concatenate([lo, hi], axis=-1)


def _gelu_tanh(x):
    return 0.5 * x * (1.0 + jnp.tanh(0.7978845608028654 * (x + 0.044715 * (x * x * x))))


def _ffn_tail(x, act, wdown_ref, lnf_ref):
    y = x + _dot(act.astype(BF16), wdown_ref[...])
    return _rmsnorm(y, lnf_ref[...])


def _prompt_body(x_ref, mk_ref, mv_ref, ln1_ref, win_ref, poolw_ref, pscale_ref, lbl_ref, onorm_ref, wout_ref,
                 ln2_ref, wup_ref, cw_ref, cb_ref, wdown_ref, lnf_ref,
                 y_ref, npool_ref, ns_ref, nconv_ref,
                 pbuf, st, abuf, x2s, *, tb, nt, nblk):
    g = pl.program_id(0)
    jm = jnp.minimum(g, nblk - 1) % nt
    jf = jnp.maximum(g - 1, 0) % nt

    @pl.when(g == 0)
    def _():
        x2s[...] = jnp.zeros(x2s.shape, F32)

    @pl.when(jm == 0)
    def _():
        pbuf[pl.ds(0, POOL_PAD), :] = jnp.zeros((POOL_PAD, POOL_WIDTH), F32)
        st[...] = jnp.zeros(st.shape, F32)

    @pl.when(jf == 0)
    def _():
        abuf[pl.ds(0, CONV_PAD), :] = jnp.zeros((CONV_PAD, D_FF), F32)

    x2 = x2s[...]
    h2 = _rmsnorm(x2, ln2_ref[...]).astype(BF16)
    ab = _dot(h2, wup_ref[...])
    abuf[pl.ds(CONV_PAD, tb), :] = ab[:, :D_FF]
    conv = cb_ref[...]
    for t in range(CONV_W):
        conv = conv + cw_ref[t:t + 1, :] * abuf[pl.ds(CONV_PAD - (CONV_W - 1) + t, tb), :]
    act = _gelu_tanh(conv) * ab[:, D_FF:]
    abuf[pl.ds(0, CONV_PAD), :] = abuf[pl.ds(tb, CONV_PAD), :]
    y_ref[0] = _ffn_tail(x2, act, wdown_ref, lnf_ref)

    x = x_ref[0]
    h = _rmsnorm(x, ln1_ref[...]).astype(BF16)
    proj = _dot(h, win_ref[...])

    pbuf[pl.ds(POOL_PAD, tb), :] = proj[:, OFF_U:OFF_U + POOL_WIDTH]
    pos = jm * tb + lax.broadcasted_iota(jnp.int32, (tb, 1), 0)
    dm = _pool_means(pbuf, tb, pos)
    o_pool = _dot(dm.astype(BF16), poolw_ref[...]) * pscale_ref[...]
    pbuf[pl.ds(0, POOL_PAD), :] = pbuf[pl.ds(tb, POOL_PAD), :]

    qx = proj[:, OFF_X:OFF_X + XATTN_WIDTH] * (XATTN_DH ** -0.5)
    o_x = _cross_attention(qx, mk_ref[0].astype(BF16), mv_ref[0].astype(BF16))

    lb = _forget_lower_bound(lbl_ref[...])
    qf, k, log_f = _hgrn_gates(proj, lb)
    v = proj[:, OFF_I:OFF_I + HGRN_WIDTH]
    gg = proj[:, OFF_G:OFF_G + HGRN_WIDTH]
    gate = gg * jax.nn.sigmoid(gg) * onorm_ref[...]
    a_all = _segment_cumsum(log_f, CHUNK)
    states = [st[hd] for hd in range(HGRN_HEADS)]
    o_rows = []
    for c in range(tb // CHUNK):
        rs = slice(c * CHUNK, (c + 1) * CHUNK)
        qf_c, k_c, lf_c, v_c, a = qf[rs], k[rs], log_f[rs], v[rs], a_all[rs]
        a_end = a[CHUNK - 1:CHUNK, :]
        q_in = (qf_c * jnp.exp(a)).astype(BF16)
        k_out = (k_c * jnp.exp(a_end - a)).astype(BF16)
        decay = jnp.exp(a_end)
        v_b = v_c.astype(BF16)
        qk = qf_c * k_c
        factors = _level_factors(a, qf_c, k_c, lf_c, CHUNK, CHUNK)
        o_heads = []
        for hd in range(HGRN_HEADS):
            sl = slice(hd * HGRN_DK, (hd + 1) * HGRN_DK)
            p = _intra_scores(factors, hd, CHUNK, CHUNK)
            o = _dot(p.astype(BF16), v_b[:, sl])
            o = o + _dot_nt(q_in[:, sl], states[hd].astype(BF16))
            o = o + jnp.sum(qk[:, sl], axis=-1, keepdims=True) * v_c[:, sl]
            states[hd] = states[hd] * decay[:, sl] + _dot_tn(v_b[:, sl], k_out[:, sl])
            o_heads.append(_head_norm_gate(o, gate[rs, sl]))
        o_rows.append(jnp.concatenate(o_heads, axis=-1))
    for hd in range(HGRN_HEADS):
        st[hd] = states[hd]
    o_hgrn = jnp.concatenate(o_rows, axis=0)

    mixed = jnp.concatenate([o_pool, o_hgrn, o_x], axis=-1).astype(BF16)
    x2s[...] = x + _dot(mixed, wout_ref[...])

    @pl.when(jnp.logical_and(jm == nt - 1, g < nblk))
    def _():
        npool_ref[0] = pbuf[pl.ds(1, POOL_HIST), :]
        for hd in range(HGRN_HEADS):
            ns_ref[0, hd] = st[hd].T

    @pl.when(jnp.logical_and(jf == nt - 1, g >= 1))
    def _():
        nconv_ref[0] = abuf[pl.ds(CONV_PAD - (CONV_W - 1), CONV_W - 1), :]


def _prompt_call(x, mk, mv, mixer_w, ffn_w, tb):
    b, l, d = x.shape
    nt = l // tb
    nblk = b * nt
    ln1, w_in, pool_wbd, pool_scale, lb_logits, onorm, w_out = mixer_w
    ln2, w_up, conv_w, conv_b, w_down, lnf = ffn_w

    def mixer_blk(g):
        return jnp.minimum(g, nblk - 1)

    def ffn_blk(g):
        return jnp.maximum(g - 1, 0)

    x_spec = pl.BlockSpec((1, tb, d), lambda g: (mixer_blk(g) // nt, mixer_blk(g) % nt, 0))
    mem = pl.BlockSpec((1, N_MEM, XATTN_WIDTH), lambda g: (mixer_blk(g) // nt, 0, 0))
    y_spec = pl.BlockSpec((1, tb, d), lambda g: (ffn_blk(g) // nt, ffn_blk(g) % nt, 0))
    return pl.pallas_call(
        functools.partial(_prompt_body, tb=tb, nt=nt, nblk=nblk),
        grid=(nblk + 1,),
        in_specs=[x_spec, mem, mem,
                  _const_spec((1, d)), _const_spec((d, D_IN)), _const_spec((POOL_WIDTH, POOL_WIDTH)),
                  _const_spec((1, POOL_WIDTH)), _const_spec(lb_logits.shape), _const_spec((1, HGRN_WIDTH)),
                  _const_spec((d, d)),
                  _const_spec((1, d)), _const_spec((d, 2 * D_FF)), _const_spec((CONV_W, D_FF)),
                  _const_spec((1, D_FF)), _const_spec((D_FF, d)), _const_spec((1, d))],
        out_specs=[y_spec,
                   pl.BlockSpec((1, POOL_HIST, POOL_WIDTH), lambda g: (mixer_blk(g) // nt, 0, 0)),
                   pl.BlockSpec((1, HGRN_HEADS, HGRN_DK, HGRN_DV), lambda g: (mixer_blk(g) // nt, 0, 0, 0)),
                   pl.BlockSpec((1, CONV_W - 1, D_FF), lambda g: (ffn_blk(g) // nt, 0, 0))],
        out_shape=[jax.ShapeDtypeStruct((b, l, d), F32),
                   jax.ShapeDtypeStruct((b, POOL_HIST, POOL_WIDTH), F32),
                   jax.ShapeDtypeStruct((b, HGRN_HEADS, HGRN_DK, HGRN_DV), F32),
                   jax.ShapeDtypeStruct((b, CONV_W - 1, D_FF), F32)],
        scratch_shapes=[pltpu.VMEM((POOL_PAD + tb, POOL_WIDTH), F32),
                        pltpu.VMEM((HGRN_HEADS, HGRN_DV, HGRN_DK), F32),
                        pltpu.VMEM((CONV_PAD + tb, D_FF), F32),
                        pltpu.VMEM((tb, d), F32)],
        compiler_params=pltpu.CompilerParams(dimension_semantics=("arbitrary",),
                                             vmem_limit_bytes=VMEM_LIMIT_BYTES),
        name="prompt_layer",
    )(x, mk, mv, ln1, w_in, pool_wbd, pool_scale, lb_logits, onorm, w_out,
      ln2, w_up, conv_w, conv_b, w_down, lnf)


def _sample_mixer_body(x_ref, hist_ref, s0_ref, mk_ref, mv_ref, ln1_ref, win_ref, poolw_ref, pscale_ref, lbl_ref,
                       onorm_ref, wout_ref, x2_ref, npool_ref, ns_ref, pbuf, *, gs, sl_len):
    rows = gs * sl_len
    x = x_ref[...].reshape(rows, D_MODEL)
    h = _rmsnorm(x, ln1_ref[...]).astype(BF16)
    proj = _dot(h, win_ref[...])

    u = proj[:, OFF_U:OFF_U + POOL_WIDTH]
    pos = jnp.full((sl_len, 1), PAST_LEN, jnp.int32) + lax.broadcasted_iota(jnp.int32, (sl_len, 1), 0)
    dms = []
    for s in range(gs):
        pbuf[pl.ds(1, POOL_HIST), :] = hist_ref[s]
        pbuf[pl.ds(POOL_PAD, sl_len), :] = u[s * sl_len:(s + 1) * sl_len, :]
        dms.append(_pool_means(pbuf, sl_len, pos))
        npool_ref[s] = pbuf[pl.ds(sl_len + 1, POOL_HIST), :]
    dm = jnp.concatenate(dms, axis=0)
    o_pool = _dot(dm.astype(BF16), poolw_ref[...]) * pscale_ref[...]

    qx = proj[:, OFF_X:OFF_X + XATTN_WIDTH] * (XATTN_DH ** -0.5)
    oxs = []
    for s in range(gs):
        oxs.append(_cross_attention(qx[s * sl_len:(s + 1) * sl_len, :],
                                    mk_ref[s].astype(BF16), mv_ref[s].astype(BF16)))
    o_x = jnp.concatenate(oxs, axis=0)

    lb = _forget_lower_bound(lbl_ref[...])
    qf, k, log_f = _hgrn_gates(proj, lb)
    v = proj[:, OFF_I:OFF_I + HGRN_WIDTH]
    gg = proj[:, OFF_G:OFF_G + HGRN_WIDTH]
    gate = gg * jax.nn.sigmoid(gg) * onorm_ref[...]
    a = _segment_cumsum(log_f, sl_len)
    a3 = a.reshape(gs, sl_len, HGRN_WIDTH)
    a_end = jnp.broadcast_to(a3[:, sl_len - 1:sl_len, :], a3.shape).reshape(rows, HGRN_WIDTH)
    q_in = (qf * jnp.exp(a)).astype(BF16)
    k_out = (k * jnp.exp(a_end - a)).astype(BF16)
    decay = jnp.exp(a_end)
    v_b = v.astype(BF16)
    qk = qf * k
    factors = _level_factors(a, qf, k, log_f, rows, sl_len)
    o_heads = []
    for hd in range(HGRN_HEADS):
        sl = slice(hd * HGRN_DK, (hd + 1) * HGRN_DK)
        p = _intra_scores(factors, hd, rows, sl_len)
        o = _dot(p.astype(BF16), v_b[:, sl])
        o = o + jnp.sum(qk[:, sl], axis=-1, keepdims=True) * v[:, sl]
        inter = []
        for s in range(gs):
            r0 = s * sl_len
            s0 = s0_ref[s, hd]
            inter.append(_dot(q_in[r0:r0 + sl_len, sl], s0.astype(BF16)))
            dec = decay[r0:r0 + sl_len, sl]
            only_last = lax.broadcasted_iota(jnp.int32, dec.shape, 0) == sl_len - 1
            dec = jnp.where(only_last, dec, 0.0)
            d_hi = dec.astype(BF16)
            r1 = dec - d_hi.astype(F32)
            d_mid = r1.astype(BF16)
            d_lo = (r1 - d_mid.astype(F32)).astype(BF16)
            ones = jnp.ones((sl_len, HGRN_DV), BF16)
            dmat = _dot_tn(d_hi, ones) + _dot_tn(d_mid, ones) + _dot_tn(d_lo, ones)
            ns_ref[s, hd] = dmat * s0 + _dot_tn(k_out[r0:r0 + sl_len, sl], v_b[r0:r0 + sl_len, sl])
        o = o + jnp.concatenate(inter, axis=0)
        o_heads.append(_head_norm_gate(o, gate[:, sl]))
    mixed = jnp.concatenate([o_pool] + o_heads + [o_x], axis=-1).astype(BF16)
    x2_ref[...] = (x + _dot(mixed, wout_ref[...])).reshape(gs, sl_len, D_MODEL)


def _sample_mixer_call(x, hist, s0, mk, mv, mixer_w, gs):
    b, l, d = x.shape
    ln1, w_in, pool_wbd, pool_scale, lb_logits, onorm, w_out = mixer_w
    grid = (b // gs,)
    blk = pl.BlockSpec((gs, l, d), lambda i: (i, 0, 0))
    histb = pl.BlockSpec((gs, POOL_HIST, POOL_WIDTH), lambda i: (i, 0, 0))
    sb = pl.BlockSpec((gs, HGRN_HEADS, HGRN_DK, HGRN_DV), lambda i: (i, 0, 0, 0))
    mem = pl.BlockSpec((gs, N_MEM, XATTN_WIDTH), lambda i: (i, 0, 0))
    return pl.pallas_call(
        functools.partial(_sample_mixer_body, gs=gs, sl_len=l),
        grid=grid,
        in_specs=[blk, histb, sb, mem, mem,
                  _const_spec((1, d)), _const_spec((d, D_IN)), _const_spec((POOL_WIDTH, POOL_WIDTH)),
                  _const_spec((1, POOL_WIDTH)), _const_spec(lb_logits.shape), _const_spec((1, HGRN_WIDTH)),
                  _const_spec((d, d))],
        out_specs=[blk, histb, sb],
        out_shape=[jax.ShapeDtypeStruct((b, l, d), F32),
                   jax.ShapeDtypeStruct((b, POOL_HIST, POOL_WIDTH), F32),
                   jax.ShapeDtypeStruct((b, HGRN_HEADS, HGRN_DK, HGRN_DV), F32)],
        scratch_shapes=[pltpu.VMEM((POOL_PAD + l, POOL_WIDTH), F32)],
        compiler_params=pltpu.CompilerParams(dimension_semantics=("arbitrary",),
                                             vmem_limit_bytes=VMEM_LIMIT_BYTES),
        name="sample_mixer",
    )(x, hist, s0, mk, mv, ln1, w_in, pool_wbd, pool_scale, lb_logits, onorm, w_out)


def _sample_ffn_body(x_ref, chist_ref, ln2_ref, wup_ref, cw_ref, cb_ref, wdown_ref, lnf_ref, y_ref, nconv_ref,
                     *, gs, sl_len):
    rows = gs * sl_len
    x = x_ref[...].reshape(rows, D_MODEL)
    h = _rmsnorm(x, ln2_ref[...]).astype(BF16)
    ab = _dot(h, wup_ref[...])
    a = ab[:, :D_FF]
    ridx = lax.broadcasted_iota(jnp.int32, (rows, D_FF), 0) % sl_len
    hist = chist_ref[...]
    h1 = jnp.broadcast_to(hist[:, 1:2, :], (gs, sl_len, D_FF)).reshape(rows, D_FF)
    h0 = jnp.broadcast_to(hist[:, 0:1, :], (gs, sl_len, D_FF)).reshape(rows, D_FF)
    a_m1 = jnp.where(ridx >= 1, pltpu.roll(a, 1, 0), h1)
    a_m2 = jnp.where(ridx >= 2, pltpu.roll(a, 2, 0), jnp.where(ridx == 1, h1, h0))
    conv = cb_ref[...] + cw_ref[0:1, :] * a_m2 + cw_ref[1:2, :] * a_m1 + cw_ref[2:3, :] * a
    act = _gelu_tanh(conv) * ab[:, D_FF:]
    nconv_ref[...] = a.reshape(gs, sl_len, D_FF)[:, sl_len - (CONV_W - 1):, :]
    y_ref[...] = _ffn_tail(x, act, wdown_ref, lnf_ref).reshape(gs, sl_len, D_MODEL)


def _sample_ffn_call(x, chist, ffn_w, gs):
    b, l, d = x.shape
    ln2, w_up, conv_w, conv_b, w_down, lnf = ffn_w
    blk = pl.BlockSpec((gs, l, d), lambda i: (i, 0, 0))
    cblk = pl.BlockSpec((gs, CONV_W - 1, D_FF), lambda i: (i, 0, 0))
    return pl.pallas_call(
        functools.partial(_sample_ffn_body, gs=gs, sl_len=l),
        grid=(b // gs,),
        in_specs=[blk, cblk, _const_spec((1, d)), _const_spec((d, 2 * D_FF)), _const_spec((CONV_W, D_FF)),
                  _const_spec((1, D_FF)), _const_spec((D_FF, d)), _const_spec((1, d))],
        out_specs=[blk, cblk],
        out_shape=[jax.ShapeDtypeStruct((b, l, d), F32),
                   jax.ShapeDtypeStruct((b, CONV_W - 1, D_FF), F32)],
        compiler_params=pltpu.CompilerParams(dimension_semantics=("arbitrary",),
                                             vmem_limit_bytes=VMEM_LIMIT_BYTES),
        name="sample_ffn",
    )(x, chist, ln2, w_up, conv_w, conv_b, w_down, lnf)


def _block_diag(pool_w):
    n = pool_w.shape[0]
    out = jnp.zeros((n * POOL_GROUP, n * POOL_GROUP), pool_w.dtype)
    for g in range(n):
        out = lax.dynamic_update_slice(out, pool_w[g], (g * POOL_GROUP, g * POOL_GROUP))
    return out


def _layer(x_prompt, x_sample, mem_prompt, state_pool, state_hgrn, state_conv, cache_mem_k, cache_mem_v,
           ln1_g, w_in, pool_w, pool_scale, hgrn_lb_logits, hgrn_onorm_g, mem_norm_g, w_mem_kv, w_out,
           ln2_g, w_up, conv_w, conv_b, w_down, lnf_g, *, prompt_tb, sample_gs):
    row = lambda a: a.reshape(1, -1)
    w_in_b, w_out_b, w_up_b, w_down_b = (w.astype(BF16) for w in (w_in, w_out, w_up, w_down))
    pool_wbd = _block_diag(pool_w).astype(BF16)
    mixer_w = (row(ln1_g), w_in_b, pool_wbd, row(pool_scale), hgrn_lb_logits, row(hgrn_onorm_g), w_out_b)
    ffn_w = (row(ln2_g), w_up_b, conv_w, row(conv_b), w_down_b, row(lnf_g))

    mk, mv = _memkv_call(mem_prompt, row(mem_norm_g), w_mem_kv.astype(BF16))
    y_prompt, new_pool_p, new_s_p, new_conv_p = _prompt_call(x_prompt, mk, mv, mixer_w, ffn_w, tb=prompt_tb)

    nb = x_sample.shape[0]
    xs, new_pool_s, new_s_s = _sample_mixer_call(
        x_sample, state_pool, state_hgrn,
        cache_mem_k.reshape(nb, N_MEM, XATTN_WIDTH), cache_mem_v.reshape(nb, N_MEM, XATTN_WIDTH),
        mixer_w, gs=sample_gs)
    y_sample, new_conv_s = _sample_ffn_call(xs, state_conv, ffn_w, gs=sample_gs)
    bp = x_prompt.shape[0]
    heads = (bp, N_MEM, XATTN_HEADS, XATTN_DH)
    return (y_prompt, y_sample, new_pool_p, new_s_p, new_conv_p, mk.reshape(heads), mv.reshape(heads),
            new_pool_s, new_s_s, new_conv_s)


def kernel(x_prompt, x_sample, mem_prompt, state_pool, state_hgrn, state_conv, cache_mem_k, cache_mem_v,
           ln1_g, w_in, pool_w, pool_scale, hgrn_lb_logits, hgrn_onorm_g, mem_norm_g, w_mem_kv, w_out,
           ln2_g, w_up, conv_w, conv_b, w_down, lnf_g):
    assert w_in.shape[0] == 1, "one layer"
    outs = _layer(x_prompt, x_sample, mem_prompt, state_pool[0], state_hgrn[0], state_conv[0],
                  cache_mem_k[0], cache_mem_v[0], ln1_g[0], w_in[0], pool_w[0], pool_scale[0], hgrn_lb_logits,
                  hgrn_onorm_g[0], mem_norm_g[0], w_mem_kv[0], w_out[0], ln2_g[0], w_up[0], conv_w[0], conv_b[0],
                  w_down[0], lnf_g, prompt_tb=256, sample_gs=16)
    y_prompt, y_sample = outs[0], outs[1]
    return (y_prompt, y_sample) + tuple(o[None] for o in outs[2:])
```

```python
import functools

import jax
import jax.numpy as jnp
from jax import lax
from jax.experimental import pallas as pl
from jax.experimental.pallas import tpu as pltpu

F32 = jnp.float32
BF16 = jnp.bfloat16

D_MODEL = 1024
POOL_WIDTH = 256
POOL_GROUP = 64
POOL_HIST = 15
HGRN_WIDTH = 512
HGRN_HEADS = 4
HGRN_DK = 128
HGRN_DV = 128
XATTN_WIDTH = 256
XATTN_HEADS = 4
XATTN_DH = 64
N_MEM = 256
D_FF = 2816
CONV_W = 3
EPS = 1e-6
PAST_LEN = 16384
D_IN = POOL_WIDTH + 4 * HGRN_WIDTH + XATTN_WIDTH
OFF_U, OFF_Q, OFF_F, OFF_I, OFF_G, OFF_X = 0, 256, 768, 1280, 1792, 2304

CHUNK = 64
POOL_PAD = 16
CONV_PAD = 8
LANES = 128
VMEM_LIMIT_BYTES = 56 * 1024 * 1024

_NT = (((1,), (1,)), ((), ()))
_TN = (((0,), (0,)), ((), ()))


def _dot(a, b):
    return jnp.dot(a, b, preferred_element_type=F32)


def _dot_nt(a, b):
    return lax.dot_general(a, b, _NT, preferred_element_type=F32)


def _dot_tn(a, b):
    return lax.dot_general(a, b, _TN, preferred_element_type=F32)


def _rmsnorm(x, g):
    return x * lax.rsqrt(jnp.mean(x * x, axis=-1, keepdims=True) + EPS) * g


def _const_spec(shape):
    nd = len(shape)
    return pl.BlockSpec(shape, lambda *_: (0,) * nd, pipeline_mode=pl.Buffered(1))


def _memkv_body(mem_ref, g_ref, w_ref, k_ref, v_ref):
    h = _rmsnorm(mem_ref[0], g_ref[...]).astype(BF16)
    kv = _dot(h, w_ref[...])
    k_ref[0] = kv[:, :XATTN_WIDTH]
    v_ref[0] = kv[:, XATTN_WIDTH:]


def _memkv_call(mem, g, w):
    b = mem.shape[0]
    out = jax.ShapeDtypeStruct((b, N_MEM, XATTN_WIDTH), F32)
    return pl.pallas_call(
        _memkv_body,
        grid=(b,),
        in_specs=[pl.BlockSpec((1, N_MEM, D_MODEL), lambda i: (i, 0, 0)),
                  _const_spec((1, D_MODEL)),
                  _const_spec((D_MODEL, 2 * XATTN_WIDTH))],
        out_specs=[pl.BlockSpec((1, N_MEM, XATTN_WIDTH), lambda i: (i, 0, 0))] * 2,
        out_shape=[out, out],
        compiler_params=pltpu.CompilerParams(dimension_semantics=("arbitrary",)),
        name="memkv",
    )(mem, g, w)


def _forget_lower_bound(logits):
    z = logits - jnp.max(logits, axis=0, keepdims=True)
    e = jnp.exp(z)
    return e[0:1, :] / jnp.sum(e, axis=0, keepdims=True)


def _hgrn_gates(proj, lb):
    fp = proj[:, OFF_F:OFF_F + HGRN_WIDTH]
    q = proj[:, OFF_Q:OFF_Q + HGRN_WIDTH]
    log_f = jnp.log(lb + (1.0 - lb) * jax.nn.sigmoid(fp))
    k = (1.0 - lb) * jax.nn.sigmoid(-fp)
    qf = q * jax.nn.sigmoid(q)
    return qf, k, log_f


def _segment_cumsum(x, seq):
    ridx = lax.broadcasted_iota(jnp.int32, x.shape, 0) & (seq - 1)
    sh = 1
    while sh < seq:
        x = x + jnp.where(ridx >= sh, pltpu.roll(x, sh, 0), 0.0)
        sh *= 2
    return x


def _level_factor(a, qf, k, log_f, m, rows):
    n = a.shape[1]
    ridx = lax.broadcasted_iota(jnp.int32, (rows, n), 0)
    upper = (ridx & m) != 0
    if m == 1:
        d = jnp.where(upper, log_f, 0.0)
    else:
        if (2 * m) % 8 == 0:
            nb = rows // (2 * m)
            a3 = a.reshape(nb, 2 * m, n)
            ref = jnp.broadcast_to(a3[:, m - 1:m, :], (nb, 2 * m, n)).reshape(rows, n)
        else:
            a3 = a.reshape(rows // 8, 8, n)
            sub = lax.broadcasted_iota(jnp.int32, a3.shape, 1)
            ref = jnp.where(sub < 4,
                            jnp.broadcast_to(a3[:, 1:2, :], a3.shape),
                            jnp.broadcast_to(a3[:, 5:6, :], a3.shape)).reshape(rows, n)
        d = -jnp.abs(a - ref)
    return (jnp.where(upper, qf, k) * jnp.exp(d)).astype(BF16)


def _level_factors(a, qf, k, log_f, rows, seq):
    out, m = [], seq // 2
    while m >= 1:
        out.append((m, _level_factor(a, qf, k, log_f, m, rows)))
        m //= 2
    return out


def _intra_scores(factors, head, rows, seq):
    sl = slice(head * HGRN_DK, (head + 1) * HGRN_DK)
    t = lax.broadcasted_iota(jnp.int32, (rows, rows), 0)
    s = lax.broadcasted_iota(jnp.int32, (rows, rows), 1)
    x = t ^ s
    total = jnp.zeros((rows, rows), F32)
    for m, y in reversed(factors):
        total = jnp.where(x >= m, _dot_nt(y[:, sl], y[:, sl]), total)
    return jnp.where((t > s) & (x < seq), total, 0.0)


def _head_norm_gate(o, gate):
    return o * lax.rsqrt(jnp.mean(o * o, axis=-1, keepdims=True) + EPS) * gate


def _softmax_rows(s):
    e = jnp.exp(s - jnp.max(s, axis=-1, keepdims=True))
    return e / jnp.sum(e, axis=-1, keepdims=True)


def _cross_attention(qx, mk, mv):
    rows = qx.shape[0]
    head_of_lane = lax.broadcasted_iota(jnp.int32, qx.shape, 1) // XATTN_DH
    qs = jnp.concatenate([jnp.where(head_of_lane == h, qx, 0.0) for h in range(XATTN_HEADS)], axis=0)
    p = _softmax_rows(_dot_nt(qs.astype(BF16), mk))
    o = _dot(p.astype(BF16), mv)
    out = jnp.zeros(qx.shape, F32)
    for h in range(XATTN_HEADS):
        out = jnp.where(head_of_lane == h, o[h * rows:(h + 1) * rows, :], out)
    return out


def _pool_means(ld, posf, shape):
    lane = lax.broadcasted_iota(jnp.int32, shape, len(shape) - 1)
    first = lane < POOL_GROUP
    u_lo, u_hi = ld(0, 0), ld(0, 1)
    t2 = u_lo + ld(1, 0)
    t4 = t2 + ld(2, 0) + ld(3, 0)
    t8 = u_hi
    for j in range(1, 8):
        t8 = t8 + ld(j, 1)
    t16 = t8
    for j in range(8, 16):
        t16 = t16 + ld(j, 1)
    cnt_lo = jnp.where(first, jnp.minimum(2.0, posf), jnp.minimum(4.0, posf))
    cnt_hi = jnp.where(first, jnp.minimum(8.0, posf), jnp.minimum(16.0, posf))
    lo = jnp.where(first, t2, t4) / cnt_lo - u_lo
    hi = jnp.where(first, t8, t16) / cnt_hi - u_hi
    return jnp.concatenate([lo, hi], axis=-1)


def _gelu_tanh(x):
    return 0.5 * x * (1.0 + jnp.tanh(0.7978845608028654 * (x + 0.044715 * (x * x * x))))


def _ffn_tail(x, act, wdown_ref, lnf_ref):
    y = x + _dot(act.astype(BF16), wdown_ref[...])
    return _rmsnorm(y, lnf_ref[...])


def _prompt_body(x_ref, mk_ref, mv_ref, ln1_ref, win_ref, poolw_ref, pscale_ref, lbl_ref, onorm_ref, wout_ref,
                 ln2_ref, wup_ref, cw_ref, cb_ref, wdown_ref, lnf_ref,
                 y_ref, npool_ref, ns_ref, nconv_ref,
                 pbuf, st, abuf, x2s, *, tb, nt, nblk):
    g = pl.program_id(0)
    jm = jnp.minimum(g, nblk - 1) % nt
    jf = jnp.maximum(g - 1, 0) % nt

    @pl.when(g == 0)
    def _():
        x2s[...] = jnp.zeros(x2s.shape, F32)

    @pl.when(jm == 0)
    def _():
        pbuf[pl.ds(0, POOL_PAD), :] = jnp.zeros((POOL_PAD, POOL_WIDTH), F32)
        st[...] = jnp.zeros(st.shape, F32)

    @pl.when(jf == 0)
    def _():
        abuf[pl.ds(0, CONV_PAD), :] = jnp.zeros((CONV_PAD, D_FF), F32)

    x2 = x2s[...]
    h2 = _rmsnorm(x2, ln2_ref[...]).astype(BF16)
    ab = _dot(h2, wup_ref[...])
    abuf[pl.ds(CONV_PAD, tb), :] = ab[:, :D_FF]
    conv = cb_ref[...]
    for t in range(CONV_W):
        conv = conv + cw_ref[t:t + 1, :] * abuf[pl.ds(CONV_PAD - (CONV_W - 1) + t, tb), :]
    act = _gelu_tanh(conv) * ab[:, D_FF:]
    abuf[pl.ds(0, CONV_PAD), :] = abuf[pl.ds(tb, CONV_PAD), :]
    y_ref[0] = _ffn_tail(x2, act, wdown_ref, lnf_ref)

    x = x_ref[0]
    h = _rmsnorm(x, ln1_ref[...]).astype(BF16)
    proj = _dot(h, win_ref[...])

    pbuf[pl.ds(POOL_PAD, tb), :] = proj[:, OFF_U:OFF_U + POOL_WIDTH]
    posf = (jm * tb + 1 + lax.broadcasted_iota(jnp.int32, (tb, 1), 0)).astype(F32)
    dm = _pool_means(lambda j, half: pbuf[pl.ds(POOL_PAD - j, tb), pl.ds(LANES * half, LANES)], posf, (tb, LANES))
    o_pool = _dot(dm.astype(BF16), poolw_ref[...]) * pscale_ref[...]
    pbuf[pl.ds(0, POOL_PAD), :] = pbuf[pl.ds(tb, POOL_PAD), :]

    qx = proj[:, OFF_X:OFF_X + XATTN_WIDTH] * (XATTN_DH ** -0.5)
    o_x = _cross_attention(qx, mk_ref[0].astype(BF16), mv_ref[0].astype(BF16))

    lb = _forget_lower_bound(lbl_ref[...])
    qf, k, log_f = _hgrn_gates(proj, lb)
    v = proj[:, OFF_I:OFF_I + HGRN_WIDTH]
    gg = proj[:, OFF_G:OFF_G + HGRN_WIDTH]
    gate = gg * jax.nn.sigmoid(gg) * onorm_ref[...]
    a_all = _segment_cumsum(log_f, CHUNK)
    states = [st[hd] for hd in range(HGRN_HEADS)]
    o_rows = []
    for c in range(tb // CHUNK):
        rs = slice(c * CHUNK, (c + 1) * CHUNK)
        qf_c, k_c, lf_c, v_c, a = qf[rs], k[rs], log_f[rs], v[rs], a_all[rs]
        a_end = a[CHUNK - 1:CHUNK, :]
        q_in = (qf_c * jnp.exp(a)).astype(BF16)
        k_out = (k_c * jnp.exp(a_end - a)).astype(BF16)
        decay = jnp.exp(a_end)
        v_b = v_c.astype(BF16)
        qk = qf_c * k_c
        factors = _level_factors(a, qf_c, k_c, lf_c, CHUNK, CHUNK)
        o_heads = []
        for hd in range(HGRN_HEADS):
            sl = slice(hd * HGRN_DK, (hd + 1) * HGRN_DK)
            p = _intra_scores(factors, hd, CHUNK, CHUNK)
            o = _dot(p.astype(BF16), v_b[:, sl])
            o = o + _dot_nt(q_in[:, sl], states[hd].astype(BF16))
            o = o + jnp.sum(qk[:, sl], axis=-1, keepdims=True) * v_c[:, sl]
            states[hd] = states[hd] * decay[:, sl] + _dot_tn(v_b[:, sl], k_out[:, sl])
            o_heads.append(_head_norm_gate(o, gate[rs, sl]))
        o_rows.append(jnp.concatenate(o_heads, axis=-1))
    for hd in range(HGRN_HEADS):
        st[hd] = states[hd]
    o_hgrn = jnp.concatenate(o_rows, axis=0)

    mixed = jnp.concatenate([o_pool, o_hgrn, o_x], axis=-1).astype(BF16)
    x2s[...] = x + _dot(mixed, wout_ref[...])

    @pl.when(jnp.logical_and(jm == nt - 1, g < nblk))
    def _():
        npool_ref[0] = pbuf[pl.ds(1, POOL_HIST), :]
        for hd in range(HGRN_HEADS):
            ns_ref[0, hd] = st[hd].T

    @pl.when(jnp.logical_and(jf == nt - 1, g >= 1))
    def _():
        nconv_ref[0] = abuf[pl.ds(CONV_PAD - (CONV_W - 1), CONV_W - 1), :]


def _prompt_call(x, mk, mv, mixer_w, ffn_w, tb):
    b, l, d = x.shape
    nt = l // tb
    nblk = b * nt
    ln1, w_in, pool_wbd, pool_scale, lb_logits, onorm, w_out = mixer_w
    ln2, w_up, conv_w, conv_b, w_down, lnf = ffn_w

    def mixer_blk(g):
        return jnp.minimum(g, nblk - 1)

    def ffn_blk(g):
        return jnp.maximum(g - 1, 0)

    x_spec = pl.BlockSpec((1, tb, d), lambda g: (mixer_blk(g) // nt, mixer_blk(g) % nt, 0))
    mem = pl.BlockSpec((1, N_MEM, XATTN_WIDTH), lambda g: (mixer_blk(g) // nt, 0, 0))
    y_spec = pl.BlockSpec((1, tb, d), lambda g: (ffn_blk(g) // nt, ffn_blk(g) % nt, 0))
    return pl.pallas_call(
        functools.partial(_prompt_body, tb=tb, nt=nt, nblk=nblk),
        grid=(nblk + 1,),
        in_specs=[x_spec, mem, mem,
                  _const_spec((1, d)), _const_spec((d, D_IN)), _const_spec((POOL_WIDTH, POOL_WIDTH)),
                  _const_spec((1, POOL_WIDTH)), _const_spec(lb_logits.shape), _const_spec((1, HGRN_WIDTH)),
                  _const_spec((d, d)),
                  _const_spec((1, d)), _const_spec((d, 2 * D_FF)), _const_spec((CONV_W, D_FF)),
                  _const_spec((1, D_FF)), _const_spec((D_FF, d)), _const_spec((1, d))],
        out_specs=[y_spec,
                   pl.BlockSpec((1, POOL_HIST, POOL_WIDTH), lambda g: (mixer_blk(g) // nt, 0, 0)),
                   pl.BlockSpec((1, HGRN_HEADS, HGRN_DK, HGRN_DV), lambda g: (mixer_blk(g) // nt, 0, 0, 0)),
                   pl.BlockSpec((1, CONV_W - 1, D_FF), lambda g: (ffn_blk(g) // nt, 0, 0))],
        out_shape=[jax.ShapeDtypeStruct((b, l, d), F32),
                   jax.ShapeDtypeStruct((b, POOL_HIST, POOL_WIDTH), F32),
                   jax.ShapeDtypeStruct((b, HGRN_HEADS, HGRN_DK, HGRN_DV), F32),
                   jax.ShapeDtypeStruct((b, CONV_W - 1, D_FF), F32)],
        scratch_shapes=[pltpu.VMEM((POOL_PAD + tb, POOL_WIDTH), F32),
                        pltpu.VMEM((HGRN_HEADS, HGRN_DV, HGRN_DK), F32),
                        pltpu.VMEM((CONV_PAD + tb, D_FF), F32),
                        pltpu.VMEM((tb, d), F32)],
        compiler_params=pltpu.CompilerParams(dimension_semantics=("arbitrary",),
                                             vmem_limit_bytes=VMEM_LIMIT_BYTES),
        name="prompt_layer",
    )(x, mk, mv, ln1, w_in, pool_wbd, pool_scale, lb_logits, onorm, w_out,
      ln2, w_up, conv_w, conv_b, w_down, lnf)


def _sample_mixer_body(x_ref, hist_ref, s0_ref, mk_ref, mv_ref, ln1_ref, win_ref, poolw_ref, pscale_ref, lbl_ref,
                       onorm_ref, wout_ref, x2_ref, npool_ref, ns_ref, pbuf, *, gs, sl_len):
    rows = gs * sl_len
    seqs = [slice(s * sl_len, (s + 1) * sl_len) for s in range(gs)]
    x = x_ref[...].reshape(rows, D_MODEL)
    h = _rmsnorm(x, ln1_ref[...]).astype(BF16)
    proj = _dot(h, win_ref[...])

    pbuf[:, pl.ds(1, POOL_HIST), :] = hist_ref[...]
    pbuf[:, pl.ds(POOL_PAD, sl_len), :] = proj[:, OFF_U:OFF_U + POOL_WIDTH].reshape(gs, sl_len, POOL_WIDTH)
    posf = (PAST_LEN + 1 + lax.broadcasted_iota(jnp.int32, (1, sl_len, 1), 1)).astype(F32)
    dm = _pool_means(lambda j, half: pbuf[:, pl.ds(POOL_PAD - j, sl_len), pl.ds(LANES * half, LANES)],
                     posf, (gs, sl_len, LANES))
    npool_ref[...] = pbuf[:, pl.ds(sl_len + 1, POOL_HIST), :]
    o_pool = _dot(dm.reshape(rows, POOL_WIDTH).astype(BF16), poolw_ref[...]) * pscale_ref[...]

    qx3 = (proj[:, OFF_X:OFF_X + XATTN_WIDTH] * (XATTN_DH ** -0.5)).reshape(gs, sl_len, XATTN_WIDTH)
    head_of_lane = lax.broadcasted_iota(jnp.int32, qx3.shape, 2) // XATTN_DH
    qs3 = jnp.concatenate([jnp.where(head_of_lane == hd, qx3, 0.0) for hd in range(XATTN_HEADS)],
                          axis=1).astype(BF16)
    hrows = XATTN_HEADS * sl_len
    scores = jnp.concatenate([_dot_nt(qs3[s], mk_ref[s]) for s in range(gs)], axis=0)
    p = _softmax_rows(scores).astype(BF16)
    o4 = jnp.concatenate([_dot(p[s * hrows:(s + 1) * hrows], mv_ref[s]) for s in range(gs)], axis=0)
    o4 = o4.reshape(gs, XATTN_HEADS, sl_len, XATTN_WIDTH)
    o_x3 = jnp.zeros(qx3.shape, F32)
    for hd in range(XATTN_HEADS):
        o_x3 = jnp.where(head_of_lane == hd, o4[:, hd], o_x3)
    o_x = o_x3.reshape(rows, XATTN_WIDTH)

    lb = _forget_lower_bound(lbl_ref[...])
    qf, k, log_f = _hgrn_gates(proj, lb)
    v = proj[:, OFF_I:OFF_I + HGRN_WIDTH]
    gg = proj[:, OFF_G:OFF_G + HGRN_WIDTH]
    gate = gg * jax.nn.sigmoid(gg) * onorm_ref[...]
    a = _segment_cumsum(log_f, sl_len)
    a3 = a.reshape(gs, sl_len, HGRN_WIDTH)
    a_end = jnp.broadcast_to(a3[:, sl_len - 1:sl_len, :], a3.shape).reshape(rows, HGRN_WIDTH)
    q_in = (qf * jnp.exp(a)).astype(BF16)
    k_out = (k * jnp.exp(a_end - a)).astype(BF16)
    decay = jnp.exp(a_end)
    v_b = v.astype(BF16)
    qk = qf * k
    factors = _level_factors(a, qf, k, log_f, rows, sl_len)
    heads = [slice(hd * HGRN_DK, (hd + 1) * HGRN_DK) for hd in range(HGRN_HEADS)]
    inter = [jnp.concatenate([_dot(q_in[r, sl], s0_ref[s, hd].astype(BF16)) for s, r in enumerate(seqs)], axis=0)
             for hd, sl in enumerate(heads)]
    o_heads = []
    for hd, sl in enumerate(heads):
        p_h = _intra_scores(factors, hd, rows, sl_len)
        o = _dot(p_h.astype(BF16), v_b[:, sl]) + inter[hd]
        o = o + jnp.sum(qk[:, sl], axis=-1, keepdims=True) * v[:, sl]
        o_heads.append(_head_norm_gate(o, gate[:, sl]))
    updates = [[_dot_tn(k_out[r, sl], v_b[r, sl]) for sl in heads] for r in seqs]
    for s, r in enumerate(seqs):
        for hd, sl in enumerate(heads):
            decay_cols = jnp.broadcast_to(decay[r, sl][sl_len - 1:sl_len, :], (HGRN_DV, HGRN_DK)).T
            ns_ref[s, hd] = decay_cols * s0_ref[s, hd] + updates[s][hd]

    mixed = jnp.concatenate([o_pool] + o_heads + [o_x], axis=-1).astype(BF16)
    x2_ref[...] = (x + _dot(mixed, wout_ref[...])).reshape(gs, sl_len, D_MODEL)


def _sample_mixer_call(x, hist, s0, mk, mv, mixer_w, gs):
    b, l, d = x.shape
    ln1, w_in, pool_wbd, pool_scale, lb_logits, onorm, w_out = mixer_w
    grid = (b // gs,)
    blk = pl.BlockSpec((gs, l, d), lambda i: (i, 0, 0))
    histb = pl.BlockSpec((gs, POOL_HIST, POOL_WIDTH), lambda i: (i, 0, 0))
    sb = pl.BlockSpec((gs, HGRN_HEADS, HGRN_DK, HGRN_DV), lambda i: (i, 0, 0, 0))
    mem = pl.BlockSpec((gs, N_MEM, XATTN_WIDTH), lambda i: (i, 0, 0))
    return pl.pallas_call(
        functools.partial(_sample_mixer_body, gs=gs, sl_len=l),
        grid=grid,
        in_specs=[blk, histb, sb, mem, mem,
                  _const_spec((1, d)), _const_spec((d, D_IN)), _const_spec((POOL_WIDTH, POOL_WIDTH)),
                  _const_spec((1, POOL_WIDTH)), _const_spec(lb_logits.shape), _const_spec((1, HGRN_WIDTH)),
                  _const_spec((d, d))],
        out_specs=[blk, histb, sb],
        out_shape=[jax.ShapeDtypeStruct((b, l, d), F32),
                   jax.ShapeDtypeStruct((b, POOL_HIST, POOL_WIDTH), F32),
                   jax.ShapeDtypeStruct((b, HGRN_HEADS, HGRN_DK, HGRN_DV), F32)],
        scratch_shapes=[pltpu.VMEM((gs, POOL_PAD + l, POOL_WIDTH), F32)],
        compiler_params=pltpu.CompilerParams(dimension_semantics=("arbitrary",),
                                             vmem_limit_bytes=VMEM_LIMIT_BYTES),
        name="sample_mixer",
    )(x, hist, s0, mk, mv, ln1, w_in, pool_wbd, pool_scale, lb_logits, onorm, w_out)


def _sample_ffn_body(x_ref, chist_ref, ln2_ref, wup_ref, cw_ref, cb_ref, wdown_ref, lnf_ref, y_ref, nconv_ref,
                     *, gs, sl_len):
    rows = gs * sl_len
    x = x_ref[...].reshape(rows, D_MODEL)
    h = _rmsnorm(x, ln2_ref[...]).astype(BF16)
    ab = _dot(h, wup_ref[...])
    a = ab[:, :D_FF]
    ridx = lax.broadcasted_iota(jnp.int32, (rows, D_FF), 0) % sl_len
    hist = chist_ref[...]
    h1 = jnp.broadcast_to(hist[:, 1:2, :], (gs, sl_len, D_FF)).reshape(rows, D_FF)
    h0 = jnp.broadcast_to(hist[:, 0:1, :], (gs, sl_len, D_FF)).reshape(rows, D_FF)
    a_m1 = jnp.where(ridx >= 1, pltpu.roll(a, 1, 0), h1)
    a_m2 = jnp.where(ridx >= 2, pltpu.roll(a, 2, 0), jnp.where(ridx == 1, h1, h0))
    conv = cb_ref[...] + cw_ref[0:1, :] * a_m2 + cw_ref[1:2, :] * a_m1 + cw_ref[2:3, :] * a
    act = _gelu_tanh(conv) * ab[:, D_FF:]
    nconv_ref[...] = a.reshape(gs, sl_len, D_FF)[:, sl_len - (CONV_W - 1):, :]
    y_ref[...] = _ffn_tail(x, act, wdown_ref, lnf_ref).reshape(gs, sl_len, D_MODEL)


def _sample_ffn_call(x, chist, ffn_w, gs):
    b, l, d = x.shape
    ln2, w_up, conv_w, conv_b, w_down, lnf = ffn_w
    blk = pl.BlockSpec((gs, l, d), lambda i: (i, 0, 0))
    cblk = pl.BlockSpec((gs, CONV_W - 1, D_FF), lambda i: (i, 0, 0))
    return pl.pallas_call(
        functools.partial(_sample_ffn_body, gs=gs, sl_len=l),
        grid=(b // gs,),
        in_specs=[blk, cblk, _const_spec((1, d)), _const_spec((d, 2 * D_FF)), _const_spec((CONV_W, D_FF)),
                  _const_spec((1, D_FF)), _const_spec((D_FF, d)), _const_spec((1, d))],
        out_specs=[blk, cblk],
        out_shape=[jax.ShapeDtypeStruct((b, l, d), F32),
                   jax.ShapeDtypeStruct((b, CONV_W - 1, D_FF), F32)],
        compiler_params=pltpu.CompilerParams(dimension_semantics=("arbitrary",),
                                             vmem_limit_bytes=VMEM_LIMIT_BYTES),
        name="sample_ffn",
    )(x, chist, ln2, w_up, conv_w, conv_b, w_down, lnf)


def _block_diag(pool_w):
    n = pool_w.shape[0]
    out = jnp.zeros((n * POOL_GROUP, n * POOL_GROUP), pool_w.dtype)
    for g in range(n):
        out = lax.dynamic_update_slice(out, pool_w[g], (g * POOL_GROUP, g * POOL_GROUP))
    return out


def _layer(x_prompt, x_sample, mem_prompt, state_pool, state_hgrn, state_conv, cache_mem_k, cache_mem_v,
           ln1_g, w_in, pool_w, pool_scale, hgrn_lb_logits, hgrn_onorm_g, mem_norm_g, w_mem_kv, w_out,
           ln2_g, w_up, conv_w, conv_b, w_down, lnf_g, *, prompt_tb, sample_gs):
    row = lambda a: a.reshape(1, -1)
    w_in_b, w_out_b, w_up_b, w_down_b = (w.astype(BF16) for w in (w_in, w_out, w_up, w_down))
    pool_wbd = _block_diag(pool_w).astype(BF16)
    mixer_w = (row(ln1_g), w_in_b, pool_wbd, row(pool_scale), hgrn_lb_logits, row(hgrn_onorm_g), w_out_b)
    ffn_w = (row(ln2_g), w_up_b, conv_w, row(conv_b), w_down_b, row(lnf_g))

    mk, mv = _memkv_call(mem_prompt, row(mem_norm_g), w_mem_kv.astype(BF16))
    y_prompt, new_pool_p, new_s_p, new_conv_p = _prompt_call(x_prompt, mk, mv, mixer_w, ffn_w, tb=prompt_tb)

    nb = x_sample.shape[0]
    smk = cache_mem_k.reshape(nb, N_MEM, XATTN_WIDTH).astype(BF16)
    smv = cache_mem_v.reshape(nb, N_MEM, XATTN_WIDTH).astype(BF16)
    xs, new_pool_s, new_s_s = _sample_mixer_call(x_sample, state_pool, state_hgrn, smk, smv, mixer_w, gs=sample_gs)
    y_sample, new_conv_s = _sample_ffn_call(xs, state_conv, ffn_w, gs=sample_gs)
    bp = x_prompt.shape[0]
    heads = (bp, N_MEM, XATTN_HEADS, XATTN_DH)
    return (y_prompt, y_sample, new_pool_p, new_s_p, new_conv_p, mk.reshape(heads), mv.reshape(heads),
            new_pool_s, new_s_s, new_conv_s)


def kernel(x_prompt, x_sample, mem_prompt, state_pool, state_hgrn, state_conv, cache_mem_k, cache_mem_v,
           ln1_g, w_in, pool_w, pool_scale, hgrn_lb_logits, hgrn_onorm_g, mem_norm_g, w_mem_kv, w_out,
           ln2_g, w_up, conv_w, conv_b, w_down, lnf_g):
    assert w_in.shape[0] == 1, "one layer"
    outs = _layer(x_prompt, x_sample, mem_prompt, state_pool[0], state_hgrn[0], state_conv[0],
                  cache_mem_k[0], cache_mem_v[0], ln1_g[0], w_in[0], pool_w[0], pool_scale[0], hgrn_lb_logits,
                  hgrn_onorm_g[0], mem_norm_g[0], w_mem_kv[0], w_out[0], ln2_g[0], w_up[0], conv_w[0], conv_b[0],
                  w_down[0], lnf_g, prompt_tb=256, sample_gs=16)
    y_prompt, y_sample = outs[0], outs[1]
    return (y_prompt, y_sample) + tuple(o[None] for o in outs[2:])
```

```python
import functools

import jax
import jax.numpy as jnp
from jax import lax
from jax.experimental import pallas as pl
from jax.experimental.pallas import tpu as pltpu

F32 = jnp.float32
BF16 = jnp.bfloat16

D_MODEL = 1024
POOL_WIDTH = 256
POOL_GROUP = 64
POOL_HIST = 15
HGRN_WIDTH = 512
HGRN_HEADS = 4
HGRN_DK = 128
HGRN_DV = 128
XATTN_WIDTH = 256
XATTN_HEADS = 4
XATTN_DH = 64
N_MEM = 256
D_FF = 2816
CONV_W = 3
EPS = 1e-6
PAST_LEN = 16384
D_IN = POOL_WIDTH + 4 * HGRN_WIDTH + XATTN_WIDTH
OFF_U, OFF_Q, OFF_F, OFF_I, OFF_G, OFF_X = 0, 256, 768, 1280, 1792, 2304

CHUNK = 64
POOL_PAD = 16
CONV_PAD = 8
LANES = 128
VMEM_LIMIT_BYTES = 56 * 1024 * 1024

_NT = (((1,), (1,)), ((), ()))
_TN = (((0,), (0,)), ((), ()))


def _dot(a, b):
    return jnp.dot(a, b, preferred_element_type=F32)


def _dot_nt(a, b):
    return lax.dot_general(a, b, _NT, preferred_element_type=F32)


def _dot_tn(a, b):
    return lax.dot_general(a, b, _TN, preferred_element_type=F32)


def _rmsnorm(x, g):
    return x * lax.rsqrt(jnp.mean(x * x, axis=-1, keepdims=True) + EPS) * g


def _const_spec(shape):
    nd = len(shape)
    return pl.BlockSpec(shape, lambda *_: (0,) * nd, pipeline_mode=pl.Buffered(1))


def _memkv_body(mem_ref, g_ref, w_ref, k_ref, v_ref):
    h = _rmsnorm(mem_ref[0], g_ref[...]).astype(BF16)
    kv = _dot(h, w_ref[...])
    k_ref[0] = kv[:, :XATTN_WIDTH]
    v_ref[0] = kv[:, XATTN_WIDTH:]


def _memkv_call(mem, g, w):
    b = mem.shape[0]
    out = jax.ShapeDtypeStruct((b, N_MEM, XATTN_WIDTH), F32)
    return pl.pallas_call(
        _memkv_body,
        grid=(b,),
        in_specs=[pl.BlockSpec((1, N_MEM, D_MODEL), lambda i: (i, 0, 0)),
                  _const_spec((1, D_MODEL)),
                  _const_spec((D_MODEL, 2 * XATTN_WIDTH))],
        out_specs=[pl.BlockSpec((1, N_MEM, XATTN_WIDTH), lambda i: (i, 0, 0))] * 2,
        out_shape=[out, out],
        compiler_params=pltpu.CompilerParams(dimension_semantics=("arbitrary",)),
        name="memkv",
    )(mem, g, w)


def _forget_lower_bound(logits):
    z = logits - jnp.max(logits, axis=0, keepdims=True)
    e = jnp.exp(z)
    return e[0:1, :] / jnp.sum(e, axis=0, keepdims=True)


def _hgrn_gates(proj, lb):
    fp = proj[:, OFF_F:OFF_F + HGRN_WIDTH]
    q = proj[:, OFF_Q:OFF_Q + HGRN_WIDTH]
    log_f = jnp.log(lb + (1.0 - lb) * jax.nn.sigmoid(fp))
    k = (1.0 - lb) * jax.nn.sigmoid(-fp)
    qf = q * jax.nn.sigmoid(q)
    return qf, k, log_f


def _segment_cumsum(x, seq):
    ridx = lax.broadcasted_iota(jnp.int32, x.shape, 0) & (seq - 1)
    sh = 1
    while sh < seq:
        x = x + jnp.where(ridx >= sh, pltpu.roll(x, sh, 0), 0.0)
        sh *= 2
    return x


def _level_factor(a, qf, k, log_f, m, rows):
    n = a.shape[1]
    ridx = lax.broadcasted_iota(jnp.int32, (rows, n), 0)
    upper = (ridx & m) != 0
    if m == 1:
        d = jnp.where(upper, log_f, 0.0)
    else:
        if (2 * m) % 8 == 0:
            nb = rows // (2 * m)
            a3 = a.reshape(nb, 2 * m, n)
            ref = jnp.broadcast_to(a3[:, m - 1:m, :], (nb, 2 * m, n)).reshape(rows, n)
        else:
            a3 = a.reshape(rows // 8, 8, n)
            sub = lax.broadcasted_iota(jnp.int32, a3.shape, 1)
            ref = jnp.where(sub < 4,
                            jnp.broadcast_to(a3[:, 1:2, :], a3.shape),
                            jnp.broadcast_to(a3[:, 5:6, :], a3.shape)).reshape(rows, n)
        d = -jnp.abs(a - ref)
    return (jnp.where(upper, qf, k) * jnp.exp(d)).astype(BF16)


def _level_factors(a, qf, k, log_f, rows, seq):
    out, m = [], seq // 2
    while m >= 1:
        out.append((m, _level_factor(a, qf, k, log_f, m, rows)))
        m //= 2
    return out


def _intra_scores(factors, head, rows, seq):
    sl = slice(head * HGRN_DK, (head + 1) * HGRN_DK)
    t = lax.broadcasted_iota(jnp.int32, (rows, rows), 0)
    s = lax.broadcasted_iota(jnp.int32, (rows, rows), 1)
    x = t ^ s
    products = [(m, _dot_nt(y[:, sl], y[:, sl])) for m, y in reversed(factors)]
    total = jnp.zeros((rows, rows), F32)
    for m, p in products:
        total = jnp.where(x >= m, p, total)
    return jnp.where((t > s) & (x < seq), total, 0.0)


def _head_norm_gate(o, gate):
    return o * lax.rsqrt(jnp.mean(o * o, axis=-1, keepdims=True) + EPS) * gate


def _softmax_rows(s):
    e = jnp.exp(s - jnp.max(s, axis=-1, keepdims=True))
    return e / jnp.sum(e, axis=-1, keepdims=True)


def _cross_attention(qx, mk, mv):
    rows = qx.shape[0]
    head_of_lane = lax.broadcasted_iota(jnp.int32, qx.shape, 1) // XATTN_DH
    qs = jnp.concatenate([jnp.where(head_of_lane == h, qx, 0.0) for h in range(XATTN_HEADS)], axis=0)
    p = _softmax_rows(_dot_nt(qs.astype(BF16), mk))
    o = _dot(p.astype(BF16), mv)
    out = jnp.zeros(qx.shape, F32)
    for h in range(XATTN_HEADS):
        out = jnp.where(head_of_lane == h, o[h * rows:(h + 1) * rows, :], out)
    return out


def _pool_means(ld, posf, shape):
    lane = lax.broadcasted_iota(jnp.int32, shape, len(shape) - 1)
    first = lane < POOL_GROUP
    u_lo, u_hi = ld(0, 0), ld(0, 1)
    t2 = u_lo + ld(1, 0)
    t4 = t2 + ld(2, 0) + ld(3, 0)
    t8 = u_hi
    for j in range(1, 8):
        t8 = t8 + ld(j, 1)
    t16 = t8
    for j in range(8, 16):
        t16 = t16 + ld(j, 1)
    cnt_lo = jnp.where(first, jnp.minimum(2.0, posf), jnp.minimum(4.0, posf))
    cnt_hi = jnp.where(first, jnp.minimum(8.0, posf), jnp.minimum(16.0, posf))
    lo = jnp.where(first, t2, t4) / cnt_lo - u_lo
    hi = jnp.where(first, t8, t16) / cnt_hi - u_hi
    return jnp.concatenate([lo, hi], axis=-1)


def _gelu_tanh(x):
    return 0.5 * x * (1.0 + jnp.tanh(0.7978845608028654 * (x + 0.044715 * (x * x * x))))


def _ffn_tail(x, act, wdown_ref, lnf_ref):
    y = x + _dot(act.astype(BF16), wdown_ref[...])
    return _rmsnorm(y, lnf_ref[...])


def _prompt_body(x_ref, mk_ref, mv_ref, ln1_ref, win_ref, poolw_ref, pscale_ref, lbl_ref, onorm_ref, wout_ref,
                 ln2_ref, wup_ref, cw_ref, cb_ref, wdown_ref, lnf_ref,
                 y_ref, npool_ref, ns_ref, nconv_ref,
                 pbuf, st, abuf, x2s, *, tb, nt, nblk):
    g = pl.program_id(0)
    jm = jnp.minimum(g, nblk - 1) % nt
    jf = jnp.maximum(g - 1, 0) % nt

    @pl.when(g == 0)
    def _():
        x2s[...] = jnp.zeros(x2s.shape, F32)

    @pl.when(jm == 0)
    def _():
        pbuf[pl.ds(0, POOL_PAD), :] = jnp.zeros((POOL_PAD, POOL_WIDTH), F32)
        st[...] = jnp.zeros(st.shape, F32)

    @pl.when(jf == 0)
    def _():
        abuf[pl.ds(0, CONV_PAD), :] = jnp.zeros((CONV_PAD, D_FF), F32)

    x2 = x2s[...]
    h2 = _rmsnorm(x2, ln2_ref[...]).astype(BF16)
    ab = _dot(h2, wup_ref[...])
    abuf[pl.ds(CONV_PAD, tb), :] = ab[:, :D_FF]
    conv = cb_ref[...]
    for t in range(CONV_W):
        conv = conv + cw_ref[t:t + 1, :] * abuf[pl.ds(CONV_PAD - (CONV_W - 1) + t, tb), :]
    act = _gelu_tanh(conv) * ab[:, D_FF:]
    abuf[pl.ds(0, CONV_PAD), :] = abuf[pl.ds(tb, CONV_PAD), :]
    y_ref[0] = _ffn_tail(x2, act, wdown_ref, lnf_ref)

    x = x_ref[0]
    h = _rmsnorm(x, ln1_ref[...]).astype(BF16)
    proj = _dot(h, win_ref[...])

    pbuf[pl.ds(POOL_PAD, tb), :] = proj[:, OFF_U:OFF_U + POOL_WIDTH]
    posf = (jm * tb + 1 + lax.broadcasted_iota(jnp.int32, (tb, 1), 0)).astype(F32)
    dm = _pool_means(lambda j, half: pbuf[pl.ds(POOL_PAD - j, tb), pl.ds(LANES * half, LANES)], posf, (tb, LANES))
    o_pool = _dot(dm.astype(BF16), poolw_ref[...]) * pscale_ref[...]
    pbuf[pl.ds(0, POOL_PAD), :] = pbuf[pl.ds(tb, POOL_PAD), :]

    qx = proj[:, OFF_X:OFF_X + XATTN_WIDTH] * (XATTN_DH ** -0.5)
    o_x = _cross_attention(qx, mk_ref[0].astype(BF16), mv_ref[0].astype(BF16))

    lb = _forget_lower_bound(lbl_ref[...])
    qf, k, log_f = _hgrn_gates(proj, lb)
    v = proj[:, OFF_I:OFF_I + HGRN_WIDTH]
    gg = proj[:, OFF_G:OFF_G + HGRN_WIDTH]
    gate = gg * jax.nn.sigmoid(gg) * onorm_ref[...]
    a_all = _segment_cumsum(log_f, CHUNK)
    states = [st[hd] for hd in range(HGRN_HEADS)]
    o_rows = []
    for c in range(tb // CHUNK):
        rs = slice(c * CHUNK, (c + 1) * CHUNK)
        qf_c, k_c, lf_c, v_c, a = qf[rs], k[rs], log_f[rs], v[rs], a_all[rs]
        a_end = a[CHUNK - 1:CHUNK, :]
        q_in = (qf_c * jnp.exp(a)).astype(BF16)
        k_out = (k_c * jnp.exp(a_end - a)).astype(BF16)
        decay = jnp.exp(a_end)
        v_b = v_c.astype(BF16)
        qk = qf_c * k_c
        factors = _level_factors(a, qf_c, k_c, lf_c, CHUNK, CHUNK)
        heads = [slice(hd * HGRN_DK, (hd + 1) * HGRN_DK) for hd in range(HGRN_HEADS)]
        inter = [_dot_nt(q_in[:, sl], states[hd].astype(BF16)) for hd, sl in enumerate(heads)]
        update = [_dot_tn(v_b[:, sl], k_out[:, sl]) for sl in heads]
        scores = [_intra_scores(factors, hd, CHUNK, CHUNK).astype(BF16) for hd in range(HGRN_HEADS)]
        intra = [_dot(scores[hd], v_b[:, sl]) for hd, sl in enumerate(heads)]
        o_heads = []
        for hd, sl in enumerate(heads):
            o = intra[hd] + inter[hd] + jnp.sum(qk[:, sl], axis=-1, keepdims=True) * v_c[:, sl]
            states[hd] = states[hd] * decay[:, sl] + update[hd]
            o_heads.append(_head_norm_gate(o, gate[rs, sl]))
        o_rows.append(jnp.concatenate(o_heads, axis=-1))
    for hd in range(HGRN_HEADS):
        st[hd] = states[hd]
    o_hgrn = jnp.concatenate(o_rows, axis=0)

    mixed = jnp.concatenate([o_pool, o_hgrn, o_x], axis=-1).astype(BF16)
    x2s[...] = x + _dot(mixed, wout_ref[...])

    @pl.when(jnp.logical_and(jm == nt - 1, g < nblk))
    def _():
        npool_ref[0] = pbuf[pl.ds(1, POOL_HIST), :]
        for hd in range(HGRN_HEADS):
            ns_ref[0, hd] = st[hd].T

    @pl.when(jnp.logical_and(jf == nt - 1, g >= 1))
    def _():
        nconv_ref[0] = abuf[pl.ds(CONV_PAD - (CONV_W - 1), CONV_W - 1), :]


def _prompt_call(x, mk, mv, mixer_w, ffn_w, tb):
    b, l, d = x.shape
    nt = l // tb
    nblk = b * nt
    ln1, w_in, pool_wbd, pool_scale, lb_logits, onorm, w_out = mixer_w
    ln2, w_up, conv_w, conv_b, w_down, lnf = ffn_w

    def mixer_blk(g):
        return jnp.minimum(g, nblk - 1)

    def ffn_blk(g):
        return jnp.maximum(g - 1, 0)

    x_spec = pl.BlockSpec((1, tb, d), lambda g: (mixer_blk(g) // nt, mixer_blk(g) % nt, 0))
    mem = pl.BlockSpec((1, N_MEM, XATTN_WIDTH), lambda g: (mixer_blk(g) // nt, 0, 0))
    y_spec = pl.BlockSpec((1, tb, d), lambda g: (ffn_blk(g) // nt, ffn_blk(g) % nt, 0))
    return pl.pallas_call(
        functools.partial(_prompt_body, tb=tb, nt=nt, nblk=nblk),
        grid=(nblk + 1,),
        in_specs=[x_spec, mem, mem,
                  _const_spec((1, d)), _const_spec((d, D_IN)), _const_spec((POOL_WIDTH, POOL_WIDTH)),
                  _const_spec((1, POOL_WIDTH)), _const_spec(lb_logits.shape), _const_spec((1, HGRN_WIDTH)),
                  _const_spec((d, d)),
                  _const_spec((1, d)), _const_spec((d, 2 * D_FF)), _const_spec((CONV_W, D_FF)),
                  _const_spec((1, D_FF)), _const_spec((D_FF, d)), _const_spec((1, d))],
        out_specs=[y_spec,
                   pl.BlockSpec((1, POOL_HIST, POOL_WIDTH), lambda g: (mixer_blk(g) // nt, 0, 0)),
                   pl.BlockSpec((1, HGRN_HEADS, HGRN_DK, HGRN_DV), lambda g: (mixer_blk(g) // nt, 0, 0, 0)),
                   pl.BlockSpec((1, CONV_W - 1, D_FF), lambda g: (ffn_blk(g) // nt, 0, 0))],
        out_shape=[jax.ShapeDtypeStruct((b, l, d), F32),
                   jax.ShapeDtypeStruct((b, POOL_HIST, POOL_WIDTH), F32),
                   jax.ShapeDtypeStruct((b, HGRN_HEADS, HGRN_DK, HGRN_DV), F32),
                   jax.ShapeDtypeStruct((b, CONV_W - 1, D_FF), F32)],
        scratch_shapes=[pltpu.VMEM((POOL_PAD + tb, POOL_WIDTH), F32),
                        pltpu.VMEM((HGRN_HEADS, HGRN_DV, HGRN_DK), F32),
                        pltpu.VMEM((CONV_PAD + tb, D_FF), F32),
                        pltpu.VMEM((tb, d), F32)],
        compiler_params=pltpu.CompilerParams(dimension_semantics=("arbitrary",),
                                             vmem_limit_bytes=VMEM_LIMIT_BYTES),
        name="prompt_layer",
    )(x, mk, mv, ln1, w_in, pool_wbd, pool_scale, lb_logits, onorm, w_out,
      ln2, w_up, conv_w, conv_b, w_down, lnf)


def _sample_mixer_body(x_ref, hist_ref, s0_ref, mk_ref, mv_ref, ln1_ref, win_ref, poolw_ref, pscale_ref, lbl_ref,
                       onorm_ref, wout_ref, x2_ref, npool_ref, ns_ref, pbuf, *, gs, sl_len):
    rows = gs * sl_len
    seqs = [slice(s * sl_len, (s + 1) * sl_len) for s in range(gs)]
    x = x_ref[...].reshape(rows, D_MODEL)
    h = _rmsnorm(x, ln1_ref[...]).astype(BF16)
    proj = _dot(h, win_ref[...])

    pbuf[:, pl.ds(1, POOL_HIST), :] = hist_ref[...]
    pbuf[:, pl.ds(POOL_PAD, sl_len), :] = proj[:, OFF_U:OFF_U + POOL_WIDTH].reshape(gs, sl_len, POOL_WIDTH)
    posf = (PAST_LEN + 1 + lax.broadcasted_iota(jnp.int32, (1, sl_len, 1), 1)).astype(F32)
    dm = _pool_means(lambda j, half: pbuf[:, pl.ds(POOL_PAD - j, sl_len), pl.ds(LANES * half, LANES)],
                     posf, (gs, sl_len, LANES))
    npool_ref[...] = pbuf[:, pl.ds(sl_len + 1, POOL_HIST), :]
    o_pool = _dot(dm.reshape(rows, POOL_WIDTH).astype(BF16), poolw_ref[...]) * pscale_ref[...]

    qx3 = (proj[:, OFF_X:OFF_X + XATTN_WIDTH] * (XATTN_DH ** -0.5)).reshape(gs, sl_len, XATTN_WIDTH)
    head_of_lane = lax.broadcasted_iota(jnp.int32, qx3.shape, 2) // XATTN_DH
    qs3 = jnp.concatenate([jnp.where(head_of_lane == hd, qx3, 0.0) for hd in range(XATTN_HEADS)],
                          axis=1).astype(BF16)
    hrows = XATTN_HEADS * sl_len
    scores = jnp.concatenate([_dot_nt(qs3[s], mk_ref[s]) for s in range(gs)], axis=0)
    p = _softmax_rows(scores).astype(BF16)
    o4 = jnp.concatenate([_dot(p[s * hrows:(s + 1) * hrows], mv_ref[s]) for s in range(gs)], axis=0)
    o4 = o4.reshape(gs, XATTN_HEADS, sl_len, XATTN_WIDTH)
    o_x3 = jnp.zeros(qx3.shape, F32)
    for hd in range(XATTN_HEADS):
        o_x3 = jnp.where(head_of_lane == hd, o4[:, hd], o_x3)
    o_x = o_x3.reshape(rows, XATTN_WIDTH)

    lb = _forget_lower_bound(lbl_ref[...])
    qf, k, log_f = _hgrn_gates(proj, lb)
    v = proj[:, OFF_I:OFF_I + HGRN_WIDTH]
    gg = proj[:, OFF_G:OFF_G + HGRN_WIDTH]
    gate = gg * jax.nn.sigmoid(gg) * onorm_ref[...]
    a = _segment_cumsum(log_f, sl_len)
    a3 = a.reshape(gs, sl_len, HGRN_WIDTH)
    a_end = jnp.broadcast_to(a3[:, sl_len - 1:sl_len, :], a3.shape).reshape(rows, HGRN_WIDTH)
    q_in = (qf * jnp.exp(a)).astype(BF16)
    k_out = (k * jnp.exp(a_end - a)).astype(BF16)
    decay = jnp.exp(a_end)
    v_b = v.astype(BF16)
    qk = qf * k
    factors = _level_factors(a, qf, k, log_f, rows, sl_len)
    heads = [slice(hd * HGRN_DK, (hd + 1) * HGRN_DK) for hd in range(HGRN_HEADS)]
    inter = [jnp.concatenate([_dot(q_in[r, sl], s0_ref[s, hd].astype(BF16)) for s, r in enumerate(seqs)], axis=0)
             for hd, sl in enumerate(heads)]
    o_heads = []
    for hd, sl in enumerate(heads):
        p_h = _intra_scores(factors, hd, rows, sl_len)
        o = _dot(p_h.astype(BF16), v_b[:, sl]) + inter[hd]
        o = o + jnp.sum(qk[:, sl], axis=-1, keepdims=True) * v[:, sl]
        o_heads.append(_head_norm_gate(o, gate[:, sl]))
    updates = [[_dot_tn(k_out[r, sl], v_b[r, sl]) for sl in heads] for r in seqs]
    for s, r in enumerate(seqs):
        for hd, sl in enumerate(heads):
            decay_cols = jnp.broadcast_to(decay[r, sl][sl_len - 1:sl_len, :], (HGRN_DV, HGRN_DK)).T
            ns_ref[s, hd] = decay_cols * s0_ref[s, hd] + updates[s][hd]

    mixed = jnp.concatenate([o_pool] + o_heads + [o_x], axis=-1).astype(BF16)
    x2_ref[...] = (x + _dot(mixed, wout_ref[...])).reshape(gs, sl_len, D_MODEL)


def _sample_mixer_call(x, hist, s0, mk, mv, mixer_w, gs):
    b, l, d = x.shape
    ln1, w_in, pool_wbd, pool_scale, lb_logits, onorm, w_out = mixer_w
    grid = (b // gs,)
    blk = pl.BlockSpec((gs, l, d), lambda i: (i, 0, 0))
    histb = pl.BlockSpec((gs, POOL_HIST, POOL_WIDTH), lambda i: (i, 0, 0))
    sb = pl.BlockSpec((gs, HGRN_HEADS, HGRN_DK, HGRN_DV), lambda i: (i, 0, 0, 0))
    mem = pl.BlockSpec((gs, N_MEM, XATTN_WIDTH), lambda i: (i, 0, 0))
    return pl.pallas_call(
        functools.partial(_sample_mixer_body, gs=gs, sl_len=l),
        grid=grid,
        in_specs=[blk, histb, sb, mem, mem,
                  _const_spec((1, d)), _const_spec((d, D_IN)), _const_spec((POOL_WIDTH, POOL_WIDTH)),
                  _const_spec((1, POOL_WIDTH)), _const_spec(lb_logits.shape), _const_spec((1, HGRN_WIDTH)),
                  _const_spec((d, d))],
        out_specs=[blk, histb, sb],
        out_shape=[jax.ShapeDtypeStruct((b, l, d), F32),
                   jax.ShapeDtypeStruct((b, POOL_HIST, POOL_WIDTH), F32),
                   jax.ShapeDtypeStruct((b, HGRN_HEADS, HGRN_DK, HGRN_DV), F32)],
        scratch_shapes=[pltpu.VMEM((gs, POOL_PAD + l, POOL_WIDTH), F32)],
        compiler_params=pltpu.CompilerParams(dimension_semantics=("arbitrary",),
                                             vmem_limit_bytes=VMEM_LIMIT_BYTES),
        name="sample_mixer",
    )(x, hist, s0, mk, mv, ln1, w_in, pool_wbd, pool_scale, lb_logits, onorm, w_out)


def _sample_ffn_body(x_ref, chist_ref, ln2_ref, wup_ref, cw_ref, cb_ref, wdown_ref, lnf_ref, y_ref, nconv_ref,
                     *, gs, sl_len):
    rows = gs * sl_len
    x = x_ref[...].reshape(rows, D_MODEL)
    h = _rmsnorm(x, ln2_ref[...]).astype(BF16)
    ab = _dot(h, wup_ref[...])
    a = ab[:, :D_FF]
    ridx = lax.broadcasted_iota(jnp.int32, (rows, D_FF), 0) % sl_len
    hist = chist_ref[...]
    h1 = jnp.broadcast_to(hist[:, 1:2, :], (gs, sl_len, D_FF)).reshape(rows, D_FF)
    h0 = jnp.broadcast_to(hist[:, 0:1, :], (gs, sl_len, D_FF)).reshape(rows, D_FF)
    a_m1 = jnp.where(ridx >= 1, pltpu.roll(a, 1, 0), h1)
    a_m2 = jnp.where(ridx >= 2, pltpu.roll(a, 2, 0), jnp.where(ridx == 1, h1, h0))
    conv = cb_ref[...] + cw_ref[0:1, :] * a_m2 + cw_ref[1:2, :] * a_m1 + cw_ref[2:3, :] * a
    act = _gelu_tanh(conv) * ab[:, D_FF:]
    nconv_ref[...] = a.reshape(gs, sl_len, D_FF)[:, sl_len - (CONV_W - 1):, :]
    y_ref[...] = _ffn_tail(x, act, wdown_ref, lnf_ref).reshape(gs, sl_len, D_MODEL)


def _sample_ffn_call(x, chist, ffn_w, gs):
    b, l, d = x.shape
    ln2, w_up, conv_w, conv_b, w_down, lnf = ffn_w
    blk = pl.BlockSpec((gs, l, d), lambda i: (i, 0, 0))
    cblk = pl.BlockSpec((gs, CONV_W - 1, D_FF), lambda i: (i, 0, 0))
    return pl.pallas_call(
        functools.partial(_sample_ffn_body, gs=gs, sl_len=l),
        grid=(b // gs,),
        in_specs=[blk, cblk, _const_spec((1, d)), _const_spec((d, 2 * D_FF)), _const_spec((CONV_W, D_FF)),
                  _const_spec((1, D_FF)), _const_spec((D_FF, d)), _const_spec((1, d))],
        out_specs=[blk, cblk],
        out_shape=[jax.ShapeDtypeStruct((b, l, d), F32),
                   jax.ShapeDtypeStruct((b, CONV_W - 1, D_FF), F32)],
        compiler_params=pltpu.CompilerParams(dimension_semantics=("arbitrary",),
                                             vmem_limit_bytes=VMEM_LIMIT_BYTES),
        name="sample_ffn",
    )(x, chist, ln2, w_up, conv_w, conv_b, w_down, lnf)


def _block_diag(pool_w):
    n = pool_w.shape[0]
    out = jnp.zeros((n * POOL_GROUP, n * POOL_GROUP), pool_w.dtype)
    for g in range(n):
        out = lax.dynamic_update_slice(out, pool_w[g], (g * POOL_GROUP, g * POOL_GROUP))
    return out


def _layer(x_prompt, x_sample, mem_prompt, state_pool, state_hgrn, state_conv, cache_mem_k, cache_mem_v,
           ln1_g, w_in, pool_w, pool_scale, hgrn_lb_logits, hgrn_onorm_g, mem_norm_g, w_mem_kv, w_out,
           ln2_g, w_up, conv_w, conv_b, w_down, lnf_g, *, prompt_tb, sample_gs):
    row = lambda a: a.reshape(1, -1)
    w_in_b, w_out_b, w_up_b, w_down_b = (w.astype(BF16) for w in (w_in, w_out, w_up, w_down))
    pool_wbd = _block_diag(pool_w).astype(BF16)
    mixer_w = (row(ln1_g), w_in_b, pool_wbd, row(pool_scale), hgrn_lb_logits, row(hgrn_onorm_g), w_out_b)
    ffn_w = (row(ln2_g), w_up_b, conv_w, row(conv_b), w_down_b, row(lnf_g))

    mk, mv = _memkv_call(mem_prompt, row(mem_norm_g), w_mem_kv.astype(BF16))
    y_prompt, new_pool_p, new_s_p, new_conv_p = _prompt_call(x_prompt, mk, mv, mixer_w, ffn_w, tb=prompt_tb)

    nb = x_sample.shape[0]
    smk = cache_mem_k.astype(BF16).reshape(nb, N_MEM, XATTN_WIDTH)
    smv = cache_mem_v.astype(BF16).reshape(nb, N_MEM, XATTN_WIDTH)
    xs, new_pool_s, new_s_s = _sample_mixer_call(x_sample, state_pool, state_hgrn, smk, smv, mixer_w, gs=sample_gs)
    y_sample, new_conv_s = _sample_ffn_call(xs, state_conv, ffn_w, gs=sample_gs)
    bp = x_prompt.shape[0]
    heads = (bp, N_MEM, XATTN_HEADS, XATTN_DH)
    return (y_prompt, y_sample, new_pool_p, new_s_p, new_conv_p, mk.reshape(heads), mv.reshape(heads),
            new_pool_s, new_s_s, new_conv_s)


def kernel(x_prompt, x_sample, mem_prompt, state_pool, state_hgrn, state_conv, cache_mem_k, cache_mem_v,
           ln1_g, w_in, pool_w, pool_scale, hgrn_lb_logits, hgrn_onorm_g, mem_norm_g, w_mem_kv, w_out,
           ln2_g, w_up, conv_w, conv_b, w_down, lnf_g):
    assert w_in.shape[0] == 1, "one layer"
    outs = _layer(x_prompt, x_sample, mem_prompt, state_pool[0], state_hgrn[0], state_conv[0],
                  cache_mem_k[0], cache_mem_v[0], ln1_g[0], w_in[0], pool_w[0], pool_scale[0], hgrn_lb_logits,
                  hgrn_onorm_g[0], mem_norm_g[0], w_mem_kv[0], w_out[0], ln2_g[0], w_up[0], conv_w[0], conv_b[0],
                  w_down[0], lnf_g, prompt_tb=256, sample_gs=16)
    y_prompt, y_sample = outs[0], outs[1]
    return (y_prompt, y_sample) + tuple(o[None] for o in outs[2:])
```

```python
import functools

import jax
import jax.numpy as jnp
from jax import lax
from jax.experimental import pallas as pl
from jax.experimental.pallas import tpu as pltpu

F32 = jnp.float32
BF16 = jnp.bfloat16

D_MODEL = 1024
POOL_WIDTH = 256
POOL_GROUP = 64
POOL_HIST = 15
HGRN_WIDTH = 512
HGRN_HEADS = 4
HGRN_DK = 128
HGRN_DV = 128
XATTN_WIDTH = 256
XATTN_HEADS = 4
XATTN_DH = 64
N_MEM = 256
D_FF = 2816
CONV_W = 3
EPS = 1e-6
PAST_LEN = 16384
D_IN = POOL_WIDTH + 4 * HGRN_WIDTH + XATTN_WIDTH
OFF_U, OFF_Q, OFF_F, OFF_I, OFF_G, OFF_X = 0, 256, 768, 1280, 1792, 2304

CHUNK = 64
POOL_PAD = 16
CONV_PAD = 8
LANES = 128
MEMKV_GROUP = 4
VMEM_LIMIT_BYTES = 56 * 1024 * 1024

_NT = (((1,), (1,)), ((), ()))
_TN = (((0,), (0,)), ((), ()))


def _dot(a, b):
    return jnp.dot(a, b, preferred_element_type=F32)


def _dot_nt(a, b):
    return lax.dot_general(a, b, _NT, preferred_element_type=F32)


def _dot_tn(a, b):
    return lax.dot_general(a, b, _TN, preferred_element_type=F32)


def _rmsnorm(x, g):
    return x * lax.rsqrt(jnp.mean(x * x, axis=-1, keepdims=True) + EPS) * g


def _const_spec(shape):
    nd = len(shape)
    return pl.BlockSpec(shape, lambda *_: (0,) * nd, pipeline_mode=pl.Buffered(1))


def _memkv_body(mem_ref, g_ref, w_ref, k_ref, v_ref):
    nb = mem_ref.shape[0]
    h = _rmsnorm(mem_ref[...].reshape(nb * N_MEM, D_MODEL), g_ref[...]).astype(BF16)
    kv = _dot(h, w_ref[...])
    k_ref[...] = kv[:, :XATTN_WIDTH].reshape(nb, N_MEM, XATTN_WIDTH)
    v_ref[...] = kv[:, XATTN_WIDTH:].reshape(nb, N_MEM, XATTN_WIDTH)


def _memkv_call(mem, g, w):
    b = mem.shape[0]
    nb = MEMKV_GROUP
    out = jax.ShapeDtypeStruct((b, N_MEM, XATTN_WIDTH), F32)
    return pl.pallas_call(
        _memkv_body,
        grid=(b // nb,),
        in_specs=[pl.BlockSpec((nb, N_MEM, D_MODEL), lambda i: (i, 0, 0)),
                  _const_spec((1, D_MODEL)),
                  _const_spec((D_MODEL, 2 * XATTN_WIDTH))],
        out_specs=[pl.BlockSpec((nb, N_MEM, XATTN_WIDTH), lambda i: (i, 0, 0))] * 2,
        out_shape=[out, out],
        compiler_params=pltpu.CompilerParams(dimension_semantics=("arbitrary",)),
        name="memkv",
    )(mem, g, w)


def _forget_lower_bound(logits):
    z = logits - jnp.max(logits, axis=0, keepdims=True)
    e = jnp.exp(z)
    return e[0:1, :] / jnp.sum(e, axis=0, keepdims=True)


def _hgrn_gates(proj, lb):
    fp = proj[:, OFF_F:OFF_F + HGRN_WIDTH]
    q = proj[:, OFF_Q:OFF_Q + HGRN_WIDTH]
    log_f = jnp.log(lb + (1.0 - lb) * jax.nn.sigmoid(fp))
    k = (1.0 - lb) * jax.nn.sigmoid(-fp)
    qf = q * jax.nn.sigmoid(q)
    return qf, k, log_f


def _segment_cumsum(x, seq):
    ridx = lax.broadcasted_iota(jnp.int32, x.shape, 0) & (seq - 1)
    sh = 1
    while sh < seq:
        x = x + jnp.where(ridx >= sh, pltpu.roll(x, sh, 0), 0.0)
        sh *= 2
    return x


def _level_factor(a, qf, k, log_f, m, rows):
    n = a.shape[1]
    ridx = lax.broadcasted_iota(jnp.int32, (rows, n), 0)
    upper = (ridx & m) != 0
    if m == 1:
        d = jnp.where(upper, log_f, 0.0)
    else:
        if (2 * m) % 8 == 0:
            nb = rows // (2 * m)
            a3 = a.reshape(nb, 2 * m, n)
            ref = jnp.broadcast_to(a3[:, m - 1:m, :], (nb, 2 * m, n)).reshape(rows, n)
        else:
            a3 = a.reshape(rows // 8, 8, n)
            sub = lax.broadcasted_iota(jnp.int32, a3.shape, 1)
            ref = jnp.where(sub < 4,
                            jnp.broadcast_to(a3[:, 1:2, :], a3.shape),
                            jnp.broadcast_to(a3[:, 5:6, :], a3.shape)).reshape(rows, n)
        d = -jnp.abs(a - ref)
    return (jnp.where(upper, qf, k) * jnp.exp(d)).astype(BF16)


def _level_factors(a, qf, k, log_f, rows, seq):
    out, m = [], seq // 2
    while m >= 1:
        out.append((m, _level_factor(a, qf, k, log_f, m, rows)))
        m //= 2
    return out


def _intra_scores(factors, head, rows, seq):
    sl = slice(head * HGRN_DK, (head + 1) * HGRN_DK)
    t = lax.broadcasted_iota(jnp.int32, (rows, rows), 0)
    s = lax.broadcasted_iota(jnp.int32, (rows, rows), 1)
    x = t ^ s
    products = [(m, _dot_nt(y[:, sl], y[:, sl])) for m, y in reversed(factors)]
    total = jnp.zeros((rows, rows), F32)
    for m, p in products:
        total = jnp.where(x >= m, p, total)
    return jnp.where((t > s) & (x < seq), total, 0.0)


def _head_norm_gate(o, gate):
    return o * lax.rsqrt(jnp.mean(o * o, axis=-1, keepdims=True) + EPS) * gate


def _softmax_rows(s):
    e = jnp.exp(s - jnp.max(s, axis=-1, keepdims=True))
    return e / jnp.sum(e, axis=-1, keepdims=True)


def _cross_attention(qx, mk, mv):
    rows = qx.shape[0]
    head_of_lane = lax.broadcasted_iota(jnp.int32, qx.shape, 1) // XATTN_DH
    qs = jnp.concatenate([jnp.where(head_of_lane == h, qx, 0.0) for h in range(XATTN_HEADS)], axis=0)
    p = _softmax_rows(_dot_nt(qs.astype(BF16), mk))
    o = _dot(p.astype(BF16), mv)
    out = jnp.zeros(qx.shape, F32)
    for h in range(XATTN_HEADS):
        out = jnp.where(head_of_lane == h, o[h * rows:(h + 1) * rows, :], out)
    return out


def _pool_means(ld, posf, shape):
    lane = lax.broadcasted_iota(jnp.int32, shape, len(shape) - 1)
    first = lane < POOL_GROUP
    u_lo, u_hi = ld(0, 0), ld(0, 1)
    t2 = u_lo + ld(1, 0)
    t4 = t2 + ld(2, 0) + ld(3, 0)
    t8 = u_hi
    for j in range(1, 8):
        t8 = t8 + ld(j, 1)
    t16 = t8
    for j in range(8, 16):
        t16 = t16 + ld(j, 1)
    cnt_lo = jnp.where(first, jnp.minimum(2.0, posf), jnp.minimum(4.0, posf))
    cnt_hi = jnp.where(first, jnp.minimum(8.0, posf), jnp.minimum(16.0, posf))
    lo = jnp.where(first, t2, t4) / cnt_lo - u_lo
    hi = jnp.where(first, t8, t16) / cnt_hi - u_hi
    return jnp.concatenate([lo, hi], axis=-1)


def _after(x, anchor, zero):
    r, n = anchor.shape
    s = jnp.sum(anchor.reshape(r // 8, 8, n), axis=0)
    c = s[:, :LANES]
    for i in range(1, n // LANES):
        c = c + s[:, i * LANES:(i + 1) * LANES]
    z = lax.bitcast_convert_type(lax.bitcast_convert_type(c[0:1, :], jnp.int32) & zero, F32)
    return jnp.concatenate([x[:, :LANES] + z, x[:, LANES:]], axis=1)


def _gelu_tanh(x):
    return 0.5 * x * (1.0 + jnp.tanh(0.7978845608028654 * (x + 0.044715 * (x * x * x))))


def _ffn_tail(x, act, wdown_ref, lnf_ref):
    y = x + _dot(act.astype(BF16), wdown_ref[...])
    return _rmsnorm(y, lnf_ref[...])


def _prompt_body(x_ref, mk_ref, mv_ref, ln1_ref, win_ref, poolw_ref, pscale_ref, lbl_ref, onorm_ref,
                 wout_ref, ln2_ref, wup_ref, cw_ref, cb_ref, wdown_ref, lnf_ref, zero_ref,
                 y_ref, npool_ref, ns_ref, nconv_ref,
                 pbuf, st, abuf, x2s, *, tb, nt, nblk):
    g = pl.program_id(0)
    jm = jnp.minimum(g, nblk - 1) % nt
    jf = jnp.maximum(g - 1, 0) % nt

    @pl.when(g == 0)
    def _():
        x2s[...] = jnp.zeros(x2s.shape, F32)

    @pl.when(jm == 0)
    def _():
        pbuf[pl.ds(0, POOL_PAD), :] = jnp.zeros((POOL_PAD, POOL_WIDTH), F32)
        st[...] = jnp.zeros(st.shape, F32)

    @pl.when(jf == 0)
    def _():
        abuf[pl.ds(0, CONV_PAD), :] = jnp.zeros((CONV_PAD, D_FF), F32)

    x = x_ref[0]
    h = _rmsnorm(x, ln1_ref[...]).astype(BF16)
    proj = _dot(h, win_ref[...])

    x2 = x2s[...]
    h2 = _rmsnorm(x2, ln2_ref[...]).astype(BF16)
    ab = _dot(h2, wup_ref[...])
    abuf[pl.ds(CONV_PAD, tb), :] = ab[:, :D_FF]
    conv = cb_ref[...]
    for t in range(CONV_W):
        conv = conv + cw_ref[t:t + 1, :] * abuf[pl.ds(CONV_PAD - (CONV_W - 1) + t, tb), :]
    act = _gelu_tanh(conv) * ab[:, D_FF:]
    abuf[pl.ds(0, CONV_PAD), :] = abuf[pl.ds(tb, CONV_PAD), :]
    act = _after(act, proj, zero_ref[...]).astype(BF16)
    y_ref[0] = _rmsnorm(x2 + _dot(act, wdown_ref[...]), lnf_ref[...])


    pbuf[pl.ds(POOL_PAD, tb), :] = proj[:, OFF_U:OFF_U + POOL_WIDTH]
    posf = (jm * tb + 1 + lax.broadcasted_iota(jnp.int32, (tb, 1), 0)).astype(F32)
    dm = _pool_means(lambda j, half: pbuf[pl.ds(POOL_PAD - j, tb), pl.ds(LANES * half, LANES)], posf, (tb, LANES))
    o_pool = _dot(dm.astype(BF16), poolw_ref[...]) * pscale_ref[...]
    pbuf[pl.ds(0, POOL_PAD), :] = pbuf[pl.ds(tb, POOL_PAD), :]

    qx = proj[:, OFF_X:OFF_X + XATTN_WIDTH] * (XATTN_DH ** -0.5)
    o_x = _cross_attention(qx, mk_ref[0].astype(BF16), mv_ref[0].astype(BF16))

    lb = _forget_lower_bound(lbl_ref[...])
    qf, k, log_f = _hgrn_gates(proj, lb)
    v = proj[:, OFF_I:OFF_I + HGRN_WIDTH]
    gg = proj[:, OFF_G:OFF_G + HGRN_WIDTH]
    gate = gg * jax.nn.sigmoid(gg) * onorm_ref[...]
    a_all = _segment_cumsum(log_f, CHUNK)
    states = [st[hd] for hd in range(HGRN_HEADS)]
    o_rows = []
    for c in range(tb // CHUNK):
        rs = slice(c * CHUNK, (c + 1) * CHUNK)
        qf_c, k_c, lf_c, v_c, a = qf[rs], k[rs], log_f[rs], v[rs], a_all[rs]
        a_end = a[CHUNK - 1:CHUNK, :]
        q_in = (qf_c * jnp.exp(a)).astype(BF16)
        k_out = (k_c * jnp.exp(a_end - a)).astype(BF16)
        decay = jnp.exp(a_end)
        v_b = v_c.astype(BF16)
        qk = qf_c * k_c
        factors = _level_factors(a, qf_c, k_c, lf_c, CHUNK, CHUNK)
        heads = [slice(hd * HGRN_DK, (hd + 1) * HGRN_DK) for hd in range(HGRN_HEADS)]
        inter = [_dot_nt(q_in[:, sl], states[hd].astype(BF16)) for hd, sl in enumerate(heads)]
        update = [_dot_tn(v_b[:, sl], k_out[:, sl]) for sl in heads]
        scores = [_intra_scores(factors, hd, CHUNK, CHUNK).astype(BF16) for hd in range(HGRN_HEADS)]
        intra = [_dot(scores[hd], v_b[:, sl]) for hd, sl in enumerate(heads)]
        o_heads = []
        for hd, sl in enumerate(heads):
            o = intra[hd] + inter[hd] + jnp.sum(qk[:, sl], axis=-1, keepdims=True) * v_c[:, sl]
            states[hd] = states[hd] * decay[:, sl] + update[hd]
            o_heads.append(_head_norm_gate(o, gate[rs, sl]))
        o_rows.append(jnp.concatenate(o_heads, axis=-1))
    for hd in range(HGRN_HEADS):
        st[hd] = states[hd]
    o_hgrn = jnp.concatenate(o_rows, axis=0)

    mixed = jnp.concatenate([o_pool, o_hgrn, o_x], axis=-1).astype(BF16)
    x2s[...] = x + _dot(mixed, wout_ref[...])

    @pl.when(jnp.logical_and(jm == nt - 1, g < nblk))
    def _():
        npool_ref[0] = pbuf[pl.ds(1, POOL_HIST), :]
        for hd in range(HGRN_HEADS):
            ns_ref[0, hd] = st[hd].T

    @pl.when(jnp.logical_and(jf == nt - 1, g >= 1))
    def _():
        nconv_ref[0] = abuf[pl.ds(CONV_PAD - (CONV_W - 1), CONV_W - 1), :]


def _prompt_call(x, mk, mv, mixer_w, ffn_w, tb):
    b, l, d = x.shape
    nt = l // tb
    nblk = b * nt
    ln1, w_in, pool_wbd, pool_scale, lb_logits, onorm, w_out = mixer_w
    ln2, w_up, conv_w, conv_b, w_down, lnf = ffn_w

    def mixer_blk(g):
        return jnp.minimum(g, nblk - 1)

    def ffn_blk(g):
        return jnp.maximum(g - 1, 0)

    x_spec = pl.BlockSpec((1, tb, d), lambda g: (mixer_blk(g) // nt, mixer_blk(g) % nt, 0))
    mem = pl.BlockSpec((1, N_MEM, XATTN_WIDTH), lambda g: (mixer_blk(g) // nt, 0, 0))
    y_spec = pl.BlockSpec((1, tb, d), lambda g: (ffn_blk(g) // nt, ffn_blk(g) % nt, 0))
    return pl.pallas_call(
        functools.partial(_prompt_body, tb=tb, nt=nt, nblk=nblk),
        grid=(nblk + 1,),
        in_specs=[x_spec, mem, mem,
                  _const_spec((1, d)), _const_spec((d, D_IN)), _const_spec((POOL_WIDTH, POOL_WIDTH)),
                  _const_spec((1, POOL_WIDTH)), _const_spec(lb_logits.shape), _const_spec((1, HGRN_WIDTH)),
                  _const_spec((d, d)),
                  _const_spec((1, d)), _const_spec((d, 2 * D_FF)), _const_spec((CONV_W, D_FF)),
                  _const_spec((1, D_FF)), _const_spec((D_FF, d)), _const_spec((1, d)), _const_spec((1, LANES))],
        out_specs=[y_spec,
                   pl.BlockSpec((1, POOL_HIST, POOL_WIDTH), lambda g: (mixer_blk(g) // nt, 0, 0)),
                   pl.BlockSpec((1, HGRN_HEADS, HGRN_DK, HGRN_DV), lambda g: (mixer_blk(g) // nt, 0, 0, 0)),
                   pl.BlockSpec((1, CONV_W - 1, D_FF), lambda g: (ffn_blk(g) // nt, 0, 0))],
        out_shape=[jax.ShapeDtypeStruct((b, l, d), F32),
                   jax.ShapeDtypeStruct((b, POOL_HIST, POOL_WIDTH), F32),
                   jax.ShapeDtypeStruct((b, HGRN_HEADS, HGRN_DK, HGRN_DV), F32),
                   jax.ShapeDtypeStruct((b, CONV_W - 1, D_FF), F32)],
        scratch_shapes=[pltpu.VMEM((POOL_PAD + tb, POOL_WIDTH), F32),
                        pltpu.VMEM((HGRN_HEADS, HGRN_DV, HGRN_DK), F32),
                        pltpu.VMEM((CONV_PAD + tb, D_FF), F32),
                        pltpu.VMEM((tb, d), F32)],
        compiler_params=pltpu.CompilerParams(dimension_semantics=("arbitrary",),
                                             vmem_limit_bytes=VMEM_LIMIT_BYTES),
        name="prompt_layer",
    )(x, mk, mv, ln1, w_in, pool_wbd, pool_scale, lb_logits, onorm, w_out,
      ln2, w_up, conv_w, conv_b, w_down, lnf, jnp.zeros((1, LANES), jnp.int32))


MXU_WIDTH = 256
FF_TILE = MXU_WIDTH
PROJ_TILE = 2 * MXU_WIDTH
UP_TILES = tuple((part, off, min(PROJ_TILE, D_FF - off)) for part in (0, 1) for off in range(0, D_FF, PROJ_TILE))
IN_TILES = (("f", OFF_F, HGRN_WIDTH), ("q", OFF_Q, HGRN_WIDTH), ("i", OFF_I, HGRN_WIDTH),
            ("g", OFF_G, HGRN_WIDTH), ("u", OFF_U, POOL_WIDTH), ("x", OFF_X, XATTN_WIDTH))


def _zero_bits_of(anchor, zero):
    return lax.bitcast_convert_type(anchor[0:1, 0:LANES], jnp.int32) & zero


def _row_after(row, anchor, zero):
    d = jnp.concatenate([_zero_bits_of(anchor, zero)] * (row.shape[1] // LANES), axis=1)
    return lax.bitcast_convert_type(lax.bitcast_convert_type(row, jnp.int32) | d, F32)


def _lhs_after(lhs, anchor, zero):
    if anchor is None:
        return lhs
    head = pltpu.bitcast(lhs[:, :LANES], jnp.int32) | _zero_bits_of(anchor, zero)
    return jnp.concatenate([pltpu.bitcast(head, BF16), lhs[:, LANES:]], axis=1)


def _prompt3_body(x_ref, mk_ref, mv_ref, ln1_ref, win_ref, poolw_ref, pscale_ref, lbl_ref, onorm_ref, wout_ref,
                  ln2_ref, wup_ref, cw_ref, cb_ref, wdown_ref, lnf_ref, zero_ref,
                  y_ref, npool_ref, ns_ref, nconv_ref,
                  pbuf, st, abuf, bbuf, acts, xs, mixs, x2s, *, tb, nt, nblk):
    g = pl.program_id(0)
    jm = jnp.minimum(g, nblk - 1) % nt
    jc = jnp.maximum(g - 2, 0) % nt

    @pl.when(g == 0)
    def _():
        xs[...] = jnp.zeros(xs.shape, F32)
        mixs[...] = jnp.zeros(mixs.shape, BF16)
        x2s[...] = jnp.zeros(x2s.shape, F32)
        abuf[...] = jnp.zeros(abuf.shape, F32)
        bbuf[...] = jnp.zeros(bbuf.shape, F32)

    @pl.when(jm == 0)
    def _():
        pbuf[pl.ds(0, POOL_PAD), :] = jnp.zeros((POOL_PAD, POOL_WIDTH), F32)
        st[...] = jnp.zeros(st.shape, F32)

    @pl.when(jc == 0)
    def _():
        abuf[pl.ds(0, CONV_PAD), :] = jnp.zeros((CONV_PAD, D_FF), F32)

    zero = zero_ref[...]
    chain = {"mxu": None, "valu": None}

    def big_dot(lhs, rhs):
        r = _dot(_lhs_after(lhs, chain["mxu"], zero), rhs)
        chain["mxu"] = r
        return r

    def ffn_act_tile(i):
        cols = pl.ds(i * FF_TILE, FF_TILE)
        conv = cb_ref[:, cols]
        if chain["valu"] is not None:
            conv = _row_after(conv, chain["valu"], zero)
        for t in range(CONV_W):
            conv = conv + cw_ref[t:t + 1, cols] * abuf[pl.ds(CONV_PAD - (CONV_W - 1) + t, tb), cols]
        act = _gelu_tanh(conv) * bbuf[:, cols]
        acts[:, cols] = act.astype(BF16)
        abuf[pl.ds(0, CONV_PAD), cols] = abuf[pl.ds(tb, CONV_PAD), cols]
        chain["valu"] = act

    n_act = D_FF // FF_TILE
    x = x_ref[0]
    h = _rmsnorm(x, ln1_ref[...]).astype(BF16)
    mixed_prev = mixs[...]
    proj = {}
    x2_tiles = []
    for i in range(D_MODEL // PROJ_TILE):
        cols = pl.ds(i * PROJ_TILE, PROJ_TILE)
        ffn_act_tile(i)
        x2_tiles.append(xs[:, cols] + big_dot(mixed_prev, wout_ref[:, cols]))
    for n, (name, off, width) in enumerate(IN_TILES):
        ffn_act_tile(D_MODEL // PROJ_TILE + n)
        proj[name] = big_dot(h, win_ref[:, pl.ds(off, width)])
    x2 = jnp.concatenate(x2_tiles, axis=-1)
    h2 = _rmsnorm(x2, ln2_ref[...]).astype(BF16)

    def up_proj_tile(i):
        part, off, width = UP_TILES[i]
        r = big_dot(h2, wup_ref[:, pl.ds(part * D_FF + off, width)])
        if part == 0:
            abuf[pl.ds(CONV_PAD, tb), pl.ds(off, width)] = r
        else:
            bbuf[:, pl.ds(off, width)] = r

    n_early_up = 3
    for i in range(n_early_up):
        if D_MODEL // PROJ_TILE + len(IN_TILES) + i < n_act:
            ffn_act_tile(D_MODEL // PROJ_TILE + len(IN_TILES) + i)
        up_proj_tile(i)
    acts_all = acts[...]
    down = jnp.concatenate([big_dot(acts_all, wdown_ref[:, pl.ds(i * PROJ_TILE, PROJ_TILE)])
                            for i in range(D_MODEL // PROJ_TILE)], axis=-1)
    y_ref[0] = _rmsnorm(x2s[...] + down, lnf_ref[...])

    lb = _row_after(_forget_lower_bound(lbl_ref[...]), chain["valu"], zero)
    fp, q = proj["f"], proj["q"]
    log_f = jnp.log(lb + (1.0 - lb) * jax.nn.sigmoid(fp))
    k = (1.0 - lb) * jax.nn.sigmoid(-fp)
    qf = q * jax.nn.sigmoid(q)
    v = proj["i"]
    gate = proj["g"] * jax.nn.sigmoid(proj["g"]) * onorm_ref[...]
    a_all = _segment_cumsum(log_f, CHUNK)

    pbuf[pl.ds(POOL_PAD, tb), :] = proj["u"]
    posf = (jm * tb + 1 + lax.broadcasted_iota(jnp.int32, (tb, 1), 0)).astype(F32)
    dm = _pool_means(lambda j, half: pbuf[pl.ds(POOL_PAD - j, tb), pl.ds(LANES * half, LANES)], posf, (tb, LANES))
    o_pool = _dot(dm.astype(BF16), poolw_ref[...]) * pscale_ref[...]
    pbuf[pl.ds(0, POOL_PAD), :] = pbuf[pl.ds(tb, POOL_PAD), :]

    o_x = _cross_attention(proj["x"] * (XATTN_DH ** -0.5), mk_ref[0].astype(BF16), mv_ref[0].astype(BF16))

    states = [st[hd] for hd in range(HGRN_HEADS)]
    heads = [slice(hd * HGRN_DK, (hd + 1) * HGRN_DK) for hd in range(HGRN_HEADS)]
    o_rows = []
    n_chunks = tb // CHUNK
    up_next = n_early_up
    for c in range(n_chunks):
        target = n_early_up + ((c + 1) * (len(UP_TILES) - n_early_up)) // n_chunks
        while up_next < target:
            up_proj_tile(up_next)
            up_next += 1
        rs = slice(c * CHUNK, (c + 1) * CHUNK)
        qf_c, k_c, lf_c, v_c, a = qf[rs], k[rs], log_f[rs], v[rs], a_all[rs]
        if c > 0:
            a = _row_after(a, chain["valu"], zero)
        a_end = a[CHUNK - 1:CHUNK, :]
        q_in = (qf_c * jnp.exp(a)).astype(BF16)
        k_out = (k_c * jnp.exp(a_end - a)).astype(BF16)
        decay = jnp.exp(a_end)
        v_b = v_c.astype(BF16)
        qk = qf_c * k_c
        factors = _level_factors(a, qf_c, k_c, lf_c, CHUNK, CHUNK)
        inter = [_dot_nt(q_in[:, sl], states[hd].astype(BF16)) for hd, sl in enumerate(heads)]
        update = [_dot_tn(v_b[:, sl], k_out[:, sl]) for sl in heads]
        scores = [_intra_scores(factors, hd, CHUNK, CHUNK).astype(BF16) for hd in range(HGRN_HEADS)]
        intra = [_dot(scores[hd], v_b[:, sl]) for hd, sl in enumerate(heads)]
        o_heads = []
        for hd, sl in enumerate(heads):
            o = intra[hd] + inter[hd] + jnp.sum(qk[:, sl], axis=-1, keepdims=True) * v_c[:, sl]
            states[hd] = states[hd] * decay[:, sl] + update[hd]
            o_heads.append(_head_norm_gate(o, gate[rs, sl]))
        chain["valu"] = o_heads[-1]
        o_rows.append(jnp.concatenate(o_heads, axis=-1))
    for hd in range(HGRN_HEADS):
        st[hd] = states[hd]
    o_hgrn = jnp.concatenate(o_rows, axis=0)

    mixs[...] = jnp.concatenate([o_pool, o_hgrn, o_x], axis=-1).astype(BF16)
    xs[...] = x
    x2s[...] = x2

    @pl.when(jnp.logical_and(jm == nt - 1, g < nblk))
    def _():
        npool_ref[0] = pbuf[pl.ds(1, POOL_HIST), :]
        for hd in range(HGRN_HEADS):
            ns_ref[0, hd] = st[hd].T

    @pl.when(jnp.logical_and(jc == nt - 1, g >= 2))
    def _():
        nconv_ref[0] = abuf[pl.ds(CONV_PAD - (CONV_W - 1), CONV_W - 1), :]


TICK_CYCLES = 140.0
JOIN_SLACK_TICKS = 6


class _OrderClock:
    def __init__(self, zero):
        self.zero = zero
        self.ticks = [zero]
        self.pending = {}

    def _bits(self, value):
        return lax.bitcast_convert_type(value[0:1, 0:LANES].astype(F32), jnp.int32) & self.zero

    def at(self, t):
        while len(self.ticks) <= int(t):
            n = len(self.ticks)
            prev = lax.bitcast_convert_type(self.ticks[-1], F32)
            nxt = jnp.broadcast_to(jnp.sum(prev, axis=-1, keepdims=True), prev.shape)
            nxt = lax.bitcast_convert_type(nxt, jnp.int32) & self.zero
            for value in self.pending.pop(n, ()):
                nxt = nxt | self._bits(value)
            self.ticks.append(nxt)
        return self.ticks[int(t)]

    def join(self, t, value):
        n = max(int(-(-t // 1)), len(self.ticks))
        self.pending.setdefault(n, []).append(value)


def _cover(x):
    r, n = x.shape
    y = jnp.sum(x.reshape(r // 8, 8, n), axis=0)
    z = y[:, :LANES]
    for i in range(1, n // LANES):
        z = z + y[:, i * LANES:(i + 1) * LANES]
    return z


def _f32_after(x, tick):
    d = jnp.concatenate([tick] * (x.shape[1] // LANES), axis=1)
    return lax.bitcast_convert_type(lax.bitcast_convert_type(x, jnp.int32) | d, F32)


def _bf16_lhs_after(lhs, tick):
    head = pltpu.bitcast(pltpu.bitcast(lhs[:, :LANES], jnp.int32) | tick, BF16)
    return jnp.concatenate([head, lhs[:, LANES:]], axis=1)


def _prompt4_body(x_ref, mk_ref, mv_ref, ln1_ref, win_ref, poolw_ref, pscale_ref, lbl_ref, onorm_ref, wout_ref,
                  ln2_ref, wup_ref, cw_ref, cb_ref, wdown_ref, lnf_ref, zero_ref,
                  y_ref, npool_ref, ns_ref, nconv_ref,
                  pbuf, st, abuf, bbuf, acts, xs, mixs, x2s, *, tb, nt, nblk):
    g = pl.program_id(0)
    jm = jnp.minimum(g, nblk - 1) % nt
    jc = jnp.maximum(g - 2, 0) % nt

    @pl.when(g == 0)
    def _():
        xs[...] = jnp.zeros(xs.shape, F32)
        mixs[...] = jnp.zeros(mixs.shape, BF16)
        x2s[...] = jnp.zeros(x2s.shape, F32)
        abuf[...] = jnp.zeros(abuf.shape, F32)
        bbuf[...] = jnp.zeros(bbuf.shape, F32)

    @pl.when(jm == 0)
    def _():
        pbuf[pl.ds(0, POOL_PAD), :] = jnp.zeros((POOL_PAD, POOL_WIDTH), F32)
        st[...] = jnp.zeros(st.shape, F32)

    @pl.when(jc == 0)
    def _():
        abuf[pl.ds(0, CONV_PAD), :] = jnp.zeros((CONV_PAD, D_FF), F32)

    clk = _OrderClock(zero_ref[...])
    t = {"m": 0.0, "v": 0.0}

    def big_dot(lhs, rhs, not_before=0.0):
        k, n = rhs.shape
        t["m"] = max(t["m"], not_before)
        r = _dot(lhs, rhs)
        t["m"] += (tb // 2) * (k // MXU_WIDTH) * (n // MXU_WIDTH) / 2 / TICK_CYCLES
        clk.join(t["m"] + JOIN_SLACK_TICKS, r)
        return r

    def elementwise(cycles, not_before=0.0):
        t["v"] = max(t["v"], not_before)
        tick = clk.at(t["v"])
        t["v"] += cycles / TICK_CYCLES
        return tick

    def done(value):
        clk.join(t["v"] + JOIN_SLACK_TICKS, value)
        return t["v"]

    def ffn_act_tile(i):
        cols = pl.ds(i * FF_TILE, FF_TILE)
        conv = _f32_after(cb_ref[:, cols], elementwise(430))
        for s in range(CONV_W):
            conv = conv + cw_ref[s:s + 1, cols] * abuf[pl.ds(CONV_PAD - (CONV_W - 1) + s, tb), cols]
        act = _gelu_tanh(conv) * bbuf[:, cols]
        acts[:, cols] = act.astype(BF16)
        abuf[pl.ds(0, CONV_PAD), cols] = abuf[pl.ds(tb, CONV_PAD), cols]
        return done(_cover(act))

    def up_proj_tile(i, not_before):
        part, off, width = UP_TILES[i]
        r = big_dot(h2, wup_ref[:, pl.ds(part * D_FF + off, width)], not_before)
        if part == 0:
            abuf[pl.ds(CONV_PAD, tb), pl.ds(off, width)] = r
        else:
            bbuf[:, pl.ds(off, width)] = r

    x = x_ref[0]
    mixed_prev = mixs[...]
    n_out = D_MODEL // PROJ_TILE
    out_cols = [pl.ds(i * PROJ_TILE, PROJ_TILE) for i in range(n_out)]

    h = _rmsnorm(x, _f32_after(ln1_ref[...], elementwise(300))).astype(BF16)
    t_h = done(h)
    ffn_act_tile(0)
    x2_tiles = [xs[:, out_cols[0]] + big_dot(mixed_prev, wout_ref[:, out_cols[0]])]
    ffn_act_tile(1)
    x2_tiles.append(xs[:, out_cols[1]] + big_dot(mixed_prev, wout_ref[:, out_cols[1]]))
    t_x2 = t["m"] + 2.0
    ffn_act_tile(2)
    x2 = jnp.concatenate(x2_tiles, axis=-1)
    h2 = _rmsnorm(x2, _f32_after(ln2_ref[...], elementwise(300, t_x2))).astype(BF16)
    t_h2 = done(h2)
    proj = {}
    act_next = 3
    t_act = 0.0
    for name, off, width in IN_TILES:
        proj[name] = big_dot(h, win_ref[:, pl.ds(off, width)], t_h)
        t_proj = t["m"] + 2.0
        t_act = ffn_act_tile(act_next)
        act_next += 1
    n_act = D_FF // FF_TILE
    up_next = 0
    while act_next < n_act:
        up_proj_tile(up_next, t_h2)
        up_next += 1
        t_act = ffn_act_tile(act_next)
        act_next += 1

    lb = _f32_after(_forget_lower_bound(lbl_ref[...]), elementwise(1000, t_proj))
    fp, q = proj["f"], proj["q"]
    log_f = jnp.log(lb + (1.0 - lb) * jax.nn.sigmoid(fp))
    k = (1.0 - lb) * jax.nn.sigmoid(-fp)
    qf = q * jax.nn.sigmoid(q)
    v = proj["i"]
    gate = proj["g"] * jax.nn.sigmoid(proj["g"]) * onorm_ref[...]
    a_all = _segment_cumsum(log_f, CHUNK)
    done(_cover(a_all))
    up_proj_tile(up_next, t_h2)
    up_proj_tile(up_next + 1, t_h2)
    up_next += 2

    pbuf[pl.ds(POOL_PAD, tb), :] = _f32_after(proj["u"], elementwise(400))
    posf = (jm * tb + 1 + lax.broadcasted_iota(jnp.int32, (tb, 1), 0)).astype(F32)
    dm = _pool_means(lambda j, half: pbuf[pl.ds(POOL_PAD - j, tb), pl.ds(LANES * half, LANES)], posf, (tb, LANES))
    o_pool = _dot(dm.astype(BF16), poolw_ref[...]) * pscale_ref[...]
    pbuf[pl.ds(0, POOL_PAD), :] = pbuf[pl.ds(tb, POOL_PAD), :]
    done(_cover(o_pool))
    up_proj_tile(up_next, t_h2)
    up_next += 1

    qx = _f32_after(proj["x"], elementwise(1200)) * (XATTN_DH ** -0.5)
    o_x = _cross_attention(qx, mk_ref[0].astype(BF16), mv_ref[0].astype(BF16))
    done(_cover(o_x))
    up_proj_tile(up_next, t_h2)
    up_proj_tile(up_next + 1, t_h2)
    up_next += 2

    states = [st[hd] for hd in range(HGRN_HEADS)]
    heads = [slice(hd * HGRN_DK, (hd + 1) * HGRN_DK) for hd in range(HGRN_HEADS)]
    o_rows = []
    n_chunks = tb // CHUNK
    acts_all = None
    down_tiles = []
    for c in range(n_chunks):
        rs = slice(c * CHUNK, (c + 1) * CHUNK)
        qf_c, k_c, lf_c, v_c = qf[rs], k[rs], log_f[rs], v[rs]
        a = _f32_after(a_all[rs], elementwise(1350))
        a_end = a[CHUNK - 1:CHUNK, :]
        q_in = (qf_c * jnp.exp(a)).astype(BF16)
        k_out = (k_c * jnp.exp(a_end - a)).astype(BF16)
        decay = jnp.exp(a_end)
        v_b = v_c.astype(BF16)
        qk = qf_c * k_c
        factors = _level_factors(a, qf_c, k_c, lf_c, CHUNK, CHUNK)
        inter = [_dot_nt(q_in[:, sl], states[hd].astype(BF16)) for hd, sl in enumerate(heads)]
        update = [_dot_tn(v_b[:, sl], k_out[:, sl]) for sl in heads]
        scores = [_intra_scores(factors, hd, CHUNK, CHUNK).astype(BF16) for hd in range(HGRN_HEADS)]
        intra = [_dot(scores[hd], v_b[:, sl]) for hd, sl in enumerate(heads)]
        o_heads = []
        for hd, sl in enumerate(heads):
            o = intra[hd] + inter[hd] + jnp.sum(qk[:, sl], axis=-1, keepdims=True) * v_c[:, sl]
            states[hd] = states[hd] * decay[:, sl] + update[hd]
            o_heads.append(_head_norm_gate(o, gate[rs, sl]))
        o_rows.append(jnp.concatenate(o_heads, axis=-1))
        done(_cover(o_rows[-1]))
        while up_next < len(UP_TILES) and t["m"] < t["v"]:
            up_proj_tile(up_next, t_h2)
            up_next += 1
        if up_next == len(UP_TILES) and len(down_tiles) < n_out and t["m"] < t["v"] + 5.0:
            if acts_all is None:
                acts_all = acts[...]
            down_tiles.append(big_dot(acts_all, wdown_ref[:, out_cols[len(down_tiles)]], t_act))
    while up_next < len(UP_TILES):
        up_proj_tile(up_next, t_h2)
        up_next += 1
    if acts_all is None:
        acts_all = acts[...]
    while len(down_tiles) < n_out:
        down_tiles.append(big_dot(acts_all, wdown_ref[:, out_cols[len(down_tiles)]], t_act))
    for hd in range(HGRN_HEADS):
        st[hd] = states[hd]
    o_hgrn = jnp.concatenate(o_rows, axis=0)

    down = jnp.concatenate(down_tiles, axis=-1)
    y_ref[0] = _rmsnorm(x2s[...] + down, _f32_after(lnf_ref[...], elementwise(300, t["m"] + 2.0)))

    mixs[...] = jnp.concatenate([o_pool, o_hgrn, o_x], axis=-1).astype(BF16)
    xs[...] = x
    x2s[...] = x2

    @pl.when(jnp.logical_and(jm == nt - 1, g < nblk))
    def _():
        npool_ref[0] = pbuf[pl.ds(1, POOL_HIST), :]
        for hd in range(HGRN_HEADS):
            ns_ref[0, hd] = st[hd].T

    @pl.when(jnp.logical_and(jc == nt - 1, g >= 2))
    def _():
        nconv_ref[0] = abuf[pl.ds(CONV_PAD - (CONV_W - 1), CONV_W - 1), :]


def _prompt3_call(x, mk, mv, mixer_w, ffn_w, tb):
    b, l, d = x.shape
    nt = l // tb
    nblk = b * nt
    ln1, w_in, pool_wbd, pool_scale, lb_logits, onorm, w_out = mixer_w
    ln2, w_up, conv_w, conv_b, w_down, lnf = ffn_w

    def mixer_blk(g):
        return jnp.minimum(g, nblk - 1)

    def ffn_blk(g):
        return jnp.maximum(g - 2, 0)

    x_spec = pl.BlockSpec((1, tb, d), lambda g: (mixer_blk(g) // nt, mixer_blk(g) % nt, 0))
    mem = pl.BlockSpec((1, N_MEM, XATTN_WIDTH), lambda g: (mixer_blk(g) // nt, 0, 0))
    y_spec = pl.BlockSpec((1, tb, d), lambda g: (ffn_blk(g) // nt, ffn_blk(g) % nt, 0))
    return pl.pallas_call(
        functools.partial(_prompt4_body, tb=tb, nt=nt, nblk=nblk),
        grid=(nblk + 2,),
        in_specs=[x_spec, mem, mem,
                  _const_spec((1, d)), _const_spec((d, D_IN)), _const_spec((POOL_WIDTH, POOL_WIDTH)),
                  _const_spec((1, POOL_WIDTH)), _const_spec(lb_logits.shape), _const_spec((1, HGRN_WIDTH)),
                  _const_spec((d, d)),
                  _const_spec((1, d)), _const_spec((d, 2 * D_FF)), _const_spec((CONV_W, D_FF)),
                  _const_spec((1, D_FF)), _const_spec((D_FF, d)), _const_spec((1, d)), _const_spec((1, LANES))],
        out_specs=[y_spec,
                   pl.BlockSpec((1, POOL_HIST, POOL_WIDTH), lambda g: (mixer_blk(g) // nt, 0, 0)),
                   pl.BlockSpec((1, HGRN_HEADS, HGRN_DK, HGRN_DV), lambda g: (mixer_blk(g) // nt, 0, 0, 0)),
                   pl.BlockSpec((1, CONV_W - 1, D_FF), lambda g: (ffn_blk(g) // nt, 0, 0))],
        out_shape=[jax.ShapeDtypeStruct((b, l, d), F32),
                   jax.ShapeDtypeStruct((b, POOL_HIST, POOL_WIDTH), F32),
                   jax.ShapeDtypeStruct((b, HGRN_HEADS, HGRN_DK, HGRN_DV), F32),
                   jax.ShapeDtypeStruct((b, CONV_W - 1, D_FF), F32)],
        scratch_shapes=[pltpu.VMEM((POOL_PAD + tb, POOL_WIDTH), F32),
                        pltpu.VMEM((HGRN_HEADS, HGRN_DV, HGRN_DK), F32),
                        pltpu.VMEM((CONV_PAD + tb, D_FF), F32),
                        pltpu.VMEM((tb, D_FF), F32),
                        pltpu.VMEM((tb, D_FF), BF16),
                        pltpu.VMEM((tb, d), F32),
                        pltpu.VMEM((tb, d), BF16),
                        pltpu.VMEM((tb, d), F32)],
        compiler_params=pltpu.CompilerParams(dimension_semantics=("arbitrary",),
                                             vmem_limit_bytes=VMEM_LIMIT_BYTES),
        name="prompt_layer",
    )(x, mk, mv, ln1, w_in, pool_wbd, pool_scale, lb_logits, onorm, w_out,
      ln2, w_up, conv_w, conv_b, w_down, lnf, jnp.zeros((1, LANES), jnp.int32))


def _sample_mixer_body(x_ref, hist_ref, s0_ref, mk_ref, mv_ref, ln1_ref, win_ref, poolw_ref, pscale_ref, lbl_ref,
                       onorm_ref, wout_ref, x2_ref, npool_ref, ns_ref, pbuf, *, gs, sl_len):
    rows = gs * sl_len
    seqs = [slice(s * sl_len, (s + 1) * sl_len) for s in range(gs)]
    x = x_ref[...].reshape(rows, D_MODEL)
    h = _rmsnorm(x, ln1_ref[...]).astype(BF16)
    proj = _dot(h, win_ref[...])

    pbuf[:, pl.ds(1, POOL_HIST), :] = hist_ref[...]
    pbuf[:, pl.ds(POOL_PAD, sl_len), :] = proj[:, OFF_U:OFF_U + POOL_WIDTH].reshape(gs, sl_len, POOL_WIDTH)
    posf = (PAST_LEN + 1 + lax.broadcasted_iota(jnp.int32, (1, sl_len, 1), 1)).astype(F32)
    dm = _pool_means(lambda j, half: pbuf[:, pl.ds(POOL_PAD - j, sl_len), pl.ds(LANES * half, LANES)],
                     posf, (gs, sl_len, LANES))
    npool_ref[...] = pbuf[:, pl.ds(sl_len + 1, POOL_HIST), :]
    o_pool = _dot(dm.reshape(rows, POOL_WIDTH).astype(BF16), poolw_ref[...]) * pscale_ref[...]

    qx3 = (proj[:, OFF_X:OFF_X + XATTN_WIDTH] * (XATTN_DH ** -0.5)).reshape(gs, sl_len, XATTN_WIDTH)
    head_of_lane = lax.broadcasted_iota(jnp.int32, qx3.shape, 2) // XATTN_DH
    qs3 = jnp.concatenate([jnp.where(head_of_lane == hd, qx3, 0.0) for hd in range(XATTN_HEADS)],
                          axis=1).astype(BF16)
    hrows = XATTN_HEADS * sl_len
    scores = jnp.concatenate([_dot_nt(qs3[s], mk_ref[s].astype(BF16)) for s in range(gs)], axis=0)
    p = _softmax_rows(scores).astype(BF16)
    o4 = jnp.concatenate([_dot(p[s * hrows:(s + 1) * hrows], mv_ref[s].astype(BF16)) for s in range(gs)], axis=0)
    o4 = o4.reshape(gs, XATTN_HEADS, sl_len, XATTN_WIDTH)
    o_x3 = jnp.zeros(qx3.shape, F32)
    for hd in range(XATTN_HEADS):
        o_x3 = jnp.where(head_of_lane == hd, o4[:, hd], o_x3)
    o_x = o_x3.reshape(rows, XATTN_WIDTH)

    lb = _forget_lower_bound(lbl_ref[...])
    qf, k, log_f = _hgrn_gates(proj, lb)
    v = proj[:, OFF_I:OFF_I + HGRN_WIDTH]
    gg = proj[:, OFF_G:OFF_G + HGRN_WIDTH]
    gate = gg * jax.nn.sigmoid(gg) * onorm_ref[...]
    a = _segment_cumsum(log_f, sl_len)
    a3 = a.reshape(gs, sl_len, HGRN_WIDTH)
    a_end = jnp.broadcast_to(a3[:, sl_len - 1:sl_len, :], a3.shape).reshape(rows, HGRN_WIDTH)
    q_in = (qf * jnp.exp(a)).astype(BF16)
    k_out = (k * jnp.exp(a_end - a)).astype(BF16)
    decay = jnp.exp(a_end)
    v_b = v.astype(BF16)
    qk = qf * k
    factors = _level_factors(a, qf, k, log_f, rows, sl_len)
    heads = [slice(hd * HGRN_DK, (hd + 1) * HGRN_DK) for hd in range(HGRN_HEADS)]
    inter = [jnp.concatenate([_dot(q_in[r, sl], s0_ref[s, hd].astype(BF16)) for s, r in enumerate(seqs)], axis=0)
             for hd, sl in enumerate(heads)]
    o_heads = []
    for hd, sl in enumerate(heads):
        p_h = _intra_scores(factors, hd, rows, sl_len)
        o = _dot(p_h.astype(BF16), v_b[:, sl]) + inter[hd]
        o = o + jnp.sum(qk[:, sl], axis=-1, keepdims=True) * v[:, sl]
        o_heads.append(_head_norm_gate(o, gate[:, sl]))
    updates = [[_dot_tn(k_out[r, sl], v_b[r, sl]) for sl in heads] for r in seqs]
    for s, r in enumerate(seqs):
        for hd, sl in enumerate(heads):
            decay_cols = jnp.broadcast_to(decay[r, sl][sl_len - 1:sl_len, :], (HGRN_DV, HGRN_DK)).T
            ns_ref[s, hd] = decay_cols * s0_ref[s, hd] + updates[s][hd]

    mixed = jnp.concatenate([o_pool] + o_heads + [o_x], axis=-1).astype(BF16)
    x2_ref[...] = (x + _dot(mixed, wout_ref[...])).reshape(gs, sl_len, D_MODEL)


def _sample_mixer_call(x, hist, s0, mk, mv, mixer_w, gs):
    b, l, d = x.shape
    ln1, w_in, pool_wbd, pool_scale, lb_logits, onorm, w_out = mixer_w
    grid = (b // gs,)
    blk = pl.BlockSpec((gs, l, d), lambda i: (i, 0, 0))
    histb = pl.BlockSpec((gs, POOL_HIST, POOL_WIDTH), lambda i: (i, 0, 0))
    sb = pl.BlockSpec((gs, HGRN_HEADS, HGRN_DK, HGRN_DV), lambda i: (i, 0, 0, 0))
    mem = pl.BlockSpec((gs, N_MEM, XATTN_WIDTH), lambda i: (i, 0, 0))
    return pl.pallas_call(
        functools.partial(_sample_mixer_body, gs=gs, sl_len=l),
        grid=grid,
        in_specs=[blk, histb, sb, mem, mem,
                  _const_spec((1, d)), _const_spec((d, D_IN)), _const_spec((POOL_WIDTH, POOL_WIDTH)),
                  _const_spec((1, POOL_WIDTH)), _const_spec(lb_logits.shape), _const_spec((1, HGRN_WIDTH)),
                  _const_spec((d, d))],
        out_specs=[blk, histb, sb],
        out_shape=[jax.ShapeDtypeStruct((b, l, d), F32),
                   jax.ShapeDtypeStruct((b, POOL_HIST, POOL_WIDTH), F32),
                   jax.ShapeDtypeStruct((b, HGRN_HEADS, HGRN_DK, HGRN_DV), F32)],
        scratch_shapes=[pltpu.VMEM((gs, POOL_PAD + l, POOL_WIDTH), F32)],
        compiler_params=pltpu.CompilerParams(dimension_semantics=("arbitrary",),
                                             vmem_limit_bytes=VMEM_LIMIT_BYTES),
        name="sample_mixer",
    )(x, hist, s0, mk, mv, ln1, w_in, pool_wbd, pool_scale, lb_logits, onorm, w_out)


def _sample_ffn_body(x_ref, chist_ref, ln2_ref, wup_ref, cw_ref, cb_ref, wdown_ref, lnf_ref, y_ref, nconv_ref,
                     *, gs, sl_len):
    rows = gs * sl_len
    x = x_ref[...].reshape(rows, D_MODEL)
    h = _rmsnorm(x, ln2_ref[...]).astype(BF16)
    ab = _dot(h, wup_ref[...])
    a = ab[:, :D_FF]
    ridx = lax.broadcasted_iota(jnp.int32, (rows, D_FF), 0) % sl_len
    hist = chist_ref[...]
    h1 = jnp.broadcast_to(hist[:, 1:2, :], (gs, sl_len, D_FF)).reshape(rows, D_FF)
    h0 = jnp.broadcast_to(hist[:, 0:1, :], (gs, sl_len, D_FF)).reshape(rows, D_FF)
    a_m1 = jnp.where(ridx >= 1, pltpu.roll(a, 1, 0), h1)
    a_m2 = jnp.where(ridx >= 2, pltpu.roll(a, 2, 0), jnp.where(ridx == 1, h1, h0))
    conv = cb_ref[...] + cw_ref[0:1, :] * a_m2 + cw_ref[1:2, :] * a_m1 + cw_ref[2:3, :] * a
    act = _gelu_tanh(conv) * ab[:, D_FF:]
    nconv_ref[...] = a.reshape(gs, sl_len, D_FF)[:, sl_len - (CONV_W - 1):, :]
    y_ref[...] = _ffn_tail(x, act, wdown_ref, lnf_ref).reshape(gs, sl_len, D_MODEL)


def _sample_ffn_call(x, chist, ffn_w, gs):
    b, l, d = x.shape
    ln2, w_up, conv_w, conv_b, w_down, lnf = ffn_w
    blk = pl.BlockSpec((gs, l, d), lambda i: (i, 0, 0))
    cblk = pl.BlockSpec((gs, CONV_W - 1, D_FF), lambda i: (i, 0, 0))
    return pl.pallas_call(
        functools.partial(_sample_ffn_body, gs=gs, sl_len=l),
        grid=(b // gs,),
        in_specs=[blk, cblk, _const_spec((1, d)), _const_spec((d, 2 * D_FF)), _const_spec((CONV_W, D_FF)),
                  _const_spec((1, D_FF)), _const_spec((D_FF, d)), _const_spec((1, d))],
        out_specs=[blk, cblk],
        out_shape=[jax.ShapeDtypeStruct((b, l, d), F32),
                   jax.ShapeDtypeStruct((b, CONV_W - 1, D_FF), F32)],
        compiler_params=pltpu.CompilerParams(dimension_semantics=("arbitrary",),
                                             vmem_limit_bytes=VMEM_LIMIT_BYTES),
        name="sample_ffn",
    )(x, chist, ln2, w_up, conv_w, conv_b, w_down, lnf)


def _block_diag(pool_w):
    n = pool_w.shape[0]
    out = jnp.zeros((n * POOL_GROUP, n * POOL_GROUP), pool_w.dtype)
    for g in range(n):
        out = lax.dynamic_update_slice(out, pool_w[g], (g * POOL_GROUP, g * POOL_GROUP))
    return out


def _layer(x_prompt, x_sample, mem_prompt, state_pool, state_hgrn, state_conv, cache_mem_k, cache_mem_v,
           ln1_g, w_in, pool_w, pool_scale, hgrn_lb_logits, hgrn_onorm_g, mem_norm_g, w_mem_kv, w_out,
           ln2_g, w_up, conv_w, conv_b, w_down, lnf_g, *, prompt_tb, mixer_gs, ffn_gs):
    row = lambda a: a.reshape(1, -1)
    w_in_b, w_out_b, w_up_b, w_down_b = (w.astype(BF16) for w in (w_in, w_out, w_up, w_down))
    pool_wbd = _block_diag(pool_w).astype(BF16)
    mixer_w = (row(ln1_g), w_in_b, pool_wbd, row(pool_scale), hgrn_lb_logits, row(hgrn_onorm_g), w_out_b)
    ffn_w = (row(ln2_g), w_up_b, conv_w, row(conv_b), w_down_b, row(lnf_g))

    mk, mv = _memkv_call(mem_prompt, row(mem_norm_g), w_mem_kv.astype(BF16))
    y_prompt, new_pool_p, new_s_p, new_conv_p = _prompt_call(x_prompt, mk, mv, mixer_w, ffn_w, tb=prompt_tb)

    nb = x_sample.shape[0]
    smk = cache_mem_k.reshape(nb, N_MEM, XATTN_WIDTH)
    smv = cache_mem_v.reshape(nb, N_MEM, XATTN_WIDTH)
    xs, new_pool_s, new_s_s = _sample_mixer_call(x_sample, state_pool, state_hgrn, smk, smv, mixer_w, gs=mixer_gs)
    y_sample, new_conv_s = _sample_ffn_call(xs, state_conv, ffn_w, gs=ffn_gs)
    bp = x_prompt.shape[0]
    heads = (bp, N_MEM, XATTN_HEADS, XATTN_DH)
    return (y_prompt, y_sample, new_pool_p, new_s_p, new_conv_p, mk.reshape(heads), mv.reshape(heads),
            new_pool_s, new_s_s, new_conv_s)


def kernel(x_prompt, x_sample, mem_prompt, state_pool, state_hgrn, state_conv, cache_mem_k, cache_mem_v,
           ln1_g, w_in, pool_w, pool_scale, hgrn_lb_logits, hgrn_onorm_g, mem_norm_g, w_mem_kv, w_out,
           ln2_g, w_up, conv_w, conv_b, w_down, lnf_g):
    assert w_in.shape[0] == 1, "one layer"
    outs = _layer(x_prompt, x_sample, mem_prompt, state_pool[0], state_hgrn[0], state_conv[0],
                  cache_mem_k[0], cache_mem_v[0], ln1_g[0], w_in[0], pool_w[0], pool_scale[0], hgrn_lb_logits,
                  hgrn_onorm_g[0], mem_norm_g[0], w_mem_kv[0], w_out[0], ln2_g[0], w_up[0], conv_w[0], conv_b[0],
                  w_down[0], lnf_g, prompt_tb=256, mixer_gs=16, ffn_gs=32)
    y_prompt, y_sample = outs[0], outs[1]
    return (y_prompt, y_sample) + tuple(o[None] for o in outs[2:])
```

```python
import functools

import jax
import jax.numpy as jnp
from jax import lax
from jax.experimental import pallas as pl
from jax.experimental.pallas import tpu as pltpu

F32 = jnp.float32
BF16 = jnp.bfloat16

D_MODEL = 1024
POOL_WIDTH = 256
POOL_GROUP = 64
POOL_HIST = 15
HGRN_WIDTH = 512
HGRN_HEADS = 4
HGRN_DK = 128
HGRN_DV = 128
XATTN_WIDTH = 256
XATTN_HEADS = 4
XATTN_DH = 64
N_MEM = 256
D_FF = 2816
CONV_W = 3
EPS = 1e-6
PAST_LEN = 16384
D_IN = POOL_WIDTH + 4 * HGRN_WIDTH + XATTN_WIDTH
OFF_U, OFF_Q, OFF_F, OFF_I, OFF_G, OFF_X = 0, 256, 768, 1280, 1792, 2304

CHUNK = 64
POOL_PAD = 16
CONV_PAD = 8
LANES = 128
MEMKV_GROUP = 4
VMEM_LIMIT_BYTES = 56 * 1024 * 1024

_NT = (((1,), (1,)), ((), ()))
_TN = (((0,), (0,)), ((), ()))


def _dot(a, b):
    return jnp.dot(a, b, preferred_element_type=F32)


def _dot_nt(a, b):
    return lax.dot_general(a, b, _NT, preferred_element_type=F32)


def _dot_tn(a, b):
    return lax.dot_general(a, b, _TN, preferred_element_type=F32)


def _rmsnorm(x, g):
    return x * lax.rsqrt(jnp.mean(x * x, axis=-1, keepdims=True) + EPS) * g


def _const_spec(shape):
    nd = len(shape)
    return pl.BlockSpec(shape, lambda *_: (0,) * nd, pipeline_mode=pl.Buffered(1))


def _memkv_body(mem_ref, g_ref, w_ref, k_ref, v_ref):
    nb = mem_ref.shape[0]
    h = _rmsnorm(mem_ref[...].reshape(nb * N_MEM, D_MODEL), g_ref[...]).astype(BF16)
    kv = _dot(h, w_ref[...])
    k_ref[...] = kv[:, :XATTN_WIDTH].reshape(nb, N_MEM, XATTN_WIDTH)
    v_ref[...] = kv[:, XATTN_WIDTH:].reshape(nb, N_MEM, XATTN_WIDTH)


def _memkv_call(mem, g, w):
    b = mem.shape[0]
    nb = MEMKV_GROUP
    out = jax.ShapeDtypeStruct((b, N_MEM, XATTN_WIDTH), F32)
    return pl.pallas_call(
        _memkv_body,
        grid=(b // nb,),
        in_specs=[pl.BlockSpec((nb, N_MEM, D_MODEL), lambda i: (i, 0, 0)),
                  _const_spec((1, D_MODEL)),
                  _const_spec((D_MODEL, 2 * XATTN_WIDTH))],
        out_specs=[pl.BlockSpec((nb, N_MEM, XATTN_WIDTH), lambda i: (i, 0, 0))] * 2,
        out_shape=[out, out],
        compiler_params=pltpu.CompilerParams(dimension_semantics=("arbitrary",)),
        name="memkv",
    )(mem, g, w)


def _forget_lower_bound(logits):
    z = logits - jnp.max(logits, axis=0, keepdims=True)
    e = jnp.exp(z)
    return e[0:1, :] / jnp.sum(e, axis=0, keepdims=True)


def _hgrn_gates(proj, lb):
    fp = proj[:, OFF_F:OFF_F + HGRN_WIDTH]
    q = proj[:, OFF_Q:OFF_Q + HGRN_WIDTH]
    log_f = jnp.log2(lb + (1.0 - lb) * jax.nn.sigmoid(fp))
    k = (1.0 - lb) * jax.nn.sigmoid(-fp)
    qf = q * jax.nn.sigmoid(q)
    return qf, k, log_f


def _segment_cumsum(x, seq):
    ridx = lax.broadcasted_iota(jnp.int32, x.shape, 0) & (seq - 1)
    sh = 1
    while sh < seq:
        x = x + jnp.where(ridx >= sh, pltpu.roll(x, sh, 0), 0.0)
        sh *= 2
    return x


def _level_factor(a, qf, k, log_f, m, rows):
    n = a.shape[1]
    ridx = lax.broadcasted_iota(jnp.int32, (rows, n), 0)
    upper = (ridx & m) != 0
    if m == 1:
        d = jnp.where(upper, log_f, 0.0)
    else:
        if (2 * m) % 8 == 0:
            nb = rows // (2 * m)
            a3 = a.reshape(nb, 2 * m, n)
            ref = jnp.broadcast_to(a3[:, m - 1:m, :], (nb, 2 * m, n)).reshape(rows, n)
        else:
            a3 = a.reshape(rows // 8, 8, n)
            sub = lax.broadcasted_iota(jnp.int32, a3.shape, 1)
            ref = jnp.where(sub < 4,
                            jnp.broadcast_to(a3[:, 1:2, :], a3.shape),
                            jnp.broadcast_to(a3[:, 5:6, :], a3.shape)).reshape(rows, n)
        d = -jnp.abs(a - ref)
    return (jnp.where(upper, qf, k) * jnp.exp2(d)).astype(BF16)


def _level_factors(a, qf, k, log_f, rows, seq):
    out, m = [], seq // 2
    while m >= 1:
        out.append((m, _level_factor(a, qf, k, log_f, m, rows)))
        m //= 2
    return out


def _intra_scores(factors, head, rows, seq):
    sl = slice(head * HGRN_DK, (head + 1) * HGRN_DK)
    t = lax.broadcasted_iota(jnp.int32, (rows, rows), 0)
    s = lax.broadcasted_iota(jnp.int32, (rows, rows), 1)
    x = t ^ s
    products = [(m, _dot_nt(y[:, sl], y[:, sl])) for m, y in reversed(factors)]
    total = jnp.zeros((rows, rows), F32)
    for m, p in products:
        total = jnp.where(x >= m, p, total)
    return jnp.where((t > s) & (x < seq), total, 0.0)


def _head_norm_gate(o, gate):
    return o * lax.rsqrt(jnp.mean(o * o, axis=-1, keepdims=True) + EPS) * gate


def _softmax_rows(s):
    e = jnp.exp(s - jnp.max(s, axis=-1, keepdims=True))
    return e / jnp.sum(e, axis=-1, keepdims=True)


def _cross_attention(qx, mk, mv):
    rows = qx.shape[0]
    head_of_lane = lax.broadcasted_iota(jnp.int32, qx.shape, 1) // XATTN_DH
    qs = jnp.concatenate([jnp.where(head_of_lane == h, qx, 0.0) for h in range(XATTN_HEADS)], axis=0)
    p = _softmax_rows(_dot_nt(qs.astype(BF16), mk))
    o = _dot(p.astype(BF16), mv)
    out = jnp.zeros(qx.shape, F32)
    for h in range(XATTN_HEADS):
        out = jnp.where(head_of_lane == h, o[h * rows:(h + 1) * rows, :], out)
    return out


def _pool_means(ld, posf, shape):
    lane = lax.broadcasted_iota(jnp.int32, shape, len(shape) - 1)
    first = lane < POOL_GROUP
    u_lo, u_hi = ld(0, 0), ld(0, 1)
    t2 = u_lo + ld(1, 0)
    t4 = t2 + ld(2, 0) + ld(3, 0)
    t8 = u_hi
    for j in range(1, 8):
        t8 = t8 + ld(j, 1)
    t16 = t8
    for j in range(8, 16):
        t16 = t16 + ld(j, 1)
    cnt_lo = jnp.where(first, jnp.minimum(2.0, posf), jnp.minimum(4.0, posf))
    cnt_hi = jnp.where(first, jnp.minimum(8.0, posf), jnp.minimum(16.0, posf))
    lo = jnp.where(first, t2, t4) / cnt_lo - u_lo
    hi = jnp.where(first, t8, t16) / cnt_hi - u_hi
    return jnp.concatenate([lo, hi], axis=-1)


def _after(x, anchor, zero):
    r, n = anchor.shape
    s = jnp.sum(anchor.reshape(r // 8, 8, n), axis=0)
    c = s[:, :LANES]
    for i in range(1, n // LANES):
        c = c + s[:, i * LANES:(i + 1) * LANES]
    z = lax.bitcast_convert_type(lax.bitcast_convert_type(c[0:1, :], jnp.int32) & zero, F32)
    return jnp.concatenate([x[:, :LANES] + z, x[:, LANES:]], axis=1)


def _gelu_tanh(x):
    c = 0.7978845608028654
    half_x = 0.5 * x
    return half_x + half_x * jnp.tanh(x * (c + (0.044715 * c) * (x * x)))


def _ffn_tail(x, act, wdown_ref, lnf_ref):
    y = x + _dot(act.astype(BF16), wdown_ref[...])
    return _rmsnorm(y, lnf_ref[...])


def _prompt_body(x_ref, mk_ref, mv_ref, ln1_ref, win_ref, poolw_ref, pscale_ref, lbl_ref, onorm_ref,
                 wout_ref, ln2_ref, wup_ref, cw_ref, cb_ref, wdown_ref, lnf_ref, zero_ref,
                 y_ref, npool_ref, ns_ref, nconv_ref,
                 pbuf, st, abuf, x2s, *, tb, nt, nblk):
    g = pl.program_id(0)
    jm = jnp.minimum(g, nblk - 1) % nt
    jf = jnp.maximum(g - 1, 0) % nt

    @pl.when(g == 0)
    def _():
        x2s[...] = jnp.zeros(x2s.shape, F32)

    @pl.when(jm == 0)
    def _():
        pbuf[pl.ds(0, POOL_PAD), :] = jnp.zeros((POOL_PAD, POOL_WIDTH), F32)
        st[...] = jnp.zeros(st.shape, F32)

    @pl.when(jf == 0)
    def _():
        abuf[...] = jnp.zeros(abuf.shape, F32)

    x = x_ref[0]
    h = _rmsnorm(x, ln1_ref[...]).astype(BF16)
    proj = _dot(h, win_ref[...])

    x2 = x2s[...]
    h2 = _rmsnorm(x2, ln2_ref[...]).astype(BF16)
    ab = _dot(h2, wup_ref[...])
    a = ab[:, :D_FF]
    hist = abuf[...]
    sub = lax.broadcasted_iota(jnp.int32, (CONV_PAD, D_FF), 0)
    r1 = pltpu.roll(a, 1, 0)
    r2 = pltpu.roll(a, 2, 0)
    h1 = jnp.broadcast_to(hist[CONV_PAD - 1:CONV_PAD, :], (CONV_PAD, D_FF))
    h0 = jnp.broadcast_to(hist[CONV_PAD - 2:CONV_PAD - 1, :], (CONV_PAD, D_FF))
    a_m1 = jnp.concatenate([jnp.where(sub >= 1, r1[:CONV_PAD], h1), r1[CONV_PAD:]], axis=0)
    a_m2 = jnp.concatenate([jnp.where(sub >= 2, r2[:CONV_PAD], jnp.where(sub == 1, h1, h0)), r2[CONV_PAD:]], axis=0)
    conv = cb_ref[...] + cw_ref[0:1, :] * a_m2 + cw_ref[1:2, :] * a_m1 + cw_ref[2:3, :] * a
    act = _gelu_tanh(conv) * ab[:, D_FF:]
    abuf[...] = a[tb - CONV_PAD:, :]
    act = _after(act, proj, zero_ref[...]).astype(BF16)
    y_ref[0] = _rmsnorm(x2 + _dot(act, wdown_ref[...]), lnf_ref[...])


    pbuf[pl.ds(POOL_PAD, tb), :] = proj[:, OFF_U:OFF_U + POOL_WIDTH]
    posf = (jm * tb + 1 + lax.broadcasted_iota(jnp.int32, (tb, 1), 0)).astype(F32)
    dm = _pool_means(lambda j, half: pbuf[pl.ds(POOL_PAD - j, tb), pl.ds(LANES * half, LANES)], posf, (tb, LANES))
    o_pool = _dot(dm.astype(BF16), poolw_ref[...]) * pscale_ref[...]
    pbuf[pl.ds(0, POOL_PAD), :] = pbuf[pl.ds(tb, POOL_PAD), :]

    qx = proj[:, OFF_X:OFF_X + XATTN_WIDTH] * (XATTN_DH ** -0.5)
    o_x = _cross_attention(qx, mk_ref[0].astype(BF16), mv_ref[0].astype(BF16))

    lb = _forget_lower_bound(lbl_ref[...])
    qf, k, log_f = _hgrn_gates(proj, lb)
    v = proj[:, OFF_I:OFF_I + HGRN_WIDTH]
    gg = proj[:, OFF_G:OFF_G + HGRN_WIDTH]
    gate = gg * jax.nn.sigmoid(gg) * onorm_ref[...]
    a_all = _segment_cumsum(log_f, CHUNK)
    states = [st[hd] for hd in range(HGRN_HEADS)]
    o_rows = []
    for c in range(tb // CHUNK):
        rs = slice(c * CHUNK, (c + 1) * CHUNK)
        qf_c, k_c, lf_c, v_c, a = qf[rs], k[rs], log_f[rs], v[rs], a_all[rs]
        a_end = a[CHUNK - 1:CHUNK, :]
        q_in = (qf_c * jnp.exp2(a)).astype(BF16)
        k_out = (k_c * jnp.exp2(a_end - a)).astype(BF16)
        decay = jnp.exp2(a_end)
        v_b = v_c.astype(BF16)
        qk = qf_c * k_c
        factors = _level_factors(a, qf_c, k_c, lf_c, CHUNK, CHUNK)
        heads = [slice(hd * HGRN_DK, (hd + 1) * HGRN_DK) for hd in range(HGRN_HEADS)]
        inter = [_dot_nt(q_in[:, sl], states[hd].astype(BF16)) for hd, sl in enumerate(heads)]
        update = [_dot_tn(v_b[:, sl], k_out[:, sl]) for sl in heads]
        scores = [_intra_scores(factors, hd, CHUNK, CHUNK).astype(BF16) for hd in range(HGRN_HEADS)]
        intra = [_dot(scores[hd], v_b[:, sl]) for hd, sl in enumerate(heads)]
        o_heads = []
        for hd, sl in enumerate(heads):
            o = intra[hd] + inter[hd] + jnp.sum(qk[:, sl], axis=-1, keepdims=True) * v_c[:, sl]
            states[hd] = states[hd] * decay[:, sl] + update[hd]
            o_heads.append(_head_norm_gate(o, gate[rs, sl]))
        o_rows.append(jnp.concatenate(o_heads, axis=-1))
    for hd in range(HGRN_HEADS):
        st[hd] = states[hd]
    o_hgrn = jnp.concatenate(o_rows, axis=0)

    mixed = jnp.concatenate([o_pool, o_hgrn, o_x], axis=-1).astype(BF16)
    x2s[...] = x + _dot(mixed, wout_ref[...])

    @pl.when(jnp.logical_and(jm == nt - 1, g < nblk))
    def _():
        npool_ref[0] = pbuf[pl.ds(1, POOL_HIST), :]
        for hd in range(HGRN_HEADS):
            ns_ref[0, hd] = st[hd].T

    @pl.when(jnp.logical_and(jf == nt - 1, g >= 1))
    def _():
        nconv_ref[0] = abuf[pl.ds(CONV_PAD - (CONV_W - 1), CONV_W - 1), :]


def _prompt_call(x, mk, mv, mixer_w, ffn_w, tb):
    b, l, d = x.shape
    nt = l // tb
    nblk = b * nt
    ln1, w_in, pool_wbd, pool_scale, lb_logits, onorm, w_out = mixer_w
    ln2, w_up, conv_w, conv_b, w_down, lnf = ffn_w

    def mixer_blk(g):
        return jnp.minimum(g, nblk - 1)

    def ffn_blk(g):
        return jnp.maximum(g - 1, 0)

    x_spec = pl.BlockSpec((1, tb, d), lambda g: (mixer_blk(g) // nt, mixer_blk(g) % nt, 0))
    mem = pl.BlockSpec((1, N_MEM, XATTN_WIDTH), lambda g: (mixer_blk(g) // nt, 0, 0))
    y_spec = pl.BlockSpec((1, tb, d), lambda g: (ffn_blk(g) // nt, ffn_blk(g) % nt, 0))
    return pl.pallas_call(
        functools.partial(_prompt_body, tb=tb, nt=nt, nblk=nblk),
        grid=(nblk + 1,),
        in_specs=[x_spec, mem, mem,
                  _const_spec((1, d)), _const_spec((d, D_IN)), _const_spec((POOL_WIDTH, POOL_WIDTH)),
                  _const_spec((1, POOL_WIDTH)), _const_spec(lb_logits.shape), _const_spec((1, HGRN_WIDTH)),
                  _const_spec((d, d)),
                  _const_spec((1, d)), _const_spec((d, 2 * D_FF)), _const_spec((CONV_W, D_FF)),
                  _const_spec((1, D_FF)), _const_spec((D_FF, d)), _const_spec((1, d)), _const_spec((1, LANES))],
        out_specs=[y_spec,
                   pl.BlockSpec((1, POOL_HIST, POOL_WIDTH), lambda g: (mixer_blk(g) // nt, 0, 0)),
                   pl.BlockSpec((1, HGRN_HEADS, HGRN_DK, HGRN_DV), lambda g: (mixer_blk(g) // nt, 0, 0, 0)),
                   pl.BlockSpec((1, CONV_W - 1, D_FF), lambda g: (ffn_blk(g) // nt, 0, 0))],
        out_shape=[jax.ShapeDtypeStruct((b, l, d), F32),
                   jax.ShapeDtypeStruct((b, POOL_HIST, POOL_WIDTH), F32),
                   jax.ShapeDtypeStruct((b, HGRN_HEADS, HGRN_DK, HGRN_DV), F32),
                   jax.ShapeDtypeStruct((b, CONV_W - 1, D_FF), F32)],
        scratch_shapes=[pltpu.VMEM((POOL_PAD + tb, POOL_WIDTH), F32),
                        pltpu.VMEM((HGRN_HEADS, HGRN_DV, HGRN_DK), F32),
                        pltpu.VMEM((CONV_PAD, D_FF), F32),
                        pltpu.VMEM((tb, d), F32)],
        compiler_params=pltpu.CompilerParams(dimension_semantics=("arbitrary",),
                                             vmem_limit_bytes=VMEM_LIMIT_BYTES),
        name="prompt_layer",
    )(x, mk, mv, ln1, w_in, pool_wbd, pool_scale, lb_logits, onorm, w_out,
      ln2, w_up, conv_w, conv_b, w_down, lnf, jnp.zeros((1, LANES), jnp.int32))


def _sample_mixer_body(x_ref, hist_ref, s0_ref, mk_ref, mv_ref, ln1_ref, win_ref, poolw_ref, pscale_ref, lbl_ref,
                       onorm_ref, wout_ref, x2_ref, npool_ref, ns_ref, pbuf, *, gs, sl_len):
    rows = gs * sl_len
    seqs = [slice(s * sl_len, (s + 1) * sl_len) for s in range(gs)]
    x = x_ref[...].reshape(rows, D_MODEL)
    h = _rmsnorm(x, ln1_ref[...]).astype(BF16)
    proj = _dot(h, win_ref[...])

    pbuf[:, pl.ds(1, POOL_HIST), :] = hist_ref[...]
    pbuf[:, pl.ds(POOL_PAD, sl_len), :] = proj[:, OFF_U:OFF_U + POOL_WIDTH].reshape(gs, sl_len, POOL_WIDTH)
    posf = (PAST_LEN + 1 + lax.broadcasted_iota(jnp.int32, (1, sl_len, 1), 1)).astype(F32)
    dm = _pool_means(lambda j, half: pbuf[:, pl.ds(POOL_PAD - j, sl_len), pl.ds(LANES * half, LANES)],
                     posf, (gs, sl_len, LANES))
    npool_ref[...] = pbuf[:, pl.ds(sl_len + 1, POOL_HIST), :]
    o_pool = _dot(dm.reshape(rows, POOL_WIDTH).astype(BF16), poolw_ref[...]) * pscale_ref[...]

    qx3 = (proj[:, OFF_X:OFF_X + XATTN_WIDTH] * (XATTN_DH ** -0.5)).reshape(gs, sl_len, XATTN_WIDTH)
    head_of_lane = lax.broadcasted_iota(jnp.int32, qx3.shape, 2) // XATTN_DH
    qs3 = jnp.concatenate([jnp.where(head_of_lane == hd, qx3, 0.0) for hd in range(XATTN_HEADS)],
                          axis=1).astype(BF16)
    hrows = XATTN_HEADS * sl_len
    scores = jnp.concatenate([_dot_nt(qs3[s], mk_ref[s].astype(BF16)) for s in range(gs)], axis=0)
    p = _softmax_rows(scores).astype(BF16)
    o4 = jnp.concatenate([_dot(p[s * hrows:(s + 1) * hrows], mv_ref[s].astype(BF16)) for s in range(gs)], axis=0)
    o4 = o4.reshape(gs, XATTN_HEADS, sl_len, XATTN_WIDTH)
    o_x3 = jnp.zeros(qx3.shape, F32)
    for hd in range(XATTN_HEADS):
        o_x3 = jnp.where(head_of_lane == hd, o4[:, hd], o_x3)
    o_x = o_x3.reshape(rows, XATTN_WIDTH)

    lb = _forget_lower_bound(lbl_ref[...])
    qf, k, log_f = _hgrn_gates(proj, lb)
    v = proj[:, OFF_I:OFF_I + HGRN_WIDTH]
    gg = proj[:, OFF_G:OFF_G + HGRN_WIDTH]
    gate = gg * jax.nn.sigmoid(gg) * onorm_ref[...]
    a = _segment_cumsum(log_f, sl_len)
    a3 = a.reshape(gs, sl_len, HGRN_WIDTH)
    a_end = jnp.broadcast_to(a3[:, sl_len - 1:sl_len, :], a3.shape).reshape(rows, HGRN_WIDTH)
    q_in = (qf * jnp.exp2(a)).astype(BF16)
    k_out = (k * jnp.exp2(a_end - a)).astype(BF16)
    decay = jnp.exp2(a_end)
    v_b = v.astype(BF16)
    qk = qf * k
    factors = _level_factors(a, qf, k, log_f, rows, sl_len)
    heads = [slice(hd * HGRN_DK, (hd + 1) * HGRN_DK) for hd in range(HGRN_HEADS)]
    inter = [jnp.concatenate([_dot(q_in[r, sl], s0_ref[s, hd].astype(BF16)) for s, r in enumerate(seqs)], axis=0)
             for hd, sl in enumerate(heads)]
    o_heads = []
    for hd, sl in enumerate(heads):
        p_h = _intra_scores(factors, hd, rows, sl_len)
        o = _dot(p_h.astype(BF16), v_b[:, sl]) + inter[hd]
        o = o + jnp.sum(qk[:, sl], axis=-1, keepdims=True) * v[:, sl]
        o_heads.append(_head_norm_gate(o, gate[:, sl]))
    updates = [[_dot_tn(k_out[r, sl], v_b[r, sl]) for sl in heads] for r in seqs]
    for s, r in enumerate(seqs):
        for hd, sl in enumerate(heads):
            decay_cols = jnp.broadcast_to(decay[r, sl][sl_len - 1:sl_len, :], (HGRN_DV, HGRN_DK)).T
            ns_ref[s, hd] = decay_cols * s0_ref[s, hd] + updates[s][hd]

    mixed = jnp.concatenate([o_pool] + o_heads + [o_x], axis=-1).astype(BF16)
    x2_ref[...] = (x + _dot(mixed, wout_ref[...])).reshape(gs, sl_len, D_MODEL)


def _sample_mixer_call(x, hist, s0, mk, mv, mixer_w, gs):
    b, l, d = x.shape
    ln1, w_in, pool_wbd, pool_scale, lb_logits, onorm, w_out = mixer_w
    grid = (b // gs,)
    blk = pl.BlockSpec((gs, l, d), lambda i: (i, 0, 0))
    histb = pl.BlockSpec((gs, POOL_HIST, POOL_WIDTH), lambda i: (i, 0, 0))
    sb = pl.BlockSpec((gs, HGRN_HEADS, HGRN_DK, HGRN_DV), lambda i: (i, 0, 0, 0))
    mem = pl.BlockSpec((gs, N_MEM, XATTN_WIDTH), lambda i: (i, 0, 0))
    return pl.pallas_call(
        functools.partial(_sample_mixer_body, gs=gs, sl_len=l),
        grid=grid,
        in_specs=[blk, histb, sb, mem, mem,
                  _const_spec((1, d)), _const_spec((d, D_IN)), _const_spec((POOL_WIDTH, POOL_WIDTH)),
                  _const_spec((1, POOL_WIDTH)), _const_spec(lb_logits.shape), _const_spec((1, HGRN_WIDTH)),
                  _const_spec((d, d))],
        out_specs=[blk, histb, sb],
        out_shape=[jax.ShapeDtypeStruct((b, l, d), F32),
                   jax.ShapeDtypeStruct((b, POOL_HIST, POOL_WIDTH), F32),
                   jax.ShapeDtypeStruct((b, HGRN_HEADS, HGRN_DK, HGRN_DV), F32)],
        scratch_shapes=[pltpu.VMEM((gs, POOL_PAD + l, POOL_WIDTH), F32)],
        compiler_params=pltpu.CompilerParams(dimension_semantics=("arbitrary",),
                                             vmem_limit_bytes=VMEM_LIMIT_BYTES),
        name="sample_mixer",
    )(x, hist, s0, mk, mv, ln1, w_in, pool_wbd, pool_scale, lb_logits, onorm, w_out)


def _sample_ffn_body(x_ref, chist_ref, ln2_ref, wup_ref, cw_ref, cb_ref, wdown_ref, lnf_ref, y_ref, nconv_ref,
                     *, gs, sl_len):
    rows = gs * sl_len
    x = x_ref[...].reshape(rows, D_MODEL)
    h = _rmsnorm(x, ln2_ref[...]).astype(BF16)
    ab = _dot(h, wup_ref[...])
    a = ab[:, :D_FF]
    ridx = lax.broadcasted_iota(jnp.int32, (rows, D_FF), 0) % sl_len
    hist = chist_ref[...]
    h1 = jnp.broadcast_to(hist[:, 1:2, :], (gs, sl_len, D_FF)).reshape(rows, D_FF)
    h0 = jnp.broadcast_to(hist[:, 0:1, :], (gs, sl_len, D_FF)).reshape(rows, D_FF)
    a_m1 = jnp.where(ridx >= 1, pltpu.roll(a, 1, 0), h1)
    a_m2 = jnp.where(ridx >= 2, pltpu.roll(a, 2, 0), jnp.where(ridx == 1, h1, h0))
    conv = cb_ref[...] + cw_ref[0:1, :] * a_m2 + cw_ref[1:2, :] * a_m1 + cw_ref[2:3, :] * a
    act = _gelu_tanh(conv) * ab[:, D_FF:]
    nconv_ref[...] = a.reshape(gs, sl_len, D_FF)[:, sl_len - (CONV_W - 1):, :]
    y_ref[...] = _ffn_tail(x, act, wdown_ref, lnf_ref).reshape(gs, sl_len, D_MODEL)


def _sample_ffn_call(x, chist, ffn_w, gs):
    b, l, d = x.shape
    ln2, w_up, conv_w, conv_b, w_down, lnf = ffn_w
    blk = pl.BlockSpec((gs, l, d), lambda i: (i, 0, 0))
    cblk = pl.BlockSpec((gs, CONV_W - 1, D_FF), lambda i: (i, 0, 0))
    return pl.pallas_call(
        functools.partial(_sample_ffn_body, gs=gs, sl_len=l),
        grid=(b // gs,),
        in_specs=[blk, cblk, _const_spec((1, d)), _const_spec((d, 2 * D_FF)), _const_spec((CONV_W, D_FF)),
                  _const_spec((1, D_FF)), _const_spec((D_FF, d)), _const_spec((1, d))],
        out_specs=[blk, cblk],
        out_shape=[jax.ShapeDtypeStruct((b, l, d), F32),
                   jax.ShapeDtypeStruct((b, CONV_W - 1, D_FF), F32)],
        compiler_params=pltpu.CompilerParams(dimension_semantics=("arbitrary",),
                                             vmem_limit_bytes=VMEM_LIMIT_BYTES),
        name="sample_ffn",
    )(x, chist, ln2, w_up, conv_w, conv_b, w_down, lnf)


def _block_diag(pool_w):
    n = pool_w.shape[0]
    out = jnp.zeros((n * POOL_GROUP, n * POOL_GROUP), pool_w.dtype)
    for g in range(n):
        out = lax.dynamic_update_slice(out, pool_w[g], (g * POOL_GROUP, g * POOL_GROUP))
    return out


def _layer(x_prompt, x_sample, mem_prompt, state_pool, state_hgrn, state_conv, cache_mem_k, cache_mem_v,
           ln1_g, w_in, pool_w, pool_scale, hgrn_lb_logits, hgrn_onorm_g, mem_norm_g, w_mem_kv, w_out,
           ln2_g, w_up, conv_w, conv_b, w_down, lnf_g, *, prompt_tb, mixer_gs, ffn_gs):
    row = lambda a: a.reshape(1, -1)
    w_in_b, w_out_b, w_up_b, w_down_b = (w.astype(BF16) for w in (w_in, w_out, w_up, w_down))
    pool_wbd = _block_diag(pool_w).astype(BF16)
    mixer_w = (row(ln1_g), w_in_b, pool_wbd, row(pool_scale), hgrn_lb_logits, row(hgrn_onorm_g), w_out_b)
    ffn_w = (row(ln2_g), w_up_b, conv_w, row(conv_b), w_down_b, row(lnf_g))

    mk, mv = _memkv_call(mem_prompt, row(mem_norm_g), w_mem_kv.astype(BF16))
    y_prompt, new_pool_p, new_s_p, new_conv_p = _prompt_call(x_prompt, mk, mv, mixer_w, ffn_w, tb=prompt_tb)

    nb = x_sample.shape[0]
    smk = cache_mem_k.reshape(nb, N_MEM, XATTN_WIDTH)
    smv = cache_mem_v.reshape(nb, N_MEM, XATTN_WIDTH)
    xs, new_pool_s, new_s_s = _sample_mixer_call(x_sample, state_pool, state_hgrn, smk, smv, mixer_w, gs=mixer_gs)
    y_sample, new_conv_s = _sample_ffn_call(xs, state_conv, ffn_w, gs=ffn_gs)
    bp = x_prompt.shape[0]
    heads = (bp, N_MEM, XATTN_HEADS, XATTN_DH)
    return (y_prompt, y_sample, new_pool_p, new_s_p, new_conv_p, mk.reshape(heads), mv.reshape(heads),
            new_pool_s, new_s_s, new_conv_s)


def kernel(x_prompt, x_sample, mem_prompt, state_pool, state_hgrn, state_conv, cache_mem_k, cache_mem_v,
           ln1_g, w_in, pool_w, pool_scale, hgrn_lb_logits, hgrn_onorm_g, mem_norm_g, w_mem_kv, w_out,
           ln2_g, w_up, conv_w, conv_b, w_down, lnf_g):
    assert w_in.shape[0] == 1, "one layer"
    outs = _layer(x_prompt, x_sample, mem_prompt, state_pool[0], state_hgrn[0], state_conv[0],
                  cache_mem_k[0], cache_mem_v[0], ln1_g[0], w_in[0], pool_w[0], pool_scale[0], hgrn_lb_logits,
                  hgrn_onorm_g[0], mem_norm_g[0], w_mem_kv[0], w_out[0], ln2_g[0], w_up[0], conv_w[0], conv_b[0],
                  w_down[0], lnf_g, prompt_tb=256, mixer_gs=16, ffn_gs=32)
    y_prompt, y_sample = outs[0], outs[1]
    return (y_prompt, y_sample) + tuple(o[None] for o in outs[2:])
```

```python
import functools

import jax
import jax.numpy as jnp
from jax import lax
from jax.experimental import pallas as pl
from jax.experimental.pallas import tpu as pltpu

F32 = jnp.float32
BF16 = jnp.bfloat16

D_MODEL = 1024
POOL_WIDTH = 256
POOL_GROUP = 64
POOL_HIST = 15
HGRN_WIDTH = 512
HGRN_HEADS = 4
HGRN_DK = 128
HGRN_DV = 128
XATTN_WIDTH = 256
XATTN_HEADS = 4
XATTN_DH = 64
N_MEM = 256
D_FF = 2816
CONV_W = 3
EPS = 1e-6
PAST_LEN = 16384
D_IN = POOL_WIDTH + 4 * HGRN_WIDTH + XATTN_WIDTH
OFF_U, OFF_Q, OFF_F, OFF_I, OFF_G, OFF_X = 0, 256, 768, 1280, 1792, 2304

CHUNK = 64
POOL_PAD = 16
CONV_PAD = 8
LANES = 128
MEMKV_GROUP = 4
VMEM_LIMIT_BYTES = 56 * 1024 * 1024

_NT = (((1,), (1,)), ((), ()))
_TN = (((0,), (0,)), ((), ()))


def _dot(a, b):
    return jnp.dot(a, b, preferred_element_type=F32)


def _dot_nt(a, b):
    return lax.dot_general(a, b, _NT, preferred_element_type=F32)


def _dot_tn(a, b):
    return lax.dot_general(a, b, _TN, preferred_element_type=F32)


def _rmsnorm(x, g):
    return x * lax.rsqrt(jnp.mean(x * x, axis=-1, keepdims=True) + EPS) * g


def _const_spec(shape):
    nd = len(shape)
    return pl.BlockSpec(shape, lambda *_: (0,) * nd, pipeline_mode=pl.Buffered(1))


def _memkv_body(mem_ref, g_ref, w_ref, k_ref, v_ref):
    nb = mem_ref.shape[0]
    h = _rmsnorm(mem_ref[...].reshape(nb * N_MEM, D_MODEL), g_ref[...]).astype(BF16)
    kv = _dot(h, w_ref[...])
    k_ref[...] = kv[:, :XATTN_WIDTH].reshape(nb, N_MEM, XATTN_WIDTH)
    v_ref[...] = kv[:, XATTN_WIDTH:].reshape(nb, N_MEM, XATTN_WIDTH)


def _memkv_call(mem, g, w):
    b = mem.shape[0]
    nb = MEMKV_GROUP
    out = jax.ShapeDtypeStruct((b, N_MEM, XATTN_WIDTH), F32)
    return pl.pallas_call(
        _memkv_body,
        grid=(b // nb,),
        in_specs=[pl.BlockSpec((nb, N_MEM, D_MODEL), lambda i: (i, 0, 0)),
                  _const_spec((1, D_MODEL)),
                  _const_spec((D_MODEL, 2 * XATTN_WIDTH))],
        out_specs=[pl.BlockSpec((nb, N_MEM, XATTN_WIDTH), lambda i: (i, 0, 0))] * 2,
        out_shape=[out, out],
        compiler_params=pltpu.CompilerParams(dimension_semantics=("arbitrary",)),
        name="memkv",
    )(mem, g, w)


def _forget_lower_bound(logits):
    z = logits - jnp.max(logits, axis=0, keepdims=True)
    e = jnp.exp(z)
    return e[0:1, :] / jnp.sum(e, axis=0, keepdims=True)


def _hgrn_gates(proj, lb):
    fp = proj[:, OFF_F:OFF_F + HGRN_WIDTH]
    q = proj[:, OFF_Q:OFF_Q + HGRN_WIDTH]
    log_f = jnp.log2(lb + (1.0 - lb) * jax.nn.sigmoid(fp))
    k = (1.0 - lb) * jax.nn.sigmoid(-fp)
    qf = q * jax.nn.sigmoid(q)
    return qf, k, log_f


def _segment_cumsum(x, seq):
    ridx = lax.broadcasted_iota(jnp.int32, x.shape, 0) & (seq - 1)
    sh = 1
    while sh < seq:
        x = x + jnp.where(ridx >= sh, pltpu.roll(x, sh, 0), 0.0)
        sh *= 2
    return x


def _level_factor(a, qf, k, log_f, m, rows):
    n = a.shape[1]
    ridx = lax.broadcasted_iota(jnp.int32, (rows, n), 0)
    upper = (ridx & m) != 0
    if m == 1:
        d = jnp.where(upper, log_f, 0.0)
    else:
        if (2 * m) % 8 == 0:
            nb = rows // (2 * m)
            a3 = a.reshape(nb, 2 * m, n)
            ref = jnp.broadcast_to(a3[:, m - 1:m, :], (nb, 2 * m, n)).reshape(rows, n)
        else:
            a3 = a.reshape(rows // 8, 8, n)
            sub = lax.broadcasted_iota(jnp.int32, a3.shape, 1)
            ref = jnp.where(sub < 4,
                            jnp.broadcast_to(a3[:, 1:2, :], a3.shape),
                            jnp.broadcast_to(a3[:, 5:6, :], a3.shape)).reshape(rows, n)
        d = -jnp.abs(a - ref)
    return (jnp.where(upper, qf, k) * jnp.exp2(d)).astype(BF16)


def _level_factors(a, qf, k, log_f, rows, seq):
    out, m = [], seq // 2
    while m >= 1:
        out.append((m, _level_factor(a, qf, k, log_f, m, rows)))
        m //= 2
    return out


def _intra_scores(factors, head, rows, seq):
    sl = slice(head * HGRN_DK, (head + 1) * HGRN_DK)
    t = lax.broadcasted_iota(jnp.int32, (rows, rows), 0)
    s = lax.broadcasted_iota(jnp.int32, (rows, rows), 1)
    x = t ^ s
    products = [(m, _dot_nt(y[:, sl], y[:, sl])) for m, y in reversed(factors)]
    total = jnp.zeros((rows, rows), F32)
    for m, p in products:
        total = jnp.where(x >= m, p, total)
    return jnp.where((t > s) & (x < seq), total, 0.0)


def _head_norm_gate(o, gate):
    return o * lax.rsqrt(jnp.mean(o * o, axis=-1, keepdims=True) + EPS) * gate


def _softmax_rows(s):
    e = jnp.exp(s - jnp.max(s, axis=-1, keepdims=True))
    return e / jnp.sum(e, axis=-1, keepdims=True)


def _cross_attention(qx, mk, mv):
    rows = qx.shape[0]
    head_of_lane = lax.broadcasted_iota(jnp.int32, qx.shape, 1) // XATTN_DH
    qs = jnp.concatenate([jnp.where(head_of_lane == h, qx, 0.0) for h in range(XATTN_HEADS)], axis=0)
    p = _softmax_rows(_dot_nt(qs.astype(BF16), mk))
    o = _dot(p.astype(BF16), mv)
    out = jnp.zeros(qx.shape, F32)
    for h in range(XATTN_HEADS):
        out = jnp.where(head_of_lane == h, o[h * rows:(h + 1) * rows, :], out)
    return out


def _pool_means(ld, posf, shape):
    lane = lax.broadcasted_iota(jnp.int32, shape, len(shape) - 1)
    first = lane < POOL_GROUP
    u_lo, u_hi = ld(0, 0), ld(0, 1)
    t2 = u_lo + ld(1, 0)
    t4 = t2 + ld(2, 0) + ld(3, 0)
    t8 = u_hi
    for j in range(1, 8):
        t8 = t8 + ld(j, 1)
    t16 = t8
    for j in range(8, 16):
        t16 = t16 + ld(j, 1)
    cnt_lo = jnp.where(first, jnp.minimum(2.0, posf), jnp.minimum(4.0, posf))
    cnt_hi = jnp.where(first, jnp.minimum(8.0, posf), jnp.minimum(16.0, posf))
    lo = jnp.where(first, t2, t4) / cnt_lo - u_lo
    hi = jnp.where(first, t8, t16) / cnt_hi - u_hi
    return jnp.concatenate([lo, hi], axis=-1)


def _after(x, anchor, zero):
    r, n = anchor.shape
    s = jnp.sum(anchor.reshape(r // 8, 8, n), axis=0)
    c = s[:, :LANES]
    for i in range(1, n // LANES):
        c = c + s[:, i * LANES:(i + 1) * LANES]
    z = lax.bitcast_convert_type(lax.bitcast_convert_type(c[0:1, :], jnp.int32) & zero, F32)
    return jnp.concatenate([x[:, :LANES] + z, x[:, LANES:]], axis=1)


def _gelu_tanh(x):
    c = 0.7978845608028654
    half_x = 0.5 * x
    return half_x + half_x * jnp.tanh(x * (c + (0.044715 * c) * (x * x)))


def _ffn_tail(x, act, wdown_ref, lnf_ref):
    y = x + _dot(act.astype(BF16), wdown_ref[...])
    return _rmsnorm(y, lnf_ref[...])


def _prompt_body(x_ref, mk_ref, mv_ref, ln1_ref, win_ref, poolw_ref, pscale_ref, lbl_ref, onorm_ref,
                 wout_ref, ln2_ref, wup_ref, cw_ref, cb_ref, wdown_ref, lnf_ref, zero_ref,
                 y_ref, npool_ref, ns_ref, nconv_ref,
                 pbuf, st, abuf, x2s, *, tb, nt, nblk):
    g = pl.program_id(0)
    jm = jnp.minimum(g, nblk - 1) % nt
    jf = jnp.maximum(g - 1, 0) % nt

    @pl.when(jm == 0)
    def _():
        pbuf[pl.ds(0, POOL_PAD), :] = jnp.zeros((POOL_PAD, POOL_WIDTH), F32)
        st[...] = jnp.zeros(st.shape, F32)

    @pl.when(jf == 0)
    def _():
        abuf[...] = jnp.zeros(abuf.shape, F32)

    def input_projection():
        h = _rmsnorm(x_ref[0], ln1_ref[...]).astype(BF16)
        return _dot(h, win_ref[...])

    def mixer_half(proj):
        _prompt_mixer_half(proj, x_ref, mk_ref, mv_ref, poolw_ref, pscale_ref, lbl_ref, onorm_ref, wout_ref,
                           pbuf, st, x2s, jm, tb)

    def ffn_half(proj):
        x2 = x2s[...]
        h2 = _rmsnorm(x2, ln2_ref[...]).astype(BF16)
        ab = _dot(h2, wup_ref[...])
        a = ab[:, :D_FF]
        hist = abuf[...]
        sub = lax.broadcasted_iota(jnp.int32, (CONV_PAD, D_FF), 0)
        r1 = pltpu.roll(a, 1, 0)
        r2 = pltpu.roll(a, 2, 0)
        h1 = jnp.broadcast_to(hist[CONV_PAD - 1:CONV_PAD, :], (CONV_PAD, D_FF))
        h0 = jnp.broadcast_to(hist[CONV_PAD - 2:CONV_PAD - 1, :], (CONV_PAD, D_FF))
        a_m1 = jnp.concatenate([jnp.where(sub >= 1, r1[:CONV_PAD], h1), r1[CONV_PAD:]], axis=0)
        a_m2 = jnp.concatenate([jnp.where(sub >= 2, r2[:CONV_PAD], jnp.where(sub == 1, h1, h0)), r2[CONV_PAD:]],
                               axis=0)
        conv = cb_ref[...] + cw_ref[0:1, :] * a_m2 + cw_ref[1:2, :] * a_m1 + cw_ref[2:3, :] * a
        act = _gelu_tanh(conv) * ab[:, D_FF:]
        abuf[...] = a[tb - CONV_PAD:, :]
        if proj is not None:
            act = _after(act, proj, zero_ref[...])
        y_ref[0] = _rmsnorm(x2 + _dot(act.astype(BF16), wdown_ref[...]), lnf_ref[...])

    @pl.when(g == 0)
    def _():
        mixer_half(input_projection())

    @pl.when(jnp.logical_and(g > 0, g < nblk))
    def _():
        proj = input_projection()
        ffn_half(proj)
        mixer_half(proj)

    @pl.when(g == nblk)
    def _():
        ffn_half(None)

    @pl.when(jnp.logical_and(jm == nt - 1, g < nblk))
    def _():
        npool_ref[0] = pbuf[pl.ds(1, POOL_HIST), :]
        for hd in range(HGRN_HEADS):
            ns_ref[0, hd] = st[hd].T

    @pl.when(jnp.logical_and(jf == nt - 1, g >= 1))
    def _():
        nconv_ref[0] = abuf[pl.ds(CONV_PAD - (CONV_W - 1), CONV_W - 1), :]


def _prompt_mixer_half(proj, x_ref, mk_ref, mv_ref, poolw_ref, pscale_ref, lbl_ref, onorm_ref, wout_ref,
                       pbuf, st, x2s, jm, tb):
    x = x_ref[0]

    pbuf[pl.ds(POOL_PAD, tb), :] = proj[:, OFF_U:OFF_U + POOL_WIDTH]
    posf = (jm * tb + 1 + lax.broadcasted_iota(jnp.int32, (tb, 1), 0)).astype(F32)
    dm = _pool_means(lambda j, half: pbuf[pl.ds(POOL_PAD - j, tb), pl.ds(LANES * half, LANES)], posf, (tb, LANES))
    o_pool = _dot(dm.astype(BF16), poolw_ref[...]) * pscale_ref[...]
    pbuf[pl.ds(0, POOL_PAD), :] = pbuf[pl.ds(tb, POOL_PAD), :]

    qx = proj[:, OFF_X:OFF_X + XATTN_WIDTH] * (XATTN_DH ** -0.5)
    o_x = _cross_attention(qx, mk_ref[0].astype(BF16), mv_ref[0].astype(BF16))

    lb = _forget_lower_bound(lbl_ref[...])
    qf, k, log_f = _hgrn_gates(proj, lb)
    v = proj[:, OFF_I:OFF_I + HGRN_WIDTH]
    gg = proj[:, OFF_G:OFF_G + HGRN_WIDTH]
    gate = gg * jax.nn.sigmoid(gg) * onorm_ref[...]
    a_all = _segment_cumsum(log_f, CHUNK)
    states = [st[hd] for hd in range(HGRN_HEADS)]
    o_rows = []
    for c in range(tb // CHUNK):
        rs = slice(c * CHUNK, (c + 1) * CHUNK)
        qf_c, k_c, lf_c, v_c, a = qf[rs], k[rs], log_f[rs], v[rs], a_all[rs]
        a_end = a[CHUNK - 1:CHUNK, :]
        q_in = (qf_c * jnp.exp2(a)).astype(BF16)
        k_out = (k_c * jnp.exp2(a_end - a)).astype(BF16)
        decay = jnp.exp2(a_end)
        v_b = v_c.astype(BF16)
        qk = qf_c * k_c
        factors = _level_factors(a, qf_c, k_c, lf_c, CHUNK, CHUNK)
        heads = [slice(hd * HGRN_DK, (hd + 1) * HGRN_DK) for hd in range(HGRN_HEADS)]
        inter = [_dot_nt(q_in[:, sl], states[hd].astype(BF16)) for hd, sl in enumerate(heads)]
        update = [_dot_tn(v_b[:, sl], k_out[:, sl]) for sl in heads]
        scores = [_intra_scores(factors, hd, CHUNK, CHUNK).astype(BF16) for hd in range(HGRN_HEADS)]
        intra = [_dot(scores[hd], v_b[:, sl]) for hd, sl in enumerate(heads)]
        o_heads = []
        for hd, sl in enumerate(heads):
            o = intra[hd] + inter[hd] + jnp.sum(qk[:, sl], axis=-1, keepdims=True) * v_c[:, sl]
            states[hd] = states[hd] * decay[:, sl] + update[hd]
            o_heads.append(_head_norm_gate(o, gate[rs, sl]))
        o_rows.append(jnp.concatenate(o_heads, axis=-1))
    for hd in range(HGRN_HEADS):
        st[hd] = states[hd]
    o_hgrn = jnp.concatenate(o_rows, axis=0)

    mixed = jnp.concatenate([o_pool, o_hgrn, o_x], axis=-1).astype(BF16)
    x2s[...] = x + _dot(mixed, wout_ref[...])


def _prompt_call(x, mk, mv, mixer_w, ffn_w, tb):
    b, l, d = x.shape
    nt = l // tb
    nblk = b * nt
    ln1, w_in, pool_wbd, pool_scale, lb_logits, onorm, w_out = mixer_w
    ln2, w_up, conv_w, conv_b, w_down, lnf = ffn_w

    def mixer_blk(g):
        return jnp.minimum(g, nblk - 1)

    def ffn_blk(g):
        return jnp.maximum(g - 1, 0)

    x_spec = pl.BlockSpec((1, tb, d), lambda g: (mixer_blk(g) // nt, mixer_blk(g) % nt, 0))
    mem = pl.BlockSpec((1, N_MEM, XATTN_WIDTH), lambda g: (mixer_blk(g) // nt, 0, 0))
    y_spec = pl.BlockSpec((1, tb, d), lambda g: (ffn_blk(g) // nt, ffn_blk(g) % nt, 0))
    return pl.pallas_call(
        functools.partial(_prompt_body, tb=tb, nt=nt, nblk=nblk),
        grid=(nblk + 1,),
        in_specs=[x_spec, mem, mem,
                  _const_spec((1, d)), _const_spec((d, D_IN)), _const_spec((POOL_WIDTH, POOL_WIDTH)),
                  _const_spec((1, POOL_WIDTH)), _const_spec(lb_logits.shape), _const_spec((1, HGRN_WIDTH)),
                  _const_spec((d, d)),
                  _const_spec((1, d)), _const_spec((d, 2 * D_FF)), _const_spec((CONV_W, D_FF)),
                  _const_spec((1, D_FF)), _const_spec((D_FF, d)), _const_spec((1, d)), _const_spec((1, LANES))],
        out_specs=[y_spec,
                   pl.BlockSpec((1, POOL_HIST, POOL_WIDTH), lambda g: (mixer_blk(g) // nt, 0, 0)),
                   pl.BlockSpec((1, HGRN_HEADS, HGRN_DK, HGRN_DV), lambda g: (mixer_blk(g) // nt, 0, 0, 0)),
                   pl.BlockSpec((1, CONV_W - 1, D_FF), lambda g: (ffn_blk(g) // nt, 0, 0))],
        out_shape=[jax.ShapeDtypeStruct((b, l, d), F32),
                   jax.ShapeDtypeStruct((b, POOL_HIST, POOL_WIDTH), F32),
                   jax.ShapeDtypeStruct((b, HGRN_HEADS, HGRN_DK, HGRN_DV), F32),
                   jax.ShapeDtypeStruct((b, CONV_W - 1, D_FF), F32)],
        scratch_shapes=[pltpu.VMEM((POOL_PAD + tb, POOL_WIDTH), F32),
                        pltpu.VMEM((HGRN_HEADS, HGRN_DV, HGRN_DK), F32),
                        pltpu.VMEM((CONV_PAD, D_FF), F32),
                        pltpu.VMEM((tb, d), F32)],
        compiler_params=pltpu.CompilerParams(dimension_semantics=("arbitrary",),
                                             vmem_limit_bytes=VMEM_LIMIT_BYTES),
        name="prompt_layer",
    )(x, mk, mv, ln1, w_in, pool_wbd, pool_scale, lb_logits, onorm, w_out,
      ln2, w_up, conv_w, conv_b, w_down, lnf, jnp.zeros((1, LANES), jnp.int32))


def _sample_mixer_body(x_ref, hist_ref, s0_ref, mk_ref, mv_ref, ln1_ref, win_ref, poolw_ref, pscale_ref, lbl_ref,
                       onorm_ref, wout_ref, x2_ref, npool_ref, ns_ref, pbuf, *, gs, sl_len):
    rows = gs * sl_len
    seqs = [slice(s * sl_len, (s + 1) * sl_len) for s in range(gs)]
    x = x_ref[...].reshape(rows, D_MODEL)
    h = _rmsnorm(x, ln1_ref[...]).astype(BF16)
    proj = _dot(h, win_ref[...])

    pbuf[:, pl.ds(1, POOL_HIST), :] = hist_ref[...]
    pbuf[:, pl.ds(POOL_PAD, sl_len), :] = proj[:, OFF_U:OFF_U + POOL_WIDTH].reshape(gs, sl_len, POOL_WIDTH)
    posf = (PAST_LEN + 1 + lax.broadcasted_iota(jnp.int32, (1, sl_len, 1), 1)).astype(F32)
    dm = _pool_means(lambda j, half: pbuf[:, pl.ds(POOL_PAD - j, sl_len), pl.ds(LANES * half, LANES)],
                     posf, (gs, sl_len, LANES))
    npool_ref[...] = pbuf[:, pl.ds(sl_len + 1, POOL_HIST), :]
    o_pool = _dot(dm.reshape(rows, POOL_WIDTH).astype(BF16), poolw_ref[...]) * pscale_ref[...]

    qx3 = (proj[:, OFF_X:OFF_X + XATTN_WIDTH] * (XATTN_DH ** -0.5)).reshape(gs, sl_len, XATTN_WIDTH)
    head_of_lane = lax.broadcasted_iota(jnp.int32, qx3.shape, 2) // XATTN_DH
    qs3 = jnp.concatenate([jnp.where(head_of_lane == hd, qx3, 0.0) for hd in range(XATTN_HEADS)],
                          axis=1).astype(BF16)
    hrows = XATTN_HEADS * sl_len
    scores = jnp.concatenate([_dot_nt(qs3[s], mk_ref[s].astype(BF16)) for s in range(gs)], axis=0)
    p = _softmax_rows(scores).astype(BF16)
    o4 = jnp.concatenate([_dot(p[s * hrows:(s + 1) * hrows], mv_ref[s].astype(BF16)) for s in range(gs)], axis=0)
    o4 = o4.reshape(gs, XATTN_HEADS, sl_len, XATTN_WIDTH)
    o_x3 = jnp.zeros(qx3.shape, F32)
    for hd in range(XATTN_HEADS):
        o_x3 = jnp.where(head_of_lane == hd, o4[:, hd], o_x3)
    o_x = o_x3.reshape(rows, XATTN_WIDTH)

    lb = _forget_lower_bound(lbl_ref[...])
    qf, k, log_f = _hgrn_gates(proj, lb)
    v = proj[:, OFF_I:OFF_I + HGRN_WIDTH]
    gg = proj[:, OFF_G:OFF_G + HGRN_WIDTH]
    gate = gg * jax.nn.sigmoid(gg) * onorm_ref[...]
    a = _segment_cumsum(log_f, sl_len)
    a3 = a.reshape(gs, sl_len, HGRN_WIDTH)
    a_end = jnp.broadcast_to(a3[:, sl_len - 1:sl_len, :], a3.shape).reshape(rows, HGRN_WIDTH)
    q_in = (qf * jnp.exp2(a)).astype(BF16)
    k_out = (k * jnp.exp2(a_end - a)).astype(BF16)
    decay = jnp.exp2(a_end)
    v_b = v.astype(BF16)
    qk = qf * k
    factors = _level_factors(a, qf, k, log_f, rows, sl_len)
    heads = [slice(hd * HGRN_DK, (hd + 1) * HGRN_DK) for hd in range(HGRN_HEADS)]
    inter = [jnp.concatenate([_dot(q_in[r, sl], s0_ref[s, hd].astype(BF16)) for s, r in enumerate(seqs)], axis=0)
             for hd, sl in enumerate(heads)]
    o_heads = []
    for hd, sl in enumerate(heads):
        p_h = _intra_scores(factors, hd, rows, sl_len)
        o = _dot(p_h.astype(BF16), v_b[:, sl]) + inter[hd]
        o = o + jnp.sum(qk[:, sl], axis=-1, keepdims=True) * v[:, sl]
        o_heads.append(_head_norm_gate(o, gate[:, sl]))
    updates = [[_dot_tn(k_out[r, sl], v_b[r, sl]) for sl in heads] for r in seqs]
    for s, r in enumerate(seqs):
        for hd, sl in enumerate(heads):
            decay_cols = jnp.broadcast_to(decay[r, sl][sl_len - 1:sl_len, :], (HGRN_DV, HGRN_DK)).T
            ns_ref[s, hd] = decay_cols * s0_ref[s, hd] + updates[s][hd]

    mixed = jnp.concatenate([o_pool] + o_heads + [o_x], axis=-1).astype(BF16)
    x2_ref[...] = (x + _dot(mixed, wout_ref[...])).reshape(gs, sl_len, D_MODEL)


def _sample_mixer_call(x, hist, s0, mk, mv, mixer_w, gs):
    b, l, d = x.shape
    ln1, w_in, pool_wbd, pool_scale, lb_logits, onorm, w_out = mixer_w
    grid = (b // gs,)
    blk = pl.BlockSpec((gs, l, d), lambda i: (i, 0, 0))
    histb = pl.BlockSpec((gs, POOL_HIST, POOL_WIDTH), lambda i: (i, 0, 0))
    sb = pl.BlockSpec((gs, HGRN_HEADS, HGRN_DK, HGRN_DV), lambda i: (i, 0, 0, 0))
    mem = pl.BlockSpec((gs, N_MEM, XATTN_WIDTH), lambda i: (i, 0, 0))
    return pl.pallas_call(
        functools.partial(_sample_mixer_body, gs=gs, sl_len=l),
        grid=grid,
        in_specs=[blk, histb, sb, mem, mem,
                  _const_spec((1, d)), _const_spec((d, D_IN)), _const_spec((POOL_WIDTH, POOL_WIDTH)),
                  _const_spec((1, POOL_WIDTH)), _const_spec(lb_logits.shape), _const_spec((1, HGRN_WIDTH)),
                  _const_spec((d, d))],
        out_specs=[blk, histb, sb],
        out_shape=[jax.ShapeDtypeStruct((b, l, d), F32),
                   jax.ShapeDtypeStruct((b, POOL_HIST, POOL_WIDTH), F32),
                   jax.ShapeDtypeStruct((b, HGRN_HEADS, HGRN_DK, HGRN_DV), F32)],
        scratch_shapes=[pltpu.VMEM((gs, POOL_PAD + l, POOL_WIDTH), F32)],
        compiler_params=pltpu.CompilerParams(dimension_semantics=("arbitrary",),
                                             vmem_limit_bytes=VMEM_LIMIT_BYTES),
        name="sample_mixer",
    )(x, hist, s0, mk, mv, ln1, w_in, pool_wbd, pool_scale, lb_logits, onorm, w_out)


def _sample_ffn_body(x_ref, chist_ref, ln2_ref, wup_ref, cw_ref, cb_ref, wdown_ref, lnf_ref, y_ref, nconv_ref,
                     *, gs, sl_len):
    rows = gs * sl_len
    x = x_ref[...].reshape(rows, D_MODEL)
    h = _rmsnorm(x, ln2_ref[...]).astype(BF16)
    ab = _dot(h, wup_ref[...])
    a = ab[:, :D_FF]
    ridx = lax.broadcasted_iota(jnp.int32, (rows, D_FF), 0) % sl_len
    hist = chist_ref[...]
    h1 = jnp.broadcast_to(hist[:, 1:2, :], (gs, sl_len, D_FF)).reshape(rows, D_FF)
    h0 = jnp.broadcast_to(hist[:, 0:1, :], (gs, sl_len, D_FF)).reshape(rows, D_FF)
    a_m1 = jnp.where(ridx >= 1, pltpu.roll(a, 1, 0), h1)
    a_m2 = jnp.where(ridx >= 2, pltpu.roll(a, 2, 0), jnp.where(ridx == 1, h1, h0))
    conv = cb_ref[...] + cw_ref[0:1, :] * a_m2 + cw_ref[1:2, :] * a_m1 + cw_ref[2:3, :] * a
    act = _gelu_tanh(conv) * ab[:, D_FF:]
    nconv_ref[...] = a.reshape(gs, sl_len, D_FF)[:, sl_len - (CONV_W - 1):, :]
    y_ref[...] = _ffn_tail(x, act, wdown_ref, lnf_ref).reshape(gs, sl_len, D_MODEL)


def _sample_ffn_call(x, chist, ffn_w, gs):
    b, l, d = x.shape
    ln2, w_up, conv_w, conv_b, w_down, lnf = ffn_w
    blk = pl.BlockSpec((gs, l, d), lambda i: (i, 0, 0))
    cblk = pl.BlockSpec((gs, CONV_W - 1, D_FF), lambda i: (i, 0, 0))
    return pl.pallas_call(
        functools.partial(_sample_ffn_body, gs=gs, sl_len=l),
        grid=(b // gs,),
        in_specs=[blk, cblk, _const_spec((1, d)), _const_spec((d, 2 * D_FF)), _const_spec((CONV_W, D_FF)),
                  _const_spec((1, D_FF)), _const_spec((D_FF, d)), _const_spec((1, d))],
        out_specs=[blk, cblk],
        out_shape=[jax.ShapeDtypeStruct((b, l, d), F32),
                   jax.ShapeDtypeStruct((b, CONV_W - 1, D_FF), F32)],
        compiler_params=pltpu.CompilerParams(dimension_semantics=("arbitrary",),
                                             vmem_limit_bytes=VMEM_LIMIT_BYTES),
        name="sample_ffn",
    )(x, chist, ln2, w_up, conv_w, conv_b, w_down, lnf)


def _block_diag(pool_w):
    n = pool_w.shape[0]
    out = jnp.zeros((n * POOL_GROUP, n * POOL_GROUP), pool_w.dtype)
    for g in range(n):
        out = lax.dynamic_update_slice(out, pool_w[g], (g * POOL_GROUP, g * POOL_GROUP))
    return out


def _layer(x_prompt, x_sample, mem_prompt, state_pool, state_hgrn, state_conv, cache_mem_k, cache_mem_v,
           ln1_g, w_in, pool_w, pool_scale, hgrn_lb_logits, hgrn_onorm_g, mem_norm_g, w_mem_kv, w_out,
           ln2_g, w_up, conv_w, conv_b, w_down, lnf_g, *, prompt_tb, mixer_gs, ffn_gs):
    row = lambda a: a.reshape(1, -1)
    w_in_b, w_out_b, w_up_b, w_down_b = (w.astype(BF16) for w in (w_in, w_out, w_up, w_down))
    pool_wbd = _block_diag(pool_w).astype(BF16)
    mixer_w = (row(ln1_g), w_in_b, pool_wbd, row(pool_scale), hgrn_lb_logits, row(hgrn_onorm_g), w_out_b)
    ffn_w = (row(ln2_g), w_up_b, conv_w, row(conv_b), w_down_b, row(lnf_g))

    mk, mv = _memkv_call(mem_prompt, row(mem_norm_g), w_mem_kv.astype(BF16))
    y_prompt, new_pool_p, new_s_p, new_conv_p = _prompt_call(x_prompt, mk, mv, mixer_w, ffn_w, tb=prompt_tb)

    nb = x_sample.shape[0]
    smk = cache_mem_k.reshape(nb, N_MEM, XATTN_WIDTH)
    smv = cache_mem_v.reshape(nb, N_MEM, XATTN_WIDTH)
    xs, new_pool_s, new_s_s = _sample_mixer_call(x_sample, state_pool, state_hgrn, smk, smv, mixer_w, gs=mixer_gs)
    y_sample, new_conv_s = _sample_ffn_call(xs, state_conv, ffn_w, gs=ffn_gs)
    bp = x_prompt.shape[0]
    heads = (bp, N_MEM, XATTN_HEADS, XATTN_DH)
    return (y_prompt, y_sample, new_pool_p, new_s_p, new_conv_p, mk.reshape(heads), mv.reshape(heads),
            new_pool_s, new_s_s, new_conv_s)


def kernel(x_prompt, x_sample, mem_prompt, state_pool, state_hgrn, state_conv, cache_mem_k, cache_mem_v,
           ln1_g, w_in, pool_w, pool_scale, hgrn_lb_logits, hgrn_onorm_g, mem_norm_g, w_mem_kv, w_out,
           ln2_g, w_up, conv_w, conv_b, w_down, lnf_g):
    assert w_in.shape[0] == 1, "one layer"
    outs = _layer(x_prompt, x_sample, mem_prompt, state_pool[0], state_hgrn[0], state_conv[0],
                  cache_mem_k[0], cache_mem_v[0], ln1_g[0], w_in[0], pool_w[0], pool_scale[0], hgrn_lb_logits,
                  hgrn_onorm_g[0], mem_norm_g[0], w_mem_kv[0], w_out[0], ln2_g[0], w_up[0], conv_w[0], conv_b[0],
                  w_down[0], lnf_g, prompt_tb=256, mixer_gs=16, ffn_gs=32)
    y_prompt, y_sample = outs[0], outs[1]
    return (y_prompt, y_sample) + tuple(o[None] for o in outs[2:])
```

```python
import functools

import jax
import jax.numpy as jnp
from jax import lax
from jax.experimental import pallas as pl
from jax.experimental.pallas import tpu as pltpu

F32 = jnp.float32
BF16 = jnp.bfloat16

D_MODEL = 1024
POOL_WIDTH = 256
POOL_GROUP = 64
POOL_HIST = 15
HGRN_WIDTH = 512
HGRN_HEADS = 4
HGRN_DK = 128
HGRN_DV = 128
XATTN_WIDTH = 256
XATTN_HEADS = 4
XATTN_DH = 64
N_MEM = 256
D_FF = 2816
CONV_W = 3
EPS = 1e-6
PAST_LEN = 16384
D_IN = POOL_WIDTH + 4 * HGRN_WIDTH + XATTN_WIDTH
OFF_U, OFF_Q, OFF_F, OFF_I, OFF_G, OFF_X = 0, 256, 768, 1280, 1792, 2304

CHUNK = 64
POOL_PAD = 16
CONV_PAD = 8
LANES = 128
MEMKV_GROUP = 4
VMEM_LIMIT_BYTES = 56 * 1024 * 1024

_NT = (((1,), (1,)), ((), ()))
_TN = (((0,), (0,)), ((), ()))


def _dot(a, b):
    return jnp.dot(a, b, preferred_element_type=F32)


def _dot_nt(a, b):
    return lax.dot_general(a, b, _NT, preferred_element_type=F32)


def _dot_tn(a, b):
    return lax.dot_general(a, b, _TN, preferred_element_type=F32)


def _rmsnorm(x, g):
    return x * lax.rsqrt(jnp.mean(x * x, axis=-1, keepdims=True) + EPS) * g


def _const_spec(shape):
    nd = len(shape)
    return pl.BlockSpec(shape, lambda *_: (0,) * nd, pipeline_mode=pl.Buffered(1))


def _memkv_body(mem_ref, g_ref, w_ref, k_ref, v_ref):
    nb = mem_ref.shape[0]
    h = _rmsnorm(mem_ref[...].reshape(nb * N_MEM, D_MODEL), g_ref[...]).astype(BF16)
    kv = _dot(h, w_ref[...])
    k_ref[...] = kv[:, :XATTN_WIDTH].reshape(nb, N_MEM, XATTN_WIDTH)
    v_ref[...] = kv[:, XATTN_WIDTH:].reshape(nb, N_MEM, XATTN_WIDTH)


def _memkv_call(mem, g, w):
    b = mem.shape[0]
    nb = MEMKV_GROUP
    out = jax.ShapeDtypeStruct((b, N_MEM, XATTN_WIDTH), F32)
    return pl.pallas_call(
        _memkv_body,
        grid=(b // nb,),
        in_specs=[pl.BlockSpec((nb, N_MEM, D_MODEL), lambda i: (i, 0, 0)),
                  _const_spec((1, D_MODEL)),
                  _const_spec((D_MODEL, 2 * XATTN_WIDTH))],
        out_specs=[pl.BlockSpec((nb, N_MEM, XATTN_WIDTH), lambda i: (i, 0, 0))] * 2,
        out_shape=[out, out],
        compiler_params=pltpu.CompilerParams(dimension_semantics=("arbitrary",)),
        name="memkv",
    )(mem, g, w)


def _forget_lower_bound(logits):
    z = logits - jnp.max(logits, axis=0, keepdims=True)
    e = jnp.exp(z)
    return e[0:1, :] / jnp.sum(e, axis=0, keepdims=True)


def _hgrn_gates(proj, lb):
    fp = proj[:, OFF_F:OFF_F + HGRN_WIDTH]
    q = proj[:, OFF_Q:OFF_Q + HGRN_WIDTH]
    log_f = jnp.log2(lb + (1.0 - lb) * jax.nn.sigmoid(fp))
    k = (1.0 - lb) * jax.nn.sigmoid(-fp)
    qf = q * jax.nn.sigmoid(q)
    return qf, k, log_f


def _segment_cumsum(x, seq):
    ridx = lax.broadcasted_iota(jnp.int32, x.shape, 0) & (seq - 1)
    sh = 1
    while sh < seq:
        x = x + jnp.where(ridx >= sh, pltpu.roll(x, sh, 0), 0.0)
        sh *= 2
    return x


def _level_factor(a, qf, k, log_f, m, rows):
    n = a.shape[1]
    ridx = lax.broadcasted_iota(jnp.int32, (rows, n), 0)
    upper = (ridx & m) != 0
    if m == 1:
        d = jnp.where(upper, log_f, 0.0)
    else:
        if (2 * m) % 8 == 0:
            nb = rows // (2 * m)
            a3 = a.reshape(nb, 2 * m, n)
            ref = jnp.broadcast_to(a3[:, m - 1:m, :], (nb, 2 * m, n)).reshape(rows, n)
        else:
            a3 = a.reshape(rows // 8, 8, n)
            sub = lax.broadcasted_iota(jnp.int32, a3.shape, 1)
            ref = jnp.where(sub < 4,
                            jnp.broadcast_to(a3[:, 1:2, :], a3.shape),
                            jnp.broadcast_to(a3[:, 5:6, :], a3.shape)).reshape(rows, n)
        d = -jnp.abs(a - ref)
    return (jnp.where(upper, qf, k) * jnp.exp2(d)).astype(BF16)


def _level_factors(a, qf, k, log_f, rows, seq):
    out, m = [], seq // 2
    while m >= 1:
        out.append((m, _level_factor(a, qf, k, log_f, m, rows)))
        m //= 2
    return out


def _intra_scores(factors, head, rows, seq):
    sl = slice(head * HGRN_DK, (head + 1) * HGRN_DK)
    t = lax.broadcasted_iota(jnp.int32, (rows, rows), 0)
    s = lax.broadcasted_iota(jnp.int32, (rows, rows), 1)
    x = t ^ s
    products = [(m, _dot_nt(y[:, sl], y[:, sl])) for m, y in reversed(factors)]
    total = jnp.zeros((rows, rows), F32)
    for m, p in products:
        total = jnp.where(x >= m, p, total)
    return jnp.where((t > s) & (x < seq), total, 0.0)


def _head_norm_gate(o, gate):
    return o * lax.rsqrt(jnp.mean(o * o, axis=-1, keepdims=True) + EPS) * gate


def _softmax_rows(s):
    e = jnp.exp(s - jnp.max(s, axis=-1, keepdims=True))
    return e / jnp.sum(e, axis=-1, keepdims=True)


def _cross_attention(qx, mk, mv):
    rows = qx.shape[0]
    head_of_lane = lax.broadcasted_iota(jnp.int32, qx.shape, 1) // XATTN_DH
    qs = jnp.concatenate([jnp.where(head_of_lane == h, qx, 0.0) for h in range(XATTN_HEADS)], axis=0)
    p = _softmax_rows(_dot_nt(qs.astype(BF16), mk))
    o = _dot(p.astype(BF16), mv)
    out = jnp.zeros(qx.shape, F32)
    for h in range(XATTN_HEADS):
        out = jnp.where(head_of_lane == h, o[h * rows:(h + 1) * rows, :], out)
    return out


def _pool_means(ld, posf, shape):
    lane = lax.broadcasted_iota(jnp.int32, shape, len(shape) - 1)
    first = lane < POOL_GROUP
    u_lo, u_hi = ld(0, 0), ld(0, 1)
    t2 = u_lo + ld(1, 0)
    t4 = t2 + ld(2, 0) + ld(3, 0)
    t8 = u_hi
    for j in range(1, 8):
        t8 = t8 + ld(j, 1)
    t16 = t8
    for j in range(8, 16):
        t16 = t16 + ld(j, 1)
    cnt_lo = jnp.where(first, jnp.minimum(2.0, posf), jnp.minimum(4.0, posf))
    cnt_hi = jnp.where(first, jnp.minimum(8.0, posf), jnp.minimum(16.0, posf))
    lo = jnp.where(first, t2, t4) / cnt_lo - u_lo
    hi = jnp.where(first, t8, t16) / cnt_hi - u_hi
    return jnp.concatenate([lo, hi], axis=-1)


def _after(x, anchor, zero):
    r, n = anchor.shape
    s = jnp.sum(anchor.reshape(r // 8, 8, n), axis=0)
    c = s[:, :LANES]
    for i in range(1, n // LANES):
        c = c + s[:, i * LANES:(i + 1) * LANES]
    z = lax.bitcast_convert_type(lax.bitcast_convert_type(c[0:1, :], jnp.int32) & zero, F32)
    return jnp.concatenate([x[:, :LANES] + z, x[:, LANES:]], axis=1)


def _gelu_tanh(x):
    c = 0.7978845608028654
    half_x = 0.5 * x
    return half_x + half_x * jnp.tanh(x * (c + (0.044715 * c) * (x * x)))


def _ffn_tail(x, act, wdown_ref, lnf_ref):
    y = x + _dot(act.astype(BF16), wdown_ref[...])
    return _rmsnorm(y, lnf_ref[...])


def _prompt_body(x_ref, mk_ref, mv_ref, ln1_ref, win_ref, poolw_ref, pscale_ref, lbl_ref, onorm_ref,
                 wout_ref, ln2_ref, wup_ref, cw_ref, cb_ref, wdown_ref, lnf_ref, zero_ref,
                 y_ref, npool_ref, ns_ref, nconv_ref,
                 pbuf, st, abuf, x2s, *, tb, nt, nblk):
    g = pl.program_id(0)
    jm = jnp.minimum(g, nblk - 1) % nt
    jf = jnp.maximum(g - 1, 0) % nt

    @pl.when(jm == 0)
    def _():
        pbuf[pl.ds(0, POOL_PAD), :] = jnp.zeros((POOL_PAD, POOL_WIDTH), F32)
        st[...] = jnp.zeros(st.shape, F32)

    @pl.when(jf == 0)
    def _():
        abuf[...] = jnp.zeros(abuf.shape, F32)

    def input_projection():
        h = _rmsnorm(x_ref[0], ln1_ref[...]).astype(BF16)
        return _dot(h, win_ref[...])

    def mixer_half(proj):
        _prompt_mixer_half(proj, x_ref, mk_ref, mv_ref, poolw_ref, pscale_ref, lbl_ref, onorm_ref, wout_ref,
                           pbuf, st, x2s, jm, tb)

    def ffn_half(proj):
        x2 = x2s[...]
        h2 = _rmsnorm(x2, ln2_ref[...]).astype(BF16)
        ab = _dot(h2, wup_ref[...])
        a = ab[:, :D_FF]
        hist = abuf[...]
        sub = lax.broadcasted_iota(jnp.int32, (CONV_PAD, D_FF), 0)
        r1 = pltpu.roll(a, 1, 0)
        r2 = pltpu.roll(a, 2, 0)
        h1 = jnp.broadcast_to(hist[CONV_PAD - 1:CONV_PAD, :], (CONV_PAD, D_FF))
        h0 = jnp.broadcast_to(hist[CONV_PAD - 2:CONV_PAD - 1, :], (CONV_PAD, D_FF))
        a_m1 = jnp.concatenate([jnp.where(sub >= 1, r1[:CONV_PAD], h1), r1[CONV_PAD:]], axis=0)
        a_m2 = jnp.concatenate([jnp.where(sub >= 2, r2[:CONV_PAD], jnp.where(sub == 1, h1, h0)), r2[CONV_PAD:]],
                               axis=0)
        conv = cb_ref[...] + cw_ref[0:1, :] * a_m2 + cw_ref[1:2, :] * a_m1 + cw_ref[2:3, :] * a
        act = _gelu_tanh(conv) * ab[:, D_FF:]
        abuf[...] = a[tb - CONV_PAD:, :]
        if proj is not None:
            act = _after(act, proj, zero_ref[...])
        y_ref[0] = _rmsnorm(x2 + _dot(act.astype(BF16), wdown_ref[...]), lnf_ref[...])

    @pl.when(g == 0)
    def _():
        mixer_half(input_projection())

    @pl.when(jnp.logical_and(g > 0, g < nblk))
    def _():
        proj = input_projection()
        ffn_half(proj)
        mixer_half(proj)

    @pl.when(g == nblk)
    def _():
        ffn_half(None)

    @pl.when(jnp.logical_and(jm == nt - 1, g < nblk))
    def _():
        npool_ref[0] = pbuf[pl.ds(1, POOL_HIST), :]
        for hd in range(HGRN_HEADS):
            ns_ref[0, hd] = st[hd].T

    @pl.when(jnp.logical_and(jf == nt - 1, g >= 1))
    def _():
        nconv_ref[0] = abuf[pl.ds(CONV_PAD - (CONV_W - 1), CONV_W - 1), :]


def _prompt_mixer_half(proj, x_ref, mk_ref, mv_ref, poolw_ref, pscale_ref, lbl_ref, onorm_ref, wout_ref,
                       pbuf, st, x2s, jm, tb):
    x = x_ref[0]

    pbuf[pl.ds(POOL_PAD, tb), :] = proj[:, OFF_U:OFF_U + POOL_WIDTH]
    posf = (jm * tb + 1 + lax.broadcasted_iota(jnp.int32, (tb, 1), 0)).astype(F32)
    dm = _pool_means(lambda j, half: pbuf[pl.ds(POOL_PAD - j, tb), pl.ds(LANES * half, LANES)], posf, (tb, LANES))
    o_pool = _dot(dm.astype(BF16), poolw_ref[...]) * pscale_ref[...]
    pbuf[pl.ds(0, POOL_PAD), :] = pbuf[pl.ds(tb, POOL_PAD), :]

    qx = proj[:, OFF_X:OFF_X + XATTN_WIDTH] * (XATTN_DH ** -0.5)
    o_x = _cross_attention(qx, mk_ref[0].astype(BF16), mv_ref[0].astype(BF16))

    lb = _forget_lower_bound(lbl_ref[...])
    qf, k, log_f = _hgrn_gates(proj, lb)
    v = proj[:, OFF_I:OFF_I + HGRN_WIDTH]
    gg = proj[:, OFF_G:OFF_G + HGRN_WIDTH]
    gate = gg * jax.nn.sigmoid(gg) * onorm_ref[...]
    a_all = _segment_cumsum(log_f, CHUNK)
    states = [st[hd] for hd in range(HGRN_HEADS)]
    o_rows = []
    for c in range(tb // CHUNK):
        rs = slice(c * CHUNK, (c + 1) * CHUNK)
        qf_c, k_c, lf_c, v_c, a = qf[rs], k[rs], log_f[rs], v[rs], a_all[rs]
        a_end = a[CHUNK - 1:CHUNK, :]
        q_in = (qf_c * jnp.exp2(a)).astype(BF16)
        k_out = (k_c * jnp.exp2(a_end - a)).astype(BF16)
        decay = jnp.exp2(a_end)
        v_b = v_c.astype(BF16)
        qk = qf_c * k_c
        factors = _level_factors(a, qf_c, k_c, lf_c, CHUNK, CHUNK)
        heads = [slice(hd * HGRN_DK, (hd + 1) * HGRN_DK) for hd in range(HGRN_HEADS)]
        inter = [_dot_nt(q_in[:, sl], states[hd].astype(BF16)) for hd, sl in enumerate(heads)]
        update = [_dot_tn(v_b[:, sl], k_out[:, sl]) for sl in heads]
        scores = [_intra_scores(factors, hd, CHUNK, CHUNK).astype(BF16) for hd in range(HGRN_HEADS)]
        intra = [_dot(scores[hd], v_b[:, sl]) for hd, sl in enumerate(heads)]
        o_heads = []
        for hd, sl in enumerate(heads):
            o = intra[hd] + inter[hd] + jnp.sum(qk[:, sl], axis=-1, keepdims=True) * v_c[:, sl]
            states[hd] = states[hd] * decay[:, sl] + update[hd]
            o_heads.append(_head_norm_gate(o, gate[rs, sl]))
        o_rows.append(jnp.concatenate(o_heads, axis=-1))
    for hd in range(HGRN_HEADS):
        st[hd] = states[hd]
    o_hgrn = jnp.concatenate(o_rows, axis=0)

    mixed = jnp.concatenate([o_pool, o_hgrn, o_x], axis=-1).astype(BF16)
    x2s[...] = x + _dot(mixed, wout_ref[...])


def _prompt_call(x, mk, mv, mixer_w, ffn_w, tb):
    b, l, d = x.shape
    nt = l // tb
    nblk = b * nt
    ln1, w_in, pool_wbd, pool_scale, lb_logits, onorm, w_out = mixer_w
    ln2, w_up, conv_w, conv_b, w_down, lnf = ffn_w

    def mixer_blk(g):
        return jnp.minimum(g, nblk - 1)

    def ffn_blk(g):
        return jnp.maximum(g - 1, 0)

    x_spec = pl.BlockSpec((1, tb, d), lambda g: (mixer_blk(g) // nt, mixer_blk(g) % nt, 0))
    mem = pl.BlockSpec((1, N_MEM, XATTN_WIDTH), lambda g: (mixer_blk(g) // nt, 0, 0))
    y_spec = pl.BlockSpec((1, tb, d), lambda g: (ffn_blk(g) // nt, ffn_blk(g) % nt, 0))
    return pl.pallas_call(
        functools.partial(_prompt_body, tb=tb, nt=nt, nblk=nblk),
        grid=(nblk + 1,),
        in_specs=[x_spec, mem, mem,
                  _const_spec((1, d)), _const_spec((d, D_IN)), _const_spec((POOL_WIDTH, POOL_WIDTH)),
                  _const_spec((1, POOL_WIDTH)), _const_spec(lb_logits.shape), _const_spec((1, HGRN_WIDTH)),
                  _const_spec((d, d)),
                  _const_spec((1, d)), _const_spec((d, 2 * D_FF)), _const_spec((CONV_W, D_FF)),
                  _const_spec((1, D_FF)), _const_spec((D_FF, d)), _const_spec((1, d)), _const_spec((1, LANES))],
        out_specs=[y_spec,
                   pl.BlockSpec((1, POOL_HIST, POOL_WIDTH), lambda g: (mixer_blk(g) // nt, 0, 0)),
                   pl.BlockSpec((1, HGRN_HEADS, HGRN_DK, HGRN_DV), lambda g: (mixer_blk(g) // nt, 0, 0, 0)),
                   pl.BlockSpec((1, CONV_W - 1, D_FF), lambda g: (ffn_blk(g) // nt, 0, 0))],
        out_shape=[jax.ShapeDtypeStruct((b, l, d), F32),
                   jax.ShapeDtypeStruct((b, POOL_HIST, POOL_WIDTH), F32),
                   jax.ShapeDtypeStruct((b, HGRN_HEADS, HGRN_DK, HGRN_DV), F32),
                   jax.ShapeDtypeStruct((b, CONV_W - 1, D_FF), F32)],
        scratch_shapes=[pltpu.VMEM((POOL_PAD + tb, POOL_WIDTH), F32),
                        pltpu.VMEM((HGRN_HEADS, HGRN_DV, HGRN_DK), F32),
                        pltpu.VMEM((CONV_PAD, D_FF), F32),
                        pltpu.VMEM((tb, d), F32)],
        compiler_params=pltpu.CompilerParams(dimension_semantics=("arbitrary",),
                                             vmem_limit_bytes=VMEM_LIMIT_BYTES),
        name="prompt_layer",
    )(x, mk, mv, ln1, w_in, pool_wbd, pool_scale, lb_logits, onorm, w_out,
      ln2, w_up, conv_w, conv_b, w_down, lnf, jnp.zeros((1, LANES), jnp.int32))


def _sample_mixer_body(x_ref, hist_ref, s0_ref, mkt_ref, mvt_ref, ln1_ref, win_ref, poolw_ref, pscale_ref, lbl_ref,
                       onorm_ref, wout_ref, x2_ref, npool_ref, ns_ref, pbuf, *, gs, sl_len):
    rows = gs * sl_len
    seqs = [slice(s * sl_len, (s + 1) * sl_len) for s in range(gs)]
    x = x_ref[...].reshape(rows, D_MODEL)
    h = _rmsnorm(x, ln1_ref[...]).astype(BF16)
    proj = _dot(h, win_ref[...])

    pbuf[:, pl.ds(1, POOL_HIST), :] = hist_ref[...]
    pbuf[:, pl.ds(POOL_PAD, sl_len), :] = proj[:, OFF_U:OFF_U + POOL_WIDTH].reshape(gs, sl_len, POOL_WIDTH)
    posf = (PAST_LEN + 1 + lax.broadcasted_iota(jnp.int32, (1, sl_len, 1), 1)).astype(F32)
    dm = _pool_means(lambda j, half: pbuf[:, pl.ds(POOL_PAD - j, sl_len), pl.ds(LANES * half, LANES)],
                     posf, (gs, sl_len, LANES))
    npool_ref[...] = pbuf[:, pl.ds(sl_len + 1, POOL_HIST), :]
    o_pool = _dot(dm.reshape(rows, POOL_WIDTH).astype(BF16), poolw_ref[...]) * pscale_ref[...]

    qx3 = (proj[:, OFF_X:OFF_X + XATTN_WIDTH] * (XATTN_DH ** -0.5)).reshape(gs, sl_len, XATTN_WIDTH)
    head_of_lane = lax.broadcasted_iota(jnp.int32, qx3.shape, 2) // XATTN_DH
    qs3 = jnp.concatenate([jnp.where(head_of_lane == hd, qx3, 0.0) for hd in range(XATTN_HEADS)],
                          axis=1).astype(BF16)
    hrows = XATTN_HEADS * sl_len
    scores = jnp.concatenate([_dot(qs3[s], mkt_ref[s].astype(BF16)) for s in range(gs)], axis=0)
    p = _softmax_rows(scores).astype(BF16)
    o4 = jnp.concatenate([_dot_nt(p[s * hrows:(s + 1) * hrows], mvt_ref[s].astype(BF16)) for s in range(gs)],
                         axis=0)
    o4 = o4.reshape(gs, XATTN_HEADS, sl_len, XATTN_WIDTH)
    o_x3 = jnp.zeros(qx3.shape, F32)
    for hd in range(XATTN_HEADS):
        o_x3 = jnp.where(head_of_lane == hd, o4[:, hd], o_x3)
    o_x = o_x3.reshape(rows, XATTN_WIDTH)

    lb = _forget_lower_bound(lbl_ref[...])
    qf, k, log_f = _hgrn_gates(proj, lb)
    v = proj[:, OFF_I:OFF_I + HGRN_WIDTH]
    gg = proj[:, OFF_G:OFF_G + HGRN_WIDTH]
    gate = gg * jax.nn.sigmoid(gg) * onorm_ref[...]
    a = _segment_cumsum(log_f, sl_len)
    a3 = a.reshape(gs, sl_len, HGRN_WIDTH)
    a_end = jnp.broadcast_to(a3[:, sl_len - 1:sl_len, :], a3.shape).reshape(rows, HGRN_WIDTH)
    q_in = (qf * jnp.exp2(a)).astype(BF16)
    k_out = (k * jnp.exp2(a_end - a)).astype(BF16)
    decay = jnp.exp2(a_end)
    v_b = v.astype(BF16)
    qk = qf * k
    factors = _level_factors(a, qf, k, log_f, rows, sl_len)
    heads = [slice(hd * HGRN_DK, (hd + 1) * HGRN_DK) for hd in range(HGRN_HEADS)]
    inter = [jnp.concatenate([_dot(q_in[r, sl], s0_ref[s, hd].astype(BF16)) for s, r in enumerate(seqs)], axis=0)
             for hd, sl in enumerate(heads)]
    o_heads = []
    for hd, sl in enumerate(heads):
        p_h = _intra_scores(factors, hd, rows, sl_len)
        o = _dot(p_h.astype(BF16), v_b[:, sl]) + inter[hd]
        o = o + jnp.sum(qk[:, sl], axis=-1, keepdims=True) * v[:, sl]
        o_heads.append(_head_norm_gate(o, gate[:, sl]))
    updates = [[_dot_tn(k_out[r, sl], v_b[r, sl]) for sl in heads] for r in seqs]
    for s, r in enumerate(seqs):
        for hd, sl in enumerate(heads):
            decay_cols = jnp.broadcast_to(decay[r, sl][sl_len - 1:sl_len, :], (HGRN_DV, HGRN_DK)).T
            ns_ref[s, hd] = decay_cols * s0_ref[s, hd] + updates[s][hd]

    mixed = jnp.concatenate([o_pool] + o_heads + [o_x], axis=-1).astype(BF16)
    x2_ref[...] = (x + _dot(mixed, wout_ref[...])).reshape(gs, sl_len, D_MODEL)


def _sample_mixer_call(x, hist, s0, mkt, mvt, mixer_w, gs):
    b, l, d = x.shape
    ln1, w_in, pool_wbd, pool_scale, lb_logits, onorm, w_out = mixer_w
    grid = (b // gs,)
    blk = pl.BlockSpec((gs, l, d), lambda i: (i, 0, 0))
    histb = pl.BlockSpec((gs, POOL_HIST, POOL_WIDTH), lambda i: (i, 0, 0))
    sb = pl.BlockSpec((gs, HGRN_HEADS, HGRN_DK, HGRN_DV), lambda i: (i, 0, 0, 0))
    mem = pl.BlockSpec((gs, XATTN_WIDTH, N_MEM), lambda i: (i, 0, 0))
    return pl.pallas_call(
        functools.partial(_sample_mixer_body, gs=gs, sl_len=l),
        grid=grid,
        in_specs=[blk, histb, sb, mem, mem,
                  _const_spec((1, d)), _const_spec((d, D_IN)), _const_spec((POOL_WIDTH, POOL_WIDTH)),
                  _const_spec((1, POOL_WIDTH)), _const_spec(lb_logits.shape), _const_spec((1, HGRN_WIDTH)),
                  _const_spec((d, d))],
        out_specs=[blk, histb, sb],
        out_shape=[jax.ShapeDtypeStruct((b, l, d), F32),
                   jax.ShapeDtypeStruct((b, POOL_HIST, POOL_WIDTH), F32),
                   jax.ShapeDtypeStruct((b, HGRN_HEADS, HGRN_DK, HGRN_DV), F32)],
        scratch_shapes=[pltpu.VMEM((gs, POOL_PAD + l, POOL_WIDTH), F32)],
        compiler_params=pltpu.CompilerParams(dimension_semantics=("arbitrary",),
                                             vmem_limit_bytes=VMEM_LIMIT_BYTES),
        name="sample_mixer",
    )(x, hist, s0, mkt, mvt, ln1, w_in, pool_wbd, pool_scale, lb_logits, onorm, w_out)


def _sample_ffn_body(x_ref, chist_ref, ln2_ref, wup_ref, cw_ref, cb_ref, wdown_ref, lnf_ref, y_ref, nconv_ref,
                     *, gs, sl_len):
    rows = gs * sl_len
    x = x_ref[...].reshape(rows, D_MODEL)
    h = _rmsnorm(x, ln2_ref[...]).astype(BF16)
    ab = _dot(h, wup_ref[...])
    a = ab[:, :D_FF]
    ridx = lax.broadcasted_iota(jnp.int32, (rows, D_FF), 0) % sl_len
    hist = chist_ref[...]
    h1 = jnp.broadcast_to(hist[:, 1:2, :], (gs, sl_len, D_FF)).reshape(rows, D_FF)
    h0 = jnp.broadcast_to(hist[:, 0:1, :], (gs, sl_len, D_FF)).reshape(rows, D_FF)
    a_m1 = jnp.where(ridx >= 1, pltpu.roll(a, 1, 0), h1)
    a_m2 = jnp.where(ridx >= 2, pltpu.roll(a, 2, 0), jnp.where(ridx == 1, h1, h0))
    conv = cb_ref[...] + cw_ref[0:1, :] * a_m2 + cw_ref[1:2, :] * a_m1 + cw_ref[2:3, :] * a
    act = _gelu_tanh(conv) * ab[:, D_FF:]
    nconv_ref[...] = a.reshape(gs, sl_len, D_FF)[:, sl_len - (CONV_W - 1):, :]
    y_ref[...] = _ffn_tail(x, act, wdown_ref, lnf_ref).reshape(gs, sl_len, D_MODEL)


def _sample_ffn_call(x, chist, ffn_w, gs):
    b, l, d = x.shape
    ln2, w_up, conv_w, conv_b, w_down, lnf = ffn_w
    blk = pl.BlockSpec((gs, l, d), lambda i: (i, 0, 0))
    cblk = pl.BlockSpec((gs, CONV_W - 1, D_FF), lambda i: (i, 0, 0))
    return pl.pallas_call(
        functools.partial(_sample_ffn_body, gs=gs, sl_len=l),
        grid=(b // gs,),
        in_specs=[blk, cblk, _const_spec((1, d)), _const_spec((d, 2 * D_FF)), _const_spec((CONV_W, D_FF)),
                  _const_spec((1, D_FF)), _const_spec((D_FF, d)), _const_spec((1, d))],
        out_specs=[blk, cblk],
        out_shape=[jax.ShapeDtypeStruct((b, l, d), F32),
                   jax.ShapeDtypeStruct((b, CONV_W - 1, D_FF), F32)],
        compiler_params=pltpu.CompilerParams(dimension_semantics=("arbitrary",),
                                             vmem_limit_bytes=VMEM_LIMIT_BYTES),
        name="sample_ffn",
    )(x, chist, ln2, w_up, conv_w, conv_b, w_down, lnf)


def _block_diag(pool_w):
    n = pool_w.shape[0]
    out = jnp.zeros((n * POOL_GROUP, n * POOL_GROUP), pool_w.dtype)
    for g in range(n):
        out = lax.dynamic_update_slice(out, pool_w[g], (g * POOL_GROUP, g * POOL_GROUP))
    return out


def _layer(x_prompt, x_sample, mem_prompt, state_pool, state_hgrn, state_conv, cache_mem_k, cache_mem_v,
           ln1_g, w_in, pool_w, pool_scale, hgrn_lb_logits, hgrn_onorm_g, mem_norm_g, w_mem_kv, w_out,
           ln2_g, w_up, conv_w, conv_b, w_down, lnf_g, *, prompt_tb, mixer_gs, ffn_gs):
    row = lambda a: a.reshape(1, -1)
    w_in_b, w_out_b, w_up_b, w_down_b = (w.astype(BF16) for w in (w_in, w_out, w_up, w_down))
    pool_wbd = _block_diag(pool_w).astype(BF16)
    mixer_w = (row(ln1_g), w_in_b, pool_wbd, row(pool_scale), hgrn_lb_logits, row(hgrn_onorm_g), w_out_b)
    ffn_w = (row(ln2_g), w_up_b, conv_w, row(conv_b), w_down_b, row(lnf_g))

    mk, mv = _memkv_call(mem_prompt, row(mem_norm_g), w_mem_kv.astype(BF16))
    y_prompt, new_pool_p, new_s_p, new_conv_p = _prompt_call(x_prompt, mk, mv, mixer_w, ffn_w, tb=prompt_tb)

    nb = x_sample.shape[0]
    smkt = cache_mem_k.transpose(0, 2, 3, 1).reshape(nb, XATTN_WIDTH, N_MEM)
    smvt = cache_mem_v.transpose(0, 2, 3, 1).reshape(nb, XATTN_WIDTH, N_MEM)
    xs, new_pool_s, new_s_s = _sample_mixer_call(x_sample, state_pool, state_hgrn, smkt, smvt, mixer_w, gs=mixer_gs)
    y_sample, new_conv_s = _sample_ffn_call(xs, state_conv, ffn_w, gs=ffn_gs)
    bp = x_prompt.shape[0]
    heads = (bp, N_MEM, XATTN_HEADS, XATTN_DH)
    return (y_prompt, y_sample, new_pool_p, new_s_p, new_conv_p, mk.reshape(heads), mv.reshape(heads),
            new_pool_s, new_s_s, new_conv_s)


def kernel(x_prompt, x_sample, mem_prompt, state_pool, state_hgrn, state_conv, cache_mem_k, cache_mem_v,
           ln1_g, w_in, pool_w, pool_scale, hgrn_lb_logits, hgrn_onorm_g, mem_norm_g, w_mem_kv, w_out,
           ln2_g, w_up, conv_w, conv_b, w_down, lnf_g):
    assert w_in.shape[0] == 1, "one layer"
    outs = _layer(x_prompt, x_sample, mem_prompt, state_pool[0], state_hgrn[0], state_conv[0],
                  cache_mem_k[0], cache_mem_v[0], ln1_g[0], w_in[0], pool_w[0], pool_scale[0], hgrn_lb_logits,
                  hgrn_onorm_g[0], mem_norm_g[0], w_mem_kv[0], w_out[0], ln2_g[0], w_up[0], conv_w[0], conv_b[0],
                  w_down[0], lnf_g, prompt_tb=256, mixer_gs=16, ffn_gs=32)
    y_prompt, y_sample = outs[0], outs[1]
    return (y_prompt, y_sample) + tuple(o[None] for o in outs[2:])
```

```python
import functools

import jax
import jax.numpy as jnp
from jax import lax
from jax.experimental import pallas as pl
from jax.experimental.pallas import tpu as pltpu

F32 = jnp.float32
BF16 = jnp.bfloat16

D_MODEL = 1024
POOL_WIDTH = 256
POOL_GROUP = 64
POOL_HIST = 15
HGRN_WIDTH = 512
HGRN_HEADS = 4
HGRN_DK = 128
HGRN_DV = 128
XATTN_WIDTH = 256
XATTN_HEADS = 4
XATTN_DH = 64
N_MEM = 256
D_FF = 2816
CONV_W = 3
EPS = 1e-6
PAST_LEN = 16384
D_IN = POOL_WIDTH + 4 * HGRN_WIDTH + XATTN_WIDTH
OFF_U, OFF_Q, OFF_F, OFF_I, OFF_G, OFF_X = 0, 256, 768, 1280, 1792, 2304

CHUNK = 64
POOL_PAD = 16
CONV_PAD = 8
LANES = 128
MEMKV_GROUP = 4
STAGE_ROWS = 1024
STAGE_COLS = 512
VMEM_LIMIT_BYTES = 56 * 1024 * 1024

_NT = (((1,), (1,)), ((), ()))
_TN = (((0,), (0,)), ((), ()))


def _dot(a, b):
    return jnp.dot(a, b, preferred_element_type=F32)


def _dot_nt(a, b):
    return lax.dot_general(a, b, _NT, preferred_element_type=F32)


def _dot_tn(a, b):
    return lax.dot_general(a, b, _TN, preferred_element_type=F32)


def _rmsnorm(x, g):
    return x * lax.rsqrt(jnp.mean(x * x, axis=-1, keepdims=True) + EPS) * g


def _const_spec(shape):
    nd = len(shape)
    return pl.BlockSpec(shape, lambda *_: (0,) * nd, pipeline_mode=pl.Buffered(1))


def _memkv_body(mem_ref, g_ref, w_ref, k_ref, v_ref):
    nb = mem_ref.shape[0]
    h = _rmsnorm(mem_ref[...].reshape(nb * N_MEM, D_MODEL), g_ref[...]).astype(BF16)
    kv = _dot(h, w_ref[...])
    k_ref[...] = kv[:, :XATTN_WIDTH].reshape(nb, N_MEM, XATTN_WIDTH)
    v_ref[...] = kv[:, XATTN_WIDTH:].reshape(nb, N_MEM, XATTN_WIDTH)


def _memkv_call(mem, g, w):
    b = mem.shape[0]
    nb = MEMKV_GROUP
    out = jax.ShapeDtypeStruct((b, N_MEM, XATTN_WIDTH), F32)
    return pl.pallas_call(
        _memkv_body,
        grid=(b // nb,),
        in_specs=[pl.BlockSpec((nb, N_MEM, D_MODEL), lambda i: (i, 0, 0)),
                  _const_spec((1, D_MODEL)),
                  _const_spec((D_MODEL, 2 * XATTN_WIDTH))],
        out_specs=[pl.BlockSpec((nb, N_MEM, XATTN_WIDTH), lambda i: (i, 0, 0))] * 2,
        out_shape=[out, out],
        compiler_params=pltpu.CompilerParams(dimension_semantics=("arbitrary",)),
        name="memkv",
    )(mem, g, w)


def _forget_lower_bound(logits):
    z = logits - jnp.max(logits, axis=0, keepdims=True)
    e = jnp.exp(z)
    return e[0:1, :] / jnp.sum(e, axis=0, keepdims=True)


def _hgrn_gates(proj, lb):
    fp = proj[:, OFF_F:OFF_F + HGRN_WIDTH]
    q = proj[:, OFF_Q:OFF_Q + HGRN_WIDTH]
    log_f = jnp.log2(lb + (1.0 - lb) * jax.nn.sigmoid(fp))
    k = (1.0 - lb) * jax.nn.sigmoid(-fp)
    qf = q * jax.nn.sigmoid(q)
    return qf, k, log_f


def _segment_cumsum(x, seq):
    ridx = lax.broadcasted_iota(jnp.int32, x.shape, 0) & (seq - 1)
    sh = 1
    while sh < seq:
        x = x + jnp.where(ridx >= sh, pltpu.roll(x, sh, 0), 0.0)
        sh *= 2
    return x


def _level_factor(a, qf, k, log_f, m, rows):
    n = a.shape[1]
    ridx = lax.broadcasted_iota(jnp.int32, (rows, n), 0)
    upper = (ridx & m) != 0
    if m == 1:
        d = jnp.where(upper, log_f, 0.0)
    else:
        if (2 * m) % 8 == 0:
            nb = rows // (2 * m)
            a3 = a.reshape(nb, 2 * m, n)
            ref = jnp.broadcast_to(a3[:, m - 1:m, :], (nb, 2 * m, n)).reshape(rows, n)
        else:
            a3 = a.reshape(rows // 8, 8, n)
            sub = lax.broadcasted_iota(jnp.int32, a3.shape, 1)
            ref = jnp.where(sub < 4,
                            jnp.broadcast_to(a3[:, 1:2, :], a3.shape),
                            jnp.broadcast_to(a3[:, 5:6, :], a3.shape)).reshape(rows, n)
        d = -jnp.abs(a - ref)
    return (jnp.where(upper, qf, k) * jnp.exp2(d)).astype(BF16)


def _level_factors(a, qf, k, log_f, rows, seq):
    out, m = [], seq // 2
    while m >= 1:
        out.append((m, _level_factor(a, qf, k, log_f, m, rows)))
        m //= 2
    return out


def _intra_scores(factors, head, rows, seq):
    sl = slice(head * HGRN_DK, (head + 1) * HGRN_DK)
    t = lax.broadcasted_iota(jnp.int32, (rows, rows), 0)
    s = lax.broadcasted_iota(jnp.int32, (rows, rows), 1)
    x = t ^ s
    products = [(m, _dot_nt(y[:, sl], y[:, sl])) for m, y in reversed(factors)]
    total = jnp.zeros((rows, rows), F32)
    for m, p in products:
        total = jnp.where(x >= m, p, total)
    return jnp.where((t > s) & (x < seq), total, 0.0)


def _head_norm_gate(o, gate):
    return o * lax.rsqrt(jnp.mean(o * o, axis=-1, keepdims=True) + EPS) * gate


def _softmax_rows(s):
    e = jnp.exp(s - jnp.max(s, axis=-1, keepdims=True))
    return e / jnp.sum(e, axis=-1, keepdims=True)


def _cross_attention(qx, mk, mv):
    rows = qx.shape[0]
    head_of_lane = lax.broadcasted_iota(jnp.int32, qx.shape, 1) // XATTN_DH
    qs = jnp.concatenate([jnp.where(head_of_lane == h, qx, 0.0) for h in range(XATTN_HEADS)], axis=0)
    p = _softmax_rows(_dot_nt(qs.astype(BF16), mk))
    o = _dot(p.astype(BF16), mv)
    out = jnp.zeros(qx.shape, F32)
    for h in range(XATTN_HEADS):
        out = jnp.where(head_of_lane == h, o[h * rows:(h + 1) * rows, :], out)
    return out


def _pool_means(ld, posf, shape):
    lane = lax.broadcasted_iota(jnp.int32, shape, len(shape) - 1)
    first = lane < POOL_GROUP
    u_lo, u_hi = ld(0, 0), ld(0, 1)
    t2 = u_lo + ld(1, 0)
    t4 = t2 + ld(2, 0) + ld(3, 0)
    t8 = u_hi
    for j in range(1, 8):
        t8 = t8 + ld(j, 1)
    t16 = t8
    for j in range(8, 16):
        t16 = t16 + ld(j, 1)
    cnt_lo = jnp.where(first, jnp.minimum(2.0, posf), jnp.minimum(4.0, posf))
    cnt_hi = jnp.where(first, jnp.minimum(8.0, posf), jnp.minimum(16.0, posf))
    lo = jnp.where(first, t2, t4) / cnt_lo - u_lo
    hi = jnp.where(first, t8, t16) / cnt_hi - u_hi
    return jnp.concatenate([lo, hi], axis=-1)


def _after(x, anchor, zero):
    r, n = anchor.shape
    s = jnp.sum(anchor.reshape(r // 8, 8, n), axis=0)
    c = s[:, :LANES]
    for i in range(1, n // LANES):
        c = c + s[:, i * LANES:(i + 1) * LANES]
    z = lax.bitcast_convert_type(lax.bitcast_convert_type(c[0:1, :], jnp.int32) & zero, F32)
    return jnp.concatenate([x[:, :LANES] + z, x[:, LANES:]], axis=1)


def _gelu_tanh(x):
    c = 0.7978845608028654
    half_x = 0.5 * x
    return half_x + half_x * jnp.tanh(x * (c + (0.044715 * c) * (x * x)))


def _ffn_tail(x, act, wdown_ref, lnf_ref):
    y = x + _dot(act.astype(BF16), wdown_ref[...])
    return _rmsnorm(y, lnf_ref[...])


def _weight_blocks(shapes):
    return [(i, r0, min(STAGE_ROWS, rows - r0), c0)
            for i, (rows, cols) in enumerate(shapes)
            for c0 in range(0, cols, STAGE_COLS)
            for r0 in range(0, rows, STAGE_ROWS)]


def _load_weights(srcs, dsts, stage, sem):
    blocks = _weight_blocks([w.shape for w in srcs])

    def copy(n):
        i, r0, nr, c0 = blocks[n]
        return pltpu.make_async_copy(srcs[i].at[pl.ds(r0, nr), pl.ds(c0, STAGE_COLS)],
                                     stage.at[n % 2, pl.ds(0, nr), :], sem.at[n % 2])

    copy(0).start()
    for n, (i, r0, nr, c0) in enumerate(blocks):
        if n + 1 < len(blocks):
            copy(n + 1).start()
        copy(n).wait()
        dsts[i][pl.ds(r0, nr), pl.ds(c0, STAGE_COLS)] = stage[n % 2, pl.ds(0, nr), :].astype(BF16)


def _prompt_body(x_ref, mk_ref, mv_ref, ln1_ref, win_hbm, poolw_ref, pscale_ref, lbl_ref, onorm_ref,
                 wout_hbm, ln2_ref, wup_hbm, cw_ref, cb_ref, wdown_hbm, lnf_ref, zero_ref,
                 y_ref, npool_ref, ns_ref, nconv_ref, win_out, wout_out, wup_out, wdown_out,
                 pbuf, st, abuf, x2s, win_ref, wout_ref, wup_ref, wdown_ref, stage, stage_sem, out_sem,
                 *, tb, nt, nblk):
    g = pl.program_id(0)
    jm = jnp.minimum(g, nblk - 1) % nt
    jf = jnp.maximum(g - 1, 0) % nt

    weights_bf16 = (win_ref, wout_ref, wup_ref, wdown_ref)

    def weight_writeback(n):
        return pltpu.make_async_copy(weights_bf16[n], (win_out, wout_out, wup_out, wdown_out)[n], out_sem.at[n])

    @pl.when(g == 0)
    def _():
        _load_weights((win_hbm, wout_hbm, wup_hbm, wdown_hbm), weights_bf16, stage, stage_sem)
        for n in range(len(weights_bf16)):
            weight_writeback(n).start()

    @pl.when(g == nblk)
    def _():
        for n in range(len(weights_bf16)):
            weight_writeback(n).wait()

    @pl.when(jm == 0)
    def _():
        pbuf[pl.ds(0, POOL_PAD), :] = jnp.zeros((POOL_PAD, POOL_WIDTH), F32)
        st[...] = jnp.zeros(st.shape, F32)

    @pl.when(jf == 0)
    def _():
        abuf[...] = jnp.zeros(abuf.shape, F32)

    def input_projection():
        h = _rmsnorm(x_ref[0], ln1_ref[...]).astype(BF16)
        return _dot(h, win_ref[...])

    def mixer_half(proj):
        _prompt_mixer_half(proj, x_ref, mk_ref, mv_ref, poolw_ref, pscale_ref, lbl_ref, onorm_ref, wout_ref,
                           pbuf, st, x2s, jm, tb)

    def ffn_half(proj):
        x2 = x2s[...]
        h2 = _rmsnorm(x2, ln2_ref[...]).astype(BF16)
        ab = _dot(h2, wup_ref[...])
        a = ab[:, :D_FF]
        hist = abuf[...]
        sub = lax.broadcasted_iota(jnp.int32, (CONV_PAD, D_FF), 0)
        r1 = pltpu.roll(a, 1, 0)
        r2 = pltpu.roll(a, 2, 0)
        h1 = jnp.broadcast_to(hist[CONV_PAD - 1:CONV_PAD, :], (CONV_PAD, D_FF))
        h0 = jnp.broadcast_to(hist[CONV_PAD - 2:CONV_PAD - 1, :], (CONV_PAD, D_FF))
        a_m1 = jnp.concatenate([jnp.where(sub >= 1, r1[:CONV_PAD], h1), r1[CONV_PAD:]], axis=0)
        a_m2 = jnp.concatenate([jnp.where(sub >= 2, r2[:CONV_PAD], jnp.where(sub == 1, h1, h0)), r2[CONV_PAD:]],
                               axis=0)
        conv = cb_ref[...] + cw_ref[0:1, :] * a_m2 + cw_ref[1:2, :] * a_m1 + cw_ref[2:3, :] * a
        act = _gelu_tanh(conv) * ab[:, D_FF:]
        abuf[...] = a[tb - CONV_PAD:, :]
        if proj is not None:
            act = _after(act, proj, zero_ref[...])
        y_ref[0] = _rmsnorm(x2 + _dot(act.astype(BF16), wdown_ref[...]), lnf_ref[...])

    @pl.when(g == 0)
    def _():
        mixer_half(input_projection())

    @pl.when(jnp.logical_and(g > 0, g < nblk))
    def _():
        proj = input_projection()
        ffn_half(proj)
        mixer_half(proj)

    @pl.when(g == nblk)
    def _():
        ffn_half(None)

    @pl.when(jnp.logical_and(jm == nt - 1, g < nblk))
    def _():
        npool_ref[0] = pbuf[pl.ds(1, POOL_HIST), :]
        for hd in range(HGRN_HEADS):
            ns_ref[0, hd] = st[hd].T

    @pl.when(jnp.logical_and(jf == nt - 1, g >= 1))
    def _():
        nconv_ref[0] = abuf[pl.ds(CONV_PAD - (CONV_W - 1), CONV_W - 1), :]


def _prompt_mixer_half(proj, x_ref, mk_ref, mv_ref, poolw_ref, pscale_ref, lbl_ref, onorm_ref, wout_ref,
                       pbuf, st, x2s, jm, tb):
    x = x_ref[0]

    pbuf[pl.ds(POOL_PAD, tb), :] = proj[:, OFF_U:OFF_U + POOL_WIDTH]
    posf = (jm * tb + 1 + lax.broadcasted_iota(jnp.int32, (tb, 1), 0)).astype(F32)
    dm = _pool_means(lambda j, half: pbuf[pl.ds(POOL_PAD - j, tb), pl.ds(LANES * half, LANES)], posf, (tb, LANES))
    o_pool = _dot(dm.astype(BF16), poolw_ref[...]) * pscale_ref[...]
    pbuf[pl.ds(0, POOL_PAD), :] = pbuf[pl.ds(tb, POOL_PAD), :]

    qx = proj[:, OFF_X:OFF_X + XATTN_WIDTH] * (XATTN_DH ** -0.5)
    o_x = _cross_attention(qx, mk_ref[0].astype(BF16), mv_ref[0].astype(BF16))

    lb = _forget_lower_bound(lbl_ref[...])
    qf, k, log_f = _hgrn_gates(proj, lb)
    v = proj[:, OFF_I:OFF_I + HGRN_WIDTH]
    gg = proj[:, OFF_G:OFF_G + HGRN_WIDTH]
    gate = gg * jax.nn.sigmoid(gg) * onorm_ref[...]
    a_all = _segment_cumsum(log_f, CHUNK)
    states = [st[hd] for hd in range(HGRN_HEADS)]
    o_rows = []
    for c in range(tb // CHUNK):
        rs = slice(c * CHUNK, (c + 1) * CHUNK)
        qf_c, k_c, lf_c, v_c, a = qf[rs], k[rs], log_f[rs], v[rs], a_all[rs]
        a_end = a[CHUNK - 1:CHUNK, :]
        q_in = (qf_c * jnp.exp2(a)).astype(BF16)
        k_out = (k_c * jnp.exp2(a_end - a)).astype(BF16)
        decay = jnp.exp2(a_end)
        v_b = v_c.astype(BF16)
        qk = qf_c * k_c
        factors = _level_factors(a, qf_c, k_c, lf_c, CHUNK, CHUNK)
        heads = [slice(hd * HGRN_DK, (hd + 1) * HGRN_DK) for hd in range(HGRN_HEADS)]
        inter = [_dot_nt(q_in[:, sl], states[hd].astype(BF16)) for hd, sl in enumerate(heads)]
        update = [_dot_tn(v_b[:, sl], k_out[:, sl]) for sl in heads]
        scores = [_intra_scores(factors, hd, CHUNK, CHUNK).astype(BF16) for hd in range(HGRN_HEADS)]
        intra = [_dot(scores[hd], v_b[:, sl]) for hd, sl in enumerate(heads)]
        o_heads = []
        for hd, sl in enumerate(heads):
            o = intra[hd] + inter[hd] + jnp.sum(qk[:, sl], axis=-1, keepdims=True) * v_c[:, sl]
            states[hd] = states[hd] * decay[:, sl] + update[hd]
            o_heads.append(_head_norm_gate(o, gate[rs, sl]))
        o_rows.append(jnp.concatenate(o_heads, axis=-1))
    for hd in range(HGRN_HEADS):
        st[hd] = states[hd]
    o_hgrn = jnp.concatenate(o_rows, axis=0)

    mixed = jnp.concatenate([o_pool, o_hgrn, o_x], axis=-1).astype(BF16)
    x2s[...] = x + _dot(mixed, wout_ref[...])


def _prompt_call(x, mk, mv, mixer_w, ffn_w, tb):
    b, l, d = x.shape
    nt = l // tb
    nblk = b * nt
    ln1, w_in, pool_wbd, pool_scale, lb_logits, onorm, w_out = mixer_w
    ln2, w_up, conv_w, conv_b, w_down, lnf = ffn_w

    def mixer_blk(g):
        return jnp.minimum(g, nblk - 1)

    def ffn_blk(g):
        return jnp.maximum(g - 1, 0)

    x_spec = pl.BlockSpec((1, tb, d), lambda g: (mixer_blk(g) // nt, mixer_blk(g) % nt, 0))
    mem = pl.BlockSpec((1, N_MEM, XATTN_WIDTH), lambda g: (mixer_blk(g) // nt, 0, 0))
    y_spec = pl.BlockSpec((1, tb, d), lambda g: (ffn_blk(g) // nt, ffn_blk(g) % nt, 0))
    in_hbm = pl.BlockSpec(memory_space=pl.ANY)
    weights = (w_in, w_out, w_up, w_down)
    assert all(w.dtype == F32 and w.shape[0] % 16 == 0 and w.shape[1] % STAGE_COLS == 0 for w in weights)
    return pl.pallas_call(
        functools.partial(_prompt_body, tb=tb, nt=nt, nblk=nblk),
        grid=(nblk + 1,),
        in_specs=[x_spec, mem, mem,
                  _const_spec((1, d)), in_hbm, _const_spec((POOL_WIDTH, POOL_WIDTH)),
                  _const_spec((1, POOL_WIDTH)), _const_spec(lb_logits.shape), _const_spec((1, HGRN_WIDTH)),
                  in_hbm,
                  _const_spec((1, d)), in_hbm, _const_spec((CONV_W, D_FF)),
                  _const_spec((1, D_FF)), in_hbm, _const_spec((1, d)), _const_spec((1, LANES))],
        out_specs=[y_spec,
                   pl.BlockSpec((1, POOL_HIST, POOL_WIDTH), lambda g: (mixer_blk(g) // nt, 0, 0)),
                   pl.BlockSpec((1, HGRN_HEADS, HGRN_DK, HGRN_DV), lambda g: (mixer_blk(g) // nt, 0, 0, 0)),
                   pl.BlockSpec((1, CONV_W - 1, D_FF), lambda g: (ffn_blk(g) // nt, 0, 0))]
                  + [pl.BlockSpec(memory_space=pl.ANY) for _ in weights],
        out_shape=[jax.ShapeDtypeStruct((b, l, d), F32),
                   jax.ShapeDtypeStruct((b, POOL_HIST, POOL_WIDTH), F32),
                   jax.ShapeDtypeStruct((b, HGRN_HEADS, HGRN_DK, HGRN_DV), F32),
                   jax.ShapeDtypeStruct((b, CONV_W - 1, D_FF), F32)]
                  + [jax.ShapeDtypeStruct(w.shape, BF16) for w in weights],
        scratch_shapes=[pltpu.VMEM((POOL_PAD + tb, POOL_WIDTH), F32),
                        pltpu.VMEM((HGRN_HEADS, HGRN_DV, HGRN_DK), F32),
                        pltpu.VMEM((CONV_PAD, D_FF), F32),
                        pltpu.VMEM((tb, d), F32)]
                       + [pltpu.VMEM(w.shape, BF16) for w in weights]
                       + [pltpu.VMEM((2, STAGE_ROWS, STAGE_COLS), F32),
                          pltpu.SemaphoreType.DMA((2,)),
                          pltpu.SemaphoreType.DMA((len(weights),))],
        compiler_params=pltpu.CompilerParams(dimension_semantics=("arbitrary",),
                                             vmem_limit_bytes=VMEM_LIMIT_BYTES),
        name="prompt_layer",
    )(x, mk, mv, ln1, w_in, pool_wbd, pool_scale, lb_logits, onorm, w_out,
      ln2, w_up, conv_w, conv_b, w_down, lnf, jnp.zeros((1, LANES), jnp.int32))


def _sample_mixer_body(x_ref, hist_ref, s0_ref, mkt_ref, mvt_ref, ln1_ref, win_ref, poolw_ref, pscale_ref, lbl_ref,
                       onorm_ref, wout_ref, x2_ref, npool_ref, ns_ref, pbuf, *, gs, sl_len):
    rows = gs * sl_len
    seqs = [slice(s * sl_len, (s + 1) * sl_len) for s in range(gs)]
    x = x_ref[...].reshape(rows, D_MODEL)
    h = _rmsnorm(x, ln1_ref[...]).astype(BF16)
    proj = _dot(h, win_ref[...])

    pbuf[:, pl.ds(1, POOL_HIST), :] = hist_ref[...]
    pbuf[:, pl.ds(POOL_PAD, sl_len), :] = proj[:, OFF_U:OFF_U + POOL_WIDTH].reshape(gs, sl_len, POOL_WIDTH)
    posf = (PAST_LEN + 1 + lax.broadcasted_iota(jnp.int32, (1, sl_len, 1), 1)).astype(F32)
    dm = _pool_means(lambda j, half: pbuf[:, pl.ds(POOL_PAD - j, sl_len), pl.ds(LANES * half, LANES)],
                     posf, (gs, sl_len, LANES))
    npool_ref[...] = pbuf[:, pl.ds(sl_len + 1, POOL_HIST), :]
    o_pool = _dot(dm.reshape(rows, POOL_WIDTH).astype(BF16), poolw_ref[...]) * pscale_ref[...]

    qx3 = (proj[:, OFF_X:OFF_X + XATTN_WIDTH] * (XATTN_DH ** -0.5)).reshape(gs, sl_len, XATTN_WIDTH)
    head_of_lane = lax.broadcasted_iota(jnp.int32, qx3.shape, 2) // XATTN_DH
    qs3 = jnp.concatenate([jnp.where(head_of_lane == hd, qx3, 0.0) for hd in range(XATTN_HEADS)],
                          axis=1).astype(BF16)
    hrows = XATTN_HEADS * sl_len
    scores = jnp.concatenate([_dot(qs3[s], mkt_ref[s].astype(BF16)) for s in range(gs)], axis=0)
    p = _softmax_rows(scores).astype(BF16)
    o4 = jnp.concatenate([_dot_nt(p[s * hrows:(s + 1) * hrows], mvt_ref[s].astype(BF16)) for s in range(gs)],
                         axis=0)
    o4 = o4.reshape(gs, XATTN_HEADS, sl_len, XATTN_WIDTH)
    o_x3 = jnp.zeros(qx3.shape, F32)
    for hd in range(XATTN_HEADS):
        o_x3 = jnp.where(head_of_lane == hd, o4[:, hd], o_x3)
    o_x = o_x3.reshape(rows, XATTN_WIDTH)

    lb = _forget_lower_bound(lbl_ref[...])
    qf, k, log_f = _hgrn_gates(proj, lb)
    v = proj[:, OFF_I:OFF_I + HGRN_WIDTH]
    gg = proj[:, OFF_G:OFF_G + HGRN_WIDTH]
    gate = gg * jax.nn.sigmoid(gg) * onorm_ref[...]
    a = _segment_cumsum(log_f, sl_len)
    a3 = a.reshape(gs, sl_len, HGRN_WIDTH)
    a_end = jnp.broadcast_to(a3[:, sl_len - 1:sl_len, :], a3.shape).reshape(rows, HGRN_WIDTH)
    q_in = (qf * jnp.exp2(a)).astype(BF16)
    k_out = (k * jnp.exp2(a_end - a)).astype(BF16)
    decay = jnp.exp2(a_end)
    v_b = v.astype(BF16)
    qk = qf * k
    factors = _level_factors(a, qf, k, log_f, rows, sl_len)
    heads = [slice(hd * HGRN_DK, (hd + 1) * HGRN_DK) for hd in range(HGRN_HEADS)]
    inter = [jnp.concatenate([_dot(q_in[r, sl], s0_ref[s, hd].astype(BF16)) for s, r in enumerate(seqs)], axis=0)
             for hd, sl in enumerate(heads)]
    o_heads = []
    for hd, sl in enumerate(heads):
        p_h = _intra_scores(factors, hd, rows, sl_len)
        o = _dot(p_h.astype(BF16), v_b[:, sl]) + inter[hd]
        o = o + jnp.sum(qk[:, sl], axis=-1, keepdims=True) * v[:, sl]
        o_heads.append(_head_norm_gate(o, gate[:, sl]))
    updates = [[_dot_tn(k_out[r, sl], v_b[r, sl]) for sl in heads] for r in seqs]
    for s, r in enumerate(seqs):
        for hd, sl in enumerate(heads):
            decay_cols = jnp.broadcast_to(decay[r, sl][sl_len - 1:sl_len, :], (HGRN_DV, HGRN_DK)).T
            ns_ref[s, hd] = decay_cols * s0_ref[s, hd] + updates[s][hd]

    mixed = jnp.concatenate([o_pool] + o_heads + [o_x], axis=-1).astype(BF16)
    x2_ref[...] = (x + _dot(mixed, wout_ref[...])).reshape(gs, sl_len, D_MODEL)


def _sample_mixer_call(x, hist, s0, mkt, mvt, mixer_w, gs):
    b, l, d = x.shape
    ln1, w_in, pool_wbd, pool_scale, lb_logits, onorm, w_out = mixer_w
    grid = (b // gs,)
    blk = pl.BlockSpec((gs, l, d), lambda i: (i, 0, 0))
    histb = pl.BlockSpec((gs, POOL_HIST, POOL_WIDTH), lambda i: (i, 0, 0))
    sb = pl.BlockSpec((gs, HGRN_HEADS, HGRN_DK, HGRN_DV), lambda i: (i, 0, 0, 0))
    mem = pl.BlockSpec((gs, XATTN_WIDTH, N_MEM), lambda i: (i, 0, 0))
    return pl.pallas_call(
        functools.partial(_sample_mixer_body, gs=gs, sl_len=l),
        grid=grid,
        in_specs=[blk, histb, sb, mem, mem,
                  _const_spec((1, d)), _const_spec((d, D_IN)), _const_spec((POOL_WIDTH, POOL_WIDTH)),
                  _const_spec((1, POOL_WIDTH)), _const_spec(lb_logits.shape), _const_spec((1, HGRN_WIDTH)),
                  _const_spec((d, d))],
        out_specs=[blk, histb, sb],
        out_shape=[jax.ShapeDtypeStruct((b, l, d), F32),
                   jax.ShapeDtypeStruct((b, POOL_HIST, POOL_WIDTH), F32),
                   jax.ShapeDtypeStruct((b, HGRN_HEADS, HGRN_DK, HGRN_DV), F32)],
        scratch_shapes=[pltpu.VMEM((gs, POOL_PAD + l, POOL_WIDTH), F32)],
        compiler_params=pltpu.CompilerParams(dimension_semantics=("arbitrary",),
                                             vmem_limit_bytes=VMEM_LIMIT_BYTES),
        name="sample_mixer",
    )(x, hist, s0, mkt, mvt, ln1, w_in, pool_wbd, pool_scale, lb_logits, onorm, w_out)


def _sample_ffn_body(x_ref, chist_ref, ln2_ref, wup_ref, cw_ref, cb_ref, wdown_ref, lnf_ref, y_ref, nconv_ref,
                     *, gs, sl_len):
    rows = gs * sl_len
    x = x_ref[...].reshape(rows, D_MODEL)
    h = _rmsnorm(x, ln2_ref[...]).astype(BF16)
    ab = _dot(h, wup_ref[...])
    a = ab[:, :D_FF]
    ridx = lax.broadcasted_iota(jnp.int32, (rows, D_FF), 0) % sl_len
    hist = chist_ref[...]
    h1 = jnp.broadcast_to(hist[:, 1:2, :], (gs, sl_len, D_FF)).reshape(rows, D_FF)
    h0 = jnp.broadcast_to(hist[:, 0:1, :], (gs, sl_len, D_FF)).reshape(rows, D_FF)
    a_m1 = jnp.where(ridx >= 1, pltpu.roll(a, 1, 0), h1)
    a_m2 = jnp.where(ridx >= 2, pltpu.roll(a, 2, 0), jnp.where(ridx == 1, h1, h0))
    conv = cb_ref[...] + cw_ref[0:1, :] * a_m2 + cw_ref[1:2, :] * a_m1 + cw_ref[2:3, :] * a
    act = _gelu_tanh(conv) * ab[:, D_FF:]
    nconv_ref[...] = a.reshape(gs, sl_len, D_FF)[:, sl_len - (CONV_W - 1):, :]
    y_ref[...] = _ffn_tail(x, act, wdown_ref, lnf_ref).reshape(gs, sl_len, D_MODEL)


def _sample_ffn_call(x, chist, ffn_w, gs):
    b, l, d = x.shape
    ln2, w_up, conv_w, conv_b, w_down, lnf = ffn_w
    blk = pl.BlockSpec((gs, l, d), lambda i: (i, 0, 0))
    cblk = pl.BlockSpec((gs, CONV_W - 1, D_FF), lambda i: (i, 0, 0))
    return pl.pallas_call(
        functools.partial(_sample_ffn_body, gs=gs, sl_len=l),
        grid=(b // gs,),
        in_specs=[blk, cblk, _const_spec((1, d)), _const_spec((d, 2 * D_FF)), _const_spec((CONV_W, D_FF)),
                  _const_spec((1, D_FF)), _const_spec((D_FF, d)), _const_spec((1, d))],
        out_specs=[blk, cblk],
        out_shape=[jax.ShapeDtypeStruct((b, l, d), F32),
                   jax.ShapeDtypeStruct((b, CONV_W - 1, D_FF), F32)],
        compiler_params=pltpu.CompilerParams(dimension_semantics=("arbitrary",),
                                             vmem_limit_bytes=VMEM_LIMIT_BYTES),
        name="sample_ffn",
    )(x, chist, ln2, w_up, conv_w, conv_b, w_down, lnf)


def _block_diag(pool_w):
    n = pool_w.shape[0]
    out = jnp.zeros((n * POOL_GROUP, n * POOL_GROUP), pool_w.dtype)
    for g in range(n):
        out = lax.dynamic_update_slice(out, pool_w[g], (g * POOL_GROUP, g * POOL_GROUP))
    return out


def _layer(x_prompt, x_sample, mem_prompt, state_pool, state_hgrn, state_conv, cache_mem_k, cache_mem_v,
           ln1_g, w_in, pool_w, pool_scale, hgrn_lb_logits, hgrn_onorm_g, mem_norm_g, w_mem_kv, w_out,
           ln2_g, w_up, conv_w, conv_b, w_down, lnf_g, *, prompt_tb, mixer_gs, ffn_gs):
    row = lambda a: a.reshape(1, -1)
    pool_wbd = _block_diag(pool_w).astype(BF16)

    def mixer_w(w_in, w_out):
        return (row(ln1_g), w_in, pool_wbd, row(pool_scale), hgrn_lb_logits, row(hgrn_onorm_g), w_out)

    def ffn_w(w_up, w_down):
        return (row(ln2_g), w_up, conv_w, row(conv_b), w_down, row(lnf_g))

    mk, mv = _memkv_call(mem_prompt, row(mem_norm_g), w_mem_kv.astype(BF16))
    y_prompt, new_pool_p, new_s_p, new_conv_p, w_in_b, w_out_b, w_up_b, w_down_b = _prompt_call(
        x_prompt, mk, mv, mixer_w(w_in, w_out), ffn_w(w_up, w_down), tb=prompt_tb)

    nb = x_sample.shape[0]
    smkt = cache_mem_k.transpose(0, 2, 3, 1).reshape(nb, XATTN_WIDTH, N_MEM)
    smvt = cache_mem_v.transpose(0, 2, 3, 1).reshape(nb, XATTN_WIDTH, N_MEM)
    xs, new_pool_s, new_s_s = _sample_mixer_call(x_sample, state_pool, state_hgrn, smkt, smvt,
                                                 mixer_w(w_in_b, w_out_b), gs=mixer_gs)
    y_sample, new_conv_s = _sample_ffn_call(xs, state_conv, ffn_w(w_up_b, w_down_b), gs=ffn_gs)
    bp = x_prompt.shape[0]
    heads = (bp, N_MEM, XATTN_HEADS, XATTN_DH)
    return (y_prompt, y_sample, new_pool_p, new_s_p, new_conv_p, mk.reshape(heads), mv.reshape(heads),
            new_pool_s, new_s_s, new_conv_s)


def kernel(x_prompt, x_sample, mem_prompt, state_pool, state_hgrn, state_conv, cache_mem_k, cache_mem_v,
           ln1_g, w_in, pool_w, pool_scale, hgrn_lb_logits, hgrn_onorm_g, mem_norm_g, w_mem_kv, w_out,
           ln2_g, w_up, conv_w, conv_b, w_down, lnf_g):
    assert w_in.shape[0] == 1, "one layer"
    outs = _layer(x_prompt, x_sample, mem_prompt, state_pool[0], state_hgrn[0], state_conv[0],
                  cache_mem_k[0], cache_mem_v[0], ln1_g[0], w_in[0], pool_w[0], pool_scale[0], hgrn_lb_logits,
                  hgrn_onorm_g[0], mem_norm_g[0], w_mem_kv[0], w_out[0], ln2_g[0], w_up[0], conv_w[0], conv_b[0],
                  w_down[0], lnf_g, prompt_tb=256, mixer_gs=16, ffn_gs=32)
    y_prompt, y_sample = outs[0], outs[1]
    return (y_prompt, y_sample) + tuple(o[None] for o in outs[2:])
```

```python
import functools

import jax
import jax.numpy as jnp
from jax import lax
from jax.experimental import pallas as pl
from jax.experimental.pallas import tpu as pltpu

F32 = jnp.float32
BF16 = jnp.bfloat16

D_MODEL = 1024
POOL_WIDTH = 256
POOL_GROUP = 64
POOL_HIST = 15
HGRN_WIDTH = 512
HGRN_HEADS = 4
HGRN_DK = 128
HGRN_DV = 128
XATTN_WIDTH = 256
XATTN_HEADS = 4
XATTN_DH = 64
N_MEM = 256
D_FF = 2816
CONV_W = 3
EPS = 1e-6
PAST_LEN = 16384
D_IN = POOL_WIDTH + 4 * HGRN_WIDTH + XATTN_WIDTH
OFF_U, OFF_Q, OFF_F, OFF_I, OFF_G, OFF_X = 0, 256, 768, 1280, 1792, 2304

CHUNK = 64
POOL_PAD = 16
CONV_PAD = 8
LANES = 128
MEMKV_GROUP = 4
STAGE_ROWS = 512
STAGE_COLS = 512
STAGE_SLOTS = 6
VMEM_LIMIT_BYTES = 56 * 1024 * 1024

_NT = (((1,), (1,)), ((), ()))
_TN = (((0,), (0,)), ((), ()))


def _dot(a, b):
    return jnp.dot(a, b, preferred_element_type=F32)


def _dot_nt(a, b):
    return lax.dot_general(a, b, _NT, preferred_element_type=F32)


def _dot_tn(a, b):
    return lax.dot_general(a, b, _TN, preferred_element_type=F32)


def _rmsnorm(x, g):
    return x * lax.rsqrt(jnp.mean(x * x, axis=-1, keepdims=True) + EPS) * g


def _const_spec(shape):
    nd = len(shape)
    return pl.BlockSpec(shape, lambda *_: (0,) * nd, pipeline_mode=pl.Buffered(1))


def _memkv_body(mem_ref, g_ref, w_ref, k_ref, v_ref):
    nb = mem_ref.shape[0]
    h = _rmsnorm(mem_ref[...].reshape(nb * N_MEM, D_MODEL), g_ref[...]).astype(BF16)
    kv = _dot(h, w_ref[...])
    k_ref[...] = kv[:, :XATTN_WIDTH].reshape(nb, N_MEM, XATTN_WIDTH)
    v_ref[...] = kv[:, XATTN_WIDTH:].reshape(nb, N_MEM, XATTN_WIDTH)


def _memkv_call(mem, g, w):
    b = mem.shape[0]
    nb = MEMKV_GROUP
    out = jax.ShapeDtypeStruct((b, N_MEM, XATTN_WIDTH), F32)
    return pl.pallas_call(
        _memkv_body,
        grid=(b // nb,),
        in_specs=[pl.BlockSpec((nb, N_MEM, D_MODEL), lambda i: (i, 0, 0)),
                  _const_spec((1, D_MODEL)),
                  _const_spec((D_MODEL, 2 * XATTN_WIDTH))],
        out_specs=[pl.BlockSpec((nb, N_MEM, XATTN_WIDTH), lambda i: (i, 0, 0))] * 2,
        out_shape=[out, out],
        compiler_params=pltpu.CompilerParams(dimension_semantics=("arbitrary",)),
        name="memkv",
    )(mem, g, w)


def _forget_lower_bound(logits):
    z = logits - jnp.max(logits, axis=0, keepdims=True)
    e = jnp.exp(z)
    return e[0:1, :] / jnp.sum(e, axis=0, keepdims=True)


def _hgrn_gates(proj, lb):
    fp = proj[:, OFF_F:OFF_F + HGRN_WIDTH]
    q = proj[:, OFF_Q:OFF_Q + HGRN_WIDTH]
    log_f = jnp.log2(lb + (1.0 - lb) * jax.nn.sigmoid(fp))
    k = (1.0 - lb) * jax.nn.sigmoid(-fp)
    qf = q * jax.nn.sigmoid(q)
    return qf, k, log_f


def _segment_cumsum(x, seq):
    ridx = lax.broadcasted_iota(jnp.int32, x.shape, 0) & (seq - 1)
    sh = 1
    while sh < seq:
        x = x + jnp.where(ridx >= sh, pltpu.roll(x, sh, 0), 0.0)
        sh *= 2
    return x


def _level_factor(a, qf, k, log_f, m, rows):
    n = a.shape[1]
    ridx = lax.broadcasted_iota(jnp.int32, (rows, n), 0)
    upper = (ridx & m) != 0
    if m == 1:
        d = jnp.where(upper, log_f, 0.0)
    else:
        if (2 * m) % 8 == 0:
            nb = rows // (2 * m)
            a3 = a.reshape(nb, 2 * m, n)
            ref = jnp.broadcast_to(a3[:, m - 1:m, :], (nb, 2 * m, n)).reshape(rows, n)
        else:
            a3 = a.reshape(rows // 8, 8, n)
            sub = lax.broadcasted_iota(jnp.int32, a3.shape, 1)
            ref = jnp.where(sub < 4,
                            jnp.broadcast_to(a3[:, 1:2, :], a3.shape),
                            jnp.broadcast_to(a3[:, 5:6, :], a3.shape)).reshape(rows, n)
        d = -jnp.abs(a - ref)
    return (jnp.where(upper, qf, k) * jnp.exp2(d)).astype(BF16)


def _level_factors(a, qf, k, log_f, rows, seq):
    out, m = [], seq // 2
    while m >= 1:
        out.append((m, _level_factor(a, qf, k, log_f, m, rows)))
        m //= 2
    return out


def _intra_scores(factors, head, rows, seq):
    sl = slice(head * HGRN_DK, (head + 1) * HGRN_DK)
    t = lax.broadcasted_iota(jnp.int32, (rows, rows), 0)
    s = lax.broadcasted_iota(jnp.int32, (rows, rows), 1)
    x = t ^ s
    products = [(m, _dot_nt(y[:, sl], y[:, sl])) for m, y in reversed(factors)]
    total = jnp.zeros((rows, rows), F32)
    for m, p in products:
        total = jnp.where(x >= m, p, total)
    return jnp.where((t > s) & (x < seq), total, 0.0)


def _head_norm_gate(o, gate):
    return o * lax.rsqrt(jnp.mean(o * o, axis=-1, keepdims=True) + EPS) * gate


def _softmax_rows(s):
    e = jnp.exp(s - jnp.max(s, axis=-1, keepdims=True))
    return e / jnp.sum(e, axis=-1, keepdims=True)


def _cross_attention(qx, mk, mv):
    rows = qx.shape[0]
    head_of_lane = lax.broadcasted_iota(jnp.int32, qx.shape, 1) // XATTN_DH
    qs = jnp.concatenate([jnp.where(head_of_lane == h, qx, 0.0) for h in range(XATTN_HEADS)], axis=0)
    p = _softmax_rows(_dot_nt(qs.astype(BF16), mk))
    o = _dot(p.astype(BF16), mv)
    out = jnp.zeros(qx.shape, F32)
    for h in range(XATTN_HEADS):
        out = jnp.where(head_of_lane == h, o[h * rows:(h + 1) * rows, :], out)
    return out


def _pool_means(ld, posf, shape):
    lane = lax.broadcasted_iota(jnp.int32, shape, len(shape) - 1)
    first = lane < POOL_GROUP
    u_lo, u_hi = ld(0, 0), ld(0, 1)
    t2 = u_lo + ld(1, 0)
    t4 = t2 + ld(2, 0) + ld(3, 0)
    t8 = u_hi
    for j in range(1, 8):
        t8 = t8 + ld(j, 1)
    t16 = t8
    for j in range(8, 16):
        t16 = t16 + ld(j, 1)
    cnt_lo = jnp.where(first, jnp.minimum(2.0, posf), jnp.minimum(4.0, posf))
    cnt_hi = jnp.where(first, jnp.minimum(8.0, posf), jnp.minimum(16.0, posf))
    lo = jnp.where(first, t2, t4) / cnt_lo - u_lo
    hi = jnp.where(first, t8, t16) / cnt_hi - u_hi
    return jnp.concatenate([lo, hi], axis=-1)


def _after(x, anchor, zero):
    r, n = anchor.shape
    s = jnp.sum(anchor.reshape(r // 8, 8, n), axis=0)
    c = s[:, :LANES]
    for i in range(1, n // LANES):
        c = c + s[:, i * LANES:(i + 1) * LANES]
    z = lax.bitcast_convert_type(lax.bitcast_convert_type(c[0:1, :], jnp.int32) & zero, F32)
    return jnp.concatenate([x[:, :LANES] + z, x[:, LANES:]], axis=1)


def _gelu_tanh(x):
    c = 0.7978845608028654
    half_x = 0.5 * x
    return half_x + half_x * jnp.tanh(x * (c + (0.044715 * c) * (x * x)))


def _ffn_tail(x, act, wdown_ref, lnf_ref):
    y = x + _dot(act.astype(BF16), wdown_ref[...])
    return _rmsnorm(y, lnf_ref[...])


def _weight_blocks(shapes):
    return [(i, r0, min(STAGE_ROWS, rows - r0), c0)
            for i, (rows, cols) in enumerate(shapes)
            for c0 in range(0, cols, STAGE_COLS)
            for r0 in range(0, rows, STAGE_ROWS)]


def _load_weights(srcs, dsts, stage, sem):
    blocks = _weight_blocks([w.shape for w in srcs])
    nslot = stage.shape[0]

    def copy(n):
        i, r0, nr, c0 = blocks[n]
        return pltpu.make_async_copy(srcs[i].at[pl.ds(r0, nr), pl.ds(c0, STAGE_COLS)],
                                     stage.at[n % nslot, pl.ds(0, nr), :], sem.at[n % nslot])

    ahead = nslot - 1
    for n in range(min(ahead, len(blocks))):
        copy(n).start()
    for n, (i, r0, nr, c0) in enumerate(blocks):
        if n + ahead < len(blocks):
            copy(n + ahead).start()
        copy(n).wait()
        dsts[i][pl.ds(r0, nr), pl.ds(c0, STAGE_COLS)] = stage[n % nslot, pl.ds(0, nr), :].astype(BF16)


def _prompt_body(x_ref, mk_ref, mv_ref, ln1_ref, win_hbm, poolw_ref, pscale_ref, lbl_ref, onorm_ref,
                 wout_hbm, ln2_ref, wup_hbm, cw_ref, cb_ref, wdown_hbm, lnf_ref, zero_ref,
                 y_ref, npool_ref, ns_ref, nconv_ref, win_out, wout_out, wup_out, wdown_out,
                 pbuf, st, abuf, x2s, win_ref, wout_ref, wup_ref, wdown_ref, stage, stage_sem, out_sem,
                 *, tb, nt, nblk):
    g = pl.program_id(0)
    jm = jnp.minimum(g, nblk - 1) % nt
    jf = jnp.maximum(g - 1, 0) % nt

    weights_bf16 = (win_ref, wout_ref, wup_ref, wdown_ref)

    def weight_writeback(n):
        return pltpu.make_async_copy(weights_bf16[n], (win_out, wout_out, wup_out, wdown_out)[n], out_sem.at[n])

    @pl.when(g == 0)
    def _():
        _load_weights((win_hbm, wout_hbm, wup_hbm, wdown_hbm), weights_bf16, stage, stage_sem)
        for n in range(len(weights_bf16)):
            weight_writeback(n).start()

    @pl.when(g == nblk)
    def _():
        for n in range(len(weights_bf16)):
            weight_writeback(n).wait()

    @pl.when(jm == 0)
    def _():
        pbuf[pl.ds(0, POOL_PAD), :] = jnp.zeros((POOL_PAD, POOL_WIDTH), F32)
        st[...] = jnp.zeros(st.shape, F32)

    @pl.when(jf == 0)
    def _():
        abuf[...] = jnp.zeros(abuf.shape, F32)

    def input_projection():
        h = _rmsnorm(x_ref[0], ln1_ref[...]).astype(BF16)
        return _dot(h, win_ref[...])

    def mixer_half(proj):
        _prompt_mixer_half(proj, x_ref, mk_ref, mv_ref, poolw_ref, pscale_ref, lbl_ref, onorm_ref, wout_ref,
                           pbuf, st, x2s, jm, tb)

    def ffn_half(proj):
        x2 = x2s[...]
        h2 = _rmsnorm(x2, ln2_ref[...]).astype(BF16)
        ab = _dot(h2, wup_ref[...])
        a = ab[:, :D_FF]
        hist = abuf[...]
        sub = lax.broadcasted_iota(jnp.int32, (CONV_PAD, D_FF), 0)
        r1 = pltpu.roll(a, 1, 0)
        r2 = pltpu.roll(a, 2, 0)
        h1 = jnp.broadcast_to(hist[CONV_PAD - 1:CONV_PAD, :], (CONV_PAD, D_FF))
        h0 = jnp.broadcast_to(hist[CONV_PAD - 2:CONV_PAD - 1, :], (CONV_PAD, D_FF))
        a_m1 = jnp.concatenate([jnp.where(sub >= 1, r1[:CONV_PAD], h1), r1[CONV_PAD:]], axis=0)
        a_m2 = jnp.concatenate([jnp.where(sub >= 2, r2[:CONV_PAD], jnp.where(sub == 1, h1, h0)), r2[CONV_PAD:]],
                               axis=0)
        conv = cb_ref[...] + cw_ref[0:1, :] * a_m2 + cw_ref[1:2, :] * a_m1 + cw_ref[2:3, :] * a
        act = _gelu_tanh(conv) * ab[:, D_FF:]
        abuf[...] = a[tb - CONV_PAD:, :]
        if proj is not None:
            act = _after(act, proj, zero_ref[...])
        y_ref[0] = _rmsnorm(x2 + _dot(act.astype(BF16), wdown_ref[...]), lnf_ref[...])

    @pl.when(g == 0)
    def _():
        mixer_half(input_projection())

    @pl.when(jnp.logical_and(g > 0, g < nblk))
    def _():
        proj = input_projection()
        ffn_half(proj)
        mixer_half(proj)

    @pl.when(g == nblk)
    def _():
        ffn_half(None)

    @pl.when(jnp.logical_and(jm == nt - 1, g < nblk))
    def _():
        npool_ref[0] = pbuf[pl.ds(1, POOL_HIST), :]
        for hd in range(HGRN_HEADS):
            ns_ref[0, hd] = st[hd].T

    @pl.when(jnp.logical_and(jf == nt - 1, g >= 1))
    def _():
        nconv_ref[0] = abuf[pl.ds(CONV_PAD - (CONV_W - 1), CONV_W - 1), :]


def _prompt_mixer_half(proj, x_ref, mk_ref, mv_ref, poolw_ref, pscale_ref, lbl_ref, onorm_ref, wout_ref,
                       pbuf, st, x2s, jm, tb):
    x = x_ref[0]

    pbuf[pl.ds(POOL_PAD, tb), :] = proj[:, OFF_U:OFF_U + POOL_WIDTH]
    posf = (jm * tb + 1 + lax.broadcasted_iota(jnp.int32, (tb, 1), 0)).astype(F32)
    dm = _pool_means(lambda j, half: pbuf[pl.ds(POOL_PAD - j, tb), pl.ds(LANES * half, LANES)], posf, (tb, LANES))
    o_pool = _dot(dm.astype(BF16), poolw_ref[...]) * pscale_ref[...]
    pbuf[pl.ds(0, POOL_PAD), :] = pbuf[pl.ds(tb, POOL_PAD), :]

    qx = proj[:, OFF_X:OFF_X + XATTN_WIDTH] * (XATTN_DH ** -0.5)
    o_x = _cross_attention(qx, mk_ref[0].astype(BF16), mv_ref[0].astype(BF16))

    lb = _forget_lower_bound(lbl_ref[...])
    qf, k, log_f = _hgrn_gates(proj, lb)
    v = proj[:, OFF_I:OFF_I + HGRN_WIDTH]
    gg = proj[:, OFF_G:OFF_G + HGRN_WIDTH]
    gate = gg * jax.nn.sigmoid(gg) * onorm_ref[...]
    a_all = _segment_cumsum(log_f, CHUNK)
    states = [st[hd] for hd in range(HGRN_HEADS)]
    o_rows = []
    for c in range(tb // CHUNK):
        rs = slice(c * CHUNK, (c + 1) * CHUNK)
        qf_c, k_c, lf_c, v_c, a = qf[rs], k[rs], log_f[rs], v[rs], a_all[rs]
        a_end = a[CHUNK - 1:CHUNK, :]
        q_in = (qf_c * jnp.exp2(a)).astype(BF16)
        k_out = (k_c * jnp.exp2(a_end - a)).astype(BF16)
        decay = jnp.exp2(a_end)
        v_b = v_c.astype(BF16)
        qk = qf_c * k_c
        factors = _level_factors(a, qf_c, k_c, lf_c, CHUNK, CHUNK)
        heads = [slice(hd * HGRN_DK, (hd + 1) * HGRN_DK) for hd in range(HGRN_HEADS)]
        inter = [_dot_nt(q_in[:, sl], states[hd].astype(BF16)) for hd, sl in enumerate(heads)]
        update = [_dot_tn(v_b[:, sl], k_out[:, sl]) for sl in heads]
        scores = [_intra_scores(factors, hd, CHUNK, CHUNK).astype(BF16) for hd in range(HGRN_HEADS)]
        intra = [_dot(scores[hd], v_b[:, sl]) for hd, sl in enumerate(heads)]
        o_heads = []
        for hd, sl in enumerate(heads):
            o = intra[hd] + inter[hd] + jnp.sum(qk[:, sl], axis=-1, keepdims=True) * v_c[:, sl]
            states[hd] = states[hd] * decay[:, sl] + update[hd]
            o_heads.append(_head_norm_gate(o, gate[rs, sl]))
        o_rows.append(jnp.concatenate(o_heads, axis=-1))
    for hd in range(HGRN_HEADS):
        st[hd] = states[hd]
    o_hgrn = jnp.concatenate(o_rows, axis=0)

    mixed = jnp.concatenate([o_pool, o_hgrn, o_x], axis=-1).astype(BF16)
    x2s[...] = x + _dot(mixed, wout_ref[...])


def _prompt_call(x, mk, mv, mixer_w, ffn_w, tb):
    b, l, d = x.shape
    nt = l // tb
    nblk = b * nt
    ln1, w_in, pool_wbd, pool_scale, lb_logits, onorm, w_out = mixer_w
    ln2, w_up, conv_w, conv_b, w_down, lnf = ffn_w

    def mixer_blk(g):
        return jnp.minimum(g, nblk - 1)

    def ffn_blk(g):
        return jnp.maximum(g - 1, 0)

    x_spec = pl.BlockSpec((1, tb, d), lambda g: (mixer_blk(g) // nt, mixer_blk(g) % nt, 0))
    mem = pl.BlockSpec((1, N_MEM, XATTN_WIDTH), lambda g: (mixer_blk(g) // nt, 0, 0))
    y_spec = pl.BlockSpec((1, tb, d), lambda g: (ffn_blk(g) // nt, ffn_blk(g) % nt, 0))
    in_hbm = pl.BlockSpec(memory_space=pl.ANY)
    weights = (w_in, w_out, w_up, w_down)
    assert all(w.dtype == F32 and w.shape[0] % 16 == 0 and w.shape[1] % STAGE_COLS == 0 for w in weights)
    return pl.pallas_call(
        functools.partial(_prompt_body, tb=tb, nt=nt, nblk=nblk),
        grid=(nblk + 1,),
        in_specs=[x_spec, mem, mem,
                  _const_spec((1, d)), in_hbm, _const_spec((POOL_WIDTH, POOL_WIDTH)),
                  _const_spec((1, POOL_WIDTH)), _const_spec(lb_logits.shape), _const_spec((1, HGRN_WIDTH)),
                  in_hbm,
                  _const_spec((1, d)), in_hbm, _const_spec((CONV_W, D_FF)),
                  _const_spec((1, D_FF)), in_hbm, _const_spec((1, d)), _const_spec((1, LANES))],
        out_specs=[y_spec,
                   pl.BlockSpec((1, POOL_HIST, POOL_WIDTH), lambda g: (mixer_blk(g) // nt, 0, 0)),
                   pl.BlockSpec((1, HGRN_HEADS, HGRN_DK, HGRN_DV), lambda g: (mixer_blk(g) // nt, 0, 0, 0)),
                   pl.BlockSpec((1, CONV_W - 1, D_FF), lambda g: (ffn_blk(g) // nt, 0, 0))]
                  + [pl.BlockSpec(memory_space=pl.ANY) for _ in weights],
        out_shape=[jax.ShapeDtypeStruct((b, l, d), F32),
                   jax.ShapeDtypeStruct((b, POOL_HIST, POOL_WIDTH), F32),
                   jax.ShapeDtypeStruct((b, HGRN_HEADS, HGRN_DK, HGRN_DV), F32),
                   jax.ShapeDtypeStruct((b, CONV_W - 1, D_FF), F32)]
                  + [jax.ShapeDtypeStruct(w.shape, BF16) for w in weights],
        scratch_shapes=[pltpu.VMEM((POOL_PAD + tb, POOL_WIDTH), F32),
                        pltpu.VMEM((HGRN_HEADS, HGRN_DV, HGRN_DK), F32),
                        pltpu.VMEM((CONV_PAD, D_FF), F32),
                        pltpu.VMEM((tb, d), F32)]
                       + [pltpu.VMEM(w.shape, BF16) for w in weights]
                       + [pltpu.VMEM((STAGE_SLOTS, STAGE_ROWS, STAGE_COLS), F32),
                          pltpu.SemaphoreType.DMA((STAGE_SLOTS,)),
                          pltpu.SemaphoreType.DMA((len(weights),))],
        compiler_params=pltpu.CompilerParams(dimension_semantics=("arbitrary",),
                                             vmem_limit_bytes=VMEM_LIMIT_BYTES),
        name="prompt_layer",
    )(x, mk, mv, ln1, w_in, pool_wbd, pool_scale, lb_logits, onorm, w_out,
      ln2, w_up, conv_w, conv_b, w_down, lnf, jnp.zeros((1, LANES), jnp.int32))


def _sample_mixer_body(x_ref, hist_ref, s0_ref, mkt_ref, mvt_ref, ln1_ref, win_ref, poolw_ref, pscale_ref, lbl_ref,
                       onorm_ref, wout_ref, x2_ref, npool_ref, ns_ref, pbuf, *, gs, sl_len):
    rows = gs * sl_len
    seqs = [slice(s * sl_len, (s + 1) * sl_len) for s in range(gs)]
    x = x_ref[...].reshape(rows, D_MODEL)
    h = _rmsnorm(x, ln1_ref[...]).astype(BF16)
    proj = _dot(h, win_ref[...])

    pbuf[:, pl.ds(1, POOL_HIST), :] = hist_ref[...]
    pbuf[:, pl.ds(POOL_PAD, sl_len), :] = proj[:, OFF_U:OFF_U + POOL_WIDTH].reshape(gs, sl_len, POOL_WIDTH)
    posf = (PAST_LEN + 1 + lax.broadcasted_iota(jnp.int32, (1, sl_len, 1), 1)).astype(F32)
    dm = _pool_means(lambda j, half: pbuf[:, pl.ds(POOL_PAD - j, sl_len), pl.ds(LANES * half, LANES)],
                     posf, (gs, sl_len, LANES))
    npool_ref[...] = pbuf[:, pl.ds(sl_len + 1, POOL_HIST), :]
    o_pool = _dot(dm.reshape(rows, POOL_WIDTH).astype(BF16), poolw_ref[...]) * pscale_ref[...]

    qx3 = (proj[:, OFF_X:OFF_X + XATTN_WIDTH] * (XATTN_DH ** -0.5)).reshape(gs, sl_len, XATTN_WIDTH)
    head_of_lane = lax.broadcasted_iota(jnp.int32, qx3.shape, 2) // XATTN_DH
    qs3 = jnp.concatenate([jnp.where(head_of_lane == hd, qx3, 0.0) for hd in range(XATTN_HEADS)],
                          axis=1).astype(BF16)
    hrows = XATTN_HEADS * sl_len
    scores = jnp.concatenate([_dot(qs3[s], mkt_ref[s].astype(BF16)) for s in range(gs)], axis=0)
    p = _softmax_rows(scores).astype(BF16)
    o4 = jnp.concatenate([_dot_nt(p[s * hrows:(s + 1) * hrows], mvt_ref[s].astype(BF16)) for s in range(gs)],
                         axis=0)
    o4 = o4.reshape(gs, XATTN_HEADS, sl_len, XATTN_WIDTH)
    o_x3 = jnp.zeros(qx3.shape, F32)
    for hd in range(XATTN_HEADS):
        o_x3 = jnp.where(head_of_lane == hd, o4[:, hd], o_x3)
    o_x = o_x3.reshape(rows, XATTN_WIDTH)

    lb = _forget_lower_bound(lbl_ref[...])
    qf, k, log_f = _hgrn_gates(proj, lb)
    v = proj[:, OFF_I:OFF_I + HGRN_WIDTH]
    gg = proj[:, OFF_G:OFF_G + HGRN_WIDTH]
    gate = gg * jax.nn.sigmoid(gg) * onorm_ref[...]
    a = _segment_cumsum(log_f, sl_len)
    a3 = a.reshape(gs, sl_len, HGRN_WIDTH)
    a_end = jnp.broadcast_to(a3[:, sl_len - 1:sl_len, :], a3.shape).reshape(rows, HGRN_WIDTH)
    q_in = (qf * jnp.exp2(a)).astype(BF16)
    k_out = (k * jnp.exp2(a_end - a)).astype(BF16)
    decay = jnp.exp2(a_end)
    v_b = v.astype(BF16)
    qk = qf * k
    factors = _level_factors(a, qf, k, log_f, rows, sl_len)
    heads = [slice(hd * HGRN_DK, (hd + 1) * HGRN_DK) for hd in range(HGRN_HEADS)]
    inter = [jnp.concatenate([_dot(q_in[r, sl], s0_ref[s, hd].astype(BF16)) for s, r in enumerate(seqs)], axis=0)
             for hd, sl in enumerate(heads)]
    o_heads = []
    for hd, sl in enumerate(heads):
        p_h = _intra_scores(factors, hd, rows, sl_len)
        o = _dot(p_h.astype(BF16), v_b[:, sl]) + inter[hd]
        o = o + jnp.sum(qk[:, sl], axis=-1, keepdims=True) * v[:, sl]
        o_heads.append(_head_norm_gate(o, gate[:, sl]))
    updates = [[_dot_tn(k_out[r, sl], v_b[r, sl]) for sl in heads] for r in seqs]
    for s, r in enumerate(seqs):
        for hd, sl in enumerate(heads):
            decay_cols = jnp.broadcast_to(decay[r, sl][sl_len - 1:sl_len, :], (HGRN_DV, HGRN_DK)).T
            ns_ref[s, hd] = decay_cols * s0_ref[s, hd] + updates[s][hd]

    mixed = jnp.concatenate([o_pool] + o_heads + [o_x], axis=-1).astype(BF16)
    x2_ref[...] = (x + _dot(mixed, wout_ref[...])).reshape(gs, sl_len, D_MODEL)


def _sample_mixer_call(x, hist, s0, mkt, mvt, mixer_w, gs):
    b, l, d = x.shape
    ln1, w_in, pool_wbd, pool_scale, lb_logits, onorm, w_out = mixer_w
    grid = (b // gs,)
    blk = pl.BlockSpec((gs, l, d), lambda i: (i, 0, 0))
    histb = pl.BlockSpec((gs, POOL_HIST, POOL_WIDTH), lambda i: (i, 0, 0))
    sb = pl.BlockSpec((gs, HGRN_HEADS, HGRN_DK, HGRN_DV), lambda i: (i, 0, 0, 0))
    mem = pl.BlockSpec((gs, XATTN_WIDTH, N_MEM), lambda i: (i, 0, 0))
    return pl.pallas_call(
        functools.partial(_sample_mixer_body, gs=gs, sl_len=l),
        grid=grid,
        in_specs=[blk, histb, sb, mem, mem,
                  _const_spec((1, d)), _const_spec((d, D_IN)), _const_spec((POOL_WIDTH, POOL_WIDTH)),
                  _const_spec((1, POOL_WIDTH)), _const_spec(lb_logits.shape), _const_spec((1, HGRN_WIDTH)),
                  _const_spec((d, d))],
        out_specs=[blk, histb, sb],
        out_shape=[jax.ShapeDtypeStruct((b, l, d), F32),
                   jax.ShapeDtypeStruct((b, POOL_HIST, POOL_WIDTH), F32),
                   jax.ShapeDtypeStruct((b, HGRN_HEADS, HGRN_DK, HGRN_DV), F32)],
        scratch_shapes=[pltpu.VMEM((gs, POOL_PAD + l, POOL_WIDTH), F32)],
        compiler_params=pltpu.CompilerParams(dimension_semantics=("arbitrary",),
                                             vmem_limit_bytes=VMEM_LIMIT_BYTES),
        name="sample_mixer",
    )(x, hist, s0, mkt, mvt, ln1, w_in, pool_wbd, pool_scale, lb_logits, onorm, w_out)


def _sample_ffn_body(x_ref, chist_ref, ln2_ref, wup_ref, cw_ref, cb_ref, wdown_ref, lnf_ref, y_ref, nconv_ref,
                     *, gs, sl_len):
    rows = gs * sl_len
    x = x_ref[...].reshape(rows, D_MODEL)
    h = _rmsnorm(x, ln2_ref[...]).astype(BF16)
    ab = _dot(h, wup_ref[...])
    a = ab[:, :D_FF]
    ridx = lax.broadcasted_iota(jnp.int32, (rows, D_FF), 0) % sl_len
    hist = chist_ref[...]
    h1 = jnp.broadcast_to(hist[:, 1:2, :], (gs, sl_len, D_FF)).reshape(rows, D_FF)
    h0 = jnp.broadcast_to(hist[:, 0:1, :], (gs, sl_len, D_FF)).reshape(rows, D_FF)
    a_m1 = jnp.where(ridx >= 1, pltpu.roll(a, 1, 0), h1)
    a_m2 = jnp.where(ridx >= 2, pltpu.roll(a, 2, 0), jnp.where(ridx == 1, h1, h0))
    conv = cb_ref[...] + cw_ref[0:1, :] * a_m2 + cw_ref[1:2, :] * a_m1 + cw_ref[2:3, :] * a
    act = _gelu_tanh(conv) * ab[:, D_FF:]
    nconv_ref[...] = a.reshape(gs, sl_len, D_FF)[:, sl_len - (CONV_W - 1):, :]
    y_ref[...] = _ffn_tail(x, act, wdown_ref, lnf_ref).reshape(gs, sl_len, D_MODEL)


def _sample_ffn_call(x, chist, ffn_w, gs):
    b, l, d = x.shape
    ln2, w_up, conv_w, conv_b, w_down, lnf = ffn_w
    blk = pl.BlockSpec((gs, l, d), lambda i: (i, 0, 0))
    cblk = pl.BlockSpec((gs, CONV_W - 1, D_FF), lambda i: (i, 0, 0))
    return pl.pallas_call(
        functools.partial(_sample_ffn_body, gs=gs, sl_len=l),
        grid=(b // gs,),
        in_specs=[blk, cblk, _const_spec((1, d)), _const_spec((d, 2 * D_FF)), _const_spec((CONV_W, D_FF)),
                  _const_spec((1, D_FF)), _const_spec((D_FF, d)), _const_spec((1, d))],
        out_specs=[blk, cblk],
        out_shape=[jax.ShapeDtypeStruct((b, l, d), F32),
                   jax.ShapeDtypeStruct((b, CONV_W - 1, D_FF), F32)],
        compiler_params=pltpu.CompilerParams(dimension_semantics=("arbitrary",),
                                             vmem_limit_bytes=VMEM_LIMIT_BYTES),
        name="sample_ffn",
    )(x, chist, ln2, w_up, conv_w, conv_b, w_down, lnf)


def _block_diag(pool_w):
    n = pool_w.shape[0]
    out = jnp.zeros((n * POOL_GROUP, n * POOL_GROUP), pool_w.dtype)
    for g in range(n):
        out = lax.dynamic_update_slice(out, pool_w[g], (g * POOL_GROUP, g * POOL_GROUP))
    return out


def _layer(x_prompt, x_sample, mem_prompt, state_pool, state_hgrn, state_conv, cache_mem_k, cache_mem_v,
           ln1_g, w_in, pool_w, pool_scale, hgrn_lb_logits, hgrn_onorm_g, mem_norm_g, w_mem_kv, w_out,
           ln2_g, w_up, conv_w, conv_b, w_down, lnf_g, *, prompt_tb, mixer_gs, ffn_gs):
    row = lambda a: a.reshape(1, -1)
    pool_wbd = _block_diag(pool_w).astype(BF16)

    def mixer_w(w_in, w_out):
        return (row(ln1_g), w_in, pool_wbd, row(pool_scale), hgrn_lb_logits, row(hgrn_onorm_g), w_out)

    def ffn_w(w_up, w_down):
        return (row(ln2_g), w_up, conv_w, row(conv_b), w_down, row(lnf_g))

    mk, mv = _memkv_call(mem_prompt, row(mem_norm_g), w_mem_kv.astype(BF16))
    y_prompt, new_pool_p, new_s_p, new_conv_p, w_in_b, w_out_b, w_up_b, w_down_b = _prompt_call(
        x_prompt, mk, mv, mixer_w(w_in, w_out), ffn_w(w_up, w_down), tb=prompt_tb)

    nb = x_sample.shape[0]
    smkt = cache_mem_k.transpose(0, 2, 3, 1).reshape(nb, XATTN_WIDTH, N_MEM)
    smvt = cache_mem_v.transpose(0, 2, 3, 1).reshape(nb, XATTN_WIDTH, N_MEM)
    xs, new_pool_s, new_s_s = _sample_mixer_call(x_sample, state_pool, state_hgrn, smkt, smvt,
                                                 mixer_w(w_in_b, w_out_b), gs=mixer_gs)
    y_sample, new_conv_s = _sample_ffn_call(xs, state_conv, ffn_w(w_up_b, w_down_b), gs=ffn_gs)
    bp = x_prompt.shape[0]
    heads = (bp, N_MEM, XATTN_HEADS, XATTN_DH)
    return (y_prompt, y_sample, new_pool_p, new_s_p, new_conv_p, mk.reshape(heads), mv.reshape(heads),
            new_pool_s, new_s_s, new_conv_s)


def kernel(x_prompt, x_sample, mem_prompt, state_pool, state_hgrn, state_conv, cache_mem_k, cache_mem_v,
           ln1_g, w_in, pool_w, pool_scale, hgrn_lb_logits, hgrn_onorm_g, mem_norm_g, w_mem_kv, w_out,
           ln2_g, w_up, conv_w, conv_b, w_down, lnf_g):
    assert w_in.shape[0] == 1, "one layer"
    outs = _layer(x_prompt, x_sample, mem_prompt, state_pool[0], state_hgrn[0], state_conv[0],
                  cache_mem_k[0], cache_mem_v[0], ln1_g[0], w_in[0], pool_w[0], pool_scale[0], hgrn_lb_logits,
                  hgrn_onorm_g[0], mem_norm_g[0], w_mem_kv[0], w_out[0], ln2_g[0], w_up[0], conv_w[0], conv_b[0],
                  w_down[0], lnf_g, prompt_tb=256, mixer_gs=16, ffn_gs=32)
    y_prompt, y_sample = outs[0], outs[1]
    return (y_prompt, y_sample) + tuple(o[None] for o in outs[2:])
```

```python
import functools

import jax
import jax.numpy as jnp
from jax import lax
from jax.experimental import pallas as pl
from jax.experimental.pallas import tpu as pltpu

F32 = jnp.float32
BF16 = jnp.bfloat16

D_MODEL = 1024
POOL_WIDTH = 256
POOL_GROUP = 64
POOL_HIST = 15
HGRN_WIDTH = 512
HGRN_HEADS = 4
HGRN_DK = 128
HGRN_DV = 128
XATTN_WIDTH = 256
XATTN_HEADS = 4
XATTN_DH = 64
N_MEM = 256
D_FF = 2816
CONV_W = 3
EPS = 1e-6
PAST_LEN = 16384
D_IN = POOL_WIDTH + 4 * HGRN_WIDTH + XATTN_WIDTH
OFF_U, OFF_Q, OFF_F, OFF_I, OFF_G, OFF_X = 0, 256, 768, 1280, 1792, 2304

CHUNK = 64
POOL_PAD = 16
CONV_PAD = 8
LANES = 128
MEMKV_GROUP = 4
STAGE_BYTES = 720896
STAGE_SLOTS = 6
VMEM_LIMIT_BYTES = 56 * 1024 * 1024

_NT = (((1,), (1,)), ((), ()))
_TN = (((0,), (0,)), ((), ()))


def _dot(a, b):
    return jnp.dot(a, b, preferred_element_type=F32)


def _dot_nt(a, b):
    return lax.dot_general(a, b, _NT, preferred_element_type=F32)


def _dot_tn(a, b):
    return lax.dot_general(a, b, _TN, preferred_element_type=F32)


def _rmsnorm(x, g):
    return x * lax.rsqrt(jnp.mean(x * x, axis=-1, keepdims=True) + EPS) * g


def _const_spec(shape):
    nd = len(shape)
    return pl.BlockSpec(shape, lambda *_: (0,) * nd, pipeline_mode=pl.Buffered(1))


def _memkv_body(mem_ref, g_ref, w_ref, k_ref, v_ref):
    nb = mem_ref.shape[0]
    h = _rmsnorm(mem_ref[...].reshape(nb * N_MEM, D_MODEL), g_ref[...]).astype(BF16)
    kv = _dot(h, w_ref[...])
    k_ref[...] = kv[:, :XATTN_WIDTH].reshape(nb, N_MEM, XATTN_WIDTH)
    v_ref[...] = kv[:, XATTN_WIDTH:].reshape(nb, N_MEM, XATTN_WIDTH)


def _memkv_call(mem, g, w):
    b = mem.shape[0]
    nb = MEMKV_GROUP
    out = jax.ShapeDtypeStruct((b, N_MEM, XATTN_WIDTH), F32)
    return pl.pallas_call(
        _memkv_body,
        grid=(b // nb,),
        in_specs=[pl.BlockSpec((nb, N_MEM, D_MODEL), lambda i: (i, 0, 0)),
                  _const_spec((1, D_MODEL)),
                  _const_spec((D_MODEL, 2 * XATTN_WIDTH))],
        out_specs=[pl.BlockSpec((nb, N_MEM, XATTN_WIDTH), lambda i: (i, 0, 0))] * 2,
        out_shape=[out, out],
        compiler_params=pltpu.CompilerParams(dimension_semantics=("arbitrary",)),
        name="memkv",
    )(mem, g, w)


def _forget_lower_bound(logits):
    z = logits - jnp.max(logits, axis=0, keepdims=True)
    e = jnp.exp(z)
    return e[0:1, :] / jnp.sum(e, axis=0, keepdims=True)


def _hgrn_gates(proj, lb):
    fp = proj[:, OFF_F:OFF_F + HGRN_WIDTH]
    q = proj[:, OFF_Q:OFF_Q + HGRN_WIDTH]
    log_f = jnp.log2(lb + (1.0 - lb) * jax.nn.sigmoid(fp))
    k = (1.0 - lb) * jax.nn.sigmoid(-fp)
    qf = q * jax.nn.sigmoid(q)
    return qf, k, log_f


def _segment_cumsum(x, seq):
    ridx = lax.broadcasted_iota(jnp.int32, x.shape, 0) & (seq - 1)
    sh = 1
    while sh < seq:
        x = x + jnp.where(ridx >= sh, pltpu.roll(x, sh, 0), 0.0)
        sh *= 2
    return x


def _level_factor(a, qf, k, log_f, m, rows):
    n = a.shape[1]
    ridx = lax.broadcasted_iota(jnp.int32, (rows, n), 0)
    upper = (ridx & m) != 0
    if m == 1:
        d = jnp.where(upper, log_f, 0.0)
    else:
        if (2 * m) % 8 == 0:
            nb = rows // (2 * m)
            a3 = a.reshape(nb, 2 * m, n)
            ref = jnp.broadcast_to(a3[:, m - 1:m, :], (nb, 2 * m, n)).reshape(rows, n)
        else:
            a3 = a.reshape(rows // 8, 8, n)
            sub = lax.broadcasted_iota(jnp.int32, a3.shape, 1)
            ref = jnp.where(sub < 4,
                            jnp.broadcast_to(a3[:, 1:2, :], a3.shape),
                            jnp.broadcast_to(a3[:, 5:6, :], a3.shape)).reshape(rows, n)
        d = -jnp.abs(a - ref)
    return (jnp.where(upper, qf, k) * jnp.exp2(d)).astype(BF16)


def _level_factors(a, qf, k, log_f, rows, seq):
    out, m = [], seq // 2
    while m >= 1:
        out.append((m, _level_factor(a, qf, k, log_f, m, rows)))
        m //= 2
    return out


def _intra_scores(factors, head, rows, seq):
    sl = slice(head * HGRN_DK, (head + 1) * HGRN_DK)
    t = lax.broadcasted_iota(jnp.int32, (rows, rows), 0)
    s = lax.broadcasted_iota(jnp.int32, (rows, rows), 1)
    x = t ^ s
    products = [(m, _dot_nt(y[:, sl], y[:, sl])) for m, y in reversed(factors)]
    total = jnp.zeros((rows, rows), F32)
    for m, p in products:
        total = jnp.where(x >= m, p, total)
    return jnp.where((t > s) & (x < seq), total, 0.0)


def _head_norm_gate(o, gate):
    return o * lax.rsqrt(jnp.mean(o * o, axis=-1, keepdims=True) + EPS) * gate


def _softmax_rows(s):
    e = jnp.exp(s - jnp.max(s, axis=-1, keepdims=True))
    return e / jnp.sum(e, axis=-1, keepdims=True)


def _cross_attention(qx, mk, mv):
    rows = qx.shape[0]
    head_of_lane = lax.broadcasted_iota(jnp.int32, qx.shape, 1) // XATTN_DH
    qs = jnp.concatenate([jnp.where(head_of_lane == h, qx, 0.0) for h in range(XATTN_HEADS)], axis=0)
    p = _softmax_rows(_dot_nt(qs.astype(BF16), mk))
    o = _dot(p.astype(BF16), mv)
    out = jnp.zeros(qx.shape, F32)
    for h in range(XATTN_HEADS):
        out = jnp.where(head_of_lane == h, o[h * rows:(h + 1) * rows, :], out)
    return out


def _pool_means(ld, posf, shape):
    lane = lax.broadcasted_iota(jnp.int32, shape, len(shape) - 1)
    first = lane < POOL_GROUP
    u_lo, u_hi = ld(0, 0), ld(0, 1)
    t2 = u_lo + ld(1, 0)
    t4 = t2 + ld(2, 0) + ld(3, 0)
    t8 = u_hi
    for j in range(1, 8):
        t8 = t8 + ld(j, 1)
    t16 = t8
    for j in range(8, 16):
        t16 = t16 + ld(j, 1)
    cnt_lo = jnp.where(first, jnp.minimum(2.0, posf), jnp.minimum(4.0, posf))
    cnt_hi = jnp.where(first, jnp.minimum(8.0, posf), jnp.minimum(16.0, posf))
    lo = jnp.where(first, t2, t4) / cnt_lo - u_lo
    hi = jnp.where(first, t8, t16) / cnt_hi - u_hi
    return jnp.concatenate([lo, hi], axis=-1)


def _after(x, anchor, zero):
    r, n = anchor.shape
    s = jnp.sum(anchor.reshape(r // 8, 8, n), axis=0)
    c = s[:, :LANES]
    for i in range(1, n // LANES):
        c = c + s[:, i * LANES:(i + 1) * LANES]
    z = lax.bitcast_convert_type(lax.bitcast_convert_type(c[0:1, :], jnp.int32) & zero, F32)
    return jnp.concatenate([x[:, :LANES] + z, x[:, LANES:]], axis=1)


def _gelu_tanh(x):
    c = 0.7978845608028654
    half_x = 0.5 * x
    return half_x + half_x * jnp.tanh(x * (c + (0.044715 * c) * (x * x)))


def _ffn_tail(x, act, wdown_ref, lnf_ref):
    y = x + _dot(act.astype(BF16), wdown_ref[...])
    return _rmsnorm(y, lnf_ref[...])


def _stage_widths(weights):
    return list(dict.fromkeys(w.shape[1] for w in weights))


def _stage_rows(cols):
    return STAGE_BYTES // (4 * cols) // 16 * 16


def _load_weights(srcs, dsts, stages, sem):
    widths = _stage_widths(srcs)
    blocks, used = [], [0] * len(stages)
    for i, w in enumerate(srcs):
        f = widths.index(w.shape[1])
        rows = stages[f].shape[1]
        for r0 in range(0, w.shape[0], rows):
            blocks.append((i, r0, min(rows, w.shape[0] - r0), f, used[f] % STAGE_SLOTS))
            used[f] += 1

    def copy(n):
        i, r0, nr, f, slot = blocks[n]
        return pltpu.make_async_copy(srcs[i].at[pl.ds(r0, nr), :], stages[f].at[slot, pl.ds(0, nr), :],
                                     sem.at[f, slot])

    ahead = STAGE_SLOTS - 1
    for n in range(min(ahead, len(blocks))):
        copy(n).start()
    for n, (i, r0, nr, f, slot) in enumerate(blocks):
        if n + ahead < len(blocks):
            copy(n + ahead).start()
        copy(n).wait()
        dsts[i][pl.ds(r0, nr), :] = stages[f][slot, pl.ds(0, nr), :].astype(BF16)


def _prompt_body(x_ref, mk_ref, mv_ref, ln1_ref, win_hbm, poolw_ref, pscale_ref, lbl_ref, onorm_ref,
                 wout_hbm, ln2_ref, wup_hbm, cw_ref, cb_ref, wdown_hbm, lnf_ref, zero_ref,
                 y_ref, npool_ref, ns_ref, nconv_ref, win_out, wout_out, wup_out, wdown_out,
                 pbuf, st, abuf, x2s, win_ref, wout_ref, wup_ref, wdown_ref, out_sem, stage_sem, *stages,
                 tb, nt, nblk):
    g = pl.program_id(0)
    jm = jnp.minimum(g, nblk - 1) % nt
    jf = jnp.maximum(g - 1, 0) % nt

    weights_bf16 = (win_ref, wout_ref, wup_ref, wdown_ref)

    def weight_writeback(n):
        return pltpu.make_async_copy(weights_bf16[n], (win_out, wout_out, wup_out, wdown_out)[n], out_sem.at[n])

    @pl.when(g == 0)
    def _():
        _load_weights((win_hbm, wout_hbm, wup_hbm, wdown_hbm), weights_bf16, stages, stage_sem)
        for n in range(len(weights_bf16)):
            weight_writeback(n).start()

    @pl.when(g == nblk)
    def _():
        for n in range(len(weights_bf16)):
            weight_writeback(n).wait()

    @pl.when(jm == 0)
    def _():
        pbuf[pl.ds(0, POOL_PAD), :] = jnp.zeros((POOL_PAD, POOL_WIDTH), F32)
        st[...] = jnp.zeros(st.shape, F32)

    @pl.when(jf == 0)
    def _():
        abuf[...] = jnp.zeros(abuf.shape, F32)

    def input_projection():
        h = _rmsnorm(x_ref[0], ln1_ref[...]).astype(BF16)
        return _dot(h, win_ref[...])

    def mixer_half(proj):
        _prompt_mixer_half(proj, x_ref, mk_ref, mv_ref, poolw_ref, pscale_ref, lbl_ref, onorm_ref, wout_ref,
                           pbuf, st, x2s, jm, tb)

    def ffn_half(proj):
        x2 = x2s[...]
        h2 = _rmsnorm(x2, ln2_ref[...]).astype(BF16)
        ab = _dot(h2, wup_ref[...])
        a = ab[:, :D_FF]
        hist = abuf[...]
        sub = lax.broadcasted_iota(jnp.int32, (CONV_PAD, D_FF), 0)
        r1 = pltpu.roll(a, 1, 0)
        r2 = pltpu.roll(a, 2, 0)
        h1 = jnp.broadcast_to(hist[CONV_PAD - 1:CONV_PAD, :], (CONV_PAD, D_FF))
        h0 = jnp.broadcast_to(hist[CONV_PAD - 2:CONV_PAD - 1, :], (CONV_PAD, D_FF))
        a_m1 = jnp.concatenate([jnp.where(sub >= 1, r1[:CONV_PAD], h1), r1[CONV_PAD:]], axis=0)
        a_m2 = jnp.concatenate([jnp.where(sub >= 2, r2[:CONV_PAD], jnp.where(sub == 1, h1, h0)), r2[CONV_PAD:]],
                               axis=0)
        conv = cb_ref[...] + cw_ref[0:1, :] * a_m2 + cw_ref[1:2, :] * a_m1 + cw_ref[2:3, :] * a
        act = _gelu_tanh(conv) * ab[:, D_FF:]
        abuf[...] = a[tb - CONV_PAD:, :]
        if proj is not None:
            act = _after(act, proj, zero_ref[...])
        y_ref[0] = _rmsnorm(x2 + _dot(act.astype(BF16), wdown_ref[...]), lnf_ref[...])

    @pl.when(g == 0)
    def _():
        mixer_half(input_projection())

    @pl.when(jnp.logical_and(g > 0, g < nblk))
    def _():
        proj = input_projection()
        ffn_half(proj)
        mixer_half(proj)

    @pl.when(g == nblk)
    def _():
        ffn_half(None)

    @pl.when(jnp.logical_and(jm == nt - 1, g < nblk))
    def _():
        npool_ref[0] = pbuf[pl.ds(1, POOL_HIST), :]
        for hd in range(HGRN_HEADS):
            ns_ref[0, hd] = st[hd].T

    @pl.when(jnp.logical_and(jf == nt - 1, g >= 1))
    def _():
        nconv_ref[0] = abuf[pl.ds(CONV_PAD - (CONV_W - 1), CONV_W - 1), :]


def _prompt_mixer_half(proj, x_ref, mk_ref, mv_ref, poolw_ref, pscale_ref, lbl_ref, onorm_ref, wout_ref,
                       pbuf, st, x2s, jm, tb):
    x = x_ref[0]

    pbuf[pl.ds(POOL_PAD, tb), :] = proj[:, OFF_U:OFF_U + POOL_WIDTH]
    posf = (jm * tb + 1 + lax.broadcasted_iota(jnp.int32, (tb, 1), 0)).astype(F32)
    dm = _pool_means(lambda j, half: pbuf[pl.ds(POOL_PAD - j, tb), pl.ds(LANES * half, LANES)], posf, (tb, LANES))
    o_pool = _dot(dm.astype(BF16), poolw_ref[...]) * pscale_ref[...]
    pbuf[pl.ds(0, POOL_PAD), :] = pbuf[pl.ds(tb, POOL_PAD), :]

    qx = proj[:, OFF_X:OFF_X + XATTN_WIDTH] * (XATTN_DH ** -0.5)
    o_x = _cross_attention(qx, mk_ref[0].astype(BF16), mv_ref[0].astype(BF16))

    lb = _forget_lower_bound(lbl_ref[...])
    qf, k, log_f = _hgrn_gates(proj, lb)
    v = proj[:, OFF_I:OFF_I + HGRN_WIDTH]
    gg = proj[:, OFF_G:OFF_G + HGRN_WIDTH]
    gate = gg * jax.nn.sigmoid(gg) * onorm_ref[...]
    a_all = _segment_cumsum(log_f, CHUNK)
    states = [st[hd] for hd in range(HGRN_HEADS)]
    o_rows = []
    for c in range(tb // CHUNK):
        rs = slice(c * CHUNK, (c + 1) * CHUNK)
        qf_c, k_c, lf_c, v_c, a = qf[rs], k[rs], log_f[rs], v[rs], a_all[rs]
        a_end = a[CHUNK - 1:CHUNK, :]
        q_in = (qf_c * jnp.exp2(a)).astype(BF16)
        k_out = (k_c * jnp.exp2(a_end - a)).astype(BF16)
        decay = jnp.exp2(a_end)
        v_b = v_c.astype(BF16)
        qk = qf_c * k_c
        factors = _level_factors(a, qf_c, k_c, lf_c, CHUNK, CHUNK)
        heads = [slice(hd * HGRN_DK, (hd + 1) * HGRN_DK) for hd in range(HGRN_HEADS)]
        inter = [_dot_nt(q_in[:, sl], states[hd].astype(BF16)) for hd, sl in enumerate(heads)]
        update = [_dot_tn(v_b[:, sl], k_out[:, sl]) for sl in heads]
        scores = [_intra_scores(factors, hd, CHUNK, CHUNK).astype(BF16) for hd in range(HGRN_HEADS)]
        intra = [_dot(scores[hd], v_b[:, sl]) for hd, sl in enumerate(heads)]
        o_heads = []
        for hd, sl in enumerate(heads):
            o = intra[hd] + inter[hd] + jnp.sum(qk[:, sl], axis=-1, keepdims=True) * v_c[:, sl]
            states[hd] = states[hd] * decay[:, sl] + update[hd]
            o_heads.append(_head_norm_gate(o, gate[rs, sl]))
        o_rows.append(jnp.concatenate(o_heads, axis=-1))
    for hd in range(HGRN_HEADS):
        st[hd] = states[hd]
    o_hgrn = jnp.concatenate(o_rows, axis=0)

    mixed = jnp.concatenate([o_pool, o_hgrn, o_x], axis=-1).astype(BF16)
    x2s[...] = x + _dot(mixed, wout_ref[...])


def _prompt_call(x, mk, mv, mixer_w, ffn_w, tb):
    b, l, d = x.shape
    nt = l // tb
    nblk = b * nt
    ln1, w_in, pool_wbd, pool_scale, lb_logits, onorm, w_out = mixer_w
    ln2, w_up, conv_w, conv_b, w_down, lnf = ffn_w

    def mixer_blk(g):
        return jnp.minimum(g, nblk - 1)

    def ffn_blk(g):
        return jnp.maximum(g - 1, 0)

    x_spec = pl.BlockSpec((1, tb, d), lambda g: (mixer_blk(g) // nt, mixer_blk(g) % nt, 0))
    mem = pl.BlockSpec((1, N_MEM, XATTN_WIDTH), lambda g: (mixer_blk(g) // nt, 0, 0))
    y_spec = pl.BlockSpec((1, tb, d), lambda g: (ffn_blk(g) // nt, ffn_blk(g) % nt, 0))
    in_hbm = pl.BlockSpec(memory_space=pl.ANY)
    weights = (w_in, w_out, w_up, w_down)
    assert all(w.dtype == F32 and w.shape[0] % 16 == 0 and w.shape[1] % LANES == 0 for w in weights)
    widths = _stage_widths(weights)
    return pl.pallas_call(
        functools.partial(_prompt_body, tb=tb, nt=nt, nblk=nblk),
        grid=(nblk + 1,),
        in_specs=[x_spec, mem, mem,
                  _const_spec((1, d)), in_hbm, _const_spec((POOL_WIDTH, POOL_WIDTH)),
                  _const_spec((1, POOL_WIDTH)), _const_spec(lb_logits.shape), _const_spec((1, HGRN_WIDTH)),
                  in_hbm,
                  _const_spec((1, d)), in_hbm, _const_spec((CONV_W, D_FF)),
                  _const_spec((1, D_FF)), in_hbm, _const_spec((1, d)), _const_spec((1, LANES))],
        out_specs=[y_spec,
                   pl.BlockSpec((1, POOL_HIST, POOL_WIDTH), lambda g: (mixer_blk(g) // nt, 0, 0)),
                   pl.BlockSpec((1, HGRN_HEADS, HGRN_DK, HGRN_DV), lambda g: (mixer_blk(g) // nt, 0, 0, 0)),
                   pl.BlockSpec((1, CONV_W - 1, D_FF), lambda g: (ffn_blk(g) // nt, 0, 0))]
                  + [pl.BlockSpec(memory_space=pl.ANY) for _ in weights],
        out_shape=[jax.ShapeDtypeStruct((b, l, d), F32),
                   jax.ShapeDtypeStruct((b, POOL_HIST, POOL_WIDTH), F32),
                   jax.ShapeDtypeStruct((b, HGRN_HEADS, HGRN_DK, HGRN_DV), F32),
                   jax.ShapeDtypeStruct((b, CONV_W - 1, D_FF), F32)]
                  + [jax.ShapeDtypeStruct(w.shape, BF16) for w in weights],
        scratch_shapes=[pltpu.VMEM((POOL_PAD + tb, POOL_WIDTH), F32),
                        pltpu.VMEM((HGRN_HEADS, HGRN_DV, HGRN_DK), F32),
                        pltpu.VMEM((CONV_PAD, D_FF), F32),
                        pltpu.VMEM((tb, d), F32)]
                       + [pltpu.VMEM(w.shape, BF16) for w in weights]
                       + [pltpu.SemaphoreType.DMA((len(weights),)),
                          pltpu.SemaphoreType.DMA((len(widths), STAGE_SLOTS))]
                       + [pltpu.VMEM((STAGE_SLOTS, _stage_rows(c), c), F32) for c in widths],
        compiler_params=pltpu.CompilerParams(dimension_semantics=("arbitrary",),
                                             vmem_limit_bytes=VMEM_LIMIT_BYTES),
        name="prompt_layer",
    )(x, mk, mv, ln1, w_in, pool_wbd, pool_scale, lb_logits, onorm, w_out,
      ln2, w_up, conv_w, conv_b, w_down, lnf, jnp.zeros((1, LANES), jnp.int32))


def _sample_mixer_body(x_ref, hist_ref, s0_ref, mkt_ref, mvt_ref, ln1_ref, win_ref, poolw_ref, pscale_ref, lbl_ref,
                       onorm_ref, wout_ref, x2_ref, npool_ref, ns_ref, pbuf, *, gs, sl_len):
    rows = gs * sl_len
    seqs = [slice(s * sl_len, (s + 1) * sl_len) for s in range(gs)]
    x = x_ref[...].reshape(rows, D_MODEL)
    h = _rmsnorm(x, ln1_ref[...]).astype(BF16)
    proj = _dot(h, win_ref[...])

    pbuf[:, pl.ds(1, POOL_HIST), :] = hist_ref[...]
    pbuf[:, pl.ds(POOL_PAD, sl_len), :] = proj[:, OFF_U:OFF_U + POOL_WIDTH].reshape(gs, sl_len, POOL_WIDTH)
    posf = (PAST_LEN + 1 + lax.broadcasted_iota(jnp.int32, (1, sl_len, 1), 1)).astype(F32)
    dm = _pool_means(lambda j, half: pbuf[:, pl.ds(POOL_PAD - j, sl_len), pl.ds(LANES * half, LANES)],
                     posf, (gs, sl_len, LANES))
    npool_ref[...] = pbuf[:, pl.ds(sl_len + 1, POOL_HIST), :]
    o_pool = _dot(dm.reshape(rows, POOL_WIDTH).astype(BF16), poolw_ref[...]) * pscale_ref[...]

    qx3 = (proj[:, OFF_X:OFF_X + XATTN_WIDTH] * (XATTN_DH ** -0.5)).reshape(gs, sl_len, XATTN_WIDTH)
    head_of_lane = lax.broadcasted_iota(jnp.int32, qx3.shape, 2) // XATTN_DH
    qs3 = jnp.concatenate([jnp.where(head_of_lane == hd, qx3, 0.0) for hd in range(XATTN_HEADS)],
                          axis=1).astype(BF16)
    hrows = XATTN_HEADS * sl_len
    scores = jnp.concatenate([_dot(qs3[s], mkt_ref[s].astype(BF16)) for s in range(gs)], axis=0)
    p = _softmax_rows(scores).astype(BF16)
    o4 = jnp.concatenate([_dot_nt(p[s * hrows:(s + 1) * hrows], mvt_ref[s].astype(BF16)) for s in range(gs)],
                         axis=0)
    o4 = o4.reshape(gs, XATTN_HEADS, sl_len, XATTN_WIDTH)
    o_x3 = jnp.zeros(qx3.shape, F32)
    for hd in range(XATTN_HEADS):
        o_x3 = jnp.where(head_of_lane == hd, o4[:, hd], o_x3)
    o_x = o_x3.reshape(rows, XATTN_WIDTH)

    lb = _forget_lower_bound(lbl_ref[...])
    qf, k, log_f = _hgrn_gates(proj, lb)
    v = proj[:, OFF_I:OFF_I + HGRN_WIDTH]
    gg = proj[:, OFF_G:OFF_G + HGRN_WIDTH]
    gate = gg * jax.nn.sigmoid(gg) * onorm_ref[...]
    a = _segment_cumsum(log_f, sl_len)
    a3 = a.reshape(gs, sl_len, HGRN_WIDTH)
    a_end = jnp.broadcast_to(a3[:, sl_len - 1:sl_len, :], a3.shape).reshape(rows, HGRN_WIDTH)
    q_in = (qf * jnp.exp2(a)).astype(BF16)
    k_out = (k * jnp.exp2(a_end - a)).astype(BF16)
    decay = jnp.exp2(a_end)
    v_b = v.astype(BF16)
    qk = qf * k
    factors = _level_factors(a, qf, k, log_f, rows, sl_len)
    heads = [slice(hd * HGRN_DK, (hd + 1) * HGRN_DK) for hd in range(HGRN_HEADS)]
    inter = [jnp.concatenate([_dot(q_in[r, sl], s0_ref[s, hd].astype(BF16)) for s, r in enumerate(seqs)], axis=0)
             for hd, sl in enumerate(heads)]
    o_heads = []
    for hd, sl in enumerate(heads):
        p_h = _intra_scores(factors, hd, rows, sl_len)
        o = _dot(p_h.astype(BF16), v_b[:, sl]) + inter[hd]
        o = o + jnp.sum(qk[:, sl], axis=-1, keepdims=True) * v[:, sl]
        o_heads.append(_head_norm_gate(o, gate[:, sl]))
    updates = [[_dot_tn(k_out[r, sl], v_b[r, sl]) for sl in heads] for r in seqs]
    for s, r in enumerate(seqs):
        for hd, sl in enumerate(heads):
            decay_cols = jnp.broadcast_to(decay[r, sl][sl_len - 1:sl_len, :], (HGRN_DV, HGRN_DK)).T
            ns_ref[s, hd] = decay_cols * s0_ref[s, hd] + updates[s][hd]

    mixed = jnp.concatenate([o_pool] + o_heads + [o_x], axis=-1).astype(BF16)
    x2_ref[...] = (x + _dot(mixed, wout_ref[...])).reshape(gs, sl_len, D_MODEL)


def _sample_mixer_call(x, hist, s0, mkt, mvt, mixer_w, gs):
    b, l, d = x.shape
    ln1, w_in, pool_wbd, pool_scale, lb_logits, onorm, w_out = mixer_w
    grid = (b // gs,)
    blk = pl.BlockSpec((gs, l, d), lambda i: (i, 0, 0))
    histb = pl.BlockSpec((gs, POOL_HIST, POOL_WIDTH), lambda i: (i, 0, 0))
    sb = pl.BlockSpec((gs, HGRN_HEADS, HGRN_DK, HGRN_DV), lambda i: (i, 0, 0, 0))
    mem = pl.BlockSpec((gs, XATTN_WIDTH, N_MEM), lambda i: (i, 0, 0))
    return pl.pallas_call(
        functools.partial(_sample_mixer_body, gs=gs, sl_len=l),
        grid=grid,
        in_specs=[blk, histb, sb, mem, mem,
                  _const_spec((1, d)), _const_spec((d, D_IN)), _const_spec((POOL_WIDTH, POOL_WIDTH)),
                  _const_spec((1, POOL_WIDTH)), _const_spec(lb_logits.shape), _const_spec((1, HGRN_WIDTH)),
                  _const_spec((d, d))],
        out_specs=[blk, histb, sb],
        out_shape=[jax.ShapeDtypeStruct((b, l, d), F32),
                   jax.ShapeDtypeStruct((b, POOL_HIST, POOL_WIDTH), F32),
                   jax.ShapeDtypeStruct((b, HGRN_HEADS, HGRN_DK, HGRN_DV), F32)],
        scratch_shapes=[pltpu.VMEM((gs, POOL_PAD + l, POOL_WIDTH), F32)],
        compiler_params=pltpu.CompilerParams(dimension_semantics=("arbitrary",),
                                             vmem_limit_bytes=VMEM_LIMIT_BYTES),
        name="sample_mixer",
    )(x, hist, s0, mkt, mvt, ln1, w_in, pool_wbd, pool_scale, lb_logits, onorm, w_out)


def _sample_ffn_body(x_ref, chist_ref, ln2_ref, wup_ref, cw_ref, cb_ref, wdown_ref, lnf_ref, y_ref, nconv_ref,
                     *, gs, sl_len):
    rows = gs * sl_len
    x = x_ref[...].reshape(rows, D_MODEL)
    h = _rmsnorm(x, ln2_ref[...]).astype(BF16)
    ab = _dot(h, wup_ref[...])
    a = ab[:, :D_FF]
    ridx = lax.broadcasted_iota(jnp.int32, (rows, D_FF), 0) % sl_len
    hist = chist_ref[...]
    h1 = jnp.broadcast_to(hist[:, 1:2, :], (gs, sl_len, D_FF)).reshape(rows, D_FF)
    h0 = jnp.broadcast_to(hist[:, 0:1, :], (gs, sl_len, D_FF)).reshape(rows, D_FF)
    a_m1 = jnp.where(ridx >= 1, pltpu.roll(a, 1, 0), h1)
    a_m2 = jnp.where(ridx >= 2, pltpu.roll(a, 2, 0), jnp.where(ridx == 1, h1, h0))
    conv = cb_ref[...] + cw_ref[0:1, :] * a_m2 + cw_ref[1:2, :] * a_m1 + cw_ref[2:3, :] * a
    act = _gelu_tanh(conv) * ab[:, D_FF:]
    nconv_ref[...] = a.reshape(gs, sl_len, D_FF)[:, sl_len - (CONV_W - 1):, :]
    y_ref[...] = _ffn_tail(x, act, wdown_ref, lnf_ref).reshape(gs, sl_len, D_MODEL)


def _sample_ffn_call(x, chist, ffn_w, gs):
    b, l, d = x.shape
    ln2, w_up, conv_w, conv_b, w_down, lnf = ffn_w
    blk = pl.BlockSpec((gs, l, d), lambda i: (i, 0, 0))
    cblk = pl.BlockSpec((gs, CONV_W - 1, D_FF), lambda i: (i, 0, 0))
    return pl.pallas_call(
        functools.partial(_sample_ffn_body, gs=gs, sl_len=l),
        grid=(b // gs,),
        in_specs=[blk, cblk, _const_spec((1, d)), _const_spec((d, 2 * D_FF)), _const_spec((CONV_W, D_FF)),
                  _const_spec((1, D_FF)), _const_spec((D_FF, d)), _const_spec((1, d))],
        out_specs=[blk, cblk],
        out_shape=[jax.ShapeDtypeStruct((b, l, d), F32),
                   jax.ShapeDtypeStruct((b, CONV_W - 1, D_FF), F32)],
        compiler_params=pltpu.CompilerParams(dimension_semantics=("arbitrary",),
                                             vmem_limit_bytes=VMEM_LIMIT_BYTES),
        name="sample_ffn",
    )(x, chist, ln2, w_up, conv_w, conv_b, w_down, lnf)


def _block_diag(pool_w):
    n = pool_w.shape[0]
    same_group = jnp.eye(n, dtype=bool)[:, None, :, None]
    return jnp.where(same_group, pool_w[:, :, None, :], 0.0).reshape(n * POOL_GROUP, n * POOL_GROUP)


def _layer(x_prompt, x_sample, mem_prompt, state_pool, state_hgrn, state_conv, cache_mem_k, cache_mem_v,
           ln1_g, w_in, pool_w, pool_scale, hgrn_lb_logits, hgrn_onorm_g, mem_norm_g, w_mem_kv, w_out,
           ln2_g, w_up, conv_w, conv_b, w_down, lnf_g, *, prompt_tb, mixer_gs, ffn_gs):
    row = lambda a: a.reshape(1, -1)
    pool_wbd = _block_diag(pool_w).astype(BF16)

    def mixer_w(w_in, w_out):
        return (row(ln1_g), w_in, pool_wbd, row(pool_scale), hgrn_lb_logits, row(hgrn_onorm_g), w_out)

    def ffn_w(w_up, w_down):
        return (row(ln2_g), w_up, conv_w, row(conv_b), w_down, row(lnf_g))

    mk, mv = _memkv_call(mem_prompt, row(mem_norm_g), w_mem_kv.astype(BF16))
    y_prompt, new_pool_p, new_s_p, new_conv_p, w_in_b, w_out_b, w_up_b, w_down_b = _prompt_call(
        x_prompt, mk, mv, mixer_w(w_in, w_out), ffn_w(w_up, w_down), tb=prompt_tb)

    nb = x_sample.shape[0]
    smkt = cache_mem_k.transpose(0, 2, 3, 1).reshape(nb, XATTN_WIDTH, N_MEM)
    smvt = cache_mem_v.transpose(0, 2, 3, 1).reshape(nb, XATTN_WIDTH, N_MEM)
    xs, new_pool_s, new_s_s = _sample_mixer_call(x_sample, state_pool, state_hgrn, smkt, smvt,
                                                 mixer_w(w_in_b, w_out_b), gs=mixer_gs)
    y_sample, new_conv_s = _sample_ffn_call(xs, state_conv, ffn_w(w_up_b, w_down_b), gs=ffn_gs)
    bp = x_prompt.shape[0]
    heads = (bp, N_MEM, XATTN_HEADS, XATTN_DH)
    return (y_prompt, y_sample, new_pool_p, new_s_p, new_conv_p, mk.reshape(heads), mv.reshape(heads),
            new_pool_s, new_s_s, new_conv_s)


def kernel(x_prompt, x_sample, mem_prompt, state_pool, state_hgrn, state_conv, cache_mem_k, cache_mem_v,
           ln1_g, w_in, pool_w, pool_scale, hgrn_lb_logits, hgrn_onorm_g, mem_norm_g, w_mem_kv, w_out,
           ln2_g, w_up, conv_w, conv_b, w_down, lnf_g):
    assert w_in.shape[0] == 1, "one layer"
    outs = _layer(x_prompt, x_sample, mem_prompt, state_pool[0], state_hgrn[0], state_conv[0],
                  cache_mem_k[0], cache_mem_v[0], ln1_g[0], w_in[0], pool_w[0], pool_scale[0], hgrn_lb_logits,
                  hgrn_onorm_g[0], mem_norm_g[0], w_mem_kv[0], w_out[0], ln2_g[0], w_up[0], conv_w[0], conv_b[0],
                  w_down[0], lnf_g, prompt_tb=256, mixer_gs=16, ffn_gs=32)
    y_prompt, y_sample = outs[0], outs[1]
    return (y_prompt, y_sample) + tuple(o[None] for o in outs[2:])
```

```python
import functools

import jax
import jax.numpy as jnp
from jax import lax
from jax.experimental import pallas as pl
from jax.experimental.pallas import tpu as pltpu

F32 = jnp.float32
BF16 = jnp.bfloat16

D_MODEL = 1024
POOL_WIDTH = 256
POOL_GROUP = 64
POOL_HIST = 15
HGRN_WIDTH = 512
HGRN_HEADS = 4
HGRN_DK = 128
HGRN_DV = 128
XATTN_WIDTH = 256
XATTN_HEADS = 4
XATTN_DH = 64
N_MEM = 256
D_FF = 2816
CONV_W = 3
EPS = 1e-6
PAST_LEN = 16384
D_IN = POOL_WIDTH + 4 * HGRN_WIDTH + XATTN_WIDTH
OFF_U, OFF_Q, OFF_F, OFF_I, OFF_G, OFF_X = 0, 256, 768, 1280, 1792, 2304

CHUNK = 64
POOL_PAD = 16
CONV_PAD = 8
LANES = 128
MEMKV_GROUP = 4
STAGE_BYTES = 720896
STAGE_SLOTS = 6
VMEM_LIMIT_BYTES = 56 * 1024 * 1024

_NT = (((1,), (1,)), ((), ()))
_TN = (((0,), (0,)), ((), ()))


def _dot(a, b):
    return jnp.dot(a, b, preferred_element_type=F32)


def _dot_nt(a, b):
    return lax.dot_general(a, b, _NT, preferred_element_type=F32)


def _dot_tn(a, b):
    return lax.dot_general(a, b, _TN, preferred_element_type=F32)


def _rmsnorm(x, g):
    return x * lax.rsqrt(jnp.mean(x * x, axis=-1, keepdims=True) + EPS) * g


def _const_spec(shape):
    nd = len(shape)
    return pl.BlockSpec(shape, lambda *_: (0,) * nd, pipeline_mode=pl.Buffered(1))


def _memkv_body(mem_ref, g_ref, w_ref, kt_ref, vt_ref):
    nb = mem_ref.shape[0]
    h = _rmsnorm(mem_ref[...].reshape(nb * N_MEM, D_MODEL), g_ref[...]).astype(BF16)
    kv = _dot(h, w_ref[...].astype(BF16))
    for s in range(nb):
        kvt = kv[s * N_MEM:(s + 1) * N_MEM, :].T
        kt_ref[s] = kvt[:XATTN_WIDTH, :]
        vt_ref[s] = kvt[XATTN_WIDTH:, :]


def _memkv_call(mem, g, w):
    b = mem.shape[0]
    nb = MEMKV_GROUP
    out = jax.ShapeDtypeStruct((b, XATTN_WIDTH, N_MEM), F32)
    return pl.pallas_call(
        _memkv_body,
        grid=(b // nb,),
        in_specs=[pl.BlockSpec((nb, N_MEM, D_MODEL), lambda i: (i, 0, 0)),
                  _const_spec((1, D_MODEL)),
                  _const_spec((D_MODEL, 2 * XATTN_WIDTH))],
        out_specs=[pl.BlockSpec((nb, XATTN_WIDTH, N_MEM), lambda i: (i, 0, 0))] * 2,
        out_shape=[out, out],
        compiler_params=pltpu.CompilerParams(dimension_semantics=("arbitrary",)),
        name="memkv",
    )(mem, g, w)


def _forget_lower_bound(logits):
    z = logits - jnp.max(logits, axis=0, keepdims=True)
    e = jnp.exp(z)
    return e[0:1, :] / jnp.sum(e, axis=0, keepdims=True)


def _hgrn_gates(proj, lb):
    fp = proj[:, OFF_F:OFF_F + HGRN_WIDTH]
    q = proj[:, OFF_Q:OFF_Q + HGRN_WIDTH]
    log_f = jnp.log2(lb + (1.0 - lb) * jax.nn.sigmoid(fp))
    k = (1.0 - lb) * jax.nn.sigmoid(-fp)
    qf = q * jax.nn.sigmoid(q)
    return qf, k, log_f


def _segment_cumsum(x, seq):
    ridx = lax.broadcasted_iota(jnp.int32, x.shape, 0) & (seq - 1)
    sh = 1
    while sh < seq:
        x = x + jnp.where(ridx >= sh, pltpu.roll(x, sh, 0), 0.0)
        sh *= 2
    return x


def _level_factor(a, qf, k, log_f, m, rows):
    n = a.shape[1]
    ridx = lax.broadcasted_iota(jnp.int32, (rows, n), 0)
    upper = (ridx & m) != 0
    if m == 1:
        d = jnp.where(upper, log_f, 0.0)
    else:
        if (2 * m) % 8 == 0:
            nb = rows // (2 * m)
            a3 = a.reshape(nb, 2 * m, n)
            ref = jnp.broadcast_to(a3[:, m - 1:m, :], (nb, 2 * m, n)).reshape(rows, n)
        else:
            a3 = a.reshape(rows // 8, 8, n)
            sub = lax.broadcasted_iota(jnp.int32, a3.shape, 1)
            ref = jnp.where(sub < 4,
                            jnp.broadcast_to(a3[:, 1:2, :], a3.shape),
                            jnp.broadcast_to(a3[:, 5:6, :], a3.shape)).reshape(rows, n)
        d = -jnp.abs(a - ref)
    return (jnp.where(upper, qf, k) * jnp.exp2(d)).astype(BF16)


def _level_factors(a, qf, k, log_f, rows, seq):
    out, m = [], seq // 2
    while m >= 1:
        out.append((m, _level_factor(a, qf, k, log_f, m, rows)))
        m //= 2
    return out


def _intra_scores(factors, head, rows, seq):
    sl = slice(head * HGRN_DK, (head + 1) * HGRN_DK)
    t = lax.broadcasted_iota(jnp.int32, (rows, rows), 0)
    s = lax.broadcasted_iota(jnp.int32, (rows, rows), 1)
    x = t ^ s
    products = [(m, _dot_nt(y[:, sl], y[:, sl])) for m, y in reversed(factors)]
    total = jnp.zeros((rows, rows), F32)
    for m, p in products:
        total = jnp.where(x >= m, p, total)
    return jnp.where((t > s) & (x < seq), total, 0.0)


def _head_norm_gate(o, gate):
    return o * lax.rsqrt(jnp.mean(o * o, axis=-1, keepdims=True) + EPS) * gate


def _softmax_rows(s):
    e = jnp.exp(s - jnp.max(s, axis=-1, keepdims=True))
    return e / jnp.sum(e, axis=-1, keepdims=True)


def _cross_attention(qx, mkt, mvt):
    rows = qx.shape[0]
    head_of_lane = lax.broadcasted_iota(jnp.int32, qx.shape, 1) // XATTN_DH
    qs = jnp.concatenate([jnp.where(head_of_lane == h, qx, 0.0) for h in range(XATTN_HEADS)], axis=0)
    p = _softmax_rows(_dot(qs.astype(BF16), mkt))
    o = _dot_nt(p.astype(BF16), mvt)
    out = jnp.zeros(qx.shape, F32)
    for h in range(XATTN_HEADS):
        out = jnp.where(head_of_lane == h, o[h * rows:(h + 1) * rows, :], out)
    return out


def _pool_means(ld, posf, shape):
    lane = lax.broadcasted_iota(jnp.int32, shape, len(shape) - 1)
    first = lane < POOL_GROUP
    u_lo, u_hi = ld(0, 0), ld(0, 1)
    t2 = u_lo + ld(1, 0)
    t4 = t2 + ld(2, 0) + ld(3, 0)
    t8 = u_hi
    for j in range(1, 8):
        t8 = t8 + ld(j, 1)
    t16 = t8
    for j in range(8, 16):
        t16 = t16 + ld(j, 1)
    cnt_lo = jnp.where(first, jnp.minimum(2.0, posf), jnp.minimum(4.0, posf))
    cnt_hi = jnp.where(first, jnp.minimum(8.0, posf), jnp.minimum(16.0, posf))
    lo = jnp.where(first, t2, t4) / cnt_lo - u_lo
    hi = jnp.where(first, t8, t16) / cnt_hi - u_hi
    return jnp.concatenate([lo, hi], axis=-1)


def _after(x, anchor, zero):
    r, n = anchor.shape
    s = jnp.sum(anchor.reshape(r // 8, 8, n), axis=0)
    c = s[:, :LANES]
    for i in range(1, n // LANES):
        c = c + s[:, i * LANES:(i + 1) * LANES]
    z = lax.bitcast_convert_type(lax.bitcast_convert_type(c[0:1, :], jnp.int32) & zero, F32)
    return jnp.concatenate([x[:, :LANES] + z, x[:, LANES:]], axis=1)


def _gelu_tanh(x):
    c = 0.7978845608028654
    half_x = 0.5 * x
    return half_x + half_x * jnp.tanh(x * (c + (0.044715 * c) * (x * x)))


def _ffn_tail(x, act, wdown_ref, lnf_ref):
    y = x + _dot(act.astype(BF16), wdown_ref[...])
    return _rmsnorm(y, lnf_ref[...])


def _stage_widths(weights):
    return list(dict.fromkeys(w.shape[1] for w in weights))


def _stage_rows(cols):
    return STAGE_BYTES // (4 * cols) // 16 * 16


def _load_weights(srcs, dsts, stages, sem):
    widths = _stage_widths(srcs)
    blocks, used = [], [0] * len(stages)
    for i, w in enumerate(srcs):
        f = widths.index(w.shape[1])
        rows = stages[f].shape[1]
        for r0 in range(0, w.shape[0], rows):
            blocks.append((i, r0, min(rows, w.shape[0] - r0), f, used[f] % STAGE_SLOTS))
            used[f] += 1

    def copy(n):
        i, r0, nr, f, slot = blocks[n]
        return pltpu.make_async_copy(srcs[i].at[pl.ds(r0, nr), :], stages[f].at[slot, pl.ds(0, nr), :],
                                     sem.at[f, slot])

    ahead = STAGE_SLOTS - 1
    for n in range(min(ahead, len(blocks))):
        copy(n).start()
    for n, (i, r0, nr, f, slot) in enumerate(blocks):
        if n + ahead < len(blocks):
            copy(n + ahead).start()
        copy(n).wait()
        dsts[i][pl.ds(r0, nr), :] = stages[f][slot, pl.ds(0, nr), :].astype(BF16)


def _prompt_body(x_ref, mkt_ref, mvt_ref, ln1_ref, win_hbm, poolw_ref, pscale_ref, lbl_ref, onorm_ref,
                 wout_hbm, ln2_ref, wup_hbm, cw_ref, cb_ref, wdown_hbm, lnf_ref, zero_ref,
                 y_ref, npool_ref, ns_ref, nconv_ref, win_out, wout_out, wup_out, wdown_out,
                 pbuf, st, abuf, x2s, win_ref, wout_ref, wup_ref, wdown_ref, out_sem, stage_sem, *stages,
                 tb, nt, nblk):
    g = pl.program_id(0)
    jm = jnp.minimum(g, nblk - 1) % nt
    jf = jnp.maximum(g - 1, 0) % nt

    weights_bf16 = (win_ref, wout_ref, wup_ref, wdown_ref)

    def weight_writeback(n):
        return pltpu.make_async_copy(weights_bf16[n], (win_out, wout_out, wup_out, wdown_out)[n], out_sem.at[n])

    @pl.when(g == 0)
    def _():
        _load_weights((win_hbm, wout_hbm, wup_hbm, wdown_hbm), weights_bf16, stages, stage_sem)
        for n in range(len(weights_bf16)):
            weight_writeback(n).start()

    @pl.when(g == nblk)
    def _():
        for n in range(len(weights_bf16)):
            weight_writeback(n).wait()

    @pl.when(jm == 0)
    def _():
        pbuf[pl.ds(0, POOL_PAD), :] = jnp.zeros((POOL_PAD, POOL_WIDTH), F32)
        st[...] = jnp.zeros(st.shape, F32)

    @pl.when(jf == 0)
    def _():
        abuf[...] = jnp.zeros(abuf.shape, F32)

    def input_projection():
        h = _rmsnorm(x_ref[0], ln1_ref[...]).astype(BF16)
        return _dot(h, win_ref[...])

    def mixer_half(proj):
        _prompt_mixer_half(proj, x_ref, mkt_ref, mvt_ref, poolw_ref, pscale_ref, lbl_ref, onorm_ref, wout_ref,
                           pbuf, st, x2s, jm, tb)

    def ffn_half(proj):
        x2 = x2s[...]
        h2 = _rmsnorm(x2, ln2_ref[...]).astype(BF16)
        ab = _dot(h2, wup_ref[...])
        a = ab[:, :D_FF]
        hist = abuf[...]
        sub = lax.broadcasted_iota(jnp.int32, (CONV_PAD, D_FF), 0)
        r1 = pltpu.roll(a, 1, 0)
        r2 = pltpu.roll(a, 2, 0)
        h1 = jnp.broadcast_to(hist[CONV_PAD - 1:CONV_PAD, :], (CONV_PAD, D_FF))
        h0 = jnp.broadcast_to(hist[CONV_PAD - 2:CONV_PAD - 1, :], (CONV_PAD, D_FF))
        a_m1 = jnp.concatenate([jnp.where(sub >= 1, r1[:CONV_PAD], h1), r1[CONV_PAD:]], axis=0)
        a_m2 = jnp.concatenate([jnp.where(sub >= 2, r2[:CONV_PAD], jnp.where(sub == 1, h1, h0)), r2[CONV_PAD:]],
                               axis=0)
        conv = cb_ref[...] + cw_ref[0] * a_m2 + cw_ref[1] * a_m1 + cw_ref[2] * a
        act = _gelu_tanh(conv) * ab[:, D_FF:]
        abuf[...] = a[tb - CONV_PAD:, :]
        if proj is not None:
            act = _after(act, proj, zero_ref[...])
        y_ref[0] = _rmsnorm(x2 + _dot(act.astype(BF16), wdown_ref[...]), lnf_ref[...])

    @pl.when(g == 0)
    def _():
        mixer_half(input_projection())

    @pl.when(jnp.logical_and(g > 0, g < nblk))
    def _():
        proj = input_projection()
        ffn_half(proj)
        mixer_half(proj)

    @pl.when(g == nblk)
    def _():
        ffn_half(None)

    @pl.when(jnp.logical_and(jm == nt - 1, g < nblk))
    def _():
        npool_ref[0] = pbuf[pl.ds(1, POOL_HIST), :]
        for hd in range(HGRN_HEADS):
            ns_ref[0, hd] = st[hd].T

    @pl.when(jnp.logical_and(jf == nt - 1, g >= 1))
    def _():
        nconv_ref[0] = abuf[pl.ds(CONV_PAD - (CONV_W - 1), CONV_W - 1), :]


def _prompt_mixer_half(proj, x_ref, mkt_ref, mvt_ref, poolw_ref, pscale_ref, lbl_ref, onorm_ref, wout_ref,
                       pbuf, st, x2s, jm, tb):
    x = x_ref[0]

    pbuf[pl.ds(POOL_PAD, tb), :] = proj[:, OFF_U:OFF_U + POOL_WIDTH]
    posf = (jm * tb + 1 + lax.broadcasted_iota(jnp.int32, (tb, 1), 0)).astype(F32)
    dm = _pool_means(lambda j, half: pbuf[pl.ds(POOL_PAD - j, tb), pl.ds(LANES * half, LANES)], posf, (tb, LANES))
    o_pool = _dot(dm.astype(BF16), poolw_ref[...]) * pscale_ref[...]
    pbuf[pl.ds(0, POOL_PAD), :] = pbuf[pl.ds(tb, POOL_PAD), :]

    qx = proj[:, OFF_X:OFF_X + XATTN_WIDTH] * (XATTN_DH ** -0.5)
    o_x = _cross_attention(qx, mkt_ref[0].astype(BF16), mvt_ref[0].astype(BF16))

    lb = _forget_lower_bound(lbl_ref[...])
    qf, k, log_f = _hgrn_gates(proj, lb)
    v = proj[:, OFF_I:OFF_I + HGRN_WIDTH]
    gg = proj[:, OFF_G:OFF_G + HGRN_WIDTH]
    gate = gg * jax.nn.sigmoid(gg) * onorm_ref[...]
    a_all = _segment_cumsum(log_f, CHUNK)
    states = [st[hd] for hd in range(HGRN_HEADS)]
    o_rows = []
    for c in range(tb // CHUNK):
        rs = slice(c * CHUNK, (c + 1) * CHUNK)
        qf_c, k_c, lf_c, v_c, a = qf[rs], k[rs], log_f[rs], v[rs], a_all[rs]
        a_end = a[CHUNK - 1:CHUNK, :]
        q_in = (qf_c * jnp.exp2(a)).astype(BF16)
        k_out = (k_c * jnp.exp2(a_end - a)).astype(BF16)
        decay = jnp.exp2(a_end)
        v_b = v_c.astype(BF16)
        qk = qf_c * k_c
        factors = _level_factors(a, qf_c, k_c, lf_c, CHUNK, CHUNK)
        heads = [slice(hd * HGRN_DK, (hd + 1) * HGRN_DK) for hd in range(HGRN_HEADS)]
        inter = [_dot_nt(q_in[:, sl], states[hd].astype(BF16)) for hd, sl in enumerate(heads)]
        update = [_dot_tn(v_b[:, sl], k_out[:, sl]) for sl in heads]
        scores = [_intra_scores(factors, hd, CHUNK, CHUNK).astype(BF16) for hd in range(HGRN_HEADS)]
        intra = [_dot(scores[hd], v_b[:, sl]) for hd, sl in enumerate(heads)]
        o_heads = []
        for hd, sl in enumerate(heads):
            o = intra[hd] + inter[hd] + jnp.sum(qk[:, sl], axis=-1, keepdims=True) * v_c[:, sl]
            states[hd] = states[hd] * decay[:, sl] + update[hd]
            o_heads.append(_head_norm_gate(o, gate[rs, sl]))
        o_rows.append(jnp.concatenate(o_heads, axis=-1))
    for hd in range(HGRN_HEADS):
        st[hd] = states[hd]
    o_hgrn = jnp.concatenate(o_rows, axis=0)

    mixed = jnp.concatenate([o_pool, o_hgrn, o_x], axis=-1).astype(BF16)
    x2s[...] = x + _dot(mixed, wout_ref[...])


def _prompt_call(x, mkt, mvt, mixer_w, ffn_w, tb):
    b, l, d = x.shape
    nt = l // tb
    nblk = b * nt
    ln1, w_in, pool_wbd, pool_scale, lb_logits, onorm, w_out = mixer_w
    ln2, w_up, conv_w, conv_b, w_down, lnf = ffn_w

    def mixer_blk(g):
        return jnp.minimum(g, nblk - 1)

    def ffn_blk(g):
        return jnp.maximum(g - 1, 0)

    x_spec = pl.BlockSpec((1, tb, d), lambda g: (mixer_blk(g) // nt, mixer_blk(g) % nt, 0))
    mem = pl.BlockSpec((1, XATTN_WIDTH, N_MEM), lambda g: (mixer_blk(g) // nt, 0, 0))
    y_spec = pl.BlockSpec((1, tb, d), lambda g: (ffn_blk(g) // nt, ffn_blk(g) % nt, 0))
    in_hbm = pl.BlockSpec(memory_space=pl.ANY)
    weights = (w_in, w_out, w_up, w_down)
    assert all(w.dtype == F32 and w.shape[0] % 16 == 0 and w.shape[1] % LANES == 0 for w in weights)
    widths = _stage_widths(weights)
    return pl.pallas_call(
        functools.partial(_prompt_body, tb=tb, nt=nt, nblk=nblk),
        grid=(nblk + 1,),
        in_specs=[x_spec, mem, mem,
                  _const_spec((1, d)), in_hbm, _const_spec((POOL_WIDTH, POOL_WIDTH)),
                  _const_spec((1, POOL_WIDTH)), _const_spec(lb_logits.shape), _const_spec((1, HGRN_WIDTH)),
                  in_hbm,
                  _const_spec((1, d)), in_hbm, _const_spec((CONV_W, 1, D_FF)),
                  _const_spec((1, D_FF)), in_hbm, _const_spec((1, d)), _const_spec((1, LANES))],
        out_specs=[y_spec,
                   pl.BlockSpec((1, POOL_HIST, POOL_WIDTH), lambda g: (mixer_blk(g) // nt, 0, 0)),
                   pl.BlockSpec((1, HGRN_HEADS, HGRN_DK, HGRN_DV), lambda g: (mixer_blk(g) // nt, 0, 0, 0)),
                   pl.BlockSpec((1, CONV_W - 1, D_FF), lambda g: (ffn_blk(g) // nt, 0, 0))]
                  + [pl.BlockSpec(memory_space=pl.ANY) for _ in weights],
        out_shape=[jax.ShapeDtypeStruct((b, l, d), F32),
                   jax.ShapeDtypeStruct((b, POOL_HIST, POOL_WIDTH), F32),
                   jax.ShapeDtypeStruct((b, HGRN_HEADS, HGRN_DK, HGRN_DV), F32),
                   jax.ShapeDtypeStruct((b, CONV_W - 1, D_FF), F32)]
                  + [jax.ShapeDtypeStruct(w.shape, BF16) for w in weights],
        scratch_shapes=[pltpu.VMEM((POOL_PAD + tb, POOL_WIDTH), F32),
                        pltpu.VMEM((HGRN_HEADS, HGRN_DV, HGRN_DK), F32),
                        pltpu.VMEM((CONV_PAD, D_FF), F32),
                        pltpu.VMEM((tb, d), F32)]
                       + [pltpu.VMEM(w.shape, BF16) for w in weights]
                       + [pltpu.SemaphoreType.DMA((len(weights),)),
                          pltpu.SemaphoreType.DMA((len(widths), STAGE_SLOTS))]
                       + [pltpu.VMEM((STAGE_SLOTS, _stage_rows(c), c), F32) for c in widths],
        compiler_params=pltpu.CompilerParams(dimension_semantics=("arbitrary",),
                                             vmem_limit_bytes=VMEM_LIMIT_BYTES),
        name="prompt_layer",
    )(x, mkt, mvt, ln1, w_in, pool_wbd, pool_scale, lb_logits, onorm, w_out,
      ln2, w_up, conv_w, conv_b, w_down, lnf, jnp.zeros((1, LANES), jnp.int32))


def _sample_mixer_body(x_ref, hist_ref, s0_ref, mkt_ref, mvt_ref, ln1_ref, win_ref, poolw_ref, pscale_ref, lbl_ref,
                       onorm_ref, wout_ref, x2_ref, npool_ref, ns_ref, pbuf, *, gs, sl_len):
    rows = gs * sl_len
    seqs = [slice(s * sl_len, (s + 1) * sl_len) for s in range(gs)]
    x = x_ref[...].reshape(rows, D_MODEL)
    h = _rmsnorm(x, ln1_ref[...]).astype(BF16)
    proj = _dot(h, win_ref[...])

    pbuf[:, pl.ds(1, POOL_HIST), :] = hist_ref[...]
    pbuf[:, pl.ds(POOL_PAD, sl_len), :] = proj[:, OFF_U:OFF_U + POOL_WIDTH].reshape(gs, sl_len, POOL_WIDTH)
    posf = (PAST_LEN + 1 + lax.broadcasted_iota(jnp.int32, (1, sl_len, 1), 1)).astype(F32)
    dm = _pool_means(lambda j, half: pbuf[:, pl.ds(POOL_PAD - j, sl_len), pl.ds(LANES * half, LANES)],
                     posf, (gs, sl_len, LANES))
    npool_ref[...] = pbuf[:, pl.ds(sl_len + 1, POOL_HIST), :]
    o_pool = _dot(dm.reshape(rows, POOL_WIDTH).astype(BF16), poolw_ref[...]) * pscale_ref[...]

    qx3 = (proj[:, OFF_X:OFF_X + XATTN_WIDTH] * (XATTN_DH ** -0.5)).reshape(gs, sl_len, XATTN_WIDTH)
    head_of_lane = lax.broadcasted_iota(jnp.int32, qx3.shape, 2) // XATTN_DH
    qs3 = jnp.concatenate([jnp.where(head_of_lane == hd, qx3, 0.0) for hd in range(XATTN_HEADS)],
                          axis=1).astype(BF16)
    hrows = XATTN_HEADS * sl_len
    scores = jnp.concatenate([_dot(qs3[s], mkt_ref[s].astype(BF16)) for s in range(gs)], axis=0)
    p = _softmax_rows(scores).astype(BF16)
    o4 = jnp.concatenate([_dot_nt(p[s * hrows:(s + 1) * hrows], mvt_ref[s].astype(BF16)) for s in range(gs)],
                         axis=0)
    o4 = o4.reshape(gs, XATTN_HEADS, sl_len, XATTN_WIDTH)
    o_x3 = jnp.zeros(qx3.shape, F32)
    for hd in range(XATTN_HEADS):
        o_x3 = jnp.where(head_of_lane == hd, o4[:, hd], o_x3)
    o_x = o_x3.reshape(rows, XATTN_WIDTH)

    lb = _forget_lower_bound(lbl_ref[...])
    qf, k, log_f = _hgrn_gates(proj, lb)
    v = proj[:, OFF_I:OFF_I + HGRN_WIDTH]
    gg = proj[:, OFF_G:OFF_G + HGRN_WIDTH]
    gate = gg * jax.nn.sigmoid(gg) * onorm_ref[...]
    a = _segment_cumsum(log_f, sl_len)
    a3 = a.reshape(gs, sl_len, HGRN_WIDTH)
    a_end = jnp.broadcast_to(a3[:, sl_len - 1:sl_len, :], a3.shape).reshape(rows, HGRN_WIDTH)
    q_in = (qf * jnp.exp2(a)).astype(BF16)
    k_out = (k * jnp.exp2(a_end - a)).astype(BF16)
    decay = jnp.exp2(a_end)
    v_b = v.astype(BF16)
    qk = qf * k
    factors = _level_factors(a, qf, k, log_f, rows, sl_len)
    heads = [slice(hd * HGRN_DK, (hd + 1) * HGRN_DK) for hd in range(HGRN_HEADS)]
    inter = [jnp.concatenate([_dot(q_in[r, sl], s0_ref[s, hd].astype(BF16)) for s, r in enumerate(seqs)], axis=0)
             for hd, sl in enumerate(heads)]
    o_heads = []
    for hd, sl in enumerate(heads):
        p_h = _intra_scores(factors, hd, rows, sl_len)
        o = _dot(p_h.astype(BF16), v_b[:, sl]) + inter[hd]
        o = o + jnp.sum(qk[:, sl], axis=-1, keepdims=True) * v[:, sl]
        o_heads.append(_head_norm_gate(o, gate[:, sl]))
    updates = [[_dot_tn(k_out[r, sl], v_b[r, sl]) for sl in heads] for r in seqs]
    for s, r in enumerate(seqs):
        for hd, sl in enumerate(heads):
            decay_cols = jnp.broadcast_to(decay[r, sl][sl_len - 1:sl_len, :], (HGRN_DV, HGRN_DK)).T
            ns_ref[s, hd] = decay_cols * s0_ref[s, hd] + updates[s][hd]

    mixed = jnp.concatenate([o_pool] + o_heads + [o_x], axis=-1).astype(BF16)
    x2_ref[...] = (x + _dot(mixed, wout_ref[...])).reshape(gs, sl_len, D_MODEL)


def _sample_mixer_call(x, hist, s0, mkt, mvt, mixer_w, gs):
    b, l, d = x.shape
    ln1, w_in, pool_wbd, pool_scale, lb_logits, onorm, w_out = mixer_w
    grid = (b // gs,)
    blk = pl.BlockSpec((gs, l, d), lambda i: (i, 0, 0))
    histb = pl.BlockSpec((gs, POOL_HIST, POOL_WIDTH), lambda i: (i, 0, 0))
    sb = pl.BlockSpec((gs, HGRN_HEADS, HGRN_DK, HGRN_DV), lambda i: (i, 0, 0, 0))
    mem = pl.BlockSpec((gs, XATTN_WIDTH, N_MEM), lambda i: (i, 0, 0))
    return pl.pallas_call(
        functools.partial(_sample_mixer_body, gs=gs, sl_len=l),
        grid=grid,
        in_specs=[blk, histb, sb, mem, mem,
                  _const_spec((1, d)), _const_spec((d, D_IN)), _const_spec((POOL_WIDTH, POOL_WIDTH)),
                  _const_spec((1, POOL_WIDTH)), _const_spec(lb_logits.shape), _const_spec((1, HGRN_WIDTH)),
                  _const_spec((d, d))],
        out_specs=[blk, histb, sb],
        out_shape=[jax.ShapeDtypeStruct((b, l, d), F32),
                   jax.ShapeDtypeStruct((b, POOL_HIST, POOL_WIDTH), F32),
                   jax.ShapeDtypeStruct((b, HGRN_HEADS, HGRN_DK, HGRN_DV), F32)],
        scratch_shapes=[pltpu.VMEM((gs, POOL_PAD + l, POOL_WIDTH), F32)],
        compiler_params=pltpu.CompilerParams(dimension_semantics=("arbitrary",),
                                             vmem_limit_bytes=VMEM_LIMIT_BYTES),
        name="sample_mixer",
    )(x, hist, s0, mkt, mvt, ln1, w_in, pool_wbd, pool_scale, lb_logits, onorm, w_out)


def _sample_ffn_body(x_ref, chist_ref, ln2_ref, wup_ref, cw_ref, cb_ref, wdown_ref, lnf_ref, y_ref, nconv_ref,
                     *, gs, sl_len):
    rows = gs * sl_len
    x = x_ref[...].reshape(rows, D_MODEL)
    h = _rmsnorm(x, ln2_ref[...]).astype(BF16)
    ab = _dot(h, wup_ref[...])
    a = ab[:, :D_FF]
    ridx = lax.broadcasted_iota(jnp.int32, (rows, D_FF), 0) % sl_len
    hist = chist_ref[...]
    h1 = jnp.broadcast_to(hist[:, 1:2, :], (gs, sl_len, D_FF)).reshape(rows, D_FF)
    h0 = jnp.broadcast_to(hist[:, 0:1, :], (gs, sl_len, D_FF)).reshape(rows, D_FF)
    a_m1 = jnp.where(ridx >= 1, pltpu.roll(a, 1, 0), h1)
    a_m2 = jnp.where(ridx >= 2, pltpu.roll(a, 2, 0), jnp.where(ridx == 1, h1, h0))
    conv = cb_ref[...] + cw_ref[0] * a_m2 + cw_ref[1] * a_m1 + cw_ref[2] * a
    act = _gelu_tanh(conv) * ab[:, D_FF:]
    nconv_ref[...] = a.reshape(gs, sl_len, D_FF)[:, sl_len - (CONV_W - 1):, :]
    y_ref[...] = _ffn_tail(x, act, wdown_ref, lnf_ref).reshape(gs, sl_len, D_MODEL)


def _sample_ffn_call(x, chist, ffn_w, gs):
    b, l, d = x.shape
    ln2, w_up, conv_w, conv_b, w_down, lnf = ffn_w
    blk = pl.BlockSpec((gs, l, d), lambda i: (i, 0, 0))
    cblk = pl.BlockSpec((gs, CONV_W - 1, D_FF), lambda i: (i, 0, 0))
    return pl.pallas_call(
        functools.partial(_sample_ffn_body, gs=gs, sl_len=l),
        grid=(b // gs,),
        in_specs=[blk, cblk, _const_spec((1, d)), _const_spec((d, 2 * D_FF)), _const_spec((CONV_W, 1, D_FF)),
                  _const_spec((1, D_FF)), _const_spec((D_FF, d)), _const_spec((1, d))],
        out_specs=[blk, cblk],
        out_shape=[jax.ShapeDtypeStruct((b, l, d), F32),
                   jax.ShapeDtypeStruct((b, CONV_W - 1, D_FF), F32)],
        compiler_params=pltpu.CompilerParams(dimension_semantics=("arbitrary",),
                                             vmem_limit_bytes=VMEM_LIMIT_BYTES),
        name="sample_ffn",
    )(x, chist, ln2, w_up, conv_w, conv_b, w_down, lnf)


def _block_diag(pool_w):
    n = pool_w.shape[0]
    same_group = jnp.eye(n, dtype=bool)[:, None, :, None]
    return jnp.where(same_group, pool_w[:, :, None, :], 0.0).reshape(n * POOL_GROUP, n * POOL_GROUP)


def _layer(x_prompt, x_sample, mem_prompt, state_pool, state_hgrn, state_conv, cache_mem_k, cache_mem_v,
           ln1_g, w_in, pool_w, pool_scale, hgrn_lb_logits, hgrn_onorm_g, mem_norm_g, w_mem_kv, w_out,
           ln2_g, w_up, conv_w, conv_b, w_down, lnf_g, *, prompt_tb, mixer_gs, ffn_gs):
    row = lambda a: a.reshape(1, -1)
    pool_wbd = _block_diag(pool_w).astype(BF16)

    def mixer_w(w_in, w_out):
        return (row(ln1_g), w_in, pool_wbd, row(pool_scale), hgrn_lb_logits, row(hgrn_onorm_g), w_out)

    def ffn_w(w_up, w_down):
        return (row(ln2_g), w_up, conv_w[:, None, :], row(conv_b), w_down, row(lnf_g))

    mkt, mvt = _memkv_call(mem_prompt, row(mem_norm_g), w_mem_kv)
    y_prompt, new_pool_p, new_s_p, new_conv_p, w_in_b, w_out_b, w_up_b, w_down_b = _prompt_call(
        x_prompt, mkt, mvt, mixer_w(w_in, w_out), ffn_w(w_up, w_down), tb=prompt_tb)

    nb = x_sample.shape[0]
    smkt = cache_mem_k.transpose(0, 2, 3, 1).reshape(nb, XATTN_WIDTH, N_MEM)
    smvt = cache_mem_v.transpose(0, 2, 3, 1).reshape(nb, XATTN_WIDTH, N_MEM)
    xs, new_pool_s, new_s_s = _sample_mixer_call(x_sample, state_pool, state_hgrn, smkt, smvt,
                                                 mixer_w(w_in_b, w_out_b), gs=mixer_gs)
    y_sample, new_conv_s = _sample_ffn_call(xs, state_conv, ffn_w(w_up_b, w_down_b), gs=ffn_gs)
    bp = x_prompt.shape[0]
    mk, mv = (t.reshape(bp, XATTN_HEADS, XATTN_DH, N_MEM).transpose(0, 3, 1, 2) for t in (mkt, mvt))
    return (y_prompt, y_sample, new_pool_p, new_s_p, new_conv_p, mk, mv, new_pool_s, new_s_s, new_conv_s)


def kernel(x_prompt, x_sample, mem_prompt, state_pool, state_hgrn, state_conv, cache_mem_k, cache_mem_v,
           ln1_g, w_in, pool_w, pool_scale, hgrn_lb_logits, hgrn_onorm_g, mem_norm_g, w_mem_kv, w_out,
           ln2_g, w_up, conv_w, conv_b, w_down, lnf_g):
    assert w_in.shape[0] == 1, "one layer"
    outs = _layer(x_prompt, x_sample, mem_prompt, state_pool[0], state_hgrn[0], state_conv[0],
                  cache_mem_k[0], cache_mem_v[0], ln1_g[0], w_in[0], pool_w[0], pool_scale[0], hgrn_lb_logits,
                  hgrn_onorm_g[0], mem_norm_g[0], w_mem_kv[0], w_out[0], ln2_g[0], w_up[0], conv_w[0], conv_b[0],
                  w_down[0], lnf_g, prompt_tb=256, mixer_gs=16, ffn_gs=32)
    y_prompt, y_sample = outs[0], outs[1]
    return (y_prompt, y_sample) + tuple(o[None] for o in outs[2:])
```

```python
import functools

import jax
import jax.numpy as jnp
from jax import lax
from jax.experimental import pallas as pl
from jax.experimental.pallas import tpu as pltpu

F32 = jnp.float32
BF16 = jnp.bfloat16

D_MODEL = 1024
POOL_WIDTH = 256
POOL_GROUP = 64
POOL_HIST = 15
HGRN_WIDTH = 512
HGRN_HEADS = 4
HGRN_DK = 128
HGRN_DV = 128
XATTN_WIDTH = 256
XATTN_HEADS = 4
XATTN_DH = 64
N_MEM = 256
D_FF = 2816
CONV_W = 3
EPS = 1e-6
PAST_LEN = 16384
D_IN = POOL_WIDTH + 4 * HGRN_WIDTH + XATTN_WIDTH
OFF_U, OFF_Q, OFF_F, OFF_I, OFF_G, OFF_X = 0, 256, 768, 1280, 1792, 2304

CHUNK = 64
POOL_PAD = 16
CONV_PAD = 8
LANES = 128
MEMKV_GROUP = 4
STAGE_BYTES = 720896
STAGE_SLOTS = 6
VMEM_LIMIT_BYTES = 56 * 1024 * 1024

_NT = (((1,), (1,)), ((), ()))
_TN = (((0,), (0,)), ((), ()))


def _dot(a, b):
    return jnp.dot(a, b, preferred_element_type=F32)


def _dot_nt(a, b):
    return lax.dot_general(a, b, _NT, preferred_element_type=F32)


def _dot_tn(a, b):
    return lax.dot_general(a, b, _TN, preferred_element_type=F32)


def _rmsnorm(x, g):
    return x * lax.rsqrt(jnp.mean(x * x, axis=-1, keepdims=True) + EPS) * g


def _const_spec(shape):
    nd = len(shape)
    return pl.BlockSpec(shape, lambda *_: (0,) * nd, pipeline_mode=pl.Buffered(1))


def _memkv_body(mem_ref, g_ref, w_ref, kt_ref, vt_ref):
    nb = mem_ref.shape[0]
    h = _rmsnorm(mem_ref[...].reshape(nb * N_MEM, D_MODEL), g_ref[...]).astype(BF16)
    kv = _dot(h, w_ref[...].astype(BF16))
    for s in range(nb):
        kvt = kv[s * N_MEM:(s + 1) * N_MEM, :].T
        kt_ref[s] = kvt[:XATTN_WIDTH, :]
        vt_ref[s] = kvt[XATTN_WIDTH:, :]


def _memkv_call(mem, g, w):
    b = mem.shape[0]
    nb = MEMKV_GROUP
    out = jax.ShapeDtypeStruct((b, XATTN_WIDTH, N_MEM), F32)
    return pl.pallas_call(
        _memkv_body,
        grid=(b // nb,),
        in_specs=[pl.BlockSpec((nb, N_MEM, D_MODEL), lambda i: (i, 0, 0)),
                  _const_spec((1, D_MODEL)),
                  _const_spec((D_MODEL, 2 * XATTN_WIDTH))],
        out_specs=[pl.BlockSpec((nb, XATTN_WIDTH, N_MEM), lambda i: (i, 0, 0))] * 2,
        out_shape=[out, out],
        compiler_params=pltpu.CompilerParams(dimension_semantics=("arbitrary",)),
        name="memkv",
    )(mem, g, w)


def _forget_lower_bound(logits):
    z = logits - jnp.max(logits, axis=0, keepdims=True)
    e = jnp.exp(z)
    return e[0:1, :] / jnp.sum(e, axis=0, keepdims=True)


def _hgrn_gates(proj, lb):
    fp = proj[:, OFF_F:OFF_F + HGRN_WIDTH]
    q = proj[:, OFF_Q:OFF_Q + HGRN_WIDTH]
    log_f = jnp.log2(lb + (1.0 - lb) * jax.nn.sigmoid(fp))
    k = (1.0 - lb) * jax.nn.sigmoid(-fp)
    qf = q * jax.nn.sigmoid(q)
    return qf, k, log_f


def _segment_cumsum(x, seq):
    ridx = lax.broadcasted_iota(jnp.int32, x.shape, 0) & (seq - 1)
    sh = 1
    while sh < seq:
        x = x + jnp.where(ridx >= sh, pltpu.roll(x, sh, 0), 0.0)
        sh *= 2
    return x


def _level_factor(a, qf, k, log_f, m, rows):
    n = a.shape[1]
    ridx = lax.broadcasted_iota(jnp.int32, (rows, n), 0)
    upper = (ridx & m) != 0
    if m == 1:
        d = jnp.where(upper, log_f, 0.0)
    else:
        if (2 * m) % 8 == 0:
            nb = rows // (2 * m)
            a3 = a.reshape(nb, 2 * m, n)
            ref = jnp.broadcast_to(a3[:, m - 1:m, :], (nb, 2 * m, n)).reshape(rows, n)
        else:
            a3 = a.reshape(rows // 8, 8, n)
            sub = lax.broadcasted_iota(jnp.int32, a3.shape, 1)
            ref = jnp.where(sub < 4,
                            jnp.broadcast_to(a3[:, 1:2, :], a3.shape),
                            jnp.broadcast_to(a3[:, 5:6, :], a3.shape)).reshape(rows, n)
        d = -jnp.abs(a - ref)
    return (jnp.where(upper, qf, k) * jnp.exp2(d)).astype(BF16)


def _level_factors(a, qf, k, log_f, rows, seq):
    out, m = [], seq // 2
    while m >= 1:
        out.append((m, _level_factor(a, qf, k, log_f, m, rows)))
        m //= 2
    return out


def _intra_scores(factors, head, rows, seq):
    sl = slice(head * HGRN_DK, (head + 1) * HGRN_DK)
    t = lax.broadcasted_iota(jnp.int32, (rows, rows), 0)
    s = lax.broadcasted_iota(jnp.int32, (rows, rows), 1)
    x = t ^ s
    products = [(m, _dot_nt(y[:, sl], y[:, sl])) for m, y in reversed(factors)]
    total = jnp.zeros((rows, rows), F32)
    for m, p in products:
        total = jnp.where(x >= m, p, total)
    return jnp.where((t > s) & (x < seq), total, 0.0)


def _head_norm_gate(o, gate):
    return o * lax.rsqrt(jnp.mean(o * o, axis=-1, keepdims=True) + EPS) * gate


def _softmax_rows(s):
    e = jnp.exp(s - jnp.max(s, axis=-1, keepdims=True))
    return e / jnp.sum(e, axis=-1, keepdims=True)


def _cross_attention(qx, mkt, mvt):
    rows = qx.shape[0]
    head_of_lane = lax.broadcasted_iota(jnp.int32, qx.shape, 1) // XATTN_DH
    qs = jnp.concatenate([jnp.where(head_of_lane == h, qx, 0.0) for h in range(XATTN_HEADS)], axis=0)
    p = _softmax_rows(_dot(qs.astype(BF16), mkt))
    o = _dot_nt(p.astype(BF16), mvt)
    out = jnp.zeros(qx.shape, F32)
    for h in range(XATTN_HEADS):
        out = jnp.where(head_of_lane == h, o[h * rows:(h + 1) * rows, :], out)
    return out


def _pool_means(ld, posf, shape):
    lane = lax.broadcasted_iota(jnp.int32, shape, len(shape) - 1)
    first = lane < POOL_GROUP
    u_lo, u_hi = ld(0, 0), ld(0, 1)
    t2 = u_lo + ld(1, 0)
    t4 = t2 + ld(2, 0) + ld(3, 0)
    t8 = u_hi
    for j in range(1, 8):
        t8 = t8 + ld(j, 1)
    t16 = t8
    for j in range(8, 16):
        t16 = t16 + ld(j, 1)
    cnt_lo = jnp.where(first, jnp.minimum(2.0, posf), jnp.minimum(4.0, posf))
    cnt_hi = jnp.where(first, jnp.minimum(8.0, posf), jnp.minimum(16.0, posf))
    lo = jnp.where(first, t2, t4) / cnt_lo - u_lo
    hi = jnp.where(first, t8, t16) / cnt_hi - u_hi
    return jnp.concatenate([lo, hi], axis=-1)


def _after(x, anchor, zero):
    r, n = anchor.shape
    s = jnp.sum(anchor.reshape(r // 8, 8, n), axis=0)
    c = s[:, :LANES]
    for i in range(1, n // LANES):
        c = c + s[:, i * LANES:(i + 1) * LANES]
    z = lax.bitcast_convert_type(lax.bitcast_convert_type(c[0:1, :], jnp.int32) & zero, F32)
    return jnp.concatenate([x[:, :LANES] + z, x[:, LANES:]], axis=1)


def _gelu_tanh(x):
    c = 0.7978845608028654
    half_x = 0.5 * x
    return half_x + half_x * jnp.tanh(x * (c + (0.044715 * c) * (x * x)))


def _ffn_tail(x, act, wdown_ref, lnf_ref):
    y = x + _dot(act.astype(BF16), wdown_ref[...])
    return _rmsnorm(y, lnf_ref[...])


def _stage_widths(weights):
    return list(dict.fromkeys(w.shape[1] for w in weights))


def _stage_rows(cols):
    return STAGE_BYTES // (4 * cols) // 16 * 16


def _load_weights(srcs, dsts, stages, sem):
    widths = _stage_widths(srcs)
    blocks, used = [], [0] * len(stages)
    for i, w in enumerate(srcs):
        f = widths.index(w.shape[1])
        rows = stages[f].shape[1]
        for r0 in range(0, w.shape[0], rows):
            blocks.append((i, r0, min(rows, w.shape[0] - r0), f, used[f] % STAGE_SLOTS))
            used[f] += 1

    def copy(n):
        i, r0, nr, f, slot = blocks[n]
        return pltpu.make_async_copy(srcs[i].at[pl.ds(r0, nr), :], stages[f].at[slot, pl.ds(0, nr), :],
                                     sem.at[f, slot])

    ahead = STAGE_SLOTS - 1
    for n in range(min(ahead, len(blocks))):
        copy(n).start()
    for n, (i, r0, nr, f, slot) in enumerate(blocks):
        if n + ahead < len(blocks):
            copy(n + ahead).start()
        copy(n).wait()
        dsts[i][pl.ds(r0, nr), :] = stages[f][slot, pl.ds(0, nr), :].astype(BF16)


def _prompt_body(x_ref, mkt_ref, mvt_ref, ln1_ref, win_hbm, poolw_ref, pscale_ref, lbl_ref, onorm_ref,
                 wout_hbm, ln2_ref, wup_hbm, cw_ref, cb_ref, wdown_hbm, lnf_ref, zero_ref,
                 y_ref, npool_ref, ns_ref, nconv_ref, win_out, wout_out, wup_out, wdown_out,
                 pbuf, st, abuf, x2s, win_ref, wout_ref, wup_ref, wdown_ref, out_sem, stage_sem, *stages,
                 tb, nt, nblk):
    g = pl.program_id(0)
    jm = jnp.minimum(g, nblk - 1) % nt
    jf = jnp.maximum(g - 1, 0) % nt

    weights_bf16 = (win_ref, wout_ref, wup_ref, wdown_ref)

    def weight_writeback(n):
        return pltpu.make_async_copy(weights_bf16[n], (win_out, wout_out, wup_out, wdown_out)[n], out_sem.at[n])

    @pl.when(g == 0)
    def _():
        _load_weights((win_hbm, wout_hbm, wup_hbm, wdown_hbm), weights_bf16, stages, stage_sem)
        for n in range(len(weights_bf16)):
            weight_writeback(n).start()

    @pl.when(g == nblk)
    def _():
        for n in range(len(weights_bf16)):
            weight_writeback(n).wait()

    @pl.when(jm == 0)
    def _():
        pbuf[pl.ds(0, POOL_PAD), :] = jnp.zeros((POOL_PAD, POOL_WIDTH), F32)
        st[...] = jnp.zeros(st.shape, F32)

    @pl.when(jf == 0)
    def _():
        abuf[...] = jnp.zeros(abuf.shape, F32)

    def input_projection():
        h = _rmsnorm(x_ref[0], ln1_ref[...]).astype(BF16)
        return _dot(h, win_ref[...])

    def mixer_half(proj):
        _prompt_mixer_half(proj, x_ref, mkt_ref, mvt_ref, poolw_ref, pscale_ref, lbl_ref, onorm_ref, wout_ref,
                           pbuf, st, x2s, jm, tb)

    def ffn_half(proj):
        x2 = x2s[...]
        h2 = _rmsnorm(x2, ln2_ref[...]).astype(BF16)
        ab = _dot(h2, wup_ref[...])
        a = ab[:, :D_FF]
        hist = abuf[...]
        sub = lax.broadcasted_iota(jnp.int32, (CONV_PAD, D_FF), 0)
        r1 = pltpu.roll(a, 1, 0)
        r2 = pltpu.roll(a, 2, 0)
        h1 = jnp.broadcast_to(hist[CONV_PAD - 1:CONV_PAD, :], (CONV_PAD, D_FF))
        h0 = jnp.broadcast_to(hist[CONV_PAD - 2:CONV_PAD - 1, :], (CONV_PAD, D_FF))
        a_m1 = jnp.concatenate([jnp.where(sub >= 1, r1[:CONV_PAD], h1), r1[CONV_PAD:]], axis=0)
        a_m2 = jnp.concatenate([jnp.where(sub >= 2, r2[:CONV_PAD], jnp.where(sub == 1, h1, h0)), r2[CONV_PAD:]],
                               axis=0)
        conv = cb_ref[...] + cw_ref[0] * a_m2 + cw_ref[1] * a_m1 + cw_ref[2] * a
        act = _gelu_tanh(conv) * ab[:, D_FF:]
        abuf[...] = a[tb - CONV_PAD:, :]
        if proj is not None:
            act = _after(act, proj, zero_ref[...])
        y_ref[0] = _rmsnorm(x2 + _dot(act.astype(BF16), wdown_ref[...]), lnf_ref[...])

    @pl.when(g == 0)
    def _():
        mixer_half(input_projection())

    @pl.when(jnp.logical_and(g > 0, g < nblk))
    def _():
        proj = input_projection()
        ffn_half(proj)
        mixer_half(proj)

    @pl.when(g == nblk)
    def _():
        ffn_half(None)

    @pl.when(jnp.logical_and(jm == nt - 1, g < nblk))
    def _():
        seq = g // nt
        for j in range(POOL_HIST):
            npool_ref[j, pl.ds(seq, 1), :] = pbuf[pl.ds(1 + j, 1), :]
        for hd in range(HGRN_HEADS):
            ns_ref[0, hd] = st[hd].T

    @pl.when(jnp.logical_and(jf == nt - 1, g >= 1))
    def _():
        nconv_ref[0] = abuf[pl.ds(CONV_PAD - (CONV_W - 1), CONV_W - 1), :]


def _prompt_mixer_half(proj, x_ref, mkt_ref, mvt_ref, poolw_ref, pscale_ref, lbl_ref, onorm_ref, wout_ref,
                       pbuf, st, x2s, jm, tb):
    x = x_ref[0]

    pbuf[pl.ds(POOL_PAD, tb), :] = proj[:, OFF_U:OFF_U + POOL_WIDTH]
    posf = (jm * tb + 1 + lax.broadcasted_iota(jnp.int32, (tb, 1), 0)).astype(F32)
    dm = _pool_means(lambda j, half: pbuf[pl.ds(POOL_PAD - j, tb), pl.ds(LANES * half, LANES)], posf, (tb, LANES))
    o_pool = _dot(dm.astype(BF16), poolw_ref[...]) * pscale_ref[...]
    pbuf[pl.ds(0, POOL_PAD), :] = pbuf[pl.ds(tb, POOL_PAD), :]

    qx = proj[:, OFF_X:OFF_X + XATTN_WIDTH] * (XATTN_DH ** -0.5)
    o_x = _cross_attention(qx, mkt_ref[0].astype(BF16), mvt_ref[0].astype(BF16))

    lb = _forget_lower_bound(lbl_ref[...])
    qf, k, log_f = _hgrn_gates(proj, lb)
    v = proj[:, OFF_I:OFF_I + HGRN_WIDTH]
    gg = proj[:, OFF_G:OFF_G + HGRN_WIDTH]
    gate = gg * jax.nn.sigmoid(gg) * onorm_ref[...]
    a_all = _segment_cumsum(log_f, CHUNK)
    states = [st[hd] for hd in range(HGRN_HEADS)]
    o_rows = []
    for c in range(tb // CHUNK):
        rs = slice(c * CHUNK, (c + 1) * CHUNK)
        qf_c, k_c, lf_c, v_c, a = qf[rs], k[rs], log_f[rs], v[rs], a_all[rs]
        a_end = a[CHUNK - 1:CHUNK, :]
        q_in = (qf_c * jnp.exp2(a)).astype(BF16)
        k_out = (k_c * jnp.exp2(a_end - a)).astype(BF16)
        decay = jnp.exp2(a_end)
        v_b = v_c.astype(BF16)
        qk = qf_c * k_c
        factors = _level_factors(a, qf_c, k_c, lf_c, CHUNK, CHUNK)
        heads = [slice(hd * HGRN_DK, (hd + 1) * HGRN_DK) for hd in range(HGRN_HEADS)]
        inter = [_dot_nt(q_in[:, sl], states[hd].astype(BF16)) for hd, sl in enumerate(heads)]
        update = [_dot_tn(v_b[:, sl], k_out[:, sl]) for sl in heads]
        scores = [_intra_scores(factors, hd, CHUNK, CHUNK).astype(BF16) for hd in range(HGRN_HEADS)]
        intra = [_dot(scores[hd], v_b[:, sl]) for hd, sl in enumerate(heads)]
        o_heads = []
        for hd, sl in enumerate(heads):
            o = intra[hd] + inter[hd] + jnp.sum(qk[:, sl], axis=-1, keepdims=True) * v_c[:, sl]
            states[hd] = states[hd] * decay[:, sl] + update[hd]
            o_heads.append(_head_norm_gate(o, gate[rs, sl]))
        o_rows.append(jnp.concatenate(o_heads, axis=-1))
    for hd in range(HGRN_HEADS):
        st[hd] = states[hd]
    o_hgrn = jnp.concatenate(o_rows, axis=0)

    mixed = jnp.concatenate([o_pool, o_hgrn, o_x], axis=-1).astype(BF16)
    x2s[...] = x + _dot(mixed, wout_ref[...])


def _prompt_call(x, mkt, mvt, mixer_w, ffn_w, tb):
    b, l, d = x.shape
    nt = l // tb
    nblk = b * nt
    ln1, w_in, pool_wbd, pool_scale, lb_logits, onorm, w_out = mixer_w
    ln2, w_up, conv_w, conv_b, w_down, lnf = ffn_w

    def mixer_blk(g):
        return jnp.minimum(g, nblk - 1)

    def ffn_blk(g):
        return jnp.maximum(g - 1, 0)

    x_spec = pl.BlockSpec((1, tb, d), lambda g: (mixer_blk(g) // nt, mixer_blk(g) % nt, 0))
    mem = pl.BlockSpec((1, XATTN_WIDTH, N_MEM), lambda g: (mixer_blk(g) // nt, 0, 0))
    y_spec = pl.BlockSpec((1, tb, d), lambda g: (ffn_blk(g) // nt, ffn_blk(g) % nt, 0))
    in_hbm = pl.BlockSpec(memory_space=pl.ANY)
    weights = (w_in, w_out, w_up, w_down)
    assert all(w.dtype == F32 and w.shape[0] % 16 == 0 and w.shape[1] % LANES == 0 for w in weights)
    widths = _stage_widths(weights)
    return pl.pallas_call(
        functools.partial(_prompt_body, tb=tb, nt=nt, nblk=nblk),
        grid=(nblk + 1,),
        in_specs=[x_spec, mem, mem,
                  _const_spec((1, d)), in_hbm, _const_spec((POOL_WIDTH, POOL_WIDTH)),
                  _const_spec((1, POOL_WIDTH)), _const_spec(lb_logits.shape), _const_spec((1, HGRN_WIDTH)),
                  in_hbm,
                  _const_spec((1, d)), in_hbm, _const_spec((CONV_W, 1, D_FF)),
                  _const_spec((1, D_FF)), in_hbm, _const_spec((1, d)), _const_spec((1, LANES))],
        out_specs=[y_spec,
                   pl.BlockSpec((POOL_HIST, b, POOL_WIDTH), lambda g: (0, 0, 0)),
                   pl.BlockSpec((1, HGRN_HEADS, HGRN_DK, HGRN_DV), lambda g: (mixer_blk(g) // nt, 0, 0, 0)),
                   pl.BlockSpec((1, CONV_W - 1, D_FF), lambda g: (ffn_blk(g) // nt, 0, 0))]
                  + [pl.BlockSpec(memory_space=pl.ANY) for _ in weights],
        out_shape=[jax.ShapeDtypeStruct((b, l, d), F32),
                   jax.ShapeDtypeStruct((POOL_HIST, b, POOL_WIDTH), F32),
                   jax.ShapeDtypeStruct((b, HGRN_HEADS, HGRN_DK, HGRN_DV), F32),
                   jax.ShapeDtypeStruct((b, CONV_W - 1, D_FF), F32)]
                  + [jax.ShapeDtypeStruct(w.shape, BF16) for w in weights],
        scratch_shapes=[pltpu.VMEM((POOL_PAD + tb, POOL_WIDTH), F32),
                        pltpu.VMEM((HGRN_HEADS, HGRN_DV, HGRN_DK), F32),
                        pltpu.VMEM((CONV_PAD, D_FF), F32),
                        pltpu.VMEM((tb, d), F32)]
                       + [pltpu.VMEM(w.shape, BF16) for w in weights]
                       + [pltpu.SemaphoreType.DMA((len(weights),)),
                          pltpu.SemaphoreType.DMA((len(widths), STAGE_SLOTS))]
                       + [pltpu.VMEM((STAGE_SLOTS, _stage_rows(c), c), F32) for c in widths],
        compiler_params=pltpu.CompilerParams(dimension_semantics=("arbitrary",),
                                             vmem_limit_bytes=VMEM_LIMIT_BYTES),
        name="prompt_layer",
    )(x, mkt, mvt, ln1, w_in, pool_wbd, pool_scale, lb_logits, onorm, w_out,
      ln2, w_up, conv_w, conv_b, w_down, lnf, jnp.zeros((1, LANES), jnp.int32))


def _sample_mixer_body(x_ref, hist_ref, s0_ref, mkt_ref, mvt_ref, ln1_ref, win_ref, poolw_ref, pscale_ref, lbl_ref,
                       onorm_ref, wout_ref, x2_ref, npool_ref, ns_ref, pbuf, ubuf, *, gs, sl_len):
    rows = gs * sl_len
    seqs = [slice(s * sl_len, (s + 1) * sl_len) for s in range(gs)]
    x = x_ref[...].reshape(rows, D_MODEL)
    h = _rmsnorm(x, ln1_ref[...]).astype(BF16)
    proj = _dot(h, win_ref[...])

    pbuf[pl.ds(1, POOL_HIST)] = hist_ref[...]
    halves = [pl.ds(LANES * half, LANES) for half in range(POOL_WIDTH // LANES)]
    for half, lanes in enumerate(halves):
        ubuf[half] = proj[:, OFF_U + LANES * half:OFF_U + LANES * (half + 1)]
    for t in range(sl_len):
        for half, lanes in enumerate(halves):
            pbuf[POOL_PAD + t, :, lanes] = ubuf[half, pl.ds(t, gs, stride=sl_len), :]
    posf = (PAST_LEN + 1 + lax.broadcasted_iota(jnp.int32, (sl_len, 1, 1), 0)).astype(F32)
    dm = _pool_means(lambda j, half: pbuf[pl.ds(POOL_PAD - j, sl_len), :, pl.ds(LANES * half, LANES)],
                     posf, (sl_len, gs, LANES))
    npool_ref[...] = pbuf[pl.ds(sl_len + 1, POOL_HIST)]
    for t in range(sl_len):
        for half in range(len(halves)):
            ubuf[half, pl.ds(t, gs, stride=sl_len), :] = dm[t][:, LANES * half:LANES * (half + 1)]
    dm_rows = jnp.concatenate([ubuf[half] for half in range(len(halves))], axis=-1)
    o_pool = _dot(dm_rows.astype(BF16), poolw_ref[...]) * pscale_ref[...]

    qx3 = (proj[:, OFF_X:OFF_X + XATTN_WIDTH] * (XATTN_DH ** -0.5)).reshape(gs, sl_len, XATTN_WIDTH)
    head_of_lane = lax.broadcasted_iota(jnp.int32, qx3.shape, 2) // XATTN_DH
    qs3 = jnp.concatenate([jnp.where(head_of_lane == hd, qx3, 0.0) for hd in range(XATTN_HEADS)],
                          axis=1).astype(BF16)
    hrows = XATTN_HEADS * sl_len
    scores = jnp.concatenate([_dot(qs3[s], mkt_ref[s].astype(BF16)) for s in range(gs)], axis=0)
    p = _softmax_rows(scores).astype(BF16)
    o4 = jnp.concatenate([_dot_nt(p[s * hrows:(s + 1) * hrows], mvt_ref[s].astype(BF16)) for s in range(gs)],
                         axis=0)
    o4 = o4.reshape(gs, XATTN_HEADS, sl_len, XATTN_WIDTH)
    o_x3 = jnp.zeros(qx3.shape, F32)
    for hd in range(XATTN_HEADS):
        o_x3 = jnp.where(head_of_lane == hd, o4[:, hd], o_x3)
    o_x = o_x3.reshape(rows, XATTN_WIDTH)

    lb = _forget_lower_bound(lbl_ref[...])
    qf, k, log_f = _hgrn_gates(proj, lb)
    v = proj[:, OFF_I:OFF_I + HGRN_WIDTH]
    gg = proj[:, OFF_G:OFF_G + HGRN_WIDTH]
    gate = gg * jax.nn.sigmoid(gg) * onorm_ref[...]
    a = _segment_cumsum(log_f, sl_len)
    a3 = a.reshape(gs, sl_len, HGRN_WIDTH)
    a_end = jnp.broadcast_to(a3[:, sl_len - 1:sl_len, :], a3.shape).reshape(rows, HGRN_WIDTH)
    q_in = (qf * jnp.exp2(a)).astype(BF16)
    k_out = (k * jnp.exp2(a_end - a)).astype(BF16)
    decay = jnp.exp2(a_end)
    v_b = v.astype(BF16)
    qk = qf * k
    factors = _level_factors(a, qf, k, log_f, rows, sl_len)
    heads = [slice(hd * HGRN_DK, (hd + 1) * HGRN_DK) for hd in range(HGRN_HEADS)]
    inter = [jnp.concatenate([_dot(q_in[r, sl], s0_ref[s, hd].astype(BF16)) for s, r in enumerate(seqs)], axis=0)
             for hd, sl in enumerate(heads)]
    o_heads = []
    for hd, sl in enumerate(heads):
        p_h = _intra_scores(factors, hd, rows, sl_len)
        o = _dot(p_h.astype(BF16), v_b[:, sl]) + inter[hd]
        o = o + jnp.sum(qk[:, sl], axis=-1, keepdims=True) * v[:, sl]
        o_heads.append(_head_norm_gate(o, gate[:, sl]))
    updates = [[_dot_tn(k_out[r, sl], v_b[r, sl]) for sl in heads] for r in seqs]
    for s, r in enumerate(seqs):
        for hd, sl in enumerate(heads):
            decay_cols = jnp.broadcast_to(decay[r, sl][sl_len - 1:sl_len, :], (HGRN_DV, HGRN_DK)).T
            ns_ref[s, hd] = decay_cols * s0_ref[s, hd] + updates[s][hd]

    mixed = jnp.concatenate([o_pool] + o_heads + [o_x], axis=-1).astype(BF16)
    x2_ref[...] = (x + _dot(mixed, wout_ref[...])).reshape(gs, sl_len, D_MODEL)


def _sample_mixer_call(x, hist, s0, mkt, mvt, mixer_w, gs):
    b, l, d = x.shape
    ln1, w_in, pool_wbd, pool_scale, lb_logits, onorm, w_out = mixer_w
    grid = (b // gs,)
    blk = pl.BlockSpec((gs, l, d), lambda i: (i, 0, 0))
    histb = pl.BlockSpec((POOL_HIST, gs, POOL_WIDTH), lambda i: (0, i, 0))
    sb = pl.BlockSpec((gs, HGRN_HEADS, HGRN_DK, HGRN_DV), lambda i: (i, 0, 0, 0))
    mem = pl.BlockSpec((gs, XATTN_WIDTH, N_MEM), lambda i: (i, 0, 0))
    return pl.pallas_call(
        functools.partial(_sample_mixer_body, gs=gs, sl_len=l),
        grid=grid,
        in_specs=[blk, histb, sb, mem, mem,
                  _const_spec((1, d)), _const_spec((d, D_IN)), _const_spec((POOL_WIDTH, POOL_WIDTH)),
                  _const_spec((1, POOL_WIDTH)), _const_spec(lb_logits.shape), _const_spec((1, HGRN_WIDTH)),
                  _const_spec((d, d))],
        out_specs=[blk, histb, sb],
        out_shape=[jax.ShapeDtypeStruct((b, l, d), F32),
                   jax.ShapeDtypeStruct((POOL_HIST, b, POOL_WIDTH), F32),
                   jax.ShapeDtypeStruct((b, HGRN_HEADS, HGRN_DK, HGRN_DV), F32)],
        scratch_shapes=[pltpu.VMEM((POOL_PAD + l, gs, POOL_WIDTH), F32),
                        pltpu.VMEM((POOL_WIDTH // LANES, gs * l, LANES), F32)],
        compiler_params=pltpu.CompilerParams(dimension_semantics=("arbitrary",),
                                             vmem_limit_bytes=VMEM_LIMIT_BYTES),
        name="sample_mixer",
    )(x, hist, s0, mkt, mvt, ln1, w_in, pool_wbd, pool_scale, lb_logits, onorm, w_out)


def _sample_ffn_body(x_ref, chist_ref, ln2_ref, wup_ref, cw_ref, cb_ref, wdown_ref, lnf_ref, y_ref, nconv_ref,
                     *, gs, sl_len):
    rows = gs * sl_len
    x = x_ref[...].reshape(rows, D_MODEL)
    h = _rmsnorm(x, ln2_ref[...]).astype(BF16)
    ab = _dot(h, wup_ref[...])
    a = ab[:, :D_FF]
    ridx = lax.broadcasted_iota(jnp.int32, (rows, D_FF), 0) % sl_len
    hist = chist_ref[...]
    h1 = jnp.broadcast_to(hist[:, 1:2, :], (gs, sl_len, D_FF)).reshape(rows, D_FF)
    h0 = jnp.broadcast_to(hist[:, 0:1, :], (gs, sl_len, D_FF)).reshape(rows, D_FF)
    a_m1 = jnp.where(ridx >= 1, pltpu.roll(a, 1, 0), h1)
    a_m2 = jnp.where(ridx >= 2, pltpu.roll(a, 2, 0), jnp.where(ridx == 1, h1, h0))
    conv = cb_ref[...] + cw_ref[0] * a_m2 + cw_ref[1] * a_m1 + cw_ref[2] * a
    act = _gelu_tanh(conv) * ab[:, D_FF:]
    nconv_ref[...] = a.reshape(gs, sl_len, D_FF)[:, sl_len - (CONV_W - 1):, :]
    y_ref[...] = _ffn_tail(x, act, wdown_ref, lnf_ref).reshape(gs, sl_len, D_MODEL)


def _sample_ffn_call(x, chist, ffn_w, gs):
    b, l, d = x.shape
    ln2, w_up, conv_w, conv_b, w_down, lnf = ffn_w
    blk = pl.BlockSpec((gs, l, d), lambda i: (i, 0, 0))
    cblk = pl.BlockSpec((gs, CONV_W - 1, D_FF), lambda i: (i, 0, 0))
    return pl.pallas_call(
        functools.partial(_sample_ffn_body, gs=gs, sl_len=l),
        grid=(b // gs,),
        in_specs=[blk, cblk, _const_spec((1, d)), _const_spec((d, 2 * D_FF)), _const_spec((CONV_W, 1, D_FF)),
                  _const_spec((1, D_FF)), _const_spec((D_FF, d)), _const_spec((1, d))],
        out_specs=[blk, cblk],
        out_shape=[jax.ShapeDtypeStruct((b, l, d), F32),
                   jax.ShapeDtypeStruct((b, CONV_W - 1, D_FF), F32)],
        compiler_params=pltpu.CompilerParams(dimension_semantics=("arbitrary",),
                                             vmem_limit_bytes=VMEM_LIMIT_BYTES),
        name="sample_ffn",
    )(x, chist, ln2, w_up, conv_w, conv_b, w_down, lnf)


def _block_diag(pool_w):
    n = pool_w.shape[0]
    same_group = jnp.eye(n, dtype=bool)[:, None, :, None]
    return jnp.where(same_group, pool_w[:, :, None, :], 0.0).reshape(n * POOL_GROUP, n * POOL_GROUP)


def _layer(x_prompt, x_sample, mem_prompt, state_pool, state_hgrn, state_conv, cache_mem_k, cache_mem_v,
           ln1_g, w_in, pool_w, pool_scale, hgrn_lb_logits, hgrn_onorm_g, mem_norm_g, w_mem_kv, w_out,
           ln2_g, w_up, conv_w, conv_b, w_down, lnf_g, *, prompt_tb, mixer_gs, ffn_gs):
    row = lambda a: a.reshape(1, -1)
    pool_wbd = _block_diag(pool_w).astype(BF16)

    def mixer_w(w_in, w_out):
        return (row(ln1_g), w_in, pool_wbd, row(pool_scale), hgrn_lb_logits, row(hgrn_onorm_g), w_out)

    def ffn_w(w_up, w_down):
        return (row(ln2_g), w_up, conv_w[:, None, :], row(conv_b), w_down, row(lnf_g))

    mkt, mvt = _memkv_call(mem_prompt, row(mem_norm_g), w_mem_kv)
    y_prompt, new_pool_pt, new_s_p, new_conv_p, w_in_b, w_out_b, w_up_b, w_down_b = _prompt_call(
        x_prompt, mkt, mvt, mixer_w(w_in, w_out), ffn_w(w_up, w_down), tb=prompt_tb)
    new_pool_p = new_pool_pt.transpose(1, 0, 2)

    nb = x_sample.shape[0]
    smkt = cache_mem_k.transpose(0, 2, 3, 1).reshape(nb, XATTN_WIDTH, N_MEM)
    smvt = cache_mem_v.transpose(0, 2, 3, 1).reshape(nb, XATTN_WIDTH, N_MEM)
    xs, new_pool_st, new_s_s = _sample_mixer_call(x_sample, state_pool.transpose(1, 0, 2), state_hgrn, smkt, smvt,
                                                  mixer_w(w_in_b, w_out_b), gs=mixer_gs)
    new_pool_s = new_pool_st.transpose(1, 0, 2)
    y_sample, new_conv_s = _sample_ffn_call(xs, state_conv, ffn_w(w_up_b, w_down_b), gs=ffn_gs)
    bp = x_prompt.shape[0]
    mk, mv = (t.reshape(bp, XATTN_HEADS, XATTN_DH, N_MEM).transpose(0, 3, 1, 2) for t in (mkt, mvt))
    return (y_prompt, y_sample, new_pool_p, new_s_p, new_conv_p, mk, mv, new_pool_s, new_s_s, new_conv_s)


def kernel(x_prompt, x_sample, mem_prompt, state_pool, state_hgrn, state_conv, cache_mem_k, cache_mem_v,
           ln1_g, w_in, pool_w, pool_scale, hgrn_lb_logits, hgrn_onorm_g, mem_norm_g, w_mem_kv, w_out,
           ln2_g, w_up, conv_w, conv_b, w_down, lnf_g):
    assert w_in.shape[0] == 1, "one layer"
    outs = _layer(x_prompt, x_sample, mem_prompt, state_pool[0], state_hgrn[0], state_conv[0],
                  cache_mem_k[0], cache_mem_v[0], ln1_g[0], w_in[0], pool_w[0], pool_scale[0], hgrn_lb_logits,
                  hgrn_onorm_g[0], mem_norm_g[0], w_mem_kv[0], w_out[0], ln2_g[0], w_up[0], conv_w[0], conv_b[0],
                  w_down[0], lnf_g, prompt_tb=256, mixer_gs=16, ffn_gs=32)
    y_prompt, y_sample = outs[0], outs[1]
    return (y_prompt, y_sample) + tuple(o[None] for o in outs[2:])
```

```python
import functools

import jax
import jax.numpy as jnp
from jax import lax
from jax.experimental import pallas as pl
from jax.experimental.pallas import tpu as pltpu

F32 = jnp.float32
BF16 = jnp.bfloat16

D_MODEL = 1024
POOL_WIDTH = 256
POOL_GROUP = 64
POOL_HIST = 15
HGRN_WIDTH = 512
HGRN_HEADS = 4
HGRN_DK = 128
HGRN_DV = 128
XATTN_WIDTH = 256
XATTN_HEADS = 4
XATTN_DH = 64
N_MEM = 256
D_FF = 2816
CONV_W = 3
EPS = 1e-6
PAST_LEN = 16384
D_IN = POOL_WIDTH + 4 * HGRN_WIDTH + XATTN_WIDTH
OFF_U, OFF_Q, OFF_F, OFF_I, OFF_G, OFF_X = 0, 256, 768, 1280, 1792, 2304

CHUNK = 64
POOL_PAD = 16
CONV_PAD = 8
LANES = 128
MEMKV_GROUP = 4
STAGE_BYTES = 720896
STAGE_SLOTS = 6
VMEM_LIMIT_BYTES = 56 * 1024 * 1024

_NT = (((1,), (1,)), ((), ()))
_TN = (((0,), (0,)), ((), ()))


def _dot(a, b):
    return jnp.dot(a, b, preferred_element_type=F32)


def _dot_nt(a, b):
    return lax.dot_general(a, b, _NT, preferred_element_type=F32)


def _dot_tn(a, b):
    return lax.dot_general(a, b, _TN, preferred_element_type=F32)


def _rmsnorm(x, g):
    return x * lax.rsqrt(jnp.mean(x * x, axis=-1, keepdims=True) + EPS) * g


def _const_spec(shape):
    nd = len(shape)
    return pl.BlockSpec(shape, lambda *_: (0,) * nd, pipeline_mode=pl.Buffered(1))


def _memkv_body(mem_ref, g_ref, w_ref, kt_ref, vt_ref):
    nb = mem_ref.shape[0]
    h = _rmsnorm(mem_ref[...].reshape(nb * N_MEM, D_MODEL), g_ref[...]).astype(BF16)
    kv = _dot(h, w_ref[...].astype(BF16))
    for s in range(nb):
        kvt = kv[s * N_MEM:(s + 1) * N_MEM, :].T
        kt_ref[s] = kvt[:XATTN_WIDTH, :]
        vt_ref[s] = kvt[XATTN_WIDTH:, :]


def _memkv_call(mem, g, w):
    b = mem.shape[0]
    nb = MEMKV_GROUP
    out = jax.ShapeDtypeStruct((b, XATTN_WIDTH, N_MEM), F32)
    return pl.pallas_call(
        _memkv_body,
        grid=(b // nb,),
        in_specs=[pl.BlockSpec((nb, N_MEM, D_MODEL), lambda i: (i, 0, 0)),
                  _const_spec((1, D_MODEL)),
                  _const_spec((D_MODEL, 2 * XATTN_WIDTH))],
        out_specs=[pl.BlockSpec((nb, XATTN_WIDTH, N_MEM), lambda i: (i, 0, 0))] * 2,
        out_shape=[out, out],
        compiler_params=pltpu.CompilerParams(dimension_semantics=("arbitrary",)),
        name="memkv",
    )(mem, g, w)


def _forget_lower_bound(logits):
    z = logits - jnp.max(logits, axis=0, keepdims=True)
    e = jnp.exp(z)
    return e[0:1, :] / jnp.sum(e, axis=0, keepdims=True)


def _hgrn_gates(proj, lb):
    fp = proj[:, OFF_F:OFF_F + HGRN_WIDTH]
    q = proj[:, OFF_Q:OFF_Q + HGRN_WIDTH]
    log_f = jnp.log2(lb + (1.0 - lb) * jax.nn.sigmoid(fp))
    k = (1.0 - lb) * jax.nn.sigmoid(-fp)
    qf = q * jax.nn.sigmoid(q)
    return qf, k, log_f


def _segment_cumsum(x, seq):
    ridx = lax.broadcasted_iota(jnp.int32, x.shape, 0) & (seq - 1)
    sh = 1
    while sh < seq:
        x = x + jnp.where(ridx >= sh, pltpu.roll(x, sh, 0), 0.0)
        sh *= 2
    return x


def _level_factor(a, qf, k, log_f, m, rows):
    n = a.shape[1]
    ridx = lax.broadcasted_iota(jnp.int32, (rows, n), 0)
    upper = (ridx & m) != 0
    if m == 1:
        d = jnp.where(upper, log_f, 0.0)
    else:
        if (2 * m) % 8 == 0:
            nb = rows // (2 * m)
            a3 = a.reshape(nb, 2 * m, n)
            ref = jnp.broadcast_to(a3[:, m - 1:m, :], (nb, 2 * m, n)).reshape(rows, n)
        else:
            a3 = a.reshape(rows // 8, 8, n)
            sub = lax.broadcasted_iota(jnp.int32, a3.shape, 1)
            ref = jnp.where(sub < 4,
                            jnp.broadcast_to(a3[:, 1:2, :], a3.shape),
                            jnp.broadcast_to(a3[:, 5:6, :], a3.shape)).reshape(rows, n)
        d = -jnp.abs(a - ref)
    return (jnp.where(upper, qf, k) * jnp.exp2(d)).astype(BF16)


def _level_factors(a, qf, k, log_f, rows, seq):
    out, m = [], seq // 2
    while m >= 1:
        out.append((m, _level_factor(a, qf, k, log_f, m, rows)))
        m //= 2
    return out


def _intra_scores(factors, head, rows, seq):
    sl = slice(head * HGRN_DK, (head + 1) * HGRN_DK)
    t = lax.broadcasted_iota(jnp.int32, (rows, rows), 0)
    s = lax.broadcasted_iota(jnp.int32, (rows, rows), 1)
    x = t ^ s
    products = [(m, _dot_nt(y[:, sl], y[:, sl])) for m, y in reversed(factors)]
    total = jnp.zeros((rows, rows), F32)
    for m, p in products:
        total = jnp.where(x >= m, p, total)
    return jnp.where((t > s) & (x < seq), total, 0.0)


def _head_norm_gate(o, gate):
    return o * lax.rsqrt(jnp.mean(o * o, axis=-1, keepdims=True) + EPS) * gate


def _softmax_rows(s):
    e = jnp.exp(s - jnp.max(s, axis=-1, keepdims=True))
    return e / jnp.sum(e, axis=-1, keepdims=True)


def _cross_attention(qx, mkt, mvt):
    rows = qx.shape[0]
    head_of_lane = lax.broadcasted_iota(jnp.int32, qx.shape, 1) // XATTN_DH
    qs = jnp.concatenate([jnp.where(head_of_lane == h, qx, 0.0) for h in range(XATTN_HEADS)], axis=0)
    p = _softmax_rows(_dot(qs.astype(BF16), mkt))
    o = _dot_nt(p.astype(BF16), mvt)
    out = jnp.zeros(qx.shape, F32)
    for h in range(XATTN_HEADS):
        out = jnp.where(head_of_lane == h, o[h * rows:(h + 1) * rows, :], out)
    return out


def _pool_means(ld, posf, shape):
    lane = lax.broadcasted_iota(jnp.int32, shape, len(shape) - 1)
    first = lane < POOL_GROUP
    u_lo, u_hi = ld(0, 0), ld(0, 1)
    t2 = u_lo + ld(1, 0)
    t4 = t2 + ld(2, 0) + ld(3, 0)
    t8 = u_hi
    for j in range(1, 8):
        t8 = t8 + ld(j, 1)
    t16 = t8
    for j in range(8, 16):
        t16 = t16 + ld(j, 1)
    cnt_lo = jnp.where(first, jnp.minimum(2.0, posf), jnp.minimum(4.0, posf))
    cnt_hi = jnp.where(first, jnp.minimum(8.0, posf), jnp.minimum(16.0, posf))
    lo = jnp.where(first, t2, t4) / cnt_lo - u_lo
    hi = jnp.where(first, t8, t16) / cnt_hi - u_hi
    return jnp.concatenate([lo, hi], axis=-1)


def _gelu_tanh(x):
    c = 0.7978845608028654
    half_x = 0.5 * x
    return half_x + half_x * jnp.tanh(x * (c + (0.044715 * c) * (x * x)))


def _ffn_tail(x, act, wdown_ref, lnf_ref):
    y = x + _dot(act.astype(BF16), wdown_ref[...])
    return _rmsnorm(y, lnf_ref[...])


def _stage_widths(weights):
    return list(dict.fromkeys(w.shape[1] for w in weights))


def _stage_rows(cols):
    return STAGE_BYTES // (4 * cols) // 16 * 16


def _load_weights(srcs, dsts, stages, sem):
    widths = _stage_widths(srcs)
    blocks, used = [], [0] * len(stages)
    for i, w in enumerate(srcs):
        f = widths.index(w.shape[1])
        rows = stages[f].shape[1]
        for r0 in range(0, w.shape[0], rows):
            blocks.append((i, r0, min(rows, w.shape[0] - r0), f, used[f] % STAGE_SLOTS))
            used[f] += 1

    def copy(n):
        i, r0, nr, f, slot = blocks[n]
        return pltpu.make_async_copy(srcs[i].at[pl.ds(r0, nr), :], stages[f].at[slot, pl.ds(0, nr), :],
                                     sem.at[f, slot])

    ahead = STAGE_SLOTS - 1
    for n in range(min(ahead, len(blocks))):
        copy(n).start()
    for n, (i, r0, nr, f, slot) in enumerate(blocks):
        if n + ahead < len(blocks):
            copy(n + ahead).start()
        copy(n).wait()
        dsts[i][pl.ds(r0, nr), :] = stages[f][slot, pl.ds(0, nr), :].astype(BF16)


def _prompt_body(x_ref, mkt_ref, mvt_ref, ln1_ref, win_hbm, poolw_ref, pscale_ref, lbl_ref, onorm_ref,
                 wout_hbm, ln2_ref, wup_hbm, cw_ref, cb_ref, wdown_hbm, lnf_ref,
                 y_ref, npool_ref, ns_ref, nconv_ref, win_out, wout_out, wup_out, wdown_out,
                 pbuf, st, abuf, x2s, win_ref, wout_ref, wup_ref, wdown_ref, out_sem, stage_sem, *stages,
                 tb, nt, nblk):
    g = pl.program_id(0)
    jm = jnp.minimum(g, nblk - 1) % nt
    jf = jnp.maximum(g - 1, 0) % nt

    weights_bf16 = (win_ref, wout_ref, wup_ref, wdown_ref)

    def weight_writeback(n):
        return pltpu.make_async_copy(weights_bf16[n], (win_out, wout_out, wup_out, wdown_out)[n], out_sem.at[n])

    @pl.when(g == 0)
    def _():
        _load_weights((win_hbm, wout_hbm, wup_hbm, wdown_hbm), weights_bf16, stages, stage_sem)
        for n in range(len(weights_bf16)):
            weight_writeback(n).start()

    @pl.when(g == nblk)
    def _():
        for n in range(len(weights_bf16)):
            weight_writeback(n).wait()

    @pl.when(jm == 0)
    def _():
        pbuf[pl.ds(0, POOL_PAD), :] = jnp.zeros((POOL_PAD, POOL_WIDTH), F32)
        st[...] = jnp.zeros(st.shape, F32)

    @pl.when(jf == 0)
    def _():
        abuf[...] = jnp.zeros(abuf.shape, F32)

    def input_projection():
        h = _rmsnorm(x_ref[0], ln1_ref[...]).astype(BF16)
        return _dot(h, win_ref[...])

    def mixer_half(proj):
        _prompt_mixer_half(proj, x_ref, mkt_ref, mvt_ref, poolw_ref, pscale_ref, lbl_ref, onorm_ref, wout_ref,
                           pbuf, st, x2s, jm, tb)

    def ffn_half():
        x2 = x2s[...]
        h2 = _rmsnorm(x2, ln2_ref[...]).astype(BF16)
        ab = _dot(h2, wup_ref[...])
        a = ab[:, :D_FF]
        hist = abuf[...]
        sub = lax.broadcasted_iota(jnp.int32, (CONV_PAD, D_FF), 0)
        r1 = pltpu.roll(a, 1, 0)
        r2 = pltpu.roll(a, 2, 0)
        h1 = jnp.broadcast_to(hist[CONV_PAD - 1:CONV_PAD, :], (CONV_PAD, D_FF))
        h0 = jnp.broadcast_to(hist[CONV_PAD - 2:CONV_PAD - 1, :], (CONV_PAD, D_FF))
        a_m1 = jnp.concatenate([jnp.where(sub >= 1, r1[:CONV_PAD], h1), r1[CONV_PAD:]], axis=0)
        a_m2 = jnp.concatenate([jnp.where(sub >= 2, r2[:CONV_PAD], jnp.where(sub == 1, h1, h0)), r2[CONV_PAD:]],
                               axis=0)
        conv = cb_ref[...] + cw_ref[0] * a_m2 + cw_ref[1] * a_m1 + cw_ref[2] * a
        act = _gelu_tanh(conv) * ab[:, D_FF:]
        abuf[...] = a[tb - CONV_PAD:, :]
        y_ref[0] = _ffn_tail(x2, act, wdown_ref, lnf_ref)

    @pl.when(g == 0)
    def _():
        mixer_half(input_projection())

    @pl.when(jnp.logical_and(g > 0, g < nblk))
    def _():
        proj = input_projection()
        ffn_half()
        mixer_half(proj)

    @pl.when(g == nblk)
    def _():
        ffn_half()

    @pl.when(jnp.logical_and(jm == nt - 1, g < nblk))
    def _():
        seq = g // nt
        for j in range(POOL_HIST):
            npool_ref[j, pl.ds(seq, 1), :] = pbuf[pl.ds(1 + j, 1), :]
        for hd in range(HGRN_HEADS):
            ns_ref[0, hd] = st[hd].T

    @pl.when(jnp.logical_and(jf == nt - 1, g >= 1))
    def _():
        nconv_ref[0] = abuf[pl.ds(CONV_PAD - (CONV_W - 1), CONV_W - 1), :]


def _prompt_mixer_half(proj, x_ref, mkt_ref, mvt_ref, poolw_ref, pscale_ref, lbl_ref, onorm_ref, wout_ref,
                       pbuf, st, x2s, jm, tb):
    x = x_ref[0]

    pbuf[pl.ds(POOL_PAD, tb), :] = proj[:, OFF_U:OFF_U + POOL_WIDTH]
    posf = (jm * tb + 1 + lax.broadcasted_iota(jnp.int32, (tb, 1), 0)).astype(F32)
    dm = _pool_means(lambda j, half: pbuf[pl.ds(POOL_PAD - j, tb), pl.ds(LANES * half, LANES)], posf, (tb, LANES))
    o_pool = _dot(dm.astype(BF16), poolw_ref[...]) * pscale_ref[...]
    pbuf[pl.ds(0, POOL_PAD), :] = pbuf[pl.ds(tb, POOL_PAD), :]

    qx = proj[:, OFF_X:OFF_X + XATTN_WIDTH] * (XATTN_DH ** -0.5)
    o_x = _cross_attention(qx, mkt_ref[0].astype(BF16), mvt_ref[0].astype(BF16))

    lb = _forget_lower_bound(lbl_ref[...])
    qf, k, log_f = _hgrn_gates(proj, lb)
    v = proj[:, OFF_I:OFF_I + HGRN_WIDTH]
    gg = proj[:, OFF_G:OFF_G + HGRN_WIDTH]
    gate = gg * jax.nn.sigmoid(gg) * onorm_ref[...]
    a_all = _segment_cumsum(log_f, CHUNK)
    states = [st[hd] for hd in range(HGRN_HEADS)]
    o_rows = []
    for c in range(tb // CHUNK):
        rs = slice(c * CHUNK, (c + 1) * CHUNK)
        qf_c, k_c, lf_c, v_c, a = qf[rs], k[rs], log_f[rs], v[rs], a_all[rs]
        a_end = a[CHUNK - 1:CHUNK, :]
        q_in = (qf_c * jnp.exp2(a)).astype(BF16)
        k_out = (k_c * jnp.exp2(a_end - a)).astype(BF16)
        decay = jnp.exp2(a_end)
        v_b = v_c.astype(BF16)
        qk = qf_c * k_c
        factors = _level_factors(a, qf_c, k_c, lf_c, CHUNK, CHUNK)
        heads = [slice(hd * HGRN_DK, (hd + 1) * HGRN_DK) for hd in range(HGRN_HEADS)]
        inter = [_dot_nt(q_in[:, sl], states[hd].astype(BF16)) for hd, sl in enumerate(heads)]
        update = [_dot_tn(v_b[:, sl], k_out[:, sl]) for sl in heads]
        scores = [_intra_scores(factors, hd, CHUNK, CHUNK).astype(BF16) for hd in range(HGRN_HEADS)]
        intra = [_dot(scores[hd], v_b[:, sl]) for hd, sl in enumerate(heads)]
        o_heads = []
        for hd, sl in enumerate(heads):
            o = intra[hd] + inter[hd] + jnp.sum(qk[:, sl], axis=-1, keepdims=True) * v_c[:, sl]
            states[hd] = states[hd] * decay[:, sl] + update[hd]
            o_heads.append(_head_norm_gate(o, gate[rs, sl]))
        o_rows.append(jnp.concatenate(o_heads, axis=-1))
    for hd in range(HGRN_HEADS):
        st[hd] = states[hd]
    o_hgrn = jnp.concatenate(o_rows, axis=0)

    mixed = jnp.concatenate([o_pool, o_hgrn, o_x], axis=-1).astype(BF16)
    x2s[...] = x + _dot(mixed, wout_ref[...])


def _prompt_call(x, mkt, mvt, mixer_w, ffn_w, tb):
    b, l, d = x.shape
    nt = l // tb
    nblk = b * nt
    ln1, w_in, pool_wbd, pool_scale, lb_logits, onorm, w_out = mixer_w
    ln2, w_up, conv_w, conv_b, w_down, lnf = ffn_w

    def mixer_blk(g):
        return jnp.minimum(g, nblk - 1)

    def ffn_blk(g):
        return jnp.maximum(g - 1, 0)

    x_spec = pl.BlockSpec((1, tb, d), lambda g: (mixer_blk(g) // nt, mixer_blk(g) % nt, 0))
    mem = pl.BlockSpec((1, XATTN_WIDTH, N_MEM), lambda g: (mixer_blk(g) // nt, 0, 0))
    y_spec = pl.BlockSpec((1, tb, d), lambda g: (ffn_blk(g) // nt, ffn_blk(g) % nt, 0))
    in_hbm = pl.BlockSpec(memory_space=pl.ANY)
    weights = (w_in, w_out, w_up, w_down)
    assert all(w.dtype == F32 and w.shape[0] % 16 == 0 and w.shape[1] % LANES == 0 for w in weights)
    widths = _stage_widths(weights)
    return pl.pallas_call(
        functools.partial(_prompt_body, tb=tb, nt=nt, nblk=nblk),
        grid=(nblk + 1,),
        in_specs=[x_spec, mem, mem,
                  _const_spec((1, d)), in_hbm, _const_spec((POOL_WIDTH, POOL_WIDTH)),
                  _const_spec((1, POOL_WIDTH)), _const_spec(lb_logits.shape), _const_spec((1, HGRN_WIDTH)),
                  in_hbm,
                  _const_spec((1, d)), in_hbm, _const_spec((CONV_W, 1, D_FF)),
                  _const_spec((1, D_FF)), in_hbm, _const_spec((1, d))],
        out_specs=[y_spec,
                   pl.BlockSpec((POOL_HIST, b, POOL_WIDTH), lambda g: (0, 0, 0)),
                   pl.BlockSpec((1, HGRN_HEADS, HGRN_DK, HGRN_DV), lambda g: (mixer_blk(g) // nt, 0, 0, 0)),
                   pl.BlockSpec((1, CONV_W - 1, D_FF), lambda g: (ffn_blk(g) // nt, 0, 0))]
                  + [pl.BlockSpec(memory_space=pl.ANY) for _ in weights],
        out_shape=[jax.ShapeDtypeStruct((b, l, d), F32),
                   jax.ShapeDtypeStruct((POOL_HIST, b, POOL_WIDTH), F32),
                   jax.ShapeDtypeStruct((b, HGRN_HEADS, HGRN_DK, HGRN_DV), F32),
                   jax.ShapeDtypeStruct((b, CONV_W - 1, D_FF), F32)]
                  + [jax.ShapeDtypeStruct(w.shape, BF16) for w in weights],
        scratch_shapes=[pltpu.VMEM((POOL_PAD + tb, POOL_WIDTH), F32),
                        pltpu.VMEM((HGRN_HEADS, HGRN_DV, HGRN_DK), F32),
                        pltpu.VMEM((CONV_PAD, D_FF), F32),
                        pltpu.VMEM((tb, d), F32)]
                       + [pltpu.VMEM(w.shape, BF16) for w in weights]
                       + [pltpu.SemaphoreType.DMA((len(weights),)),
                          pltpu.SemaphoreType.DMA((len(widths), STAGE_SLOTS))]
                       + [pltpu.VMEM((STAGE_SLOTS, _stage_rows(c), c), F32) for c in widths],
        compiler_params=pltpu.CompilerParams(dimension_semantics=("arbitrary",),
                                             vmem_limit_bytes=VMEM_LIMIT_BYTES),
        name="prompt_layer",
    )(x, mkt, mvt, ln1, w_in, pool_wbd, pool_scale, lb_logits, onorm, w_out,
      ln2, w_up, conv_w, conv_b, w_down, lnf)


def _sample_mixer_body(x_ref, hist_ref, s0_ref, mkt_ref, mvt_ref, ln1_ref, win_ref, poolw_ref, pscale_ref, lbl_ref,
                       onorm_ref, wout_ref, x2_ref, npool_ref, ns_ref, pbuf, ubuf, *, gs, sl_len):
    rows = gs * sl_len
    seqs = [slice(s * sl_len, (s + 1) * sl_len) for s in range(gs)]
    x = x_ref[...].reshape(rows, D_MODEL)
    h = _rmsnorm(x, ln1_ref[...]).astype(BF16)
    proj = _dot(h, win_ref[...])

    pbuf[pl.ds(1, POOL_HIST)] = hist_ref[...]
    halves = [pl.ds(LANES * half, LANES) for half in range(POOL_WIDTH // LANES)]
    for half, lanes in enumerate(halves):
        ubuf[half] = proj[:, OFF_U + LANES * half:OFF_U + LANES * (half + 1)]
    for t in range(sl_len):
        for half, lanes in enumerate(halves):
            pbuf[POOL_PAD + t, :, lanes] = ubuf[half, pl.ds(t, gs, stride=sl_len), :]
    posf = (PAST_LEN + 1 + lax.broadcasted_iota(jnp.int32, (sl_len, 1, 1), 0)).astype(F32)
    dm = _pool_means(lambda j, half: pbuf[pl.ds(POOL_PAD - j, sl_len), :, pl.ds(LANES * half, LANES)],
                     posf, (sl_len, gs, LANES))
    npool_ref[...] = pbuf[pl.ds(sl_len + 1, POOL_HIST)]
    for t in range(sl_len):
        for half in range(len(halves)):
            ubuf[half, pl.ds(t, gs, stride=sl_len), :] = dm[t][:, LANES * half:LANES * (half + 1)]
    dm_rows = jnp.concatenate([ubuf[half] for half in range(len(halves))], axis=-1)
    o_pool = _dot(dm_rows.astype(BF16), poolw_ref[...]) * pscale_ref[...]

    qx3 = (proj[:, OFF_X:OFF_X + XATTN_WIDTH] * (XATTN_DH ** -0.5)).reshape(gs, sl_len, XATTN_WIDTH)
    head_of_lane = lax.broadcasted_iota(jnp.int32, qx3.shape, 2) // XATTN_DH
    qs3 = jnp.concatenate([jnp.where(head_of_lane == hd, qx3, 0.0) for hd in range(XATTN_HEADS)],
                          axis=1).astype(BF16)
    hrows = XATTN_HEADS * sl_len
    scores = jnp.concatenate([_dot(qs3[s], mkt_ref[s].astype(BF16)) for s in range(gs)], axis=0)
    p = _softmax_rows(scores).astype(BF16)
    o4 = jnp.concatenate([_dot_nt(p[s * hrows:(s + 1) * hrows], mvt_ref[s].astype(BF16)) for s in range(gs)],
                         axis=0)
    o4 = o4.reshape(gs, XATTN_HEADS, sl_len, XATTN_WIDTH)
    o_x3 = jnp.zeros(qx3.shape, F32)
    for hd in range(XATTN_HEADS):
        o_x3 = jnp.where(head_of_lane == hd, o4[:, hd], o_x3)
    o_x = o_x3.reshape(rows, XATTN_WIDTH)

    lb = _forget_lower_bound(lbl_ref[...])
    qf, k, log_f = _hgrn_gates(proj, lb)
    v = proj[:, OFF_I:OFF_I + HGRN_WIDTH]
    gg = proj[:, OFF_G:OFF_G + HGRN_WIDTH]
    gate = gg * jax.nn.sigmoid(gg) * onorm_ref[...]
    a = _segment_cumsum(log_f, sl_len)
    a3 = a.reshape(gs, sl_len, HGRN_WIDTH)
    a_end = jnp.broadcast_to(a3[:, sl_len - 1:sl_len, :], a3.shape).reshape(rows, HGRN_WIDTH)
    q_in = (qf * jnp.exp2(a)).astype(BF16)
    k_out = (k * jnp.exp2(a_end - a)).astype(BF16)
    decay = jnp.exp2(a_end)
    v_b = v.astype(BF16)
    qk = qf * k
    factors = _level_factors(a, qf, k, log_f, rows, sl_len)
    heads = [slice(hd * HGRN_DK, (hd + 1) * HGRN_DK) for hd in range(HGRN_HEADS)]
    inter = [jnp.concatenate([_dot(q_in[r, sl], s0_ref[s, hd].astype(BF16)) for s, r in enumerate(seqs)], axis=0)
             for hd, sl in enumerate(heads)]
    o_heads = []
    for hd, sl in enumerate(heads):
        p_h = _intra_scores(factors, hd, rows, sl_len)
        o = _dot(p_h.astype(BF16), v_b[:, sl]) + inter[hd]
        o = o + jnp.sum(qk[:, sl], axis=-1, keepdims=True) * v[:, sl]
        o_heads.append(_head_norm_gate(o, gate[:, sl]))
    updates = [[_dot_tn(k_out[r, sl], v_b[r, sl]) for sl in heads] for r in seqs]
    for s, r in enumerate(seqs):
        for hd, sl in enumerate(heads):
            decay_cols = jnp.broadcast_to(decay[r, sl][sl_len - 1:sl_len, :], (HGRN_DV, HGRN_DK)).T
            ns_ref[s, hd] = decay_cols * s0_ref[s, hd] + updates[s][hd]

    mixed = jnp.concatenate([o_pool] + o_heads + [o_x], axis=-1).astype(BF16)
    x2_ref[...] = (x + _dot(mixed, wout_ref[...])).reshape(gs, sl_len, D_MODEL)


def _sample_mixer_call(x, hist, s0, mkt, mvt, mixer_w, gs):
    b, l, d = x.shape
    ln1, w_in, pool_wbd, pool_scale, lb_logits, onorm, w_out = mixer_w
    grid = (b // gs,)
    blk = pl.BlockSpec((gs, l, d), lambda i: (i, 0, 0))
    histb = pl.BlockSpec((POOL_HIST, gs, POOL_WIDTH), lambda i: (0, i, 0))
    sb = pl.BlockSpec((gs, HGRN_HEADS, HGRN_DK, HGRN_DV), lambda i: (i, 0, 0, 0))
    mem = pl.BlockSpec((gs, XATTN_WIDTH, N_MEM), lambda i: (i, 0, 0))
    return pl.pallas_call(
        functools.partial(_sample_mixer_body, gs=gs, sl_len=l),
        grid=grid,
        in_specs=[blk, histb, sb, mem, mem,
                  _const_spec((1, d)), _const_spec((d, D_IN)), _const_spec((POOL_WIDTH, POOL_WIDTH)),
                  _const_spec((1, POOL_WIDTH)), _const_spec(lb_logits.shape), _const_spec((1, HGRN_WIDTH)),
                  _const_spec((d, d))],
        out_specs=[blk, histb, sb],
        out_shape=[jax.ShapeDtypeStruct((b, l, d), F32),
                   jax.ShapeDtypeStruct((POOL_HIST, b, POOL_WIDTH), F32),
                   jax.ShapeDtypeStruct((b, HGRN_HEADS, HGRN_DK, HGRN_DV), F32)],
        scratch_shapes=[pltpu.VMEM((POOL_PAD + l, gs, POOL_WIDTH), F32),
                        pltpu.VMEM((POOL_WIDTH // LANES, gs * l, LANES), F32)],
        compiler_params=pltpu.CompilerParams(dimension_semantics=("arbitrary",),
                                             vmem_limit_bytes=VMEM_LIMIT_BYTES),
        name="sample_mixer",
    )(x, hist, s0, mkt, mvt, ln1, w_in, pool_wbd, pool_scale, lb_logits, onorm, w_out)


def _sample_ffn_body(x_ref, chist_ref, ln2_ref, wup_ref, cw_ref, cb_ref, wdown_ref, lnf_ref, y_ref, nconv_ref,
                     *, gs, sl_len):
    rows = gs * sl_len
    x = x_ref[...].reshape(rows, D_MODEL)
    h = _rmsnorm(x, ln2_ref[...]).astype(BF16)
    ab = _dot(h, wup_ref[...])
    a = ab[:, :D_FF]
    ridx = lax.broadcasted_iota(jnp.int32, (rows, D_FF), 0) % sl_len
    hist = chist_ref[...]
    h1 = jnp.broadcast_to(hist[:, 1:2, :], (gs, sl_len, D_FF)).reshape(rows, D_FF)
    h0 = jnp.broadcast_to(hist[:, 0:1, :], (gs, sl_len, D_FF)).reshape(rows, D_FF)
    a_m1 = jnp.where(ridx >= 1, pltpu.roll(a, 1, 0), h1)
    a_m2 = jnp.where(ridx >= 2, pltpu.roll(a, 2, 0), jnp.where(ridx == 1, h1, h0))
    conv = cb_ref[...] + cw_ref[0] * a_m2 + cw_ref[1] * a_m1 + cw_ref[2] * a
    act = _gelu_tanh(conv) * ab[:, D_FF:]
    nconv_ref[...] = a.reshape(gs, sl_len, D_FF)[:, sl_len - (CONV_W - 1):, :]
    y_ref[...] = _ffn_tail(x, act, wdown_ref, lnf_ref).reshape(gs, sl_len, D_MODEL)


def _sample_ffn_call(x, chist, ffn_w, gs):
    b, l, d = x.shape
    ln2, w_up, conv_w, conv_b, w_down, lnf = ffn_w
    blk = pl.BlockSpec((gs, l, d), lambda i: (i, 0, 0))
    cblk = pl.BlockSpec((gs, CONV_W - 1, D_FF), lambda i: (i, 0, 0))
    return pl.pallas_call(
        functools.partial(_sample_ffn_body, gs=gs, sl_len=l),
        grid=(b // gs,),
        in_specs=[blk, cblk, _const_spec((1, d)), _const_spec((d, 2 * D_FF)), _const_spec((CONV_W, 1, D_FF)),
                  _const_spec((1, D_FF)), _const_spec((D_FF, d)), _const_spec((1, d))],
        out_specs=[blk, cblk],
        out_shape=[jax.ShapeDtypeStruct((b, l, d), F32),
                   jax.ShapeDtypeStruct((b, CONV_W - 1, D_FF), F32)],
        compiler_params=pltpu.CompilerParams(dimension_semantics=("arbitrary",),
                                             vmem_limit_bytes=VMEM_LIMIT_BYTES),
        name="sample_ffn",
    )(x, chist, ln2, w_up, conv_w, conv_b, w_down, lnf)


def _block_diag(pool_w):
    n = pool_w.shape[0]
    same_group = jnp.eye(n, dtype=bool)[:, None, :, None]
    return jnp.where(same_group, pool_w[:, :, None, :], 0.0).reshape(n * POOL_GROUP, n * POOL_GROUP)


def _layer(x_prompt, x_sample, mem_prompt, state_pool, state_hgrn, state_conv, cache_mem_k, cache_mem_v,
           ln1_g, w_in, pool_w, pool_scale, hgrn_lb_logits, hgrn_onorm_g, mem_norm_g, w_mem_kv, w_out,
           ln2_g, w_up, conv_w, conv_b, w_down, lnf_g, *, prompt_tb, mixer_gs, ffn_gs):
    row = lambda a: a.reshape(1, -1)
    pool_wbd = _block_diag(pool_w).astype(BF16)

    def mixer_w(w_in, w_out):
        return (row(ln1_g), w_in, pool_wbd, row(pool_scale), hgrn_lb_logits, row(hgrn_onorm_g), w_out)

    def ffn_w(w_up, w_down):
        return (row(ln2_g), w_up, conv_w[:, None, :], row(conv_b), w_down, row(lnf_g))

    mkt, mvt = _memkv_call(mem_prompt, row(mem_norm_g), w_mem_kv)
    y_prompt, new_pool_pt, new_s_p, new_conv_p, w_in_b, w_out_b, w_up_b, w_down_b = _prompt_call(
        x_prompt, mkt, mvt, mixer_w(w_in, w_out), ffn_w(w_up, w_down), tb=prompt_tb)
    new_pool_p = new_pool_pt.transpose(1, 0, 2)

    nb = x_sample.shape[0]
    smkt = cache_mem_k.transpose(0, 2, 3, 1).reshape(nb, XATTN_WIDTH, N_MEM)
    smvt = cache_mem_v.transpose(0, 2, 3, 1).reshape(nb, XATTN_WIDTH, N_MEM)
    xs, new_pool_st, new_s_s = _sample_mixer_call(x_sample, state_pool.transpose(1, 0, 2), state_hgrn, smkt, smvt,
                                                  mixer_w(w_in_b, w_out_b), gs=mixer_gs)
    new_pool_s = new_pool_st.transpose(1, 0, 2)
    y_sample, new_conv_s = _sample_ffn_call(xs, state_conv, ffn_w(w_up_b, w_down_b), gs=ffn_gs)
    bp = x_prompt.shape[0]
    mk, mv = (t.reshape(bp, XATTN_HEADS, XATTN_DH, N_MEM).transpose(0, 3, 1, 2) for t in (mkt, mvt))
    return (y_prompt, y_sample, new_pool_p, new_s_p, new_conv_p, mk, mv, new_pool_s, new_s_s, new_conv_s)


def kernel(x_prompt, x_sample, mem_prompt, state_pool, state_hgrn, state_conv, cache_mem_k, cache_mem_v,
           ln1_g, w_in, pool_w, pool_scale, hgrn_lb_logits, hgrn_onorm_g, mem_norm_g, w_mem_kv, w_out,
           ln2_g, w_up, conv_w, conv_b, w_down, lnf_g):
    assert w_in.shape[0] == 1, "one layer"
    outs = _layer(x_prompt, x_sample, mem_prompt, state_pool[0], state_hgrn[0], state_conv[0],
                  cache_mem_k[0], cache_mem_v[0], ln1_g[0], w_in[0], pool_w[0], pool_scale[0], hgrn_lb_logits,
                  hgrn_onorm_g[0], mem_norm_g[0], w_mem_kv[0], w_out[0], ln2_g[0], w_up[0], conv_w[0], conv_b[0],
                  w_down[0], lnf_g, prompt_tb=256, mixer_gs=16, ffn_gs=32)
    y_prompt, y_sample = outs[0], outs[1]
    return (y_prompt, y_sample) + tuple(o[None] for o in outs[2:])
```

```python
import functools

import jax
import jax.numpy as jnp
from jax import lax
from jax.experimental import pallas as pl
from jax.experimental.pallas import tpu as pltpu

F32 = jnp.float32
BF16 = jnp.bfloat16

D_MODEL = 1024
POOL_WIDTH = 256
POOL_GROUP = 64
POOL_HIST = 15
HGRN_WIDTH = 512
HGRN_HEADS = 4
HGRN_DK = 128
HGRN_DV = 128
XATTN_WIDTH = 256
XATTN_HEADS = 4
XATTN_DH = 64
N_MEM = 256
D_FF = 2816
CONV_W = 3
EPS = 1e-6
PAST_LEN = 16384
D_IN = POOL_WIDTH + 4 * HGRN_WIDTH + XATTN_WIDTH
OFF_U, OFF_Q, OFF_F, OFF_I, OFF_G, OFF_X = 0, 256, 768, 1280, 1792, 2304

CHUNK = 64
POOL_PAD = 16
CONV_PAD = 8
LANES = 128
MEMKV_GROUP = 4
STAGE_BYTES = 720896
STAGE_SLOTS = 6
VMEM_LIMIT_BYTES = 56 * 1024 * 1024

_NT = (((1,), (1,)), ((), ()))
_TN = (((0,), (0,)), ((), ()))


def _dot(a, b):
    return jnp.dot(a, b, preferred_element_type=F32)


def _dot_nt(a, b):
    return lax.dot_general(a, b, _NT, preferred_element_type=F32)


def _dot_tn(a, b):
    return lax.dot_general(a, b, _TN, preferred_element_type=F32)


def _rmsnorm(x, g):
    return x * lax.rsqrt(jnp.mean(x * x, axis=-1, keepdims=True) + EPS) * g


def _const_spec(shape):
    nd = len(shape)
    return pl.BlockSpec(shape, lambda *_: (0,) * nd, pipeline_mode=pl.Buffered(1))


def _memkv_body(mem_ref, g_ref, w_ref, kt_ref, vt_ref):
    nb = mem_ref.shape[0]
    h = _rmsnorm(mem_ref[...].reshape(nb * N_MEM, D_MODEL), g_ref[...]).astype(BF16)
    kv = _dot(h, w_ref[...].astype(BF16))
    for s in range(nb):
        kvt = kv[s * N_MEM:(s + 1) * N_MEM, :].T
        kt_ref[s] = kvt[:XATTN_WIDTH, :]
        vt_ref[s] = kvt[XATTN_WIDTH:, :]


def _memkv_call(mem, g, w):
    b = mem.shape[0]
    nb = MEMKV_GROUP
    out = jax.ShapeDtypeStruct((b, XATTN_WIDTH, N_MEM), F32)
    return pl.pallas_call(
        _memkv_body,
        grid=(b // nb,),
        in_specs=[pl.BlockSpec((nb, N_MEM, D_MODEL), lambda i: (i, 0, 0)),
                  _const_spec((1, D_MODEL)),
                  _const_spec((D_MODEL, 2 * XATTN_WIDTH))],
        out_specs=[pl.BlockSpec((nb, XATTN_WIDTH, N_MEM), lambda i: (i, 0, 0))] * 2,
        out_shape=[out, out],
        compiler_params=pltpu.CompilerParams(dimension_semantics=("arbitrary",)),
        name="memkv",
    )(mem, g, w)


def _forget_lower_bound(logits):
    z = logits - jnp.max(logits, axis=0, keepdims=True)
    e = jnp.exp(z)
    return e[0:1, :] / jnp.sum(e, axis=0, keepdims=True)


def _hgrn_gates(proj, lb):
    fp = proj[:, OFF_F:OFF_F + HGRN_WIDTH]
    q = proj[:, OFF_Q:OFF_Q + HGRN_WIDTH]
    log_f = jnp.log2(lb + (1.0 - lb) * jax.nn.sigmoid(fp))
    k = (1.0 - lb) * jax.nn.sigmoid(-fp)
    qf = q * jax.nn.sigmoid(q)
    return qf, k, log_f


def _segment_cumsum(x, seq):
    ridx = lax.broadcasted_iota(jnp.int32, x.shape, 0) & (seq - 1)
    sh = 1
    while sh < seq:
        x = x + jnp.where(ridx >= sh, pltpu.roll(x, sh, 0), 0.0)
        sh *= 2
    return x


def _level_factor(a, qf, k, log_f, m, rows):
    n = a.shape[1]
    ridx = lax.broadcasted_iota(jnp.int32, (rows, n), 0)
    upper = (ridx & m) != 0
    if m == 1:
        d = jnp.where(upper, log_f, 0.0)
    else:
        if (2 * m) % 8 == 0:
            nb = rows // (2 * m)
            a3 = a.reshape(nb, 2 * m, n)
            ref = jnp.broadcast_to(a3[:, m - 1:m, :], (nb, 2 * m, n)).reshape(rows, n)
        else:
            a3 = a.reshape(rows // 8, 8, n)
            sub = lax.broadcasted_iota(jnp.int32, a3.shape, 1)
            ref = jnp.where(sub < 4,
                            jnp.broadcast_to(a3[:, 1:2, :], a3.shape),
                            jnp.broadcast_to(a3[:, 5:6, :], a3.shape)).reshape(rows, n)
        d = -jnp.abs(a - ref)
    return (jnp.where(upper, qf, k) * jnp.exp2(d)).astype(BF16)


def _level_factors(a, qf, k, log_f, rows, seq):
    out, m = [], seq // 2
    while m >= 1:
        out.append((m, _level_factor(a, qf, k, log_f, m, rows)))
        m //= 2
    return out


def _intra_scores(factors, head, rows, seq):
    sl = slice(head * HGRN_DK, (head + 1) * HGRN_DK)
    t = lax.broadcasted_iota(jnp.int32, (rows, rows), 0)
    s = lax.broadcasted_iota(jnp.int32, (rows, rows), 1)
    x = t ^ s
    products = [(m, _dot_nt(y[:, sl], y[:, sl])) for m, y in reversed(factors)]
    total = jnp.zeros((rows, rows), F32)
    for m, p in products:
        total = jnp.where(x >= m, p, total)
    return jnp.where((t > s) & (x < seq), total, 0.0)


def _head_norm_gate(o, gate):
    return o * lax.rsqrt(jnp.mean(o * o, axis=-1, keepdims=True) + EPS) * gate


def _softmax_rows(s):
    e = jnp.exp(s - jnp.max(s, axis=-1, keepdims=True))
    return e / jnp.sum(e, axis=-1, keepdims=True)


def _cross_attention(qx, mkt, mvt):
    rows = qx.shape[0]
    head_of_lane = lax.broadcasted_iota(jnp.int32, qx.shape, 1) // XATTN_DH
    qs = jnp.concatenate([jnp.where(head_of_lane == h, qx, 0.0) for h in range(XATTN_HEADS)], axis=0)
    p = _softmax_rows(_dot(qs.astype(BF16), mkt))
    o = _dot_nt(p.astype(BF16), mvt)
    out = jnp.zeros(qx.shape, F32)
    for h in range(XATTN_HEADS):
        out = jnp.where(head_of_lane == h, o[h * rows:(h + 1) * rows, :], out)
    return out


def _pool_means(ld, posf, shape):
    lane = lax.broadcasted_iota(jnp.int32, shape, len(shape) - 1)
    first = lane < POOL_GROUP
    u_lo, u_hi = ld(0, 0), ld(0, 1)
    t2 = u_lo + ld(1, 0)
    t4 = t2 + ld(2, 0) + ld(3, 0)
    t8 = u_hi
    for j in range(1, 8):
        t8 = t8 + ld(j, 1)
    t16 = t8
    for j in range(8, 16):
        t16 = t16 + ld(j, 1)
    cnt_lo = jnp.where(first, jnp.minimum(2.0, posf), jnp.minimum(4.0, posf))
    cnt_hi = jnp.where(first, jnp.minimum(8.0, posf), jnp.minimum(16.0, posf))
    lo = jnp.where(first, t2, t4) / cnt_lo - u_lo
    hi = jnp.where(first, t8, t16) / cnt_hi - u_hi
    return jnp.concatenate([lo, hi], axis=-1)


def _after(x, anchor, zero):
    r, n = anchor.shape
    s = jnp.sum(anchor.reshape(r // 8, 8, n), axis=0)
    c = s[:, :LANES]
    for i in range(1, n // LANES):
        c = c + s[:, i * LANES:(i + 1) * LANES]
    z = lax.bitcast_convert_type(lax.bitcast_convert_type(c[0:1, :], jnp.int32) & zero, F32)
    return jnp.concatenate([x[:, :LANES] + z, x[:, LANES:]], axis=1)


def _gelu_tanh(x):
    c = 0.7978845608028654
    half_x = 0.5 * x
    return half_x + half_x * jnp.tanh(x * (c + (0.044715 * c) * (x * x)))


def _ffn_tail(x, act, wdown_ref, lnf_ref):
    y = x + _dot(act.astype(BF16), wdown_ref[...])
    return _rmsnorm(y, lnf_ref[...])


def _stage_widths(weights):
    return list(dict.fromkeys(w.shape[1] for w in weights))


def _stage_rows(cols):
    return STAGE_BYTES // (4 * cols) // 16 * 16


def _load_weights(srcs, dsts, stages, sem):
    widths = _stage_widths(srcs)
    blocks, used = [], [0] * len(stages)
    for i, w in enumerate(srcs):
        f = widths.index(w.shape[1])
        rows = stages[f].shape[1]
        for r0 in range(0, w.shape[0], rows):
            blocks.append((i, r0, min(rows, w.shape[0] - r0), f, used[f] % STAGE_SLOTS))
            used[f] += 1

    def copy(n):
        i, r0, nr, f, slot = blocks[n]
        return pltpu.make_async_copy(srcs[i].at[pl.ds(r0, nr), :], stages[f].at[slot, pl.ds(0, nr), :],
                                     sem.at[f, slot])

    ahead = STAGE_SLOTS - 1
    for n in range(min(ahead, len(blocks))):
        copy(n).start()
    for n, (i, r0, nr, f, slot) in enumerate(blocks):
        if n + ahead < len(blocks):
            copy(n + ahead).start()
        copy(n).wait()
        dsts[i][pl.ds(r0, nr), :] = stages[f][slot, pl.ds(0, nr), :].astype(BF16)


def _prompt_body(x_ref, mkt_ref, mvt_ref, ln1_ref, win_hbm, poolw_ref, pscale_ref, lbl_ref, onorm_ref,
                 wout_hbm, ln2_ref, wup_hbm, cw_ref, cb_ref, wdown_hbm, lnf_ref, zero_ref,
                 y_ref, npool_ref, ns_ref, nconv_ref, win_out, wout_out, wup_out, wdown_out,
                 pbuf, st, abuf, x2s, win_ref, wout_ref, wup_ref, wdown_ref, out_sem, stage_sem, *stages,
                 tb, nt, nblk):
    g = pl.program_id(0)
    jm = jnp.minimum(g, nblk - 1) % nt
    jf = jnp.maximum(g - 1, 0) % nt

    weights_bf16 = (win_ref, wout_ref, wup_ref, wdown_ref)

    def weight_writeback(n):
        return pltpu.make_async_copy(weights_bf16[n], (win_out, wout_out, wup_out, wdown_out)[n], out_sem.at[n])

    @pl.when(g == 0)
    def _():
        _load_weights((win_hbm, wout_hbm, wup_hbm, wdown_hbm), weights_bf16, stages, stage_sem)
        for n in range(len(weights_bf16)):
            weight_writeback(n).start()

    @pl.when(g == nblk)
    def _():
        for n in range(len(weights_bf16)):
            weight_writeback(n).wait()

    @pl.when(jm == 0)
    def _():
        pbuf[pl.ds(0, POOL_PAD), :] = jnp.zeros((POOL_PAD, POOL_WIDTH), F32)
        st[...] = jnp.zeros(st.shape, F32)

    @pl.when(jf == 0)
    def _():
        abuf[...] = jnp.zeros(abuf.shape, F32)

    def input_projection():
        h = _rmsnorm(x_ref[0], ln1_ref[...]).astype(BF16)
        return _dot(h, win_ref[...])

    def mixer_half(proj):
        _prompt_mixer_half(proj, x_ref, mkt_ref, mvt_ref, poolw_ref, pscale_ref, lbl_ref, onorm_ref, wout_ref,
                           pbuf, st, x2s, jm, tb)

    def ffn_half(proj):
        x2 = x2s[...]
        h2 = _rmsnorm(x2, ln2_ref[...]).astype(BF16)
        ab = _dot(h2, wup_ref[...])
        a = ab[:, :D_FF]
        hist = abuf[...]
        sub = lax.broadcasted_iota(jnp.int32, (CONV_PAD, D_FF), 0)
        r1 = pltpu.roll(a, 1, 0)
        r2 = pltpu.roll(a, 2, 0)
        h1 = jnp.broadcast_to(hist[CONV_PAD - 1:CONV_PAD, :], (CONV_PAD, D_FF))
        h0 = jnp.broadcast_to(hist[CONV_PAD - 2:CONV_PAD - 1, :], (CONV_PAD, D_FF))
        a_m1 = jnp.concatenate([jnp.where(sub >= 1, r1[:CONV_PAD], h1), r1[CONV_PAD:]], axis=0)
        a_m2 = jnp.concatenate([jnp.where(sub >= 2, r2[:CONV_PAD], jnp.where(sub == 1, h1, h0)), r2[CONV_PAD:]],
                               axis=0)
        conv = cb_ref[...] + cw_ref[0] * a_m2 + cw_ref[1] * a_m1 + cw_ref[2] * a
        act = _gelu_tanh(conv) * ab[:, D_FF:]
        abuf[...] = a[tb - CONV_PAD:, :]
        if proj is not None:
            act = _after(act, proj, zero_ref[...])
        y_ref[0] = _rmsnorm(x2 + _dot(act.astype(BF16), wdown_ref[...]), lnf_ref[...])

    @pl.when(g == 0)
    def _():
        mixer_half(input_projection())

    @pl.when(jnp.logical_and(g > 0, g < nblk))
    def _():
        proj = input_projection()
        ffn_half(proj)
        mixer_half(proj)

    @pl.when(g == nblk)
    def _():
        ffn_half(None)

    @pl.when(jnp.logical_and(jm == nt - 1, g < nblk))
    def _():
        seq = g // nt
        for j in range(POOL_HIST):
            npool_ref[j, pl.ds(seq, 1), :] = pbuf[pl.ds(1 + j, 1), :]
        for hd in range(HGRN_HEADS):
            ns_ref[0, hd] = st[hd].T

    @pl.when(jnp.logical_and(jf == nt - 1, g >= 1))
    def _():
        nconv_ref[0] = abuf[pl.ds(CONV_PAD - (CONV_W - 1), CONV_W - 1), :]


def _prompt_mixer_half(proj, x_ref, mkt_ref, mvt_ref, poolw_ref, pscale_ref, lbl_ref, onorm_ref, wout_ref,
                       pbuf, st, x2s, jm, tb):
    x = x_ref[0]

    pbuf[pl.ds(POOL_PAD, tb), :] = proj[:, OFF_U:OFF_U + POOL_WIDTH]
    posf = (jm * tb + 1 + lax.broadcasted_iota(jnp.int32, (tb, 1), 0)).astype(F32)
    dm = _pool_means(lambda j, half: pbuf[pl.ds(POOL_PAD - j, tb), pl.ds(LANES * half, LANES)], posf, (tb, LANES))
    o_pool = _dot(dm.astype(BF16), poolw_ref[...]) * pscale_ref[...]
    pbuf[pl.ds(0, POOL_PAD), :] = pbuf[pl.ds(tb, POOL_PAD), :]

    qx = proj[:, OFF_X:OFF_X + XATTN_WIDTH] * (XATTN_DH ** -0.5)
    o_x = _cross_attention(qx, mkt_ref[0].astype(BF16), mvt_ref[0].astype(BF16))

    lb = _forget_lower_bound(lbl_ref[...])
    qf, k, log_f = _hgrn_gates(proj, lb)
    v = proj[:, OFF_I:OFF_I + HGRN_WIDTH]
    gg = proj[:, OFF_G:OFF_G + HGRN_WIDTH]
    gate = gg * jax.nn.sigmoid(gg) * onorm_ref[...]
    a_all = _segment_cumsum(log_f, CHUNK)
    states = [st[hd] for hd in range(HGRN_HEADS)]
    o_rows = []
    for c in range(tb // CHUNK):
        rs = slice(c * CHUNK, (c + 1) * CHUNK)
        qf_c, k_c, lf_c, v_c, a = qf[rs], k[rs], log_f[rs], v[rs], a_all[rs]
        a_end = a[CHUNK - 1:CHUNK, :]
        q_in = (qf_c * jnp.exp2(a)).astype(BF16)
        k_out = (k_c * jnp.exp2(a_end - a)).astype(BF16)
        decay = jnp.exp2(a_end)
        v_b = v_c.astype(BF16)
        qk = qf_c * k_c
        factors = _level_factors(a, qf_c, k_c, lf_c, CHUNK, CHUNK)
        heads = [slice(hd * HGRN_DK, (hd + 1) * HGRN_DK) for hd in range(HGRN_HEADS)]
        inter = [_dot_nt(q_in[:, sl], states[hd].astype(BF16)) for hd, sl in enumerate(heads)]
        update = [_dot_tn(v_b[:, sl], k_out[:, sl]) for sl in heads]
        scores = [_intra_scores(factors, hd, CHUNK, CHUNK).astype(BF16) for hd in range(HGRN_HEADS)]
        intra = [_dot(scores[hd], v_b[:, sl]) for hd, sl in enumerate(heads)]
        o_heads = []
        for hd, sl in enumerate(heads):
            o = intra[hd] + inter[hd] + jnp.sum(qk[:, sl], axis=-1, keepdims=True) * v_c[:, sl]
            states[hd] = states[hd] * decay[:, sl] + update[hd]
            o_heads.append(_head_norm_gate(o, gate[rs, sl]))
        o_rows.append(jnp.concatenate(o_heads, axis=-1))
    for hd in range(HGRN_HEADS):
        st[hd] = states[hd]
    o_hgrn = jnp.concatenate(o_rows, axis=0)

    mixed = jnp.concatenate([o_pool, o_hgrn, o_x], axis=-1).astype(BF16)
    x2s[...] = x + _dot(mixed, wout_ref[...])


def _prompt_call(x, mkt, mvt, mixer_w, ffn_w, tb):
    b, l, d = x.shape
    nt = l // tb
    nblk = b * nt
    ln1, w_in, pool_wbd, pool_scale, lb_logits, onorm, w_out = mixer_w
    ln2, w_up, conv_w, conv_b, w_down, lnf = ffn_w

    def mixer_blk(g):
        return jnp.minimum(g, nblk - 1)

    def ffn_blk(g):
        return jnp.maximum(g - 1, 0)

    x_spec = pl.BlockSpec((1, tb, d), lambda g: (mixer_blk(g) // nt, mixer_blk(g) % nt, 0))
    mem = pl.BlockSpec((1, XATTN_WIDTH, N_MEM), lambda g: (mixer_blk(g) // nt, 0, 0))
    y_spec = pl.BlockSpec((1, tb, d), lambda g: (ffn_blk(g) // nt, ffn_blk(g) % nt, 0))
    in_hbm = pl.BlockSpec(memory_space=pl.ANY)
    weights = (w_in, w_out, w_up, w_down)
    assert all(w.dtype == F32 and w.shape[0] % 16 == 0 and w.shape[1] % LANES == 0 for w in weights)
    widths = _stage_widths(weights)
    return pl.pallas_call(
        functools.partial(_prompt_body, tb=tb, nt=nt, nblk=nblk),
        grid=(nblk + 1,),
        in_specs=[x_spec, mem, mem,
                  _const_spec((1, d)), in_hbm, _const_spec((POOL_WIDTH, POOL_WIDTH)),
                  _const_spec((1, POOL_WIDTH)), _const_spec(lb_logits.shape), _const_spec((1, HGRN_WIDTH)),
                  in_hbm,
                  _const_spec((1, d)), in_hbm, _const_spec((CONV_W, 1, D_FF)),
                  _const_spec((1, D_FF)), in_hbm, _const_spec((1, d)), _const_spec((1, LANES))],
        out_specs=[y_spec,
                   pl.BlockSpec((POOL_HIST, b, POOL_WIDTH), lambda g: (0, 0, 0)),
                   pl.BlockSpec((1, HGRN_HEADS, HGRN_DK, HGRN_DV), lambda g: (mixer_blk(g) // nt, 0, 0, 0)),
                   pl.BlockSpec((1, CONV_W - 1, D_FF), lambda g: (ffn_blk(g) // nt, 0, 0))]
                  + [pl.BlockSpec(memory_space=pl.ANY) for _ in weights],
        out_shape=[jax.ShapeDtypeStruct((b, l, d), F32),
                   jax.ShapeDtypeStruct((POOL_HIST, b, POOL_WIDTH), F32),
                   jax.ShapeDtypeStruct((b, HGRN_HEADS, HGRN_DK, HGRN_DV), F32),
                   jax.ShapeDtypeStruct((b, CONV_W - 1, D_FF), F32)]
                  + [jax.ShapeDtypeStruct(w.shape, BF16) for w in weights],
        scratch_shapes=[pltpu.VMEM((POOL_PAD + tb, POOL_WIDTH), F32),
                        pltpu.VMEM((HGRN_HEADS, HGRN_DV, HGRN_DK), F32),
                        pltpu.VMEM((CONV_PAD, D_FF), F32),
                        pltpu.VMEM((tb, d), F32)]
                       + [pltpu.VMEM(w.shape, BF16) for w in weights]
                       + [pltpu.SemaphoreType.DMA((len(weights),)),
                          pltpu.SemaphoreType.DMA((len(widths), STAGE_SLOTS))]
                       + [pltpu.VMEM((STAGE_SLOTS, _stage_rows(c), c), F32) for c in widths],
        compiler_params=pltpu.CompilerParams(dimension_semantics=("arbitrary",),
                                             vmem_limit_bytes=VMEM_LIMIT_BYTES),
        name="prompt_layer",
    )(x, mkt, mvt, ln1, w_in, pool_wbd, pool_scale, lb_logits, onorm, w_out,
      ln2, w_up, conv_w, conv_b, w_down, lnf, jnp.zeros((1, LANES), jnp.int32))


def _sample_mixer_body(x_ref, hist_ref, s0_ref, mkt_ref, mvt_ref, ln1_ref, win_ref, poolw_ref, pscale_ref, lbl_ref,
                       onorm_ref, wout_ref, x2_ref, npool_ref, ns_ref, pbuf, ubuf, *, gs, sl_len):
    rows = gs * sl_len
    seqs = [slice(s * sl_len, (s + 1) * sl_len) for s in range(gs)]
    x = x_ref[...].reshape(rows, D_MODEL)
    h = _rmsnorm(x, ln1_ref[...]).astype(BF16)
    proj = _dot(h, win_ref[...])

    pbuf[pl.ds(1, POOL_HIST)] = hist_ref[...]
    halves = [pl.ds(LANES * half, LANES) for half in range(POOL_WIDTH // LANES)]
    for half, lanes in enumerate(halves):
        ubuf[half] = proj[:, OFF_U + LANES * half:OFF_U + LANES * (half + 1)]
    for t in range(sl_len):
        for half, lanes in enumerate(halves):
            pbuf[POOL_PAD + t, :, lanes] = ubuf[half, pl.ds(t, gs, stride=sl_len), :]
    posf = (PAST_LEN + 1 + lax.broadcasted_iota(jnp.int32, (sl_len, 1, 1), 0)).astype(F32)
    dm = _pool_means(lambda j, half: pbuf[pl.ds(POOL_PAD - j, sl_len), :, pl.ds(LANES * half, LANES)],
                     posf, (sl_len, gs, LANES))
    npool_ref[...] = pbuf[pl.ds(sl_len + 1, POOL_HIST)]
    for t in range(sl_len):
        for half in range(len(halves)):
            ubuf[half, pl.ds(t, gs, stride=sl_len), :] = dm[t][:, LANES * half:LANES * (half + 1)]
    dm_rows = jnp.concatenate([ubuf[half] for half in range(len(halves))], axis=-1)
    o_pool = _dot(dm_rows.astype(BF16), poolw_ref[...]) * pscale_ref[...]

    qx3 = (proj[:, OFF_X:OFF_X + XATTN_WIDTH] * (XATTN_DH ** -0.5)).reshape(gs, sl_len, XATTN_WIDTH)
    head_of_lane = lax.broadcasted_iota(jnp.int32, qx3.shape, 2) // XATTN_DH
    qs3 = jnp.concatenate([jnp.where(head_of_lane == hd, qx3, 0.0) for hd in range(XATTN_HEADS)],
                          axis=1).astype(BF16)
    hrows = XATTN_HEADS * sl_len
    scores = jnp.concatenate([_dot(qs3[s], mkt_ref[s].astype(BF16)) for s in range(gs)], axis=0)
    p = _softmax_rows(scores).astype(BF16)
    o4 = jnp.concatenate([_dot_nt(p[s * hrows:(s + 1) * hrows], mvt_ref[s].astype(BF16)) for s in range(gs)],
                         axis=0)
    o4 = o4.reshape(gs, XATTN_HEADS, sl_len, XATTN_WIDTH)
    o_x3 = jnp.zeros(qx3.shape, F32)
    for hd in range(XATTN_HEADS):
        o_x3 = jnp.where(head_of_lane == hd, o4[:, hd], o_x3)
    o_x = o_x3.reshape(rows, XATTN_WIDTH)

    lb = _forget_lower_bound(lbl_ref[...])
    qf, k, log_f = _hgrn_gates(proj, lb)
    v = proj[:, OFF_I:OFF_I + HGRN_WIDTH]
    gg = proj[:, OFF_G:OFF_G + HGRN_WIDTH]
    gate = gg * jax.nn.sigmoid(gg) * onorm_ref[...]
    a = _segment_cumsum(log_f, sl_len)
    a3 = a.reshape(gs, sl_len, HGRN_WIDTH)
    a_end = jnp.broadcast_to(a3[:, sl_len - 1:sl_len, :], a3.shape).reshape(rows, HGRN_WIDTH)
    q_in = (qf * jnp.exp2(a)).astype(BF16)
    k_out = (k * jnp.exp2(a_end - a)).astype(BF16)
    decay = jnp.exp2(a_end)
    v_b = v.astype(BF16)
    qk = qf * k
    factors = _level_factors(a, qf, k, log_f, rows, sl_len)
    heads = [slice(hd * HGRN_DK, (hd + 1) * HGRN_DK) for hd in range(HGRN_HEADS)]
    inter = [jnp.concatenate([_dot(q_in[r, sl], s0_ref[s, hd].astype(BF16)) for s, r in enumerate(seqs)], axis=0)
             for hd, sl in enumerate(heads)]
    o_heads = []
    for hd, sl in enumerate(heads):
        p_h = _intra_scores(factors, hd, rows, sl_len)
        o = _dot(p_h.astype(BF16), v_b[:, sl]) + inter[hd]
        o = o + jnp.sum(qk[:, sl], axis=-1, keepdims=True) * v[:, sl]
        o_heads.append(_head_norm_gate(o, gate[:, sl]))
    updates = [[_dot_tn(k_out[r, sl], v_b[r, sl]) for sl in heads] for r in seqs]
    for s, r in enumerate(seqs):
        for hd, sl in enumerate(heads):
            decay_cols = jnp.broadcast_to(decay[r, sl][sl_len - 1:sl_len, :], (HGRN_DV, HGRN_DK)).T
            ns_ref[s, hd] = decay_cols * s0_ref[s, hd] + updates[s][hd]

    mixed = jnp.concatenate([o_pool] + o_heads + [o_x], axis=-1).astype(BF16)
    x2_ref[...] = (x + _dot(mixed, wout_ref[...])).reshape(gs, sl_len, D_MODEL)


def _sample_mixer_call(x, hist, s0, mkt, mvt, mixer_w, gs):
    b, l, d = x.shape
    ln1, w_in, pool_wbd, pool_scale, lb_logits, onorm, w_out = mixer_w
    grid = (b // gs,)
    blk = pl.BlockSpec((gs, l, d), lambda i: (i, 0, 0))
    histb = pl.BlockSpec((POOL_HIST, gs, POOL_WIDTH), lambda i: (0, i, 0))
    sb = pl.BlockSpec((gs, HGRN_HEADS, HGRN_DK, HGRN_DV), lambda i: (i, 0, 0, 0))
    mem = pl.BlockSpec((gs, XATTN_WIDTH, N_MEM), lambda i: (i, 0, 0))
    return pl.pallas_call(
        functools.partial(_sample_mixer_body, gs=gs, sl_len=l),
        grid=grid,
        in_specs=[blk, histb, sb, mem, mem,
                  _const_spec((1, d)), _const_spec((d, D_IN)), _const_spec((POOL_WIDTH, POOL_WIDTH)),
                  _const_spec((1, POOL_WIDTH)), _const_spec(lb_logits.shape), _const_spec((1, HGRN_WIDTH)),
                  _const_spec((d, d))],
        out_specs=[blk, histb, sb],
        out_shape=[jax.ShapeDtypeStruct((b, l, d), F32),
                   jax.ShapeDtypeStruct((POOL_HIST, b, POOL_WIDTH), F32),
                   jax.ShapeDtypeStruct((b, HGRN_HEADS, HGRN_DK, HGRN_DV), F32)],
        scratch_shapes=[pltpu.VMEM((POOL_PAD + l, gs, POOL_WIDTH), F32),
                        pltpu.VMEM((POOL_WIDTH // LANES, gs * l, LANES), F32)],
        compiler_params=pltpu.CompilerParams(dimension_semantics=("arbitrary",),
                                             vmem_limit_bytes=VMEM_LIMIT_BYTES),
        name="sample_mixer",
    )(x, hist, s0, mkt, mvt, ln1, w_in, pool_wbd, pool_scale, lb_logits, onorm, w_out)


def _sample_ffn_body(x_ref, chist_ref, ln2_ref, wup_ref, cw_ref, cb_ref, wdown_ref, lnf_ref, y_ref, nconv_ref,
                     *, gs, sl_len):
    rows = gs * sl_len
    x = x_ref[...].reshape(rows, D_MODEL)
    h = _rmsnorm(x, ln2_ref[...]).astype(BF16)
    ab = _dot(h, wup_ref[...])
    a = ab[:, :D_FF]
    ridx = lax.broadcasted_iota(jnp.int32, (rows, D_FF), 0) % sl_len
    hist = chist_ref[...]
    h1 = jnp.broadcast_to(hist[:, 1:2, :], (gs, sl_len, D_FF)).reshape(rows, D_FF)
    h0 = jnp.broadcast_to(hist[:, 0:1, :], (gs, sl_len, D_FF)).reshape(rows, D_FF)
    a_m1 = jnp.where(ridx >= 1, pltpu.roll(a, 1, 0), h1)
    a_m2 = jnp.where(ridx >= 2, pltpu.roll(a, 2, 0), jnp.where(ridx == 1, h1, h0))
    conv = cb_ref[...] + cw_ref[0] * a_m2 + cw_ref[1] * a_m1 + cw_ref[2] * a
    act = _gelu_tanh(conv) * ab[:, D_FF:]
    nconv_ref[...] = a.reshape(gs, sl_len, D_FF)[:, sl_len - (CONV_W - 1):, :]
    y_ref[...] = _ffn_tail(x, act, wdown_ref, lnf_ref).reshape(gs, sl_len, D_MODEL)


def _sample_ffn_call(x, chist, ffn_w, gs):
    b, l, d = x.shape
    ln2, w_up, conv_w, conv_b, w_down, lnf = ffn_w
    blk = pl.BlockSpec((gs, l, d), lambda i: (i, 0, 0))
    cblk = pl.BlockSpec((gs, CONV_W - 1, D_FF), lambda i: (i, 0, 0))
    return pl.pallas_call(
        functools.partial(_sample_ffn_body, gs=gs, sl_len=l),
        grid=(b // gs,),
        in_specs=[blk, cblk, _const_spec((1, d)), _const_spec((d, 2 * D_FF)), _const_spec((CONV_W, 1, D_FF)),
                  _const_spec((1, D_FF)), _const_spec((D_FF, d)), _const_spec((1, d))],
        out_specs=[blk, cblk],
        out_shape=[jax.ShapeDtypeStruct((b, l, d), F32),
                   jax.ShapeDtypeStruct((b, CONV_W - 1, D_FF), F32)],
        compiler_params=pltpu.CompilerParams(dimension_semantics=("arbitrary",),
                                             vmem_limit_bytes=VMEM_LIMIT_BYTES),
        name="sample_ffn",
    )(x, chist, ln2, w_up, conv_w, conv_b, w_down, lnf)


def _block_diag(pool_w):
    n = pool_w.shape[0]
    same_group = jnp.eye(n, dtype=bool)[:, None, :, None]
    return jnp.where(same_group, pool_w[:, :, None, :], 0.0).reshape(n * POOL_GROUP, n * POOL_GROUP)


def _layer(x_prompt, x_sample, mem_prompt, state_pool, state_hgrn, state_conv, cache_mem_k, cache_mem_v,
           ln1_g, w_in, pool_w, pool_scale, hgrn_lb_logits, hgrn_onorm_g, mem_norm_g, w_mem_kv, w_out,
           ln2_g, w_up, conv_w, conv_b, w_down, lnf_g, *, prompt_tb, mixer_gs, ffn_gs):
    row = lambda a: a.reshape(1, -1)
    pool_wbd = _block_diag(pool_w).astype(BF16)

    def mixer_w(w_in, w_out):
        return (row(ln1_g), w_in, pool_wbd, row(pool_scale), hgrn_lb_logits, row(hgrn_onorm_g), w_out)

    def ffn_w(w_up, w_down):
        return (row(ln2_g), w_up, conv_w[:, None, :], row(conv_b), w_down, row(lnf_g))

    mkt, mvt = _memkv_call(mem_prompt, row(mem_norm_g), w_mem_kv)
    y_prompt, new_pool_pt, new_s_p, new_conv_p, w_in_b, w_out_b, w_up_b, w_down_b = _prompt_call(
        x_prompt, mkt, mvt, mixer_w(w_in, w_out), ffn_w(w_up, w_down), tb=prompt_tb)
    new_pool_p = new_pool_pt.transpose(1, 0, 2)

    nb = x_sample.shape[0]
    smkt = cache_mem_k.transpose(0, 2, 3, 1).reshape(nb, XATTN_WIDTH, N_MEM)
    smvt = cache_mem_v.transpose(0, 2, 3, 1).reshape(nb, XATTN_WIDTH, N_MEM)
    xs, new_pool_st, new_s_s = _sample_mixer_call(x_sample, state_pool.transpose(1, 0, 2), state_hgrn, smkt, smvt,
                                                  mixer_w(w_in_b, w_out_b), gs=mixer_gs)
    new_pool_s = new_pool_st.transpose(1, 0, 2)
    y_sample, new_conv_s = _sample_ffn_call(xs, state_conv, ffn_w(w_up_b, w_down_b), gs=ffn_gs)
    bp = x_prompt.shape[0]
    mk, mv = (t.reshape(bp, XATTN_HEADS, XATTN_DH, N_MEM).transpose(0, 3, 1, 2) for t in (mkt, mvt))
    return (y_prompt, y_sample, new_pool_p, new_s_p, new_conv_p, mk, mv, new_pool_s, new_s_s, new_conv_s)


def kernel(x_prompt, x_sample, mem_prompt, state_pool, state_hgrn, state_conv, cache_mem_k, cache_mem_v,
           ln1_g, w_in, pool_w, pool_scale, hgrn_lb_logits, hgrn_onorm_g, mem_norm_g, w_mem_kv, w_out,
           ln2_g, w_up, conv_w, conv_b, w_down, lnf_g):
    assert w_in.shape[0] == 1, "one layer"
    outs = _layer(x_prompt, x_sample, mem_prompt, state_pool[0], state_hgrn[0], state_conv[0],
                  cache_mem_k[0], cache_mem_v[0], ln1_g[0], w_in[0], pool_w[0], pool_scale[0], hgrn_lb_logits,
                  hgrn_onorm_g[0], mem_norm_g[0], w_mem_kv[0], w_out[0], ln2_g[0], w_up[0], conv_w[0], conv_b[0],
                  w_down[0], lnf_g, prompt_tb=256, mixer_gs=16, ffn_gs=32)
    y_prompt, y_sample = outs[0], outs[1]
    return (y_prompt, y_sample) + tuple(o[None] for o in outs[2:])
```

```python
import functools

import jax
import jax.numpy as jnp
from jax import lax
from jax.experimental import pallas as pl
from jax.experimental.pallas import tpu as pltpu

F32 = jnp.float32
BF16 = jnp.bfloat16

D_MODEL = 1024
POOL_WIDTH = 256
POOL_GROUP = 64
POOL_HIST = 15
HGRN_WIDTH = 512
HGRN_HEADS = 4
HGRN_DK = 128
HGRN_DV = 128
XATTN_WIDTH = 256
XATTN_HEADS = 4
XATTN_DH = 64
N_MEM = 256
D_FF = 2816
CONV_W = 3
EPS = 1e-6
PAST_LEN = 16384
D_IN = POOL_WIDTH + 4 * HGRN_WIDTH + XATTN_WIDTH
OFF_U, OFF_Q, OFF_F, OFF_I, OFF_G, OFF_X = 0, 256, 768, 1280, 1792, 2304

CHUNK = 64
POOL_PAD = 16
CONV_PAD = 8
LANES = 128
MEMKV_GROUP = 4
STAGE_BYTES = 720896
STAGE_SLOTS = 6
VMEM_LIMIT_BYTES = 56 * 1024 * 1024

_NT = (((1,), (1,)), ((), ()))
_TN = (((0,), (0,)), ((), ()))


def _dot(a, b):
    return jnp.dot(a, b, preferred_element_type=F32)


def _dot_nt(a, b):
    return lax.dot_general(a, b, _NT, preferred_element_type=F32)


def _dot_tn(a, b):
    return lax.dot_general(a, b, _TN, preferred_element_type=F32)


def _rmsnorm(x, g):
    return x * lax.rsqrt(jnp.mean(x * x, axis=-1, keepdims=True) + EPS) * g


def _const_spec(shape):
    nd = len(shape)
    return pl.BlockSpec(shape, lambda *_: (0,) * nd, pipeline_mode=pl.Buffered(1))


def _memkv_body(mem_ref, g_ref, w_ref, kt_ref, vt_ref):
    nb = mem_ref.shape[0]
    h = _rmsnorm(mem_ref[...].reshape(nb * N_MEM, D_MODEL), g_ref[...]).astype(BF16)
    kv = _dot(h, w_ref[...].astype(BF16))
    for s in range(nb):
        kvt = kv[s * N_MEM:(s + 1) * N_MEM, :].T
        kt_ref[s] = kvt[:XATTN_WIDTH, :]
        vt_ref[s] = kvt[XATTN_WIDTH:, :]


def _memkv_call(mem, g, w):
    b = mem.shape[0]
    nb = MEMKV_GROUP
    out = jax.ShapeDtypeStruct((b, XATTN_WIDTH, N_MEM), F32)
    return pl.pallas_call(
        _memkv_body,
        grid=(b // nb,),
        in_specs=[pl.BlockSpec((nb, N_MEM, D_MODEL), lambda i: (i, 0, 0)),
                  _const_spec((1, D_MODEL)),
                  _const_spec((D_MODEL, 2 * XATTN_WIDTH))],
        out_specs=[pl.BlockSpec((nb, XATTN_WIDTH, N_MEM), lambda i: (i, 0, 0))] * 2,
        out_shape=[out, out],
        compiler_params=pltpu.CompilerParams(dimension_semantics=("arbitrary",)),
        name="memkv",
    )(mem, g, w)


def _forget_lower_bound(logits):
    z = logits - jnp.max(logits, axis=0, keepdims=True)
    e = jnp.exp(z)
    return e[0:1, :] / jnp.sum(e, axis=0, keepdims=True)


def _hgrn_gates(proj, lb):
    fp = proj[:, OFF_F:OFF_F + HGRN_WIDTH]
    q = proj[:, OFF_Q:OFF_Q + HGRN_WIDTH]
    log_f = jnp.log2(lb + (1.0 - lb) * jax.nn.sigmoid(fp))
    k = (1.0 - lb) * jax.nn.sigmoid(-fp)
    qf = q * jax.nn.sigmoid(q)
    return qf, k, log_f


def _segment_cumsum(x, seq):
    ridx = lax.broadcasted_iota(jnp.int32, x.shape, 0) & (seq - 1)
    sh = 1
    while sh < seq:
        x = x + jnp.where(ridx >= sh, pltpu.roll(x, sh, 0), 0.0)
        sh *= 2
    return x


def _level_factor(a, qf, k, log_f, m, rows):
    n = a.shape[1]
    ridx = lax.broadcasted_iota(jnp.int32, (rows, n), 0)
    upper = (ridx & m) != 0
    if m == 1:
        d = jnp.where(upper, log_f, 0.0)
    else:
        if (2 * m) % 8 == 0:
            nb = rows // (2 * m)
            a3 = a.reshape(nb, 2 * m, n)
            ref = jnp.broadcast_to(a3[:, m - 1:m, :], (nb, 2 * m, n)).reshape(rows, n)
        else:
            a3 = a.reshape(rows // 8, 8, n)
            sub = lax.broadcasted_iota(jnp.int32, a3.shape, 1)
            ref = jnp.where(sub < 4,
                            jnp.broadcast_to(a3[:, 1:2, :], a3.shape),
                            jnp.broadcast_to(a3[:, 5:6, :], a3.shape)).reshape(rows, n)
        d = -jnp.abs(a - ref)
    return (jnp.where(upper, qf, k) * jnp.exp2(d)).astype(BF16)


def _level_factors(a, qf, k, log_f, rows, seq):
    out, m = [], seq // 2
    while m >= 1:
        out.append((m, _level_factor(a, qf, k, log_f, m, rows)))
        m //= 2
    return out


def _intra_scores(factors, head, rows, seq):
    sl = slice(head * HGRN_DK, (head + 1) * HGRN_DK)
    t = lax.broadcasted_iota(jnp.int32, (rows, rows), 0)
    s = lax.broadcasted_iota(jnp.int32, (rows, rows), 1)
    x = t ^ s
    products = [(m, _dot_nt(y[:, sl], y[:, sl])) for m, y in reversed(factors)]
    total = jnp.zeros((rows, rows), F32)
    for m, p in products:
        total = jnp.where(x >= m, p, total)
    return jnp.where((t > s) & (x < seq), total, 0.0)


def _head_norm_gate(o, gate):
    return o * lax.rsqrt(jnp.mean(o * o, axis=-1, keepdims=True) + EPS) * gate


def _softmax_rows(s):
    e = jnp.exp(s - jnp.max(s, axis=-1, keepdims=True))
    return e / jnp.sum(e, axis=-1, keepdims=True)


def _cross_attention(qx, mkt, mvt):
    rows = qx.shape[0]
    head_of_lane = lax.broadcasted_iota(jnp.int32, qx.shape, 1) // XATTN_DH
    qs = jnp.concatenate([jnp.where(head_of_lane == h, qx, 0.0) for h in range(XATTN_HEADS)], axis=0)
    p = _softmax_rows(_dot(qs.astype(BF16), mkt))
    o = _dot_nt(p.astype(BF16), mvt)
    out = jnp.zeros(qx.shape, F32)
    for h in range(XATTN_HEADS):
        out = jnp.where(head_of_lane == h, o[h * rows:(h + 1) * rows, :], out)
    return out


def _pool_means(ld, posf, shape):
    lane = lax.broadcasted_iota(jnp.int32, shape, len(shape) - 1)
    first = lane < POOL_GROUP
    u_lo, u_hi = ld(0, 0), ld(0, 1)
    t2 = u_lo + ld(1, 0)
    t4 = t2 + ld(2, 0) + ld(3, 0)
    t8 = u_hi
    for j in range(1, 8):
        t8 = t8 + ld(j, 1)
    t16 = t8
    for j in range(8, 16):
        t16 = t16 + ld(j, 1)
    cnt_lo = jnp.where(first, jnp.minimum(2.0, posf), jnp.minimum(4.0, posf))
    cnt_hi = jnp.where(first, jnp.minimum(8.0, posf), jnp.minimum(16.0, posf))
    lo = jnp.where(first, t2, t4) / cnt_lo - u_lo
    hi = jnp.where(first, t8, t16) / cnt_hi - u_hi
    return jnp.concatenate([lo, hi], axis=-1)


def _after(x, anchor, zero):
    r, n = anchor.shape
    s = jnp.sum(anchor.reshape(r // 8, 8, n), axis=0)
    c = s[:, :LANES]
    for i in range(1, n // LANES):
        c = c + s[:, i * LANES:(i + 1) * LANES]
    z = lax.bitcast_convert_type(lax.bitcast_convert_type(c[0:1, :], jnp.int32) & zero, F32)
    return jnp.concatenate([x[:, :LANES] + z, x[:, LANES:]], axis=1)


def _gelu_tanh(x):
    c = 0.7978845608028654
    half_x = 0.5 * x
    return half_x + half_x * jnp.tanh(x * (c + (0.044715 * c) * (x * x)))


def _ffn_tail(x, act, wdown_ref, lnf_ref):
    y = x + _dot(act.astype(BF16), wdown_ref[...])
    return _rmsnorm(y, lnf_ref[...])


def _stage_widths(weights):
    return list(dict.fromkeys(w.shape[1] for w in weights))


def _stage_rows(cols):
    return STAGE_BYTES // (4 * cols) // 16 * 16


def _load_weights(srcs, dsts, stages, sem):
    widths = _stage_widths(srcs)
    blocks, used = [], [0] * len(stages)
    for i, w in enumerate(srcs):
        f = widths.index(w.shape[1])
        rows = stages[f].shape[1]
        for r0 in range(0, w.shape[0], rows):
            blocks.append((i, r0, min(rows, w.shape[0] - r0), f, used[f] % STAGE_SLOTS))
            used[f] += 1

    def copy(n):
        i, r0, nr, f, slot = blocks[n]
        return pltpu.make_async_copy(srcs[i].at[pl.ds(r0, nr), :], stages[f].at[slot, pl.ds(0, nr), :],
                                     sem.at[f, slot])

    ahead = STAGE_SLOTS - 1
    for n in range(min(ahead, len(blocks))):
        copy(n).start()
    for n, (i, r0, nr, f, slot) in enumerate(blocks):
        if n + ahead < len(blocks):
            copy(n + ahead).start()
        copy(n).wait()
        dsts[i][pl.ds(r0, nr), :] = stages[f][slot, pl.ds(0, nr), :].astype(BF16)


def _prompt_body(x_ref, mkt_ref, mvt_ref, ln1_ref, win_hbm, poolw_ref, pscale_ref, lbl_ref, onorm_ref,
                 wout_hbm, ln2_ref, wup_hbm, cw_ref, cb_ref, wdown_hbm, lnf_ref, zero_ref,
                 y_ref, npool_ref, ns_ref, nconv_ref, win_out, wout_out, wup_out, wdown_out,
                 pbuf, st, abuf, x2s, win_ref, wout_ref, wup_ref, wdown_ref, out_sem, stage_sem, *stages,
                 tb, nt, nblk):
    g = pl.program_id(0)
    jm = jnp.minimum(g, nblk - 1) % nt
    jf = jnp.maximum(g - 1, 0) % nt

    weights_bf16 = (win_ref, wout_ref, wup_ref, wdown_ref)

    def weight_writeback(n):
        return pltpu.make_async_copy(weights_bf16[n], (win_out, wout_out, wup_out, wdown_out)[n], out_sem.at[n])

    @pl.when(g == 0)
    def _():
        _load_weights((win_hbm, wout_hbm, wup_hbm, wdown_hbm), weights_bf16, stages, stage_sem)
        for n in range(len(weights_bf16)):
            weight_writeback(n).start()

    @pl.when(g == nblk)
    def _():
        for n in range(len(weights_bf16)):
            weight_writeback(n).wait()

    @pl.when(jm == 0)
    def _():
        pbuf[pl.ds(0, POOL_PAD), :] = jnp.zeros((POOL_PAD, POOL_WIDTH), F32)
        st[...] = jnp.zeros(st.shape, F32)

    @pl.when(jf == 0)
    def _():
        abuf[...] = jnp.zeros(abuf.shape, F32)

    def input_projection():
        h = _rmsnorm(x_ref[0], ln1_ref[...]).astype(BF16)
        return _dot(h, win_ref[...])

    def mixer_half(proj):
        _prompt_mixer_half(proj, x_ref, mkt_ref, mvt_ref, poolw_ref, pscale_ref, lbl_ref, onorm_ref, wout_ref,
                           pbuf, st, x2s, jm, tb)

    def ffn_half(proj):
        x2 = x2s[...]
        h2 = _rmsnorm(x2, ln2_ref[...]).astype(BF16)
        ab = _dot(h2, wup_ref[...])
        a = ab[:, :D_FF]
        hist = abuf[...]
        sub = lax.broadcasted_iota(jnp.int32, (CONV_PAD, D_FF), 0)
        r1 = pltpu.roll(a, 1, 0)
        r2 = pltpu.roll(a, 2, 0)
        h1 = jnp.broadcast_to(hist[CONV_PAD - 1:CONV_PAD, :], (CONV_PAD, D_FF))
        h0 = jnp.broadcast_to(hist[CONV_PAD - 2:CONV_PAD - 1, :], (CONV_PAD, D_FF))
        a_m1 = jnp.concatenate([jnp.where(sub >= 1, r1[:CONV_PAD], h1), r1[CONV_PAD:]], axis=0)
        a_m2 = jnp.concatenate([jnp.where(sub >= 2, r2[:CONV_PAD], jnp.where(sub == 1, h1, h0)), r2[CONV_PAD:]],
                               axis=0)
        conv = cb_ref[...] + cw_ref[0] * a_m2 + cw_ref[1] * a_m1 + cw_ref[2] * a
        act = _gelu_tanh(conv) * ab[:, D_FF:]
        abuf[...] = a[tb - CONV_PAD:, :]
        if proj is not None:
            act = _after(act, proj, zero_ref[...])
        y_ref[0] = _rmsnorm(x2 + _dot(act.astype(BF16), wdown_ref[...]), lnf_ref[...])

    @pl.when(g > 0)
    def _():
        ffn_half(None)

    @pl.when(g < nblk)
    def _():
        mixer_half(input_projection())

    @pl.when(jnp.logical_and(jm == nt - 1, g < nblk))
    def _():
        seq = g // nt
        for j in range(POOL_HIST):
            npool_ref[j, pl.ds(seq, 1), :] = pbuf[pl.ds(1 + j, 1), :]
        for hd in range(HGRN_HEADS):
            ns_ref[0, hd] = st[hd].T

    @pl.when(jnp.logical_and(jf == nt - 1, g >= 1))
    def _():
        nconv_ref[0] = abuf[pl.ds(CONV_PAD - (CONV_W - 1), CONV_W - 1), :]


def _prompt_mixer_half(proj, x_ref, mkt_ref, mvt_ref, poolw_ref, pscale_ref, lbl_ref, onorm_ref, wout_ref,
                       pbuf, st, x2s, jm, tb):
    x = x_ref[0]

    pbuf[pl.ds(POOL_PAD, tb), :] = proj[:, OFF_U:OFF_U + POOL_WIDTH]
    posf = (jm * tb + 1 + lax.broadcasted_iota(jnp.int32, (tb, 1), 0)).astype(F32)
    dm = _pool_means(lambda j, half: pbuf[pl.ds(POOL_PAD - j, tb), pl.ds(LANES * half, LANES)], posf, (tb, LANES))
    o_pool = _dot(dm.astype(BF16), poolw_ref[...]) * pscale_ref[...]
    pbuf[pl.ds(0, POOL_PAD), :] = pbuf[pl.ds(tb, POOL_PAD), :]

    qx = proj[:, OFF_X:OFF_X + XATTN_WIDTH] * (XATTN_DH ** -0.5)
    o_x = _cross_attention(qx, mkt_ref[0].astype(BF16), mvt_ref[0].astype(BF16))

    lb = _forget_lower_bound(lbl_ref[...])
    qf, k, log_f = _hgrn_gates(proj, lb)
    v = proj[:, OFF_I:OFF_I + HGRN_WIDTH]
    gg = proj[:, OFF_G:OFF_G + HGRN_WIDTH]
    gate = gg * jax.nn.sigmoid(gg) * onorm_ref[...]
    a_all = _segment_cumsum(log_f, CHUNK)
    states = [st[hd] for hd in range(HGRN_HEADS)]
    o_rows = []
    for c in range(tb // CHUNK):
        rs = slice(c * CHUNK, (c + 1) * CHUNK)
        qf_c, k_c, lf_c, v_c, a = qf[rs], k[rs], log_f[rs], v[rs], a_all[rs]
        a_end = a[CHUNK - 1:CHUNK, :]
        q_in = (qf_c * jnp.exp2(a)).astype(BF16)
        k_out = (k_c * jnp.exp2(a_end - a)).astype(BF16)
        decay = jnp.exp2(a_end)
        v_b = v_c.astype(BF16)
        qk = qf_c * k_c
        factors = _level_factors(a, qf_c, k_c, lf_c, CHUNK, CHUNK)
        heads = [slice(hd * HGRN_DK, (hd + 1) * HGRN_DK) for hd in range(HGRN_HEADS)]
        inter = [_dot_nt(q_in[:, sl], states[hd].astype(BF16)) for hd, sl in enumerate(heads)]
        update = [_dot_tn(v_b[:, sl], k_out[:, sl]) for sl in heads]
        scores = [_intra_scores(factors, hd, CHUNK, CHUNK).astype(BF16) for hd in range(HGRN_HEADS)]
        intra = [_dot(scores[hd], v_b[:, sl]) for hd, sl in enumerate(heads)]
        o_heads = []
        for hd, sl in enumerate(heads):
            o = intra[hd] + inter[hd] + jnp.sum(qk[:, sl], axis=-1, keepdims=True) * v_c[:, sl]
            states[hd] = states[hd] * decay[:, sl] + update[hd]
            o_heads.append(_head_norm_gate(o, gate[rs, sl]))
        o_rows.append(jnp.concatenate(o_heads, axis=-1))
    for hd in range(HGRN_HEADS):
        st[hd] = states[hd]
    o_hgrn = jnp.concatenate(o_rows, axis=0)

    mixed = jnp.concatenate([o_pool, o_hgrn, o_x], axis=-1).astype(BF16)
    x2s[...] = x + _dot(mixed, wout_ref[...])


def _prompt_call(x, mkt, mvt, mixer_w, ffn_w, tb):
    b, l, d = x.shape
    nt = l // tb
    nblk = b * nt
    ln1, w_in, pool_wbd, pool_scale, lb_logits, onorm, w_out = mixer_w
    ln2, w_up, conv_w, conv_b, w_down, lnf = ffn_w

    def mixer_blk(g):
        return jnp.minimum(g, nblk - 1)

    def ffn_blk(g):
        return jnp.maximum(g - 1, 0)

    x_spec = pl.BlockSpec((1, tb, d), lambda g: (mixer_blk(g) // nt, mixer_blk(g) % nt, 0))
    mem = pl.BlockSpec((1, XATTN_WIDTH, N_MEM), lambda g: (mixer_blk(g) // nt, 0, 0))
    y_spec = pl.BlockSpec((1, tb, d), lambda g: (ffn_blk(g) // nt, ffn_blk(g) % nt, 0))
    in_hbm = pl.BlockSpec(memory_space=pl.ANY)
    weights = (w_in, w_out, w_up, w_down)
    assert all(w.dtype == F32 and w.shape[0] % 16 == 0 and w.shape[1] % LANES == 0 for w in weights)
    widths = _stage_widths(weights)
    return pl.pallas_call(
        functools.partial(_prompt_body, tb=tb, nt=nt, nblk=nblk),
        grid=(nblk + 1,),
        in_specs=[x_spec, mem, mem,
                  _const_spec((1, d)), in_hbm, _const_spec((POOL_WIDTH, POOL_WIDTH)),
                  _const_spec((1, POOL_WIDTH)), _const_spec(lb_logits.shape), _const_spec((1, HGRN_WIDTH)),
                  in_hbm,
                  _const_spec((1, d)), in_hbm, _const_spec((CONV_W, 1, D_FF)),
                  _const_spec((1, D_FF)), in_hbm, _const_spec((1, d)), _const_spec((1, LANES))],
        out_specs=[y_spec,
                   pl.BlockSpec((POOL_HIST, b, POOL_WIDTH), lambda g: (0, 0, 0)),
                   pl.BlockSpec((1, HGRN_HEADS, HGRN_DK, HGRN_DV), lambda g: (mixer_blk(g) // nt, 0, 0, 0)),
                   pl.BlockSpec((1, CONV_W - 1, D_FF), lambda g: (ffn_blk(g) // nt, 0, 0))]
                  + [pl.BlockSpec(memory_space=pl.ANY) for _ in weights],
        out_shape=[jax.ShapeDtypeStruct((b, l, d), F32),
                   jax.ShapeDtypeStruct((POOL_HIST, b, POOL_WIDTH), F32),
                   jax.ShapeDtypeStruct((b, HGRN_HEADS, HGRN_DK, HGRN_DV), F32),
                   jax.ShapeDtypeStruct((b, CONV_W - 1, D_FF), F32)]
                  + [jax.ShapeDtypeStruct(w.shape, BF16) for w in weights],
        scratch_shapes=[pltpu.VMEM((POOL_PAD + tb, POOL_WIDTH), F32),
                        pltpu.VMEM((HGRN_HEADS, HGRN_DV, HGRN_DK), F32),
                        pltpu.VMEM((CONV_PAD, D_FF), F32),
                        pltpu.VMEM((tb, d), F32)]
                       + [pltpu.VMEM(w.shape, BF16) for w in weights]
                       + [pltpu.SemaphoreType.DMA((len(weights),)),
                          pltpu.SemaphoreType.DMA((len(widths), STAGE_SLOTS))]
                       + [pltpu.VMEM((STAGE_SLOTS, _stage_rows(c), c), F32) for c in widths],
        compiler_params=pltpu.CompilerParams(dimension_semantics=("arbitrary",),
                                             vmem_limit_bytes=VMEM_LIMIT_BYTES),
        name="prompt_layer",
    )(x, mkt, mvt, ln1, w_in, pool_wbd, pool_scale, lb_logits, onorm, w_out,
      ln2, w_up, conv_w, conv_b, w_down, lnf, jnp.zeros((1, LANES), jnp.int32))


def _sample_mixer_body(x_ref, hist_ref, s0_ref, mkt_ref, mvt_ref, ln1_ref, win_ref, poolw_ref, pscale_ref, lbl_ref,
                       onorm_ref, wout_ref, x2_ref, npool_ref, ns_ref, pbuf, ubuf, *, gs, sl_len):
    rows = gs * sl_len
    seqs = [slice(s * sl_len, (s + 1) * sl_len) for s in range(gs)]
    x = x_ref[...].reshape(rows, D_MODEL)
    h = _rmsnorm(x, ln1_ref[...]).astype(BF16)
    proj = _dot(h, win_ref[...])

    pbuf[pl.ds(1, POOL_HIST)] = hist_ref[...]
    halves = [pl.ds(LANES * half, LANES) for half in range(POOL_WIDTH // LANES)]
    for half, lanes in enumerate(halves):
        ubuf[half] = proj[:, OFF_U + LANES * half:OFF_U + LANES * (half + 1)]
    for t in range(sl_len):
        for half, lanes in enumerate(halves):
            pbuf[POOL_PAD + t, :, lanes] = ubuf[half, pl.ds(t, gs, stride=sl_len), :]
    posf = (PAST_LEN + 1 + lax.broadcasted_iota(jnp.int32, (sl_len, 1, 1), 0)).astype(F32)
    dm = _pool_means(lambda j, half: pbuf[pl.ds(POOL_PAD - j, sl_len), :, pl.ds(LANES * half, LANES)],
                     posf, (sl_len, gs, LANES))
    npool_ref[...] = pbuf[pl.ds(sl_len + 1, POOL_HIST)]
    for t in range(sl_len):
        for half in range(len(halves)):
            ubuf[half, pl.ds(t, gs, stride=sl_len), :] = dm[t][:, LANES * half:LANES * (half + 1)]
    dm_rows = jnp.concatenate([ubuf[half] for half in range(len(halves))], axis=-1)
    o_pool = _dot(dm_rows.astype(BF16), poolw_ref[...]) * pscale_ref[...]

    qx3 = (proj[:, OFF_X:OFF_X + XATTN_WIDTH] * (XATTN_DH ** -0.5)).reshape(gs, sl_len, XATTN_WIDTH)
    head_of_lane = lax.broadcasted_iota(jnp.int32, qx3.shape, 2) // XATTN_DH
    qs3 = jnp.concatenate([jnp.where(head_of_lane == hd, qx3, 0.0) for hd in range(XATTN_HEADS)],
                          axis=1).astype(BF16)
    hrows = XATTN_HEADS * sl_len
    scores = jnp.concatenate([_dot(qs3[s], mkt_ref[s].astype(BF16)) for s in range(gs)], axis=0)
    p = _softmax_rows(scores).astype(BF16)
    o4 = jnp.concatenate([_dot_nt(p[s * hrows:(s + 1) * hrows], mvt_ref[s].astype(BF16)) for s in range(gs)],
                         axis=0)
    o4 = o4.reshape(gs, XATTN_HEADS, sl_len, XATTN_WIDTH)
    o_x3 = jnp.zeros(qx3.shape, F32)
    for hd in range(XATTN_HEADS):
        o_x3 = jnp.where(head_of_lane == hd, o4[:, hd], o_x3)
    o_x = o_x3.reshape(rows, XATTN_WIDTH)

    lb = _forget_lower_bound(lbl_ref[...])
    qf, k, log_f = _hgrn_gates(proj, lb)
    v = proj[:, OFF_I:OFF_I + HGRN_WIDTH]
    gg = proj[:, OFF_G:OFF_G + HGRN_WIDTH]
    gate = gg * jax.nn.sigmoid(gg) * onorm_ref[...]
    a = _segment_cumsum(log_f, sl_len)
    a3 = a.reshape(gs, sl_len, HGRN_WIDTH)
    a_end = jnp.broadcast_to(a3[:, sl_len - 1:sl_len, :], a3.shape).reshape(rows, HGRN_WIDTH)
    q_in = (qf * jnp.exp2(a)).astype(BF16)
    k_out = (k * jnp.exp2(a_end - a)).astype(BF16)
    decay = jnp.exp2(a_end)
    v_b = v.astype(BF16)
    qk = qf * k
    factors = _level_factors(a, qf, k, log_f, rows, sl_len)
    heads = [slice(hd * HGRN_DK, (hd + 1) * HGRN_DK) for hd in range(HGRN_HEADS)]
    inter = [jnp.concatenate([_dot(q_in[r, sl], s0_ref[s, hd].astype(BF16)) for s, r in enumerate(seqs)], axis=0)
             for hd, sl in enumerate(heads)]
    o_heads = []
    for hd, sl in enumerate(heads):
        p_h = _intra_scores(factors, hd, rows, sl_len)
        o = _dot(p_h.astype(BF16), v_b[:, sl]) + inter[hd]
        o = o + jnp.sum(qk[:, sl], axis=-1, keepdims=True) * v[:, sl]
        o_heads.append(_head_norm_gate(o, gate[:, sl]))
    updates = [[_dot_tn(k_out[r, sl], v_b[r, sl]) for sl in heads] for r in seqs]
    for s, r in enumerate(seqs):
        for hd, sl in enumerate(heads):
            decay_cols = jnp.broadcast_to(decay[r, sl][sl_len - 1:sl_len, :], (HGRN_DV, HGRN_DK)).T
            ns_ref[s, hd] = decay_cols * s0_ref[s, hd] + updates[s][hd]

    mixed = jnp.concatenate([o_pool] + o_heads + [o_x], axis=-1).astype(BF16)
    x2_ref[...] = (x + _dot(mixed, wout_ref[...])).reshape(gs, sl_len, D_MODEL)


def _sample_mixer_call(x, hist, s0, mkt, mvt, mixer_w, gs):
    b, l, d = x.shape
    ln1, w_in, pool_wbd, pool_scale, lb_logits, onorm, w_out = mixer_w
    grid = (b // gs,)
    blk = pl.BlockSpec((gs, l, d), lambda i: (i, 0, 0))
    histb = pl.BlockSpec((POOL_HIST, gs, POOL_WIDTH), lambda i: (0, i, 0))
    sb = pl.BlockSpec((gs, HGRN_HEADS, HGRN_DK, HGRN_DV), lambda i: (i, 0, 0, 0))
    mem = pl.BlockSpec((gs, XATTN_WIDTH, N_MEM), lambda i: (i, 0, 0))
    return pl.pallas_call(
        functools.partial(_sample_mixer_body, gs=gs, sl_len=l),
        grid=grid,
        in_specs=[blk, histb, sb, mem, mem,
                  _const_spec((1, d)), _const_spec((d, D_IN)), _const_spec((POOL_WIDTH, POOL_WIDTH)),
                  _const_spec((1, POOL_WIDTH)), _const_spec(lb_logits.shape), _const_spec((1, HGRN_WIDTH)),
                  _const_spec((d, d))],
        out_specs=[blk, histb, sb],
        out_shape=[jax.ShapeDtypeStruct((b, l, d), F32),
                   jax.ShapeDtypeStruct((POOL_HIST, b, POOL_WIDTH), F32),
                   jax.ShapeDtypeStruct((b, HGRN_HEADS, HGRN_DK, HGRN_DV), F32)],
        scratch_shapes=[pltpu.VMEM((POOL_PAD + l, gs, POOL_WIDTH), F32),
                        pltpu.VMEM((POOL_WIDTH // LANES, gs * l, LANES), F32)],
        compiler_params=pltpu.CompilerParams(dimension_semantics=("arbitrary",),
                                             vmem_limit_bytes=VMEM_LIMIT_BYTES),
        name="sample_mixer",
    )(x, hist, s0, mkt, mvt, ln1, w_in, pool_wbd, pool_scale, lb_logits, onorm, w_out)


def _sample_ffn_body(x_ref, chist_ref, ln2_ref, wup_ref, cw_ref, cb_ref, wdown_ref, lnf_ref, y_ref, nconv_ref,
                     *, gs, sl_len):
    rows = gs * sl_len
    x = x_ref[...].reshape(rows, D_MODEL)
    h = _rmsnorm(x, ln2_ref[...]).astype(BF16)
    ab = _dot(h, wup_ref[...])
    a = ab[:, :D_FF]
    ridx = lax.broadcasted_iota(jnp.int32, (rows, D_FF), 0) % sl_len
    hist = chist_ref[...]
    h1 = jnp.broadcast_to(hist[:, 1:2, :], (gs, sl_len, D_FF)).reshape(rows, D_FF)
    h0 = jnp.broadcast_to(hist[:, 0:1, :], (gs, sl_len, D_FF)).reshape(rows, D_FF)
    a_m1 = jnp.where(ridx >= 1, pltpu.roll(a, 1, 0), h1)
    a_m2 = jnp.where(ridx >= 2, pltpu.roll(a, 2, 0), jnp.where(ridx == 1, h1, h0))
    conv = cb_ref[...] + cw_ref[0] * a_m2 + cw_ref[1] * a_m1 + cw_ref[2] * a
    act = _gelu_tanh(conv) * ab[:, D_FF:]
    nconv_ref[...] = a.reshape(gs, sl_len, D_FF)[:, sl_len - (CONV_W - 1):, :]
    y_ref[...] = _ffn_tail(x, act, wdown_ref, lnf_ref).reshape(gs, sl_len, D_MODEL)


def _sample_ffn_call(x, chist, ffn_w, gs):
    b, l, d = x.shape
    ln2, w_up, conv_w, conv_b, w_down, lnf = ffn_w
    blk = pl.BlockSpec((gs, l, d), lambda i: (i, 0, 0))
    cblk = pl.BlockSpec((gs, CONV_W - 1, D_FF), lambda i: (i, 0, 0))
    return pl.pallas_call(
        functools.partial(_sample_ffn_body, gs=gs, sl_len=l),
        grid=(b // gs,),
        in_specs=[blk, cblk, _const_spec((1, d)), _const_spec((d, 2 * D_FF)), _const_spec((CONV_W, 1, D_FF)),
                  _const_spec((1, D_FF)), _const_spec((D_FF, d)), _const_spec((1, d))],
        out_specs=[blk, cblk],
        out_shape=[jax.ShapeDtypeStruct((b, l, d), F32),
                   jax.ShapeDtypeStruct((b, CONV_W - 1, D_FF), F32)],
        compiler_params=pltpu.CompilerParams(dimension_semantics=("arbitrary",),
                                             vmem_limit_bytes=VMEM_LIMIT_BYTES),
        name="sample_ffn",
    )(x, chist, ln2, w_up, conv_w, conv_b, w_down, lnf)


def _block_diag(pool_w):
    n = pool_w.shape[0]
    same_group = jnp.eye(n, dtype=bool)[:, None, :, None]
    return jnp.where(same_group, pool_w[:, :, None, :], 0.0).reshape(n * POOL_GROUP, n * POOL_GROUP)


def _layer(x_prompt, x_sample, mem_prompt, state_pool, state_hgrn, state_conv, cache_mem_k, cache_mem_v,
           ln1_g, w_in, pool_w, pool_scale, hgrn_lb_logits, hgrn_onorm_g, mem_norm_g, w_mem_kv, w_out,
           ln2_g, w_up, conv_w, conv_b, w_down, lnf_g, *, prompt_tb, mixer_gs, ffn_gs):
    row = lambda a: a.reshape(1, -1)
    pool_wbd = _block_diag(pool_w).astype(BF16)

    def mixer_w(w_in, w_out):
        return (row(ln1_g), w_in, pool_wbd, row(pool_scale), hgrn_lb_logits, row(hgrn_onorm_g), w_out)

    def ffn_w(w_up, w_down):
        return (row(ln2_g), w_up, conv_w[:, None, :], row(conv_b), w_down, row(lnf_g))

    mkt, mvt = _memkv_call(mem_prompt, row(mem_norm_g), w_mem_kv)
    y_prompt, new_pool_pt, new_s_p, new_conv_p, w_in_b, w_out_b, w_up_b, w_down_b = _prompt_call(
        x_prompt, mkt, mvt, mixer_w(w_in, w_out), ffn_w(w_up, w_down), tb=prompt_tb)
    new_pool_p = new_pool_pt.transpose(1, 0, 2)

    nb = x_sample.shape[0]
    smkt = cache_mem_k.transpose(0, 2, 3, 1).reshape(nb, XATTN_WIDTH, N_MEM)
    smvt = cache_mem_v.transpose(0, 2, 3, 1).reshape(nb, XATTN_WIDTH, N_MEM)
    xs, new_pool_st, new_s_s = _sample_mixer_call(x_sample, state_pool.transpose(1, 0, 2), state_hgrn, smkt, smvt,
                                                  mixer_w(w_in_b, w_out_b), gs=mixer_gs)
    new_pool_s = new_pool_st.transpose(1, 0, 2)
    y_sample, new_conv_s = _sample_ffn_call(xs, state_conv, ffn_w(w_up_b, w_down_b), gs=ffn_gs)
    bp = x_prompt.shape[0]
    mk, mv = (t.reshape(bp, XATTN_HEADS, XATTN_DH, N_MEM).transpose(0, 3, 1, 2) for t in (mkt, mvt))
    return (y_prompt, y_sample, new_pool_p, new_s_p, new_conv_p, mk, mv, new_pool_s, new_s_s, new_conv_s)


def kernel(x_prompt, x_sample, mem_prompt, state_pool, state_hgrn, state_conv, cache_mem_k, cache_mem_v,
           ln1_g, w_in, pool_w, pool_scale, hgrn_lb_logits, hgrn_onorm_g, mem_norm_g, w_mem_kv, w_out,
           ln2_g, w_up, conv_w, conv_b, w_down, lnf_g):
    assert w_in.shape[0] == 1, "one layer"
    outs = _layer(x_prompt, x_sample, mem_prompt, state_pool[0], state_hgrn[0], state_conv[0],
                  cache_mem_k[0], cache_mem_v[0], ln1_g[0], w_in[0], pool_w[0], pool_scale[0], hgrn_lb_logits,
                  hgrn_onorm_g[0], mem_norm_g[0], w_mem_kv[0], w_out[0], ln2_g[0], w_up[0], conv_w[0], conv_b[0],
                  w_down[0], lnf_g, prompt_tb=256, mixer_gs=16, ffn_gs=32)
    y_prompt, y_sample = outs[0], outs[1]
    return (y_prompt, y_sample) + tuple(o[None] for o in outs[2:])
```

```python
import functools

import jax
import jax.numpy as jnp
from jax import lax
from jax.experimental import pallas as pl
from jax.experimental.pallas import tpu as pltpu

F32 = jnp.float32
BF16 = jnp.bfloat16

D_MODEL = 1024
POOL_WIDTH = 256
POOL_GROUP = 64
POOL_HIST = 15
HGRN_WIDTH = 512
HGRN_HEADS = 4
HGRN_DK = 128
HGRN_DV = 128
XATTN_WIDTH = 256
XATTN_HEADS = 4
XATTN_DH = 64
N_MEM = 256
D_FF = 2816
CONV_W = 3
EPS = 1e-6
PAST_LEN = 16384
D_IN = POOL_WIDTH + 4 * HGRN_WIDTH + XATTN_WIDTH
OFF_U, OFF_Q, OFF_F, OFF_I, OFF_G, OFF_X = 0, 256, 768, 1280, 1792, 2304

CHUNK = 64
POOL_PAD = 16
CONV_PAD = 8
LANES = 128
MEMKV_GROUP = 4
STAGE_BYTES = 720896
STAGE_SLOTS = 6
MEM_BUFFERS = 3
SAMPLE_MIXER_VMEM_LIMIT_BYTES = 62 * 1024 * 1024
VMEM_LIMIT_BYTES = 56 * 1024 * 1024

_NT = (((1,), (1,)), ((), ()))
_TN = (((0,), (0,)), ((), ()))


def _dot(a, b):
    return jnp.dot(a, b, preferred_element_type=F32)


def _dot_nt(a, b):
    return lax.dot_general(a, b, _NT, preferred_element_type=F32)


def _dot_tn(a, b):
    return lax.dot_general(a, b, _TN, preferred_element_type=F32)


def _rmsnorm(x, g):
    return x * lax.rsqrt(jnp.mean(x * x, axis=-1, keepdims=True) + EPS) * g


def _const_spec(shape):
    nd = len(shape)
    return pl.BlockSpec(shape, lambda *_: (0,) * nd, pipeline_mode=pl.Buffered(1))


def _memkv_body(mem_ref, g_ref, w_ref, kt_ref, vt_ref):
    nb = mem_ref.shape[0]
    h = _rmsnorm(mem_ref[...].reshape(nb * N_MEM, D_MODEL), g_ref[...]).astype(BF16)
    kv = _dot(h, w_ref[...].astype(BF16))
    for s in range(nb):
        kvt = kv[s * N_MEM:(s + 1) * N_MEM, :].T
        kt_ref[s] = kvt[:XATTN_WIDTH, :]
        vt_ref[s] = kvt[XATTN_WIDTH:, :]


def _memkv_call(mem, g, w):
    b = mem.shape[0]
    nb = MEMKV_GROUP
    out = jax.ShapeDtypeStruct((b, XATTN_WIDTH, N_MEM), F32)
    return pl.pallas_call(
        _memkv_body,
        grid=(b // nb,),
        in_specs=[pl.BlockSpec((nb, N_MEM, D_MODEL), lambda i: (i, 0, 0)),
                  _const_spec((1, D_MODEL)),
                  _const_spec((D_MODEL, 2 * XATTN_WIDTH))],
        out_specs=[pl.BlockSpec((nb, XATTN_WIDTH, N_MEM), lambda i: (i, 0, 0))] * 2,
        out_shape=[out, out],
        compiler_params=pltpu.CompilerParams(dimension_semantics=("arbitrary",)),
        name="memkv",
    )(mem, g, w)


def _forget_lower_bound(logits):
    z = logits - jnp.max(logits, axis=0, keepdims=True)
    e = jnp.exp(z)
    return e[0:1, :] / jnp.sum(e, axis=0, keepdims=True)


def _hgrn_gates(proj, lb):
    fp = proj[:, OFF_F:OFF_F + HGRN_WIDTH]
    q = proj[:, OFF_Q:OFF_Q + HGRN_WIDTH]
    log_f = jnp.log2(lb + (1.0 - lb) * jax.nn.sigmoid(fp))
    k = (1.0 - lb) * jax.nn.sigmoid(-fp)
    qf = q * jax.nn.sigmoid(q)
    return qf, k, log_f


def _segment_cumsum(x, seq):
    ridx = lax.broadcasted_iota(jnp.int32, x.shape, 0) & (seq - 1)
    sh = 1
    while sh < seq:
        x = x + jnp.where(ridx >= sh, pltpu.roll(x, sh, 0), 0.0)
        sh *= 2
    return x


def _level_factor(a, qf, k, log_f, m, rows):
    n = a.shape[1]
    ridx = lax.broadcasted_iota(jnp.int32, (rows, n), 0)
    upper = (ridx & m) != 0
    if m == 1:
        d = jnp.where(upper, log_f, 0.0)
    else:
        if (2 * m) % 8 == 0:
            nb = rows // (2 * m)
            a3 = a.reshape(nb, 2 * m, n)
            ref = jnp.broadcast_to(a3[:, m - 1:m, :], (nb, 2 * m, n)).reshape(rows, n)
        else:
            a3 = a.reshape(rows // 8, 8, n)
            sub = lax.broadcasted_iota(jnp.int32, a3.shape, 1)
            ref = jnp.where(sub < 4,
                            jnp.broadcast_to(a3[:, 1:2, :], a3.shape),
                            jnp.broadcast_to(a3[:, 5:6, :], a3.shape)).reshape(rows, n)
        d = -jnp.abs(a - ref)
    return (jnp.where(upper, qf, k) * jnp.exp2(d)).astype(BF16)


def _level_factors(a, qf, k, log_f, rows, seq):
    out, m = [], seq // 2
    while m >= 1:
        out.append((m, _level_factor(a, qf, k, log_f, m, rows)))
        m //= 2
    return out


def _intra_scores(factors, head, rows, seq):
    sl = slice(head * HGRN_DK, (head + 1) * HGRN_DK)
    t = lax.broadcasted_iota(jnp.int32, (rows, rows), 0)
    s = lax.broadcasted_iota(jnp.int32, (rows, rows), 1)
    x = t ^ s
    products = [(m, _dot_nt(y[:, sl], y[:, sl])) for m, y in reversed(factors)]
    total = jnp.zeros((rows, rows), F32)
    for m, p in products:
        total = jnp.where(x >= m, p, total)
    return jnp.where((t > s) & (x < seq), total, 0.0)


def _head_norm_gate(o, gate):
    return o * lax.rsqrt(jnp.mean(o * o, axis=-1, keepdims=True) + EPS) * gate


def _softmax_rows(s):
    e = jnp.exp(s - jnp.max(s, axis=-1, keepdims=True))
    return e / jnp.sum(e, axis=-1, keepdims=True)


def _cross_attention(qx, mkt, mvt):
    rows = qx.shape[0]
    head_of_lane = lax.broadcasted_iota(jnp.int32, qx.shape, 1) // XATTN_DH
    qs = jnp.concatenate([jnp.where(head_of_lane == h, qx, 0.0) for h in range(XATTN_HEADS)], axis=0)
    p = _softmax_rows(_dot(qs.astype(BF16), mkt))
    o = _dot_nt(p.astype(BF16), mvt)
    out = jnp.zeros(qx.shape, F32)
    for h in range(XATTN_HEADS):
        out = jnp.where(head_of_lane == h, o[h * rows:(h + 1) * rows, :], out)
    return out


def _pool_means(ld, posf, shape):
    lane = lax.broadcasted_iota(jnp.int32, shape, len(shape) - 1)
    first = lane < POOL_GROUP
    u_lo, u_hi = ld(0, 0), ld(0, 1)
    t2 = u_lo + ld(1, 0)
    t4 = t2 + ld(2, 0) + ld(3, 0)
    t8 = u_hi
    for j in range(1, 8):
        t8 = t8 + ld(j, 1)
    t16 = t8
    for j in range(8, 16):
        t16 = t16 + ld(j, 1)
    cnt_lo = jnp.where(first, jnp.minimum(2.0, posf), jnp.minimum(4.0, posf))
    cnt_hi = jnp.where(first, jnp.minimum(8.0, posf), jnp.minimum(16.0, posf))
    lo = jnp.where(first, t2, t4) / cnt_lo - u_lo
    hi = jnp.where(first, t8, t16) / cnt_hi - u_hi
    return jnp.concatenate([lo, hi], axis=-1)


def _after(x, anchor, zero):
    r, n = anchor.shape
    s = jnp.sum(anchor.reshape(r // 8, 8, n), axis=0)
    c = s[:, :LANES]
    for i in range(1, n // LANES):
        c = c + s[:, i * LANES:(i + 1) * LANES]
    z = lax.bitcast_convert_type(lax.bitcast_convert_type(c[0:1, :], jnp.int32) & zero, F32)
    return jnp.concatenate([x[:, :LANES] + z, x[:, LANES:]], axis=1)


def _gelu_tanh(x):
    c = 0.7978845608028654
    half_x = 0.5 * x
    return half_x + half_x * jnp.tanh(x * (c + (0.044715 * c) * (x * x)))


def _ffn_tail(x, act, wdown_ref, lnf_ref):
    y = x + _dot(act.astype(BF16), wdown_ref[...])
    return _rmsnorm(y, lnf_ref[...])


def _stage_widths(weights):
    return list(dict.fromkeys(w.shape[1] for w in weights))


def _stage_rows(cols):
    return STAGE_BYTES // (4 * cols) // 16 * 16


def _load_weights(srcs, dsts, stages, sem):
    widths = _stage_widths(srcs)
    blocks, used = [], [0] * len(stages)
    for i, w in enumerate(srcs):
        f = widths.index(w.shape[1])
        rows = stages[f].shape[1]
        for r0 in range(0, w.shape[0], rows):
            blocks.append((i, r0, min(rows, w.shape[0] - r0), f, used[f] % STAGE_SLOTS))
            used[f] += 1

    def copy(n):
        i, r0, nr, f, slot = blocks[n]
        return pltpu.make_async_copy(srcs[i].at[pl.ds(r0, nr), :], stages[f].at[slot, pl.ds(0, nr), :],
                                     sem.at[f, slot])

    ahead = STAGE_SLOTS - 1
    for n in range(min(ahead, len(blocks))):
        copy(n).start()
    for n, (i, r0, nr, f, slot) in enumerate(blocks):
        if n + ahead < len(blocks):
            copy(n + ahead).start()
        copy(n).wait()
        dsts[i][pl.ds(r0, nr), :] = stages[f][slot, pl.ds(0, nr), :].astype(BF16)


def _prompt_body(x_ref, mkt_ref, mvt_ref, ln1_ref, win_hbm, poolw_ref, pscale_ref, lbl_ref, onorm_ref,
                 wout_hbm, ln2_ref, wup_hbm, cw_ref, cb_ref, wdown_hbm, lnf_ref, zero_ref,
                 y_ref, npool_ref, ns_ref, nconv_ref, win_out, wout_out, wup_out, wdown_out,
                 pbuf, st, abuf, x2s, win_ref, wout_ref, wup_ref, wdown_ref, out_sem, stage_sem, *stages,
                 tb, nt, nblk):
    g = pl.program_id(0)
    jm = jnp.minimum(g, nblk - 1) % nt
    jf = jnp.maximum(g - 1, 0) % nt

    weights_bf16 = (win_ref, wout_ref, wup_ref, wdown_ref)

    def weight_writeback(n):
        return pltpu.make_async_copy(weights_bf16[n], (win_out, wout_out, wup_out, wdown_out)[n], out_sem.at[n])

    @pl.when(g == 0)
    def _():
        _load_weights((win_hbm, wout_hbm, wup_hbm, wdown_hbm), weights_bf16, stages, stage_sem)
        for n in range(len(weights_bf16)):
            weight_writeback(n).start()

    @pl.when(g == nblk)
    def _():
        for n in range(len(weights_bf16)):
            weight_writeback(n).wait()

    @pl.when(jm == 0)
    def _():
        pbuf[pl.ds(0, POOL_PAD), :] = jnp.zeros((POOL_PAD, POOL_WIDTH), F32)
        st[...] = jnp.zeros(st.shape, F32)

    @pl.when(jf == 0)
    def _():
        abuf[...] = jnp.zeros(abuf.shape, F32)

    def input_projection():
        h = _rmsnorm(x_ref[0], ln1_ref[...]).astype(BF16)
        return _dot(h, win_ref[...])

    def mixer_half(proj):
        _prompt_mixer_half(proj, x_ref, mkt_ref, mvt_ref, poolw_ref, pscale_ref, lbl_ref, onorm_ref, wout_ref,
                           pbuf, st, x2s, jm, tb)

    def ffn_half(proj):
        x2 = x2s[...]
        h2 = _rmsnorm(x2, ln2_ref[...]).astype(BF16)
        ab = _dot(h2, wup_ref[...])
        a = ab[:, :D_FF]
        hist = abuf[...]
        sub = lax.broadcasted_iota(jnp.int32, (CONV_PAD, D_FF), 0)
        r1 = pltpu.roll(a, 1, 0)
        r2 = pltpu.roll(a, 2, 0)
        h1 = jnp.broadcast_to(hist[CONV_PAD - 1:CONV_PAD, :], (CONV_PAD, D_FF))
        h0 = jnp.broadcast_to(hist[CONV_PAD - 2:CONV_PAD - 1, :], (CONV_PAD, D_FF))
        a_m1 = jnp.concatenate([jnp.where(sub >= 1, r1[:CONV_PAD], h1), r1[CONV_PAD:]], axis=0)
        a_m2 = jnp.concatenate([jnp.where(sub >= 2, r2[:CONV_PAD], jnp.where(sub == 1, h1, h0)), r2[CONV_PAD:]],
                               axis=0)
        conv = cb_ref[...] + cw_ref[0] * a_m2 + cw_ref[1] * a_m1 + cw_ref[2] * a
        act = _gelu_tanh(conv) * ab[:, D_FF:]
        abuf[...] = a[tb - CONV_PAD:, :]
        if proj is not None:
            act = _after(act, proj, zero_ref[...])
        y_ref[0] = _rmsnorm(x2 + _dot(act.astype(BF16), wdown_ref[...]), lnf_ref[...])

    @pl.when(g == 0)
    def _():
        mixer_half(input_projection())

    @pl.when(jnp.logical_and(g > 0, g < nblk))
    def _():
        proj = input_projection()
        ffn_half(proj)
        mixer_half(proj)

    @pl.when(g == nblk)
    def _():
        ffn_half(None)

    @pl.when(jnp.logical_and(jm == nt - 1, g < nblk))
    def _():
        seq = g // nt
        for j in range(POOL_HIST):
            npool_ref[j, pl.ds(seq, 1), :] = pbuf[pl.ds(1 + j, 1), :]
        for hd in range(HGRN_HEADS):
            ns_ref[0, hd] = st[hd].T

    @pl.when(jnp.logical_and(jf == nt - 1, g >= 1))
    def _():
        nconv_ref[0] = abuf[pl.ds(CONV_PAD - (CONV_W - 1), CONV_W - 1), :]


def _prompt_mixer_half(proj, x_ref, mkt_ref, mvt_ref, poolw_ref, pscale_ref, lbl_ref, onorm_ref, wout_ref,
                       pbuf, st, x2s, jm, tb):
    x = x_ref[0]

    pbuf[pl.ds(POOL_PAD, tb), :] = proj[:, OFF_U:OFF_U + POOL_WIDTH]
    posf = (jm * tb + 1 + lax.broadcasted_iota(jnp.int32, (tb, 1), 0)).astype(F32)
    dm = _pool_means(lambda j, half: pbuf[pl.ds(POOL_PAD - j, tb), pl.ds(LANES * half, LANES)], posf, (tb, LANES))
    o_pool = _dot(dm.astype(BF16), poolw_ref[...]) * pscale_ref[...]
    pbuf[pl.ds(0, POOL_PAD), :] = pbuf[pl.ds(tb, POOL_PAD), :]

    qx = proj[:, OFF_X:OFF_X + XATTN_WIDTH] * (XATTN_DH ** -0.5)
    o_x = _cross_attention(qx, mkt_ref[0].astype(BF16), mvt_ref[0].astype(BF16))

    lb = _forget_lower_bound(lbl_ref[...])
    qf, k, log_f = _hgrn_gates(proj, lb)
    v = proj[:, OFF_I:OFF_I + HGRN_WIDTH]
    gg = proj[:, OFF_G:OFF_G + HGRN_WIDTH]
    gate = gg * jax.nn.sigmoid(gg) * onorm_ref[...]
    a_all = _segment_cumsum(log_f, CHUNK)
    states = [st[hd] for hd in range(HGRN_HEADS)]
    o_rows = []
    for c in range(tb // CHUNK):
        rs = slice(c * CHUNK, (c + 1) * CHUNK)
        qf_c, k_c, lf_c, v_c, a = qf[rs], k[rs], log_f[rs], v[rs], a_all[rs]
        a_end = a[CHUNK - 1:CHUNK, :]
        q_in = (qf_c * jnp.exp2(a)).astype(BF16)
        k_out = (k_c * jnp.exp2(a_end - a)).astype(BF16)
        decay = jnp.exp2(a_end)
        v_b = v_c.astype(BF16)
        qk = qf_c * k_c
        factors = _level_factors(a, qf_c, k_c, lf_c, CHUNK, CHUNK)
        heads = [slice(hd * HGRN_DK, (hd + 1) * HGRN_DK) for hd in range(HGRN_HEADS)]
        inter = [_dot_nt(q_in[:, sl], states[hd].astype(BF16)) for hd, sl in enumerate(heads)]
        update = [_dot_tn(v_b[:, sl], k_out[:, sl]) for sl in heads]
        scores = [_intra_scores(factors, hd, CHUNK, CHUNK).astype(BF16) for hd in range(HGRN_HEADS)]
        intra = [_dot(scores[hd], v_b[:, sl]) for hd, sl in enumerate(heads)]
        o_heads = []
        for hd, sl in enumerate(heads):
            o = intra[hd] + inter[hd] + jnp.sum(qk[:, sl], axis=-1, keepdims=True) * v_c[:, sl]
            states[hd] = states[hd] * decay[:, sl] + update[hd]
            o_heads.append(_head_norm_gate(o, gate[rs, sl]))
        o_rows.append(jnp.concatenate(o_heads, axis=-1))
    for hd in range(HGRN_HEADS):
        st[hd] = states[hd]
    o_hgrn = jnp.concatenate(o_rows, axis=0)

    mixed = jnp.concatenate([o_pool, o_hgrn, o_x], axis=-1).astype(BF16)
    x2s[...] = x + _dot(mixed, wout_ref[...])


def _prompt_call(x, mkt, mvt, mixer_w, ffn_w, tb):
    b, l, d = x.shape
    nt = l // tb
    nblk = b * nt
    ln1, w_in, pool_wbd, pool_scale, lb_logits, onorm, w_out = mixer_w
    ln2, w_up, conv_w, conv_b, w_down, lnf = ffn_w

    def mixer_blk(g):
        return jnp.minimum(g, nblk - 1)

    def ffn_blk(g):
        return jnp.maximum(g - 1, 0)

    x_spec = pl.BlockSpec((1, tb, d), lambda g: (mixer_blk(g) // nt, mixer_blk(g) % nt, 0))
    mem = pl.BlockSpec((1, XATTN_WIDTH, N_MEM), lambda g: (mixer_blk(g) // nt, 0, 0))
    y_spec = pl.BlockSpec((1, tb, d), lambda g: (ffn_blk(g) // nt, ffn_blk(g) % nt, 0))
    in_hbm = pl.BlockSpec(memory_space=pl.ANY)
    weights = (w_in, w_out, w_up, w_down)
    assert all(w.dtype == F32 and w.shape[0] % 16 == 0 and w.shape[1] % LANES == 0 for w in weights)
    widths = _stage_widths(weights)
    return pl.pallas_call(
        functools.partial(_prompt_body, tb=tb, nt=nt, nblk=nblk),
        grid=(nblk + 1,),
        in_specs=[x_spec, mem, mem,
                  _const_spec((1, d)), in_hbm, _const_spec((POOL_WIDTH, POOL_WIDTH)),
                  _const_spec((1, POOL_WIDTH)), _const_spec(lb_logits.shape), _const_spec((1, HGRN_WIDTH)),
                  in_hbm,
                  _const_spec((1, d)), in_hbm, _const_spec((CONV_W, 1, D_FF)),
                  _const_spec((1, D_FF)), in_hbm, _const_spec((1, d)), _const_spec((1, LANES))],
        out_specs=[y_spec,
                   pl.BlockSpec((POOL_HIST, b, POOL_WIDTH), lambda g: (0, 0, 0)),
                   pl.BlockSpec((1, HGRN_HEADS, HGRN_DK, HGRN_DV), lambda g: (mixer_blk(g) // nt, 0, 0, 0)),
                   pl.BlockSpec((1, CONV_W - 1, D_FF), lambda g: (ffn_blk(g) // nt, 0, 0))]
                  + [pl.BlockSpec(memory_space=pl.ANY) for _ in weights],
        out_shape=[jax.ShapeDtypeStruct((b, l, d), F32),
                   jax.ShapeDtypeStruct((POOL_HIST, b, POOL_WIDTH), F32),
                   jax.ShapeDtypeStruct((b, HGRN_HEADS, HGRN_DK, HGRN_DV), F32),
                   jax.ShapeDtypeStruct((b, CONV_W - 1, D_FF), F32)]
                  + [jax.ShapeDtypeStruct(w.shape, BF16) for w in weights],
        scratch_shapes=[pltpu.VMEM((POOL_PAD + tb, POOL_WIDTH), F32),
                        pltpu.VMEM((HGRN_HEADS, HGRN_DV, HGRN_DK), F32),
                        pltpu.VMEM((CONV_PAD, D_FF), F32),
                        pltpu.VMEM((tb, d), F32)]
                       + [pltpu.VMEM(w.shape, BF16) for w in weights]
                       + [pltpu.SemaphoreType.DMA((len(weights),)),
                          pltpu.SemaphoreType.DMA((len(widths), STAGE_SLOTS))]
                       + [pltpu.VMEM((STAGE_SLOTS, _stage_rows(c), c), F32) for c in widths],
        compiler_params=pltpu.CompilerParams(dimension_semantics=("arbitrary",),
                                             vmem_limit_bytes=VMEM_LIMIT_BYTES),
        name="prompt_layer",
    )(x, mkt, mvt, ln1, w_in, pool_wbd, pool_scale, lb_logits, onorm, w_out,
      ln2, w_up, conv_w, conv_b, w_down, lnf, jnp.zeros((1, LANES), jnp.int32))


def _sample_mixer_body(x_ref, hist_ref, s0_ref, mkt_hbm, mvt_hbm, ln1_ref, win_ref, poolw_ref, pscale_ref, lbl_ref,
                       onorm_ref, wout_ref, x2_ref, npool_ref, ns_ref, pbuf, ubuf, membuf, memsem, *, gs, sl_len):
    rows = gs * sl_len
    step, nsteps = pl.program_id(0), pl.num_programs(0)

    def mem_copy(which, blk):
        src = (mkt_hbm, mvt_hbm)[which]
        slot = blk % MEM_BUFFERS
        return pltpu.make_async_copy(src.at[pl.ds(blk * gs, gs)], membuf.at[which, slot], memsem.at[which, slot])

    @pl.when(step == 0)
    def _():
        for blk in range(MEM_BUFFERS - 1):
            for which in range(2):
                mem_copy(which, blk).start()

    @pl.when(step + MEM_BUFFERS - 1 < nsteps)
    def _():
        for which in range(2):
            mem_copy(which, step + MEM_BUFFERS - 1).start()

    for which in range(2):
        mem_copy(which, step).wait()
    mkt_ref = membuf.at[0, step % MEM_BUFFERS]
    mvt_ref = membuf.at[1, step % MEM_BUFFERS]
    seqs = [slice(s * sl_len, (s + 1) * sl_len) for s in range(gs)]
    x = x_ref[...].reshape(rows, D_MODEL)
    h = _rmsnorm(x, ln1_ref[...]).astype(BF16)
    proj = _dot(h, win_ref[...])

    pbuf[pl.ds(1, POOL_HIST)] = hist_ref[...]
    halves = [pl.ds(LANES * half, LANES) for half in range(POOL_WIDTH // LANES)]
    for half, lanes in enumerate(halves):
        ubuf[half] = proj[:, OFF_U + LANES * half:OFF_U + LANES * (half + 1)]
    for t in range(sl_len):
        for half, lanes in enumerate(halves):
            pbuf[POOL_PAD + t, :, lanes] = ubuf[half, pl.ds(t, gs, stride=sl_len), :]
    posf = (PAST_LEN + 1 + lax.broadcasted_iota(jnp.int32, (sl_len, 1, 1), 0)).astype(F32)
    dm = _pool_means(lambda j, half: pbuf[pl.ds(POOL_PAD - j, sl_len), :, pl.ds(LANES * half, LANES)],
                     posf, (sl_len, gs, LANES))
    npool_ref[...] = pbuf[pl.ds(sl_len + 1, POOL_HIST)]
    for t in range(sl_len):
        for half in range(len(halves)):
            ubuf[half, pl.ds(t, gs, stride=sl_len), :] = dm[t][:, LANES * half:LANES * (half + 1)]
    dm_rows = jnp.concatenate([ubuf[half] for half in range(len(halves))], axis=-1)
    o_pool = _dot(dm_rows.astype(BF16), poolw_ref[...]) * pscale_ref[...]

    qx3 = (proj[:, OFF_X:OFF_X + XATTN_WIDTH] * (XATTN_DH ** -0.5)).reshape(gs, sl_len, XATTN_WIDTH)
    head_of_lane = lax.broadcasted_iota(jnp.int32, qx3.shape, 2) // XATTN_DH
    qs3 = jnp.concatenate([jnp.where(head_of_lane == hd, qx3, 0.0) for hd in range(XATTN_HEADS)],
                          axis=1).astype(BF16)
    hrows = XATTN_HEADS * sl_len
    scores = jnp.concatenate([_dot(qs3[s], mkt_ref[s].astype(BF16)) for s in range(gs)], axis=0)
    p = _softmax_rows(scores).astype(BF16)
    o4 = jnp.concatenate([_dot_nt(p[s * hrows:(s + 1) * hrows], mvt_ref[s].astype(BF16)) for s in range(gs)],
                         axis=0)
    o4 = o4.reshape(gs, XATTN_HEADS, sl_len, XATTN_WIDTH)
    o_x3 = jnp.zeros(qx3.shape, F32)
    for hd in range(XATTN_HEADS):
        o_x3 = jnp.where(head_of_lane == hd, o4[:, hd], o_x3)
    o_x = o_x3.reshape(rows, XATTN_WIDTH)

    lb = _forget_lower_bound(lbl_ref[...])
    qf, k, log_f = _hgrn_gates(proj, lb)
    v = proj[:, OFF_I:OFF_I + HGRN_WIDTH]
    gg = proj[:, OFF_G:OFF_G + HGRN_WIDTH]
    gate = gg * jax.nn.sigmoid(gg) * onorm_ref[...]
    a = _segment_cumsum(log_f, sl_len)
    a3 = a.reshape(gs, sl_len, HGRN_WIDTH)
    a_end = jnp.broadcast_to(a3[:, sl_len - 1:sl_len, :], a3.shape).reshape(rows, HGRN_WIDTH)
    q_in = (qf * jnp.exp2(a)).astype(BF16)
    k_out = (k * jnp.exp2(a_end - a)).astype(BF16)
    decay = jnp.exp2(a_end)
    v_b = v.astype(BF16)
    qk = qf * k
    factors = _level_factors(a, qf, k, log_f, rows, sl_len)
    heads = [slice(hd * HGRN_DK, (hd + 1) * HGRN_DK) for hd in range(HGRN_HEADS)]
    inter = [jnp.concatenate([_dot(q_in[r, sl], s0_ref[s, hd].astype(BF16)) for s, r in enumerate(seqs)], axis=0)
             for hd, sl in enumerate(heads)]
    o_heads = []
    for hd, sl in enumerate(heads):
        p_h = _intra_scores(factors, hd, rows, sl_len)
        o = _dot(p_h.astype(BF16), v_b[:, sl]) + inter[hd]
        o = o + jnp.sum(qk[:, sl], axis=-1, keepdims=True) * v[:, sl]
        o_heads.append(_head_norm_gate(o, gate[:, sl]))
    updates = [[_dot_tn(k_out[r, sl], v_b[r, sl]) for sl in heads] for r in seqs]
    for s, r in enumerate(seqs):
        for hd, sl in enumerate(heads):
            decay_cols = jnp.broadcast_to(decay[r, sl][sl_len - 1:sl_len, :], (HGRN_DV, HGRN_DK)).T
            ns_ref[s, hd] = decay_cols * s0_ref[s, hd] + updates[s][hd]

    mixed = jnp.concatenate([o_pool] + o_heads + [o_x], axis=-1).astype(BF16)
    x2_ref[...] = (x + _dot(mixed, wout_ref[...])).reshape(gs, sl_len, D_MODEL)


def _sample_mixer_call(x, hist, s0, mkt, mvt, mixer_w, gs):
    b, l, d = x.shape
    ln1, w_in, pool_wbd, pool_scale, lb_logits, onorm, w_out = mixer_w
    grid = (b // gs,)
    blk = pl.BlockSpec((gs, l, d), lambda i: (i, 0, 0))
    histb = pl.BlockSpec((POOL_HIST, gs, POOL_WIDTH), lambda i: (0, i, 0))
    sb = pl.BlockSpec((gs, HGRN_HEADS, HGRN_DK, HGRN_DV), lambda i: (i, 0, 0, 0))
    mem = pl.BlockSpec(memory_space=pl.ANY)
    assert b % gs == 0 and b // gs >= MEM_BUFFERS - 1
    return pl.pallas_call(
        functools.partial(_sample_mixer_body, gs=gs, sl_len=l),
        grid=grid,
        in_specs=[blk, histb, sb, mem, mem,
                  _const_spec((1, d)), _const_spec((d, D_IN)), _const_spec((POOL_WIDTH, POOL_WIDTH)),
                  _const_spec((1, POOL_WIDTH)), _const_spec(lb_logits.shape), _const_spec((1, HGRN_WIDTH)),
                  _const_spec((d, d))],
        out_specs=[blk, histb, sb],
        out_shape=[jax.ShapeDtypeStruct((b, l, d), F32),
                   jax.ShapeDtypeStruct((POOL_HIST, b, POOL_WIDTH), F32),
                   jax.ShapeDtypeStruct((b, HGRN_HEADS, HGRN_DK, HGRN_DV), F32)],
        scratch_shapes=[pltpu.VMEM((POOL_PAD + l, gs, POOL_WIDTH), F32),
                        pltpu.VMEM((POOL_WIDTH // LANES, gs * l, LANES), F32),
                        pltpu.VMEM((2, MEM_BUFFERS, gs, XATTN_WIDTH, N_MEM), F32),
                        pltpu.SemaphoreType.DMA((2, MEM_BUFFERS))],
        compiler_params=pltpu.CompilerParams(dimension_semantics=("arbitrary",),
                                             vmem_limit_bytes=SAMPLE_MIXER_VMEM_LIMIT_BYTES),
        name="sample_mixer",
    )(x, hist, s0, mkt, mvt, ln1, w_in, pool_wbd, pool_scale, lb_logits, onorm, w_out)


def _sample_ffn_body(x_ref, chist_ref, ln2_ref, wup_ref, cw_ref, cb_ref, wdown_ref, lnf_ref, y_ref, nconv_ref,
                     *, gs, sl_len):
    rows = gs * sl_len
    x = x_ref[...].reshape(rows, D_MODEL)
    h = _rmsnorm(x, ln2_ref[...]).astype(BF16)
    ab = _dot(h, wup_ref[...])
    a = ab[:, :D_FF]
    ridx = lax.broadcasted_iota(jnp.int32, (rows, D_FF), 0) % sl_len
    hist = chist_ref[...]
    h1 = jnp.broadcast_to(hist[:, 1:2, :], (gs, sl_len, D_FF)).reshape(rows, D_FF)
    h0 = jnp.broadcast_to(hist[:, 0:1, :], (gs, sl_len, D_FF)).reshape(rows, D_FF)
    a_m1 = jnp.where(ridx >= 1, pltpu.roll(a, 1, 0), h1)
    a_m2 = jnp.where(ridx >= 2, pltpu.roll(a, 2, 0), jnp.where(ridx == 1, h1, h0))
    conv = cb_ref[...] + cw_ref[0] * a_m2 + cw_ref[1] * a_m1 + cw_ref[2] * a
    act = _gelu_tanh(conv) * ab[:, D_FF:]
    nconv_ref[...] = a.reshape(gs, sl_len, D_FF)[:, sl_len - (CONV_W - 1):, :]
    y_ref[...] = _ffn_tail(x, act, wdown_ref, lnf_ref).reshape(gs, sl_len, D_MODEL)


def _sample_ffn_call(x, chist, ffn_w, gs):
    b, l, d = x.shape
    ln2, w_up, conv_w, conv_b, w_down, lnf = ffn_w
    blk = pl.BlockSpec((gs, l, d), lambda i: (i, 0, 0))
    cblk = pl.BlockSpec((gs, CONV_W - 1, D_FF), lambda i: (i, 0, 0))
    return pl.pallas_call(
        functools.partial(_sample_ffn_body, gs=gs, sl_len=l),
        grid=(b // gs,),
        in_specs=[blk, cblk, _const_spec((1, d)), _const_spec((d, 2 * D_FF)), _const_spec((CONV_W, 1, D_FF)),
                  _const_spec((1, D_FF)), _const_spec((D_FF, d)), _const_spec((1, d))],
        out_specs=[blk, cblk],
        out_shape=[jax.ShapeDtypeStruct((b, l, d), F32),
                   jax.ShapeDtypeStruct((b, CONV_W - 1, D_FF), F32)],
        compiler_params=pltpu.CompilerParams(dimension_semantics=("arbitrary",),
                                             vmem_limit_bytes=VMEM_LIMIT_BYTES),
        name="sample_ffn",
    )(x, chist, ln2, w_up, conv_w, conv_b, w_down, lnf)


def _block_diag(pool_w):
    n = pool_w.shape[0]
    same_group = jnp.eye(n, dtype=bool)[:, None, :, None]
    return jnp.where(same_group, pool_w[:, :, None, :], 0.0).reshape(n * POOL_GROUP, n * POOL_GROUP)


def _layer(x_prompt, x_sample, mem_prompt, state_pool, state_hgrn, state_conv, cache_mem_k, cache_mem_v,
           ln1_g, w_in, pool_w, pool_scale, hgrn_lb_logits, hgrn_onorm_g, mem_norm_g, w_mem_kv, w_out,
           ln2_g, w_up, conv_w, conv_b, w_down, lnf_g, *, prompt_tb, mixer_gs, ffn_gs):
    row = lambda a: a.reshape(1, -1)
    pool_wbd = _block_diag(pool_w).astype(BF16)

    def mixer_w(w_in, w_out):
        return (row(ln1_g), w_in, pool_wbd, row(pool_scale), hgrn_lb_logits, row(hgrn_onorm_g), w_out)

    def ffn_w(w_up, w_down):
        return (row(ln2_g), w_up, conv_w[:, None, :], row(conv_b), w_down, row(lnf_g))

    mkt, mvt = _memkv_call(mem_prompt, row(mem_norm_g), w_mem_kv)
    y_prompt, new_pool_pt, new_s_p, new_conv_p, w_in_b, w_out_b, w_up_b, w_down_b = _prompt_call(
        x_prompt, mkt, mvt, mixer_w(w_in, w_out), ffn_w(w_up, w_down), tb=prompt_tb)
    new_pool_p = new_pool_pt.transpose(1, 0, 2)

    nb = x_sample.shape[0]
    smkt = cache_mem_k.transpose(0, 2, 3, 1).reshape(nb, XATTN_WIDTH, N_MEM)
    smvt = cache_mem_v.transpose(0, 2, 3, 1).reshape(nb, XATTN_WIDTH, N_MEM)
    xs, new_pool_st, new_s_s = _sample_mixer_call(x_sample, state_pool.transpose(1, 0, 2), state_hgrn, smkt, smvt,
                                                  mixer_w(w_in_b, w_out_b), gs=mixer_gs)
    new_pool_s = new_pool_st.transpose(1, 0, 2)
    y_sample, new_conv_s = _sample_ffn_call(xs, state_conv, ffn_w(w_up_b, w_down_b), gs=ffn_gs)
    bp = x_prompt.shape[0]
    mk, mv = (t.reshape(bp, XATTN_HEADS, XATTN_DH, N_MEM).transpose(0, 3, 1, 2) for t in (mkt, mvt))
    return (y_prompt, y_sample, new_pool_p, new_s_p, new_conv_p, mk, mv, new_pool_s, new_s_s, new_conv_s)


def kernel(x_prompt, x_sample, mem_prompt, state_pool, state_hgrn, state_conv, cache_mem_k, cache_mem_v,
           ln1_g, w_in, pool_w, pool_scale, hgrn_lb_logits, hgrn_onorm_g, mem_norm_g, w_mem_kv, w_out,
           ln2_g, w_up, conv_w, conv_b, w_down, lnf_g):
    assert w_in.shape[0] == 1, "one layer"
    outs = _layer(x_prompt, x_sample, mem_prompt, state_pool[0], state_hgrn[0], state_conv[0],
                  cache_mem_k[0], cache_mem_v[0], ln1_g[0], w_in[0], pool_w[0], pool_scale[0], hgrn_lb_logits,
                  hgrn_onorm_g[0], mem_norm_g[0], w_mem_kv[0], w_out[0], ln2_g[0], w_up[0], conv_w[0], conv_b[0],
                  w_down[0], lnf_g, prompt_tb=256, mixer_gs=16, ffn_gs=32)
    y_prompt, y_sample = outs[0], outs[1]
    return (y_prompt, y_sample) + tuple(o[None] for o in outs[2:])
```
